```python
import math
import jax
import jax.numpy as jnp
from jax import lax
import numpy as np

D_MODEL = 2048
BATCH = 32
SEQ = 256
DEPTH = 4
DEC_BATCH = 2
DEC_SEQ = 1024
PAST_LEN = 256

GRID_W = 64
GROUP_W = D_MODEL // 4
ML_HEADS = 4
ML_DH = GROUP_W // ML_HEADS
ML_CHUNK = 64
MLA_HEADS = 4
MLA_D_NOPE = GROUP_W // MLA_HEADS
MLA_D_ROPE = 64
MLA_D_V = GROUP_W // MLA_HEADS
MLA_Q_RANK = D_MODEL // 8
MLA_KV_RANK = D_MODEL // 8
MLA_SCALE = (MLA_D_NOPE + MLA_D_ROPE) ** -0.5
SWA_HEADS = 4
SWA_KV_HEADS = 2
SWA_DH = GROUP_W // SWA_HEADS
WINDOW = 128
DIFF_HEADS = 4
DIFF_DH = GROUP_W // (2 * DIFF_HEADS)
PEER_HEADS = 8
PEER_QDIM = 256
PEER_NKEYS = 128
PEER_N = PEER_NKEYS * PEER_NKEYS
PEER_TOPK = 16
PEER_CHUNK = 128

Q_BLOCK = 128
ROPE_BASE = 10000.0
EPS = 1e-6
NEG = -1e30
IN_SIZES = (GROUP_W, GROUP_W, GROUP_W, GROUP_W, 4 * ML_HEADS,
            MLA_Q_RANK, MLA_KV_RANK, MLA_D_ROPE,
            SWA_HEADS * SWA_DH, SWA_KV_HEADS * SWA_DH, SWA_KV_HEADS * SWA_DH,
            GROUP_W, GROUP_W, GROUP_W)
IN_W = sum(IN_SIZES)

kernel_name = 'hybrid_dit_prefix_denoise_step'


def rmsnorm(x, g):
    xf = x.astype(jnp.float32)
    y = xf * lax.rsqrt(jnp.mean(xf * xf, axis=-1, keepdims=True) + EPS)
    return (y * g.astype(jnp.float32)).astype(x.dtype)


def adaln(x, g, shift, scale):
    return rmsnorm(x, g) * (1 + scale) + shift


def modulation(cvec, w_mod, b_mod):
    mod = jnp.einsum('...d,de->...e', jax.nn.silu(cvec), w_mod) + b_mod
    return jnp.split(mod, 6, axis=-1)


def project(x, g, shift, scale, w_in):
    h = adaln(x, g, shift, scale)
    proj = jnp.einsum('btd,de->bte', h, w_in)
    cuts, acc = [], 0
    for s in IN_SIZES[:-1]:
        acc += s
        cuts.append(acc)
    return jnp.split(proj, cuts, axis=-1)


def heads(a, n):
    return a.reshape(a.shape[0], a.shape[1], n, -1)


def repeat_kv(a):
    return jnp.repeat(a, SWA_HEADS // SWA_KV_HEADS, axis=2)


def axial_rope(x):
    n_tok = x.shape[1]
    rows = n_tok // GRID_W
    row = jnp.repeat(jnp.arange(rows, dtype=jnp.float32), GRID_W)
    col = jnp.tile(jnp.arange(GRID_W, dtype=jnp.float32), rows)
    half = x.shape[-1] // 2
    inv = ROPE_BASE ** (-jnp.arange(0, half, 2, dtype=jnp.float32) / half)

    def rot(xa, pos):
        ang = pos[:, None] * inv[None, :]
        cos = jnp.cos(ang)[None, :, None, :]
        sin = jnp.sin(ang)[None, :, None, :]
        x1, x2 = jnp.split(xa.astype(jnp.float32), 2, axis=-1)
        return jnp.concatenate([x1 * cos - x2 * sin, x1 * sin + x2 * cos], axis=-1)

    out = jnp.concatenate([rot(x[..., :half], row), rot(x[..., half:], col)], axis=-1)
    return out.astype(x.dtype)


def split_q_blocks(x):
    b, t = x.shape[:2]
    return jnp.swapaxes(x.reshape((b, t // Q_BLOCK, Q_BLOCK) + x.shape[2:]), 0, 1)


def merge_q_blocks(y):
    nb, b = y.shape[:2]
    return jnp.swapaxes(y, 0, 1).reshape((b, nb * Q_BLOCK) + y.shape[3:])


def dense_attention(q, k, v, scale, sink=None):
    def one(qb):
        s = jnp.einsum('bqhd,bkhd->bhqk', qb, k).astype(jnp.float32) * scale
        if sink is None:
            p = jax.nn.softmax(s, axis=-1)
        else:
            sk = jnp.broadcast_to(sink.astype(jnp.float32)[None, :, None, None], s.shape[:-1] + (1,))
            p = jax.nn.softmax(jnp.concatenate([s, sk], axis=-1), axis=-1)[..., :-1]
        return jnp.einsum('bhqk,bkhv->bqhv', p.astype(v.dtype), v)
    return merge_q_blocks(lax.map(one, split_q_blocks(q)))


def diff_attention(q12, k12, v, lam, scale):
    def one(qb):
        s = jnp.einsum('bqchd,bkchd->bchqk', qb, k12).astype(jnp.float32) * scale
        p = jax.nn.softmax(s, axis=-1)
        pd = p[:, 0] - lam * p[:, 1]
        return jnp.einsum('bhqk,bkhv->bqhv', pd.astype(v.dtype), v)
    return merge_q_blocks(lax.map(one, split_q_blocks(q12)))


def banded_sink_attention(q, k, v, kc, vc, sink, scale):
    b, t, h, d = q.shape
    nb = t // WINDOW
    qb = q.reshape(b, nb, WINDOW, h, d)

    def neighbours(a):
        ap = jnp.pad(a, ((0, 0), (WINDOW, WINDOW), (0, 0), (0, 0)))
        ap = ap.reshape(b, nb + 2, WINDOW, h, a.shape[-1])
        return jnp.concatenate([ap[:, :-2], ap[:, 1:-1], ap[:, 2:]], axis=2)

    kn, vn = neighbours(k), neighbours(v)
    qi = jnp.arange(WINDOW)[:, None]
    kj = jnp.arange(3 * WINDOW)[None, :]
    key_pos = jnp.arange(nb)[:, None, None] * WINDOW - WINDOW + kj[None]
    valid = (jnp.abs(kj - WINDOW - qi)[None] <= WINDOW) & (key_pos >= 0) & (key_pos < t)
    s_loc = jnp.einsum('bnqhd,bnkhd->bnhqk', qb, kn).astype(jnp.float32) * scale
    s_loc = jnp.where(valid[None, :, None], s_loc, NEG)
    s_ctx = jnp.einsum('bnqhd,bphd->bnhqp', qb, kc).astype(jnp.float32) * scale
    s_sink = jnp.broadcast_to(sink.astype(jnp.float32)[None, None, :, None, None], s_loc.shape[:-1] + (1,))
    p = jax.nn.softmax(jnp.concatenate([s_loc, s_ctx, s_sink], axis=-1), axis=-1)
    n_loc = 3 * WINDOW
    n_ctx = kc.shape[1]
    out = (jnp.einsum('bnhqk,bnkhv->bnqhv', p[..., :n_loc].astype(v.dtype), vn)
           + jnp.einsum('bnhqp,bphv->bnqhv', p[..., n_loc:n_loc + n_ctx].astype(v.dtype), vc))
    return out.reshape(b, t, h, -1)


def mlstm_chunk_scan(q, k, v, ig, fg, c0, n0, m0):
    b, t, h, dk = q.shape
    nc = t // ML_CHUNK

    def chunks(a):
        return jnp.swapaxes(a.reshape((b, nc, ML_CHUNK) + a.shape[2:]), 0, 1)

    lf = jax.nn.log_sigmoid(fg)
    causal = jnp.tril(jnp.ones((ML_CHUNK, ML_CHUNK), dtype=bool))

    def step(carry, xs):
        c, n, m = carry
        qc, kc, vc, ic, lfc = xs
        bcum = jnp.cumsum(lfc, axis=1)
        dmat = bcum[:, :, None, :] - bcum[:, None, :, :] + ic[:, None, :, :]
        dmat = jnp.where(causal[None, :, :, None], dmat, NEG)
        inter = bcum + m[:, None, :]
        mt = jnp.maximum(inter, jnp.max(dmat, axis=2))
        w = jnp.exp(dmat - mt[:, :, None, :])
        s = jnp.einsum('bthd,bshd->btsh', qc, kc) * w
        a = jnp.exp(inter - mt)
        num = jnp.einsum('btsh,bshv->bthv', s, vc) + a[..., None] * jnp.einsum('bthd,bhdv->bthv', qc, c)
        den = jnp.sum(s, axis=2) + a * jnp.einsum('bthd,bhd->bth', qc, n)
        hc = num / jnp.maximum(jnp.abs(den), jnp.exp(-mt))[..., None]
        b_last = bcum[:, -1]
        wlog = b_last[:, None, :] - bcum + ic
        m_new = jnp.maximum(b_last + m, jnp.max(wlog, axis=1))
        wk = jnp.exp(wlog - m_new[:, None, :])
        decay = jnp.exp(b_last + m - m_new)
        c_new = decay[..., None, None] * c + jnp.einsum('bsh,bshd,bshv->bhdv', wk, kc, vc)
        n_new = decay[..., None] * n + jnp.einsum('bsh,bshd->bhd', wk, kc)
        return (c_new, n_new, m_new), hc

    (c, n, m), hs = lax.scan(step, (c0, n0, m0), (chunks(q), chunks(k), chunks(v), chunks(ig), chunks(lf)))
    hs = jnp.swapaxes(hs, 0, 1).reshape(b, t, h, -1)
    return hs, c, n, m


def mlstm_mixer(ml_q, ml_k, ml_v, ml_o, ml_g, i_bias, f_bias, norm_g, c0, n0, m0):
    b, t, _ = ml_q.shape
    f32 = jnp.float32
    q = heads(ml_q, ML_HEADS).astype(f32)
    k = heads(ml_k, ML_HEADS).astype(f32) * (ML_DH ** -0.5)
    v = heads(ml_v, ML_HEADS).astype(f32)
    g = ml_g.astype(f32).reshape(b, t, 2, 2, ML_HEADS)
    ig = g[:, :, 0] + i_bias.astype(f32)
    fg = g[:, :, 1] + f_bias.astype(f32)
    c0, n0, m0 = c0.astype(f32), n0.astype(f32), m0.astype(f32)
    flip = lambda a: a[:, ::-1]
    h_f, c_f, n_f, m_f = mlstm_chunk_scan(q, k, v, ig[:, :, 0], fg[:, :, 0], c0[:, 0], n0[:, 0], m0[:, 0])
    h_b, c_b, n_b, m_b = mlstm_chunk_scan(flip(q), flip(k), flip(v), flip(ig[:, :, 1]), flip(fg[:, :, 1]),
                                          c0[:, 1], n0[:, 1], m0[:, 1])
    hsum = rmsnorm(h_f + flip(h_b), norm_g.reshape(ML_HEADS, ML_DH))
    out = hsum.reshape(b, t, GROUP_W).astype(ml_o.dtype) * jax.nn.sigmoid(ml_o)
    return (out, jnp.stack([c_f, c_b], axis=1), jnp.stack([n_f, n_b], axis=1), jnp.stack([m_f, m_b], axis=1))


def mla_query(cq, qnorm_g, w_uq):
    q = heads(jnp.einsum('btr,re->bte', rmsnorm(cq, qnorm_g), w_uq), MLA_HEADS)
    return q[..., :MLA_D_NOPE], q[..., MLA_D_NOPE:]


def mla_expand(ckv_n, w_ukv):
    kv = heads(jnp.einsum('btr,re->bte', ckv_n, w_ukv), MLA_HEADS)
    return kv[..., :MLA_D_NOPE], kv[..., MLA_D_NOPE:]


def mla_keys(k_nope, k_rope):
    return jnp.concatenate([k_nope, jnp.broadcast_to(k_rope, k_nope.shape[:3] + (MLA_D_ROPE,))], axis=-1)


def diff_lambda(lq1, lk1, lq2, lk2, l):
    lam_init = 0.8 - 0.6 * math.exp(-0.3 * l)
    f32 = jnp.float32
    lam = (jnp.exp(jnp.sum(lq1.astype(f32) * lk1.astype(f32)))
           - jnp.exp(jnp.sum(lq2.astype(f32) * lk2.astype(f32))) + lam_init)
    return lam, lam_init


def diff_qk(a):
    b, t = a.shape[:2]
    return jnp.swapaxes(a.reshape(b, t, DIFF_HEADS, 2, DIFF_DH), 2, 3)


def diff_rope(a):
    b, t = a.shape[:2]
    return axial_rope(a.reshape(b, t, 2 * DIFF_HEADS, DIFF_DH)).reshape(b, t, 2, DIFF_HEADS, DIFF_DH)


def diff_output(o, norm_g, lam_init):
    b, t = o.shape[:2]
    return (rmsnorm(o, norm_g) * (1.0 - lam_init)).reshape(b, t, GROUP_W)


def peer(x, w_q, sub_keys, u_tab, v_tab):
    b, t, d = x.shape
    xt = x.reshape(b * t, d)
    q = jnp.einsum('td,de->te', xt, w_q).reshape(b * t, PEER_HEADS, 2, PEER_QDIM // 2)
    s1 = jnp.einsum('thd,kd->thk', q[:, :, 0], sub_keys[0]).astype(jnp.float32)
    s2 = jnp.einsum('thd,kd->thk', q[:, :, 1], sub_keys[1]).astype(jnp.float32)
    v1, i1 = lax.top_k(s1, PEER_TOPK)
    v2, i2 = lax.top_k(s2, PEER_TOPK)
    cand = (v1[..., :, None] + v2[..., None, :]).reshape(b * t, PEER_HEADS, PEER_TOPK * PEER_TOPK)
    cidx = (i1[..., :, None] * PEER_NKEYS + i2[..., None, :]).reshape(b * t, PEER_HEADS, PEER_TOPK * PEER_TOPK)
    sc, pos = lax.top_k(cand, PEER_TOPK)
    eidx = jnp.take_along_axis(cidx, pos, axis=-1)
    gates = jax.nn.softmax(sc, axis=-1)

    def one(args):
        xc, ec, gc = args
        u = u_tab[ec]
        act = jax.nn.gelu(jnp.einsum('cd,chkd->chk', xc, u).astype(jnp.float32))
        w = (gc * act).astype(xc.dtype)
        return jnp.einsum('chk,chkd->cd', w, v_tab[ec])

    nch = (b * t) // PEER_CHUNK
    out = lax.map(one, (xt.reshape(nch, PEER_CHUNK, d),
                        eidx.reshape(nch, PEER_CHUNK, PEER_HEADS, PEER_TOPK),
                        gates.reshape(nch, PEER_CHUNK, PEER_HEADS, PEER_TOPK)))
    return out.reshape(b, t, d)


def finish_layer(x, outs, g1, sh2, sc2, g2, p):
    mixed = jnp.concatenate(outs, axis=-1)
    x = x + g1 * jnp.einsum('bte,ed->btd', mixed, p['w_out'])
    h = adaln(x, p['norm2_g'], sh2, sc2)
    return x + g2 * peer(h, p['peer_w_q'], p['peer_sub_keys'], p['peer_u'], p['peer_v'])


def context_layer(x, c_ctx, p, l):
    b, t, _ = x.shape
    sh1, sc1, g1, sh2, sc2, g2 = modulation(c_ctx, p['w_mod'], p['b_mod'])
    ml_q, ml_k, ml_v, ml_o, ml_g, cq, ckv, kr, sq, sk, sv, dq, dk, dv = project(x, p['norm1_g'], sh1, sc1, p['w_in'])
    f32 = jnp.float32
    c0 = jnp.zeros((b, 2, ML_HEADS, ML_DH, ML_DH), f32)
    n0 = jnp.zeros((b, 2, ML_HEADS, ML_DH), f32)
    m0 = jnp.zeros((b, 2, ML_HEADS), f32)
    ml_out, c_st, n_st, m_st = mlstm_mixer(ml_q, ml_k, ml_v, ml_o, ml_g, p['mlstm_i_bias'], p['mlstm_f_bias'],
                                           p['mlstm_norm_g'], c0, n0, m0)
    q_nope, q_rope = mla_query(cq, p['mla_qnorm_g'], p['mla_w_uq'])
    ckv_n = rmsnorm(ckv, p['mla_kvnorm_g'])
    k_nope, v_mla = mla_expand(ckv_n, p['mla_w_ukv'])
    mla_out = dense_attention(jnp.concatenate([q_nope, q_rope], axis=-1), mla_keys(k_nope, kr[:, :, None, :]),
                              v_mla, MLA_SCALE).reshape(b, t, GROUP_W)
    sq_h = heads(sq, SWA_HEADS)
    sk_h = heads(sk, SWA_KV_HEADS)
    sv_h = heads(sv, SWA_KV_HEADS)
    swa_out = dense_attention(sq_h, repeat_kv(sk_h), repeat_kv(sv_h), SWA_DH ** -0.5,
                              sink=p['swa_sink']).reshape(b, t, GROUP_W)
    lam, lam_init = diff_lambda(p['diff_lq1'], p['diff_lk1'], p['diff_lq2'], p['diff_lk2'], l)
    dk_h = heads(dk, DIFF_HEADS)
    dv_h = heads(dv, DIFF_HEADS)
    df_out = diff_output(diff_attention(diff_qk(dq), diff_qk(dk), dv_h, lam, DIFF_DH ** -0.5),
                         p['diff_norm_g'], lam_init)
    x = finish_layer(x, [ml_out, mla_out, swa_out, df_out], g1, sh2, sc2, g2, p)
    return x, (ckv_n, kr, sk_h, sv_h, dk_h, dv_h, c_st, n_st, m_st)


def latent_layer(x, c, p, l, ckv_c, kr_c, swk_c, swv_c, dfk_c, dfv_c, c0, n0, m0):
    b, t, _ = x.shape
    sh1, sc1, g1, sh2, sc2, g2 = [m[:, None, :] for m in modulation(c, p['w_mod'], p['b_mod'])]
    ml_q, ml_k, ml_v, ml_o, ml_g, cq, ckv, kr, sq, sk, sv, dq, dk, dv = project(x, p['norm1_g'], sh1, sc1, p['w_in'])
    ml_out = mlstm_mixer(ml_q, ml_k, ml_v, ml_o, ml_g, p['mlstm_i_bias'], p['mlstm_f_bias'],
                         p['mlstm_norm_g'], c0, n0, m0)[0]
    q_nope, q_rope = mla_query(cq, p['mla_qnorm_g'], p['mla_w_uq'])
    q_mla = jnp.concatenate([q_nope, axial_rope(q_rope)], axis=-1)
    kn_l, v_l = mla_expand(rmsnorm(ckv, p['mla_kvnorm_g']), p['mla_w_ukv'])
    kn_c, v_c = mla_expand(ckv_c, p['mla_w_ukv'])
    k_mla = jnp.concatenate([mla_keys(kn_l, axial_rope(kr[:, :, None, :])),
                             mla_keys(kn_c, kr_c[:, :, None, :])], axis=1)
    v_mla = jnp.concatenate([v_l, v_c], axis=1)
    mla_out = dense_attention(q_mla, k_mla, v_mla, MLA_SCALE).reshape(b, t, GROUP_W)
    sq_h = axial_rope(heads(sq, SWA_HEADS))
    sk_h = axial_rope(heads(sk, SWA_KV_HEADS))
    sv_h = heads(sv, SWA_KV_HEADS)
    swa_out = banded_sink_attention(sq_h, repeat_kv(sk_h), repeat_kv(sv_h), repeat_kv(swk_c), repeat_kv(swv_c),
                                    p['swa_sink'], SWA_DH ** -0.5).reshape(b, t, GROUP_W)
    lam, lam_init = diff_lambda(p['diff_lq1'], p['diff_lk1'], p['diff_lq2'], p['diff_lk2'], l)
    k12 = jnp.concatenate([diff_rope(diff_qk(dk)), diff_qk(dfk_c.reshape(b, dfk_c.shape[1], GROUP_W))], axis=1)
    v_df = jnp.concatenate([heads(dv, DIFF_HEADS), dfv_c], axis=1)
    df_out = diff_output(diff_attention(diff_rope(diff_qk(dq)), k12, v_df, lam, DIFF_DH ** -0.5),
                         p['diff_norm_g'], lam_init)
    return finish_layer(x, [ml_out, mla_out, swa_out, df_out], g1, sh2, sc2, g2, p)


def setup_inputs(seed: int = 0) -> dict:
    key = jax.random.key(seed)
    ks = iter(jax.random.split(key, 40))
    f32 = jnp.float32

    def nrm(shape, s):
        return jax.random.normal(next(ks), shape, f32) * s

    def gain(shape):
        return 1.0 + nrm(shape, 0.02)

    P = PAST_LEN
    return {
        'x_prompt': nrm((BATCH, SEQ, D_MODEL), 1.0),
        'x_sample': nrm((DEC_BATCH, DEC_SEQ, D_MODEL), 1.0),
        'c': nrm((DEC_BATCH, D_MODEL), 1.0),
        'cache_mla_ckv': nrm((DEC_BATCH, DEPTH, P, MLA_KV_RANK), 1.0),
        'cache_mla_krope': nrm((DEC_BATCH, DEPTH, P, MLA_D_ROPE), 1.0),
        'cache_swa_k': nrm((DEC_BATCH, DEPTH, P, SWA_KV_HEADS, SWA_DH), 1.0),
        'cache_swa_v': nrm((DEC_BATCH, DEPTH, P, SWA_KV_HEADS, SWA_DH), 1.0),
        'cache_diff_k': nrm((DEC_BATCH, DEPTH, P, DIFF_HEADS, 2 * DIFF_DH), 1.0),
        'cache_diff_v': nrm((DEC_BATCH, DEPTH, P, DIFF_HEADS, 2 * DIFF_DH), 1.0),
        'state_mlstm_C': nrm((DEC_BATCH, DEPTH, 2, ML_HEADS, ML_DH, ML_DH), 0.5),
        'state_mlstm_n': nrm((DEC_BATCH, DEPTH, 2, ML_HEADS, ML_DH), 0.5),
        'state_mlstm_m': nrm((DEC_BATCH, DEPTH, 2, ML_HEADS), 1.0),
        'c_ctx': nrm((D_MODEL,), 1.0),
        'w_mod': nrm((DEPTH, D_MODEL, 6 * D_MODEL), 0.5 * D_MODEL ** -0.5),
        'b_mod': nrm((DEPTH, 6 * D_MODEL), 0.02),
        'norm1_g': gain((DEPTH, D_MODEL)),
        'w_in': nrm((DEPTH, D_MODEL, IN_W), D_MODEL ** -0.5),
        'mlstm_i_bias': nrm((DEPTH, 2, ML_HEADS), 0.1),
        'mlstm_f_bias': 3.0 + 3.0 * jax.random.uniform(next(ks), (DEPTH, 2, ML_HEADS), f32),
        'mlstm_norm_g': gain((DEPTH, GROUP_W)),
        'mla_qnorm_g': gain((DEPTH, MLA_Q_RANK)),
        'mla_w_uq': nrm((DEPTH, MLA_Q_RANK, MLA_HEADS * (MLA_D_NOPE + MLA_D_ROPE)), MLA_Q_RANK ** -0.5),
        'mla_kvnorm_g': gain((DEPTH, MLA_KV_RANK)),
        'mla_w_ukv': nrm((DEPTH, MLA_KV_RANK, MLA_HEADS * (MLA_D_NOPE + MLA_D_V)), MLA_KV_RANK ** -0.5),
        'swa_sink': nrm((DEPTH, SWA_HEADS), 0.5),
        'diff_lq1': nrm((DEPTH, DIFF_DH), 0.1),
        'diff_lk1': nrm((DEPTH, DIFF_DH), 0.1),
        'diff_lq2': nrm((DEPTH, DIFF_DH), 0.1),
        'diff_lk2': nrm((DEPTH, DIFF_DH), 0.1),
        'diff_norm_g': gain((DEPTH, 2 * DIFF_DH)),
        'w_out': nrm((DEPTH, 4 * GROUP_W, D_MODEL), (4 * GROUP_W) ** -0.5),
        'norm2_g': gain((DEPTH, D_MODEL)),
        'peer_w_q': nrm((DEPTH, D_MODEL, PEER_HEADS * PEER_QDIM), D_MODEL ** -0.5),
        'peer_sub_keys': nrm((DEPTH, 2, PEER_NKEYS, PEER_QDIM // 2), (PEER_QDIM // 2) ** -0.5),
        'peer_u': nrm((DEPTH, PEER_N, D_MODEL), D_MODEL ** -0.5),
        'peer_v': nrm((DEPTH, PEER_N, D_MODEL), 0.5),
        'final_norm_g': gain((D_MODEL,)),
    }


def reference(x_prompt, x_sample, c, cache_mla_ckv, cache_mla_krope, cache_swa_k, cache_swa_v,
              cache_diff_k, cache_diff_v, state_mlstm_C, state_mlstm_n, state_mlstm_m,
              c_ctx, w_mod, b_mod, norm1_g, w_in, mlstm_i_bias, mlstm_f_bias, mlstm_norm_g,
              mla_qnorm_g, mla_w_uq, mla_kvnorm_g, mla_w_ukv, swa_sink,
              diff_lq1, diff_lk1, diff_lq2, diff_lk2, diff_norm_g, w_out, norm2_g,
              peer_w_q, peer_sub_keys, peer_u, peer_v, final_norm_g):
    xp = x_prompt
    xs = x_sample
    ctx_states = [[] for _ in range(9)]
    for l in range(DEPTH):
        p = {
            'w_mod': w_mod[l], 'b_mod': b_mod[l], 'norm1_g': norm1_g[l], 'w_in': w_in[l],
            'mlstm_i_bias': mlstm_i_bias[l], 'mlstm_f_bias': mlstm_f_bias[l], 'mlstm_norm_g': mlstm_norm_g[l],
            'mla_qnorm_g': mla_qnorm_g[l], 'mla_w_uq': mla_w_uq[l],
            'mla_kvnorm_g': mla_kvnorm_g[l], 'mla_w_ukv': mla_w_ukv[l],
            'swa_sink': swa_sink[l],
            'diff_lq1': diff_lq1[l], 'diff_lk1': diff_lk1[l], 'diff_lq2': diff_lq2[l], 'diff_lk2': diff_lk2[l],
            'diff_norm_g': diff_norm_g[l], 'w_out': w_out[l], 'norm2_g': norm2_g[l],
            'peer_w_q': peer_w_q[l], 'peer_sub_keys': peer_sub_keys[l], 'peer_u': peer_u[l], 'peer_v': peer_v[l],
        }
        xp, st = context_layer(xp, c_ctx, p, l)
        for i in range(9):
            ctx_states[i].append(st[i])
        xs = latent_layer(xs, c, p, l, cache_mla_ckv[:, l], cache_mla_krope[:, l], cache_swa_k[:, l],
                          cache_swa_v[:, l], cache_diff_k[:, l], cache_diff_v[:, l],
                          state_mlstm_C[:, l], state_mlstm_n[:, l], state_mlstm_m[:, l])
    y_prompt = rmsnorm(xp, final_norm_g)
    y_sample = rmsnorm(xs, final_norm_g)
    new_mla_ckv = jnp.stack(ctx_states[0], axis=1)
    new_mla_krope = jnp.stack(ctx_states[1], axis=1)
    new_swa_k = jnp.stack(ctx_states[2], axis=1)
    new_swa_v = jnp.stack(ctx_states[3], axis=1)
    new_diff_k = jnp.stack(ctx_states[4], axis=1)
    new_diff_v = jnp.stack(ctx_states[5], axis=1)
    new_mlstm_C = jnp.stack(ctx_states[6], axis=1)
    new_mlstm_n = jnp.stack(ctx_states[7], axis=1)
    new_mlstm_m = jnp.stack(ctx_states[8], axis=1)
    return (y_prompt, y_sample, new_mla_ckv, new_mla_krope, new_swa_k, new_swa_v,
            new_diff_k, new_diff_v, new_mlstm_C, new_mlstm_n, new_mlstm_m)
```

```python
import functools
import math

import jax
import jax.numpy as jnp
from jax import lax
from jax.experimental import pallas as pl
from jax.experimental.pallas import tpu as pltpu

F32 = jnp.float32
BF16 = jnp.bfloat16

D_MODEL = 2048
BATCH = 32
SEQ = 256
DEPTH = 4
DEC_BATCH = 2
DEC_SEQ = 1024
PAST_LEN = 256
GRID_W = 64
GROUP_W = D_MODEL // 4
ML_HEADS = 4
ML_DH = GROUP_W // ML_HEADS
ML_CHUNK = 64
MLA_HEADS = 4
MLA_D_NOPE = GROUP_W // MLA_HEADS
MLA_D_ROPE = 64
MLA_Q_RANK = D_MODEL // 8
MLA_KV_RANK = D_MODEL // 8
MLA_SCALE = (MLA_D_NOPE + MLA_D_ROPE) ** -0.5
SWA_HEADS = 4
SWA_KV_HEADS = 2
SWA_DH = GROUP_W // SWA_HEADS
WINDOW = 128
DIFF_HEADS = 4
DIFF_DH = GROUP_W // (2 * DIFF_HEADS)
PEER_HEADS = 8
PEER_QDIM = 256
PEER_NKEYS = 128
PEER_N = PEER_NKEYS * PEER_NKEYS
PEER_TOPK = 16
ROPE_BASE = 10000.0
EPS = 1e-6
NEG = -1e30

N_CTX = BATCH * SEQ
N_LAT = DEC_BATCH * DEC_SEQ
NT = N_CTX + N_LAT
N_SETS = 1 + DEC_BATCH

C_MLQ, C_MLK, C_MLV, C_MLO = 0, 512, 1024, 1536
C_CQ, C_CKV = 2048, 2304
C_SQ, C_SK, C_SV = 2560, 3072, 3328
C_DQ, C_DK, C_DV = 3584, 4096, 4608
C_TAIL = 5120
TAIL_G = 64
PROJ_W = 5376
PROJ_TN = 896

VMEM_LIMIT = 56 * 1024 * 1024


def _cparams(*sem):
    return pltpu.CompilerParams(dimension_semantics=sem, vmem_limit_bytes=VMEM_LIMIT)


def _mod_set(row_start):
    return jnp.where(row_start >= N_CTX, (row_start - N_CTX) // DEC_SEQ + 1, 0)


def _rms(x, g):
    return x * lax.rsqrt(jnp.mean(x * x, axis=-1, keepdims=True) + EPS) * g


def _dot(a, b):
    return jnp.dot(a, b, preferred_element_type=F32)


def _dot_nt(a, b):
    return lax.dot_general(a, b, (((1,), (1,)), ((), ())), preferred_element_type=F32)


def _dot_tn(a, b):
    return lax.dot_general(a, b, (((0,), (0,)), ((), ())), preferred_element_type=F32)


def _dot_hi(a, b):
    return jnp.dot(a, b, preferred_element_type=F32, precision=lax.Precision.HIGHEST)


MOD_TK = 512
MOD_TN = 2048


def _mod_kernel(c_ref, w_ref, b_ref, o_ref, acc_ref):
    k = pl.program_id(2)

    @pl.when(k == 0)
    def _():
        acc_ref[...] = jnp.zeros_like(acc_ref)

    w = w_ref[...]
    for r in range(N_SETS):
        cv = c_ref[:, r:r + 1]
        sv = cv * jax.nn.sigmoid(cv)
        acc_ref[r] += (sv * w).reshape(MOD_TK // 8, 8, MOD_TN).sum(axis=0)

    @pl.when(k == pl.num_programs(2) - 1)
    def _():
        o_ref[...] = jnp.zeros_like(o_ref)
        for r in range(N_SETS):
            o_ref[r:r + 1, :] = acc_ref[r].sum(axis=0, keepdims=True) + b_ref[...]


def _modulation(cvec_t, w_mod, b_mod):
    return pl.pallas_call(
        _mod_kernel,
        grid=(DEPTH, 6 * D_MODEL // MOD_TN, D_MODEL // MOD_TK),
        in_specs=[
            pl.BlockSpec((MOD_TK, 8), lambda l, n, k: (k, 0)),
            pl.BlockSpec((None, MOD_TK, MOD_TN), lambda l, n, k: (l, k, n)),
            pl.BlockSpec((None, 1, MOD_TN), lambda l, n, k: (l, 0, n)),
        ],
        out_specs=pl.BlockSpec((None, 8, MOD_TN), lambda l, n, k: (l, 0, n)),
        out_shape=jax.ShapeDtypeStruct((DEPTH, 8, 6 * D_MODEL), F32),
        scratch_shapes=[pltpu.VMEM((N_SETS, 8, MOD_TN), F32)],
        compiler_params=_cparams("parallel", "parallel", "arbitrary"),
        name="modulation",
    )(cvec_t, w_mod, b_mod.reshape(DEPTH, 1, 6 * D_MODEL))


PROJ_TM = 512


def _proj_kernel(x_ref, m_ref, g_ref, w_ref, o_ref, h_ref):
    @pl.when(pl.program_id(1) == 0)
    def _():
        h = _rms(x_ref[...], g_ref[...]) * (1.0 + m_ref[1:2, :]) + m_ref[0:1, :]
        h_ref[...] = h.astype(BF16)

    o_ref[...] = _dot(h_ref[...], w_ref[...])


def _project(x, mod4, norm_g, w_in_p, l):
    return pl.pallas_call(
        _proj_kernel,
        grid=(NT // PROJ_TM, PROJ_W // PROJ_TN),
        in_specs=[
            pl.BlockSpec((PROJ_TM, D_MODEL), lambda i, j: (i, 0)),
            pl.BlockSpec((None, None, 6, D_MODEL), lambda i, j: (l, _mod_set(i * PROJ_TM), 0, 0)),
            pl.BlockSpec((None, 1, D_MODEL), lambda i, j: (l, 0, 0)),
            pl.BlockSpec((None, D_MODEL, PROJ_TN), lambda i, j: (l, 0, j)),
        ],
        out_specs=pl.BlockSpec((PROJ_TM, PROJ_TN), lambda i, j: (i, j)),
        out_shape=jax.ShapeDtypeStruct((NT, PROJ_W), F32),
        scratch_shapes=[pltpu.VMEM((PROJ_TM, D_MODEL), BF16)],
        compiler_params=_cparams("parallel", "arbitrary"),
        name="adaln_in_proj",
    )(x, mod4, norm_g, w_in_p)


def _rope(x, cos, sin, quarter):
    width = x.shape[-1]
    lane = lax.broadcasted_iota(jnp.int32, x.shape, 1)
    first = (lane % (2 * quarter)) < quarter
    partner = jnp.where(first, pltpu.roll(x, width - quarter, 1), pltpu.roll(x, quarter, 1))
    return x * cos + partner * sin


def _rope_tables(n_tok, rot_dim):
    half = rot_dim // 2
    pos = jnp.arange(n_tok)
    row = (pos // GRID_W).astype(F32)
    col = (pos % GRID_W).astype(F32)
    inv = ROPE_BASE ** (-jnp.arange(0, half, 2, dtype=F32) / half)
    a_row = row[:, None] * inv[None, :]
    a_col = col[:, None] * inv[None, :]
    ang = jnp.concatenate([a_row, a_row, a_col, a_col], axis=-1)
    sign = jnp.tile(jnp.concatenate([-jnp.ones(half // 2, F32), jnp.ones(half // 2, F32)]), 2)
    return jnp.cos(ang), jnp.sin(ang) * sign[None, :]


def _log_sigmoid(x):
    return jnp.minimum(x, 0.0) - jnp.log(1.0 + jnp.exp(-jnp.abs(x)))


def _mlstm_kernel(*refs, n_tok, has_state):
    if has_state:
        (q_ref, k_ref, v_ref, o_ref, tail_ref, gt_ref, bcol_ref, brow_ref, ng_ref, c0_ref, n0_ref, m0_ref,
         out_ref, hf_ref, hb_ref, cst_ref, nst_ref, mst_ref) = refs
    else:
        (q_ref, k_ref, v_ref, o_ref, tail_ref, gt_ref, bcol_ref, brow_ref, ng_ref,
         out_ref, cs_ref, ns_ref, ms_ref, hf_ref, hb_ref, cst_ref, nst_ref, mst_ref) = refs
    n_chunks = n_tok // ML_CHUNK
    scale = ML_DH ** -0.5

    if has_state:
        cst_ref[...] = c0_ref[...]
        nst_ref[...] = n0_ref[...]
        mst_ref[...] = m0_ref[...]
    else:
        cst_ref[...] = jnp.zeros_like(cst_ref)
        nst_ref[...] = jnp.zeros_like(nst_ref)
        mst_ref[...] = jnp.zeros_like(mst_ref)

    row = lax.broadcasted_iota(jnp.int32, (ML_CHUNK, ML_CHUNK), 0)
    col = lax.broadcasted_iota(jnp.int32, (ML_CHUNK, ML_CHUNK), 1)
    lower = (col <= row)
    upper = (col >= row)
    lower_f = lower.astype(F32)
    upper_f = upper.astype(F32)

    def chunk_step(c, carry):
        for d in range(2):
            cc = c if d == 0 else n_chunks - 1 - c
            t0 = pl.multiple_of(cc * ML_CHUNK, ML_CHUNK)
            rows = pl.ds(t0, ML_CHUNK)
            g_col = tail_ref[rows, :] + brow_ref[...]
            g_row = gt_ref[cc] + bcol_ref[...]
            lf_col = _log_sigmoid(g_col)
            lf_row = _log_sigmoid(g_row)
            if d == 0:
                bcum_col = _dot_hi(lower_f, lf_col)
                bcum_row = _dot_hi(lf_row, upper_f)
                mask = lower
            else:
                bcum_col = _dot_hi(upper_f, lf_col)
                bcum_row = _dot_hi(lf_row, lower_f)
                mask = upper
            for h in range(ML_HEADS):
                r = d * ML_HEADS + h
                lanes = slice(h * ML_DH, (h + 1) * ML_DH)
                ci = TAIL_G + r
                cf = TAIL_G + 2 * ML_HEADS + r
                ig_c = g_col[:, ci:ci + 1]
                b_c = bcum_col[:, cf:cf + 1]
                ig_r = g_row[r:r + 1, :]
                b_r = bcum_row[2 * ML_HEADS + r:2 * ML_HEADS + r + 1, :]
                tot = jnp.sum(lf_row[2 * ML_HEADS + r:2 * ML_HEADS + r + 1, :], axis=1, keepdims=True)
                m_prev = mst_ref[r:r + 1, 0:1]
                n_prev = nst_ref[r:r + 1, :]
                c_prev = cst_ref[r]
                q = q_ref[rows, lanes]
                k = k_ref[rows, lanes]
                v = v_ref[rows, lanes].astype(BF16)
                qb = q.astype(BF16)

                dmat = jnp.where(mask, b_c - b_r + ig_r, NEG)
                inter = b_c + m_prev
                mt = jnp.maximum(inter, jnp.max(dmat, axis=1, keepdims=True))
                w = jnp.exp(dmat - mt)
                s = _dot_nt(qb, k.astype(BF16)) * scale * w
                a = jnp.exp(inter - mt)
                num = _dot(s.astype(BF16), v) + a * _dot(qb, c_prev.astype(BF16))
                den = jnp.sum(s, axis=1, keepdims=True) + a * jnp.sum(q * n_prev, axis=1, keepdims=True)
                hc = num / jnp.maximum(jnp.abs(den), jnp.exp(-mt))
                if d == 0:
                    hf_ref[rows, lanes] = hc
                else:
                    hb_ref[rows, lanes] = hc

                wlog_c = tot - b_c + ig_c
                wlog_r = tot - b_r + ig_r
                m_new = jnp.maximum(tot + m_prev, jnp.max(wlog_r, axis=1, keepdims=True))
                decay = jnp.exp(tot + m_prev - m_new)
                kw = k * (scale * jnp.exp(wlog_c - m_new))
                cst_ref[r] = decay * c_prev + _dot_tn(kw.astype(BF16), v)
                nst_ref[r:r + 1, :] = decay * n_prev + jnp.sum(kw, axis=0, keepdims=True)
                mst_ref[r:r + 1, :] = jnp.broadcast_to(m_new, (1, ML_DH))
        return carry

    lax.fori_loop(0, n_chunks, chunk_step, 0)

    for h in range(ML_HEADS):
        lanes = slice(h * ML_DH, (h + 1) * ML_DH)
        hs = hf_ref[:, lanes] + hb_ref[:, lanes]
        out_ref[:, lanes] = _rms(hs, ng_ref[:, lanes]) * jax.nn.sigmoid(o_ref[:, lanes])

    if not has_state:
        cs_ref[...] = cst_ref[...]
        ns_ref[...] = nst_ref[...]
        ms_ref[...] = mst_ref[...]


def _mlstm(proj, gt, bcol, brow, norm_g, l, *, latent, states=None):
    n_tok = DEC_SEQ if latent else SEQ
    n_b = DEC_BATCH if latent else BATCH
    blk0 = N_CTX // n_tok if latent else 0
    n_chunks = n_tok // ML_CHUNK

    def col_spec(c0):
        return pl.BlockSpec((n_tok, 512), lambda b: (blk0 + b, c0 // 512))

    in_specs = [
        col_spec(C_MLQ), col_spec(C_MLK), col_spec(C_MLV), col_spec(C_MLO),
        pl.BlockSpec((n_tok, 128), lambda b: (blk0 + b, C_TAIL // 128)),
        pl.BlockSpec((n_chunks, 16, ML_CHUNK), lambda b: (blk0 + b, 0, 0)),
        pl.BlockSpec((16, 1), lambda b: (0, 0)),
        pl.BlockSpec((1, 128), lambda b: (0, 0)),
        pl.BlockSpec((None, 1, GROUP_W), lambda b: (l, 0, 0)),
    ]
    args = [proj, proj, proj, proj, proj, gt, bcol, brow, norm_g]
    n_chain = 2 * ML_HEADS
    scratch = [pltpu.VMEM((n_tok, GROUP_W), F32), pltpu.VMEM((n_tok, GROUP_W), F32),
               pltpu.VMEM((n_chain, ML_DH, ML_DH), F32), pltpu.VMEM((n_chain, ML_DH), F32),
               pltpu.VMEM((n_chain, ML_DH), F32)]
    if latent:
        c0, n0, m0 = states
        in_specs += [
            pl.BlockSpec((None, None, n_chain, ML_DH, ML_DH), lambda b: (b, l, 0, 0, 0)),
            pl.BlockSpec((None, None, n_chain, ML_DH), lambda b: (b, l, 0, 0)),
            pl.BlockSpec((None, None, n_chain, ML_DH), lambda b: (b, l, 0, 0)),
        ]
        args += [c0, n0, m0]
        out_shape = jax.ShapeDtypeStruct((N_LAT, GROUP_W), F32)
        out_specs = pl.BlockSpec((n_tok, GROUP_W), lambda b: (b, 0))
    else:
        out_shape = (jax.ShapeDtypeStruct((N_CTX, GROUP_W), F32),
                     jax.ShapeDtypeStruct((BATCH, n_chain, ML_DH, ML_DH), F32),
                     jax.ShapeDtypeStruct((BATCH, n_chain, ML_DH), F32),
                     jax.ShapeDtypeStruct((BATCH, n_chain, ML_DH), F32))
        out_specs = (pl.BlockSpec((n_tok, GROUP_W), lambda b: (b, 0)),
                     pl.BlockSpec((None, n_chain, ML_DH, ML_DH), lambda b: (b, 0, 0, 0)),
                     pl.BlockSpec((None, n_chain, ML_DH), lambda b: (b, 0, 0)),
                     pl.BlockSpec((None, n_chain, ML_DH), lambda b: (b, 0, 0)))
    return pl.pallas_call(
        functools.partial(_mlstm_kernel, n_tok=n_tok, has_state=latent),
        grid=(n_b,),
        in_specs=in_specs,
        out_specs=out_specs,
        out_shape=out_shape,
        scratch_shapes=scratch,
        compiler_params=_cparams("parallel"),
        name="mlstm_latent" if latent else "mlstm_context",
    )(*args)


def _softmax_parts(scores, sink=None):
    m = jnp.max(scores[0], axis=1, keepdims=True)
    for s in scores[1:]:
        m = jnp.maximum(m, jnp.max(s, axis=1, keepdims=True))
    if sink is not None:
        m = jnp.maximum(m, sink)
    es = [jnp.exp(s - m) for s in scores]
    den = jnp.sum(es[0], axis=1, keepdims=True)
    for e in es[1:]:
        den = den + jnp.sum(e, axis=1, keepdims=True)
    if sink is not None:
        den = den + jnp.exp(sink - m)
    return es, den


ATT_TQ = 256
LAT_QB = DEC_SEQ // ATT_TQ
N_KEYS_LAT = DEC_SEQ + PAST_LEN


def _mla_q(cq_ref, qg_ref, wuq_ref):
    return _dot(_rms(cq_ref[...], qg_ref[...]).astype(BF16), wuq_ref[...])


def _mla_heads(q, kv, kr, n_keys):
    outs = []
    for h in range(MLA_HEADS):
        qn = q[:, h * MLA_D_NOPE:(h + 1) * MLA_D_NOPE].astype(BF16)
        r0 = MLA_HEADS * MLA_D_NOPE + h * MLA_D_ROPE
        qr = q[:, r0:r0 + MLA_D_ROPE].astype(BF16)
        kn = kv[:, h * 256:h * 256 + MLA_D_NOPE]
        v = kv[:, h * 256 + MLA_D_NOPE:(h + 1) * 256]
        s = (_dot_nt(qn, kn) + _dot_nt(qr, kr)) * MLA_SCALE
        (e,), den = _softmax_parts([s])
        outs.append(_dot(e.astype(BF16), v) / den)
    return outs


def _mla_ctx_kernel(cq_ref, ckv_ref, tail_ref, qg_ref, kg_ref, wuq_ref, wukv_ref, out_ref, ckvn_ref):
    q = _mla_q(cq_ref, qg_ref, wuq_ref)
    ckvn = _rms(ckv_ref[...], kg_ref[...])
    ckvn_ref[...] = ckvn
    kv = _dot(ckvn.astype(BF16), wukv_ref[...]).astype(BF16)
    kr = tail_ref[:, 0:MLA_D_ROPE].astype(BF16)
    outs = _mla_heads(q, kv, kr, SEQ)
    for h in range(MLA_HEADS):
        out_ref[:, h * MLA_D_V:(h + 1) * MLA_D_V] = outs[h]


MLA_D_V = GROUP_W // MLA_HEADS


def _mla_lat_kernel(cq_ref, ckv_ref, tail_ref, qg_ref, kg_ref, wuq_ref, wukv_ref, ckvc_ref, krc_ref,
                    cosq_ref, sinq_ref, cosk_ref, sink_ref, out_ref, kv_ref, kr_ref):
    @pl.when(pl.program_id(1) == 0)
    def _():
        ckvn = _rms(ckv_ref[...], kg_ref[...])
        kv_ref[0:DEC_SEQ, :] = _dot(ckvn.astype(BF16), wukv_ref[...]).astype(BF16)
        kv_ref[DEC_SEQ:N_KEYS_LAT, :] = _dot(ckvc_ref[...].astype(BF16), wukv_ref[...]).astype(BF16)
        kr = _rope(tail_ref[...], cosk_ref[...], sink_ref[...], MLA_D_ROPE // 4)
        kr_ref[0:DEC_SEQ, :] = kr[:, 0:MLA_D_ROPE].astype(BF16)
        kr_ref[DEC_SEQ:N_KEYS_LAT, :] = krc_ref[...].astype(BF16)

    q = _mla_q(cq_ref, qg_ref, wuq_ref)
    n0 = MLA_HEADS * MLA_D_NOPE
    q_rope = _rope(q[:, n0:], cosq_ref[...], sinq_ref[...], MLA_D_ROPE // 4)
    q = jnp.concatenate([q[:, :n0], q_rope], axis=1)
    outs = _mla_heads(q, kv_ref[...], kr_ref[...], N_KEYS_LAT)
    for h in range(MLA_HEADS):
        out_ref[:, h * MLA_D_V:(h + 1) * MLA_D_V] = outs[h]


def _w_specs2(shape_a, shape_b, l):
    return [pl.BlockSpec((None,) + shape_a, lambda *_: (l,) + (0,) * len(shape_a)),
            pl.BlockSpec((None,) + shape_b, lambda *_: (l,) + (0,) * len(shape_b))]


def _mla_ctx(proj, qg, kg, wuq, wukv, l):
    return pl.pallas_call(
        _mla_ctx_kernel,
        grid=(BATCH,),
        in_specs=[
            pl.BlockSpec((SEQ, 256), lambda b: (b, C_CQ // 256)),
            pl.BlockSpec((SEQ, 256), lambda b: (b, C_CKV // 256)),
            pl.BlockSpec((SEQ, 128), lambda b: (b, C_TAIL // 128)),
            pl.BlockSpec((None, 1, MLA_Q_RANK), lambda b: (l, 0, 0)),
            pl.BlockSpec((None, 1, MLA_KV_RANK), lambda b: (l, 0, 0)),
            pl.BlockSpec((None, MLA_Q_RANK, 768), lambda b: (l, 0, 0)),
            pl.BlockSpec((None, MLA_KV_RANK, 1024), lambda b: (l, 0, 0)),
        ],
        out_specs=(pl.BlockSpec((SEQ, GROUP_W), lambda b: (b, 0)),
                   pl.BlockSpec((SEQ, MLA_KV_RANK), lambda b: (b, 0))),
        out_shape=(jax.ShapeDtypeStruct((N_CTX, GROUP_W), F32),
                   jax.ShapeDtypeStruct((N_CTX, MLA_KV_RANK), F32)),
        compiler_params=_cparams("parallel"),
        name="mla_context",
    )(proj, proj, proj, qg, kg, wuq, wukv)


def _mla_lat(proj, qg, kg, wuq, wukv, cache_ckv, cache_kr, cos64, sin64, l):
    qb0 = N_CTX // ATT_TQ
    bb0 = N_CTX // DEC_SEQ
    cosq = jnp.tile(cos64, (1, MLA_HEADS))
    sinq = jnp.tile(sin64, (1, MLA_HEADS))
    cosk = jnp.tile(cos64, (1, 2))
    sink = jnp.tile(sin64, (1, 2))
    return pl.pallas_call(
        _mla_lat_kernel,
        grid=(DEC_BATCH, LAT_QB),
        in_specs=[
            pl.BlockSpec((ATT_TQ, 256), lambda b, i: (qb0 + b * LAT_QB + i, C_CQ // 256)),
            pl.BlockSpec((DEC_SEQ, 256), lambda b, i: (bb0 + b, C_CKV // 256)),
            pl.BlockSpec((DEC_SEQ, 128), lambda b, i: (bb0 + b, C_TAIL // 128)),
            pl.BlockSpec((None, 1, MLA_Q_RANK), lambda b, i: (l, 0, 0)),
            pl.BlockSpec((None, 1, MLA_KV_RANK), lambda b, i: (l, 0, 0)),
            pl.BlockSpec((None, MLA_Q_RANK, 768), lambda b, i: (l, 0, 0)),
            pl.BlockSpec((None, MLA_KV_RANK, 1024), lambda b, i: (l, 0, 0)),
            pl.BlockSpec((None, None, PAST_LEN, MLA_KV_RANK), lambda b, i: (b, l, 0, 0)),
            pl.BlockSpec((None, None, PAST_LEN, MLA_D_ROPE), lambda b, i: (b, l, 0, 0)),
            pl.BlockSpec((ATT_TQ, 256), lambda b, i: (i, 0)),
            pl.BlockSpec((ATT_TQ, 256), lambda b, i: (i, 0)),
            pl.BlockSpec((DEC_SEQ, 128), lambda b, i: (0, 0)),
            pl.BlockSpec((DEC_SEQ, 128), lambda b, i: (0, 0)),
        ],
        out_specs=pl.BlockSpec((ATT_TQ, GROUP_W), lambda b, i: (b * LAT_QB + i, 0)),
        out_shape=jax.ShapeDtypeStruct((N_LAT, GROUP_W), F32),
        scratch_shapes=[pltpu.VMEM((N_KEYS_LAT, 1024), BF16), pltpu.VMEM((N_KEYS_LAT, MLA_D_ROPE), BF16)],
        compiler_params=_cparams("parallel", "arbitrary"),
        name="mla_latent",
    )(proj, proj, proj, qg, kg, wuq, wukv, cache_ckv, cache_kr, cosq, sinq, cosk, sink)


SWA_SCALE = SWA_DH ** -0.5
SWA_REP = SWA_HEADS // SWA_KV_HEADS
SWA_KWIN = ATT_TQ + 2 * WINDOW


def _swa_ctx_kernel(sink_ref, q_ref, k_ref, v_ref, out_ref):
    kb = k_ref[...].astype(BF16)
    vb = v_ref[...].astype(BF16)
    for h in range(SWA_HEADS):
        g = h // SWA_REP
        q = q_ref[:, h * SWA_DH:(h + 1) * SWA_DH].astype(BF16)
        s = _dot_nt(q, kb[:, g * SWA_DH:(g + 1) * SWA_DH]) * SWA_SCALE
        (e,), den = _softmax_parts([s], sink=sink_ref[0, h])
        out_ref[:, h * SWA_DH:(h + 1) * SWA_DH] = _dot(e.astype(BF16), vb[:, g * SWA_DH:(g + 1) * SWA_DH]) / den


def _swa_lat_kernel(sink_ref, q_ref, k_ref, v_ref, kc_ref, vc_ref, cosq_ref, sinq_ref, cosk_ref, sink_t_ref,
                    out_ref, kr_ref):
    i = pl.program_id(1)

    @pl.when(i == 0)
    def _():
        kr_ref[...] = _rope(k_ref[...], cosk_ref[...], sink_t_ref[...], SWA_DH // 4).astype(BF16)

    q_all = _rope(q_ref[...], cosq_ref[...], sinq_ref[...], SWA_DH // 4)
    k0 = pl.multiple_of(jnp.clip(i * ATT_TQ - WINDOW, 0, DEC_SEQ - SWA_KWIN), WINDOW)
    kwin = kr_ref[pl.ds(k0, SWA_KWIN), :]
    vwin = v_ref[pl.ds(k0, SWA_KWIN), :].astype(BF16)
    kc = kc_ref[...].astype(BF16)
    vc = vc_ref[...].astype(BF16)
    qpos = i * ATT_TQ + lax.broadcasted_iota(jnp.int32, (ATT_TQ, SWA_KWIN), 0)
    kpos = k0 + lax.broadcasted_iota(jnp.int32, (ATT_TQ, SWA_KWIN), 1)
    band = jnp.abs(qpos - kpos) <= WINDOW
    for h in range(SWA_HEADS):
        g = h // SWA_REP
        gl = slice(g * SWA_DH, (g + 1) * SWA_DH)
        q = q_all[:, h * SWA_DH:(h + 1) * SWA_DH].astype(BF16)
        s_loc = jnp.where(band, _dot_nt(q, kwin[:, gl]) * SWA_SCALE, NEG)
        s_ctx = _dot_nt(q, kc[:, gl]) * SWA_SCALE
        (e_loc, e_ctx), den = _softmax_parts([s_loc, s_ctx], sink=sink_ref[0, h])
        o = _dot(e_loc.astype(BF16), vwin[:, gl]) + _dot(e_ctx.astype(BF16), vc[:, gl])
        out_ref[:, h * SWA_DH:(h + 1) * SWA_DH] = o / den


def _smem_spec():
    return pl.BlockSpec(memory_space=pltpu.SMEM)


def _swa_ctx(proj, sink, l):
    return pl.pallas_call(
        _swa_ctx_kernel,
        grid=(BATCH,),
        in_specs=[
            _smem_spec(),
            pl.BlockSpec((SEQ, 512), lambda b: (b, C_SQ // 512)),
            pl.BlockSpec((SEQ, 256), lambda b: (b, C_SK // 256)),
            pl.BlockSpec((SEQ, 256), lambda b: (b, C_SV // 256)),
        ],
        out_specs=pl.BlockSpec((SEQ, GROUP_W), lambda b: (b, 0)),
        out_shape=jax.ShapeDtypeStruct((N_CTX, GROUP_W), F32),
        compiler_params=_cparams("parallel"),
        name="swa_context",
    )(sink, proj, proj, proj)


def _swa_lat(proj, sink, cache_k, cache_v, cos128, sin128, l):
    qb0 = N_CTX // ATT_TQ
    bb0 = N_CTX // DEC_SEQ
    kvw = SWA_KV_HEADS * SWA_DH
    return pl.pallas_call(
        _swa_lat_kernel,
        grid=(DEC_BATCH, LAT_QB),
        in_specs=[
            _smem_spec(),
            pl.BlockSpec((ATT_TQ, 512), lambda b, i: (qb0 + b * LAT_QB + i, C_SQ // 512)),
            pl.BlockSpec((DEC_SEQ, 256), lambda b, i: (bb0 + b, C_SK // 256)),
            pl.BlockSpec((DEC_SEQ, 256), lambda b, i: (bb0 + b, C_SV // 256)),
            pl.BlockSpec((None, None, PAST_LEN, kvw), lambda b, i: (b, l, 0, 0)),
            pl.BlockSpec((None, None, PAST_LEN, kvw), lambda b, i: (b, l, 0, 0)),
            pl.BlockSpec((ATT_TQ, 512), lambda b, i: (i, 0)),
            pl.BlockSpec((ATT_TQ, 512), lambda b, i: (i, 0)),
            pl.BlockSpec((DEC_SEQ, 256), lambda b, i: (0, 0)),
            pl.BlockSpec((DEC_SEQ, 256), lambda b, i: (0, 0)),
        ],
        out_specs=pl.BlockSpec((ATT_TQ, GROUP_W), lambda b, i: (b * LAT_QB + i, 0)),
        out_shape=jax.ShapeDtypeStruct((N_LAT, GROUP_W), F32),
        scratch_shapes=[pltpu.VMEM((DEC_SEQ, kvw), BF16)],
        compiler_params=_cparams("parallel", "arbitrary"),
        name="swa_latent",
    )(sink, proj, proj, proj, cache_k, cache_v,
      jnp.tile(cos128, (1, SWA_HEADS)), jnp.tile(sin128, (1, SWA_HEADS)),
      jnp.tile(cos128, (1, SWA_KV_HEADS)), jnp.tile(sin128, (1, SWA_KV_HEADS)))


DIFF_SCALE = DIFF_DH ** -0.5


def _diff_lambda(lq1_ref, lk1_ref, lq2_ref, lk2_ref):
    a = jnp.sum(lq1_ref[...] * lk1_ref[...], axis=1, keepdims=True)
    b = jnp.sum(lq2_ref[...] * lk2_ref[...], axis=1, keepdims=True)
    return jnp.exp(a) - jnp.exp(b)


def _diff_heads(q, k_parts, v_parts, lam, lam_init, ng_ref, out_ref):
    for h in range(DIFF_HEADS):
        ps = []
        for c in range(2):
            sl = slice(h * 2 * DIFF_DH + c * DIFF_DH, h * 2 * DIFF_DH + (c + 1) * DIFF_DH)
            qc = q[:, sl].astype(BF16)
            es, den = _softmax_parts([_dot_nt(qc, kp[:, sl]) * DIFF_SCALE for kp in k_parts])
            ps.append([e / den for e in es])
        vl = slice(h * 2 * DIFF_DH, (h + 1) * 2 * DIFF_DH)
        o = None
        for p1, p2, vp in zip(ps[0], ps[1], v_parts):
            t = _dot((p1 - lam * p2).astype(BF16), vp[:, vl])
            o = t if o is None else o + t
        out_ref[:, vl] = _rms(o, ng_ref[...]) * (1.0 - lam_init)


def _diff_ctx_kernel(q_ref, k_ref, v_ref, lq1_ref, lk1_ref, lq2_ref, lk2_ref, ng_ref, out_ref, *, lam_init):
    lam = _diff_lambda(lq1_ref, lk1_ref, lq2_ref, lk2_ref) + lam_init
    _diff_heads(q_ref[...], [k_ref[...].astype(BF16)], [v_ref[...].astype(BF16)], lam, lam_init, ng_ref, out_ref)


def _diff_lat_kernel(q_ref, k_ref, v_ref, kc_ref, vc_ref, lq1_ref, lk1_ref, lq2_ref, lk2_ref, ng_ref,
                     cosq_ref, sinq_ref, cosk_ref, sink_ref, out_ref, kr_ref, *, lam_init):
    @pl.when(pl.program_id(1) == 0)
    def _():
        kr_ref[...] = _rope(k_ref[...], cosk_ref[...], sink_ref[...], DIFF_DH // 4).astype(BF16)

    lam = _diff_lambda(lq1_ref, lk1_ref, lq2_ref, lk2_ref) + lam_init
    q = _rope(q_ref[...], cosq_ref[...], sinq_ref[...], DIFF_DH // 4)
    _diff_heads(q, [kr_ref[...], kc_ref[...].astype(BF16)], [v_ref[...].astype(BF16), vc_ref[...].astype(BF16)],
                lam, lam_init, ng_ref, out_ref)


def _vec_specs(n, width, l, nargs):
    return [pl.BlockSpec((None, 1, width), lambda *_: (l, 0, 0)) for _ in range(n)]


def _diff_ctx(proj, lq1, lk1, lq2, lk2, ng, l):
    lam_init = 0.8 - 0.6 * math.exp(-0.3 * l)
    return pl.pallas_call(
        functools.partial(_diff_ctx_kernel, lam_init=lam_init),
        grid=(BATCH,),
        in_specs=[
            pl.BlockSpec((SEQ, 512), lambda b: (b, C_DQ // 512)),
            pl.BlockSpec((SEQ, 512), lambda b: (b, C_DK // 512)),
            pl.BlockSpec((SEQ, 512), lambda b: (b, C_DV // 512)),
        ] + _vec_specs(4, DIFF_DH, l, 1) + _vec_specs(1, 2 * DIFF_DH, l, 1),
        out_specs=pl.BlockSpec((SEQ, GROUP_W), lambda b: (b, 0)),
        out_shape=jax.ShapeDtypeStruct((N_CTX, GROUP_W), F32),
        compiler_params=_cparams("parallel"),
        name="diff_context",
    )(proj, proj, proj, lq1, lk1, lq2, lk2, ng)


def _diff_lat(proj, lq1, lk1, lq2, lk2, ng, cache_k, cache_v, cos64, sin64, l):
    lam_init = 0.8 - 0.6 * math.exp(-0.3 * l)
    qb0 = N_CTX // ATT_TQ
    bb0 = N_CTX // DEC_SEQ
    cos_t = jnp.tile(cos64, (1, 2 * DIFF_HEADS))
    sin_t = jnp.tile(sin64, (1, 2 * DIFF_HEADS))
    return pl.pallas_call(
        functools.partial(_diff_lat_kernel, lam_init=lam_init),
        grid=(DEC_BATCH, LAT_QB),
        in_specs=[
            pl.BlockSpec((ATT_TQ, 512), lambda b, i: (qb0 + b * LAT_QB + i, C_DQ // 512)),
            pl.BlockSpec((DEC_SEQ, 512), lambda b, i: (bb0 + b, C_DK // 512)),
            pl.BlockSpec((DEC_SEQ, 512), lambda b, i: (bb0 + b, C_DV // 512)),
            pl.BlockSpec((None, None, PAST_LEN, GROUP_W), lambda b, i: (b, l, 0, 0)),
            pl.BlockSpec((None, None, PAST_LEN, GROUP_W), lambda b, i: (b, l, 0, 0)),
        ] + _vec_specs(4, DIFF_DH, l, 2) + _vec_specs(1, 2 * DIFF_DH, l, 2) + [
            pl.BlockSpec((ATT_TQ, 512), lambda b, i: (i, 0)),
            pl.BlockSpec((ATT_TQ, 512), lambda b, i: (i, 0)),
            pl.BlockSpec((DEC_SEQ, 512), lambda b, i: (0, 0)),
            pl.BlockSpec((DEC_SEQ, 512), lambda b, i: (0, 0)),
        ],
        out_specs=pl.BlockSpec((ATT_TQ, GROUP_W), lambda b, i: (b * LAT_QB + i, 0)),
        out_shape=jax.ShapeDtypeStruct((N_LAT, GROUP_W), F32),
        scratch_shapes=[pltpu.VMEM((DEC_SEQ, GROUP_W), BF16)],
        compiler_params=_cparams("parallel", "arbitrary"),
        name="diff_latent",
    )(proj, proj, proj, cache_k, cache_v, lq1, lk1, lq2, lk2, ng, cos_t, sin_t, cos_t, sin_t)


OUT_TM = 256


def _out_kernel(m0_ref, m1_ref, m2_ref, m3_ref, x_ref, mod_ref, w_ref, x1_ref):
    acc = None
    for g, m_ref in enumerate((m0_ref, m1_ref, m2_ref, m3_ref)):
        t = _dot(m_ref[...].astype(BF16), w_ref[g * GROUP_W:(g + 1) * GROUP_W, :])
        acc = t if acc is None else acc + t
    x1_ref[...] = x_ref[...] + mod_ref[2:3, :] * acc


def _out_proj(mixed, x, mod4, w_out, l):
    row = lambda i: (i, 0)
    return pl.pallas_call(
        _out_kernel,
        grid=(NT // OUT_TM,),
        in_specs=[pl.BlockSpec((OUT_TM, GROUP_W), row) for _ in range(4)] + [
            pl.BlockSpec((OUT_TM, D_MODEL), row),
            pl.BlockSpec((None, None, 6, D_MODEL), lambda i: (l, _mod_set(i * OUT_TM), 0, 0)),
            pl.BlockSpec((None, D_MODEL, D_MODEL), lambda i: (l, 0, 0)),
        ],
        out_specs=pl.BlockSpec((OUT_TM, D_MODEL), row),
        out_shape=jax.ShapeDtypeStruct((NT, D_MODEL), F32),
        compiler_params=_cparams("parallel"),
        name="out_proj_residual",
    )(*mixed, x, mod4, w_out)


def _peerq_kernel(x_ref, mod_ref, g_ref, w_ref, h_ref, q_ref):
    h = (_rms(x_ref[...], g_ref[...]) * (1.0 + mod_ref[4:5, :]) + mod_ref[3:4, :]).astype(BF16)
    h_ref[...] = h
    q_ref[...] = _dot(h, w_ref[...])


def _peer_query(x1, mod4, norm_g, w_q, l):
    row = lambda i: (i, 0)
    return pl.pallas_call(
        _peerq_kernel,
        grid=(NT // OUT_TM,),
        in_specs=[
            pl.BlockSpec((OUT_TM, D_MODEL), row),
            pl.BlockSpec((None, None, 6, D_MODEL), lambda i: (l, _mod_set(i * OUT_TM), 0, 0)),
            pl.BlockSpec((None, 1, D_MODEL), lambda i: (l, 0, 0)),
            pl.BlockSpec((None, D_MODEL, PEER_HEADS * PEER_QDIM), lambda i: (l, 0, 0)),
        ],
        out_specs=(pl.BlockSpec((OUT_TM, D_MODEL), row),
                   pl.BlockSpec((OUT_TM, PEER_HEADS * PEER_QDIM), row)),
        out_shape=(jax.ShapeDtypeStruct((NT, D_MODEL), BF16),
                   jax.ShapeDtypeStruct((NT, PEER_HEADS * PEER_QDIM), F32)),
        compiler_params=_cparams("parallel"),
        name="adaln_peer_query",
    )(x1, mod4, norm_g, w_q)


ROUTE_TL = 256
NOT_SEL = float(PEER_TOPK)


def _top16(s):
    idx = lax.broadcasted_iota(jnp.int32, s.shape, 0).astype(F32)
    slot = lax.broadcasted_iota(jnp.int32, (PEER_TOPK, s.shape[1]), 0)
    rank = jnp.full(s.shape, NOT_SEL, F32)
    vals = jnp.zeros((PEER_TOPK, s.shape[1]), F32)
    for k in range(PEER_TOPK):
        m = jnp.max(s, axis=0, keepdims=True)
        first = jnp.min(jnp.where(s == m, idx, float(PEER_NKEYS)), axis=0, keepdims=True)
        sel = idx == first
        rank = jnp.where(sel, float(k), rank)
        s = jnp.where(sel, NEG, s)
        vals = jnp.where(slot == k, m, vals)
    return rank, vals


CAND_HALF = PEER_TOPK // 2
CAND_ROWS = PEER_TOPK + (CAND_HALF - 1) * CAND_HALF + CAND_HALF
FLAT_NONE = float(PEER_TOPK * PEER_TOPK)


def _cand_flat(tl):
    r = lax.broadcasted_iota(jnp.int32, (CAND_ROWS, tl), 0)
    mid = r - PEER_TOPK
    mid_flat = (1 + mid // CAND_HALF) * PEER_TOPK + mid % CAND_HALF
    last_flat = (CAND_HALF + r - (CAND_ROWS - CAND_HALF)) * PEER_TOPK
    flat = jnp.where(r < PEER_TOPK, r, jnp.where(r < CAND_ROWS - CAND_HALF, mid_flat, last_flat))
    return flat.astype(F32)


def _route_kernel(q_ref, k1_ref, k2_ref, rank2_ref, e2_ref, cnt1_ref, e1_ref):
    tl = q_ref.shape[0]
    flat = _cand_flat(tl)
    for h in range(PEER_HEADS):
        half = PEER_QDIM // 2
        q1 = q_ref[:, h * PEER_QDIM:h * PEER_QDIM + half].astype(BF16)
        q2 = q_ref[:, h * PEER_QDIM + half:(h + 1) * PEER_QDIM].astype(BF16)
        s1 = _dot_nt(k1_ref[...], q1)
        s2 = _dot_nt(k2_ref[...], q2)
        rank1, v1 = _top16(s1)
        rank2, v2 = _top16(s2)
        slabs = [v1[0:1, :] + v2]
        slabs += [v1[a:a + 1, :] + v2[0:CAND_HALF, :] for a in range(1, CAND_HALF)]
        slabs.append(v1[CAND_HALF:, :] + v2[0:1, :])
        cand = jnp.concatenate(slabs, axis=0)
        top = v1[0:1, :] + v2[0:1, :]
        cnt1 = jnp.zeros(s1.shape, F32)
        z = jnp.zeros((1, tl), F32)
        for k in range(PEER_TOPK):
            m = jnp.max(cand, axis=0, keepdims=True)
            first = jnp.min(jnp.where(cand == m, flat, FLAT_NONE), axis=0, keepdims=True)
            cand = jnp.where(flat == first, NEG, cand)
            a_sel = jnp.floor(first * (1.0 / PEER_TOPK))
            cnt1 = cnt1 + jnp.where(rank1 == a_sel, 1.0, 0.0)
            z = z + jnp.exp(m - top)
        rank2_ref[h] = rank2
        cnt1_ref[h] = cnt1
        e1_ref[h] = jnp.exp(s1 - v1[0:1, :]) / z
        e2_ref[h] = jnp.exp(s2 - v2[0:1, :])


def _peer_route(q, keys, l):
    shp = jax.ShapeDtypeStruct((PEER_HEADS, PEER_NKEYS, NT), F32)
    spec = pl.BlockSpec((PEER_HEADS, PEER_NKEYS, ROUTE_TL), lambda i: (0, 0, i))
    half = PEER_QDIM // 2
    return pl.pallas_call(
        _route_kernel,
        grid=(NT // ROUTE_TL,),
        in_specs=[
            pl.BlockSpec((ROUTE_TL, PEER_HEADS * PEER_QDIM), lambda i: (i, 0)),
            pl.BlockSpec((None, None, PEER_NKEYS, half), lambda i: (l, 0, 0, 0)),
            pl.BlockSpec((None, None, PEER_NKEYS, half), lambda i: (l, 1, 0, 0)),
        ],
        out_specs=(spec, spec, spec, spec),
        out_shape=(shp, shp, shp, shp),
        compiler_params=_cparams("parallel"),
        name="peer_route",
    )(q, keys, keys)


PEER_TT = 512
PEER_EB = 512
GELU_C = math.sqrt(2.0 / math.pi)


def _gelu_tanh(x):
    return 0.5 * x * (1.0 + jnp.tanh(GELU_C * (x + 0.044715 * (x * x * x))))


def _peer_kernel(h_ref, u_ref, v_ref, rank2_ref, e2_ref, cnt1_ref, e1_ref, x_ref, mod_ref, o_ref):
    e = pl.program_id(1)

    @pl.when(e == 0)
    def _():
        o_ref[...] = jnp.zeros_like(o_ref)

    act = _gelu_tanh(_dot_nt(u_ref[...], h_ref[...]))
    n_i = PEER_EB // PEER_NKEYS
    tiles = []
    for ii in range(n_i):
        key1 = e * n_i + ii
        w = jnp.zeros((PEER_NKEYS, PEER_TT), F32)
        for h in range(PEER_HEADS):
            cnt = cnt1_ref[h, pl.ds(key1, 1), :]
            g1 = e1_ref[h, pl.ds(key1, 1), :]
            w = w + jnp.where(rank2_ref[h] < cnt, e2_ref[h] * g1, 0.0)
        tiles.append((w * act[ii * PEER_NKEYS:(ii + 1) * PEER_NKEYS, :]).astype(BF16))
    p = jnp.concatenate(tiles, axis=0)
    o_ref[...] += _dot_tn(p, v_ref[...])

    @pl.when(e == pl.num_programs(1) - 1)
    def _():
        o_ref[...] = x_ref[...] + mod_ref[5:6, :] * o_ref[...]


def _peer(h2, u_tab, v_tab, routing, x1, mod4, l):
    rank2, e2, cnt1, e1 = routing
    rspec = pl.BlockSpec((PEER_HEADS, PEER_NKEYS, PEER_TT), lambda i, e: (0, 0, i))
    return pl.pallas_call(
        _peer_kernel,
        grid=(NT // PEER_TT, PEER_N // PEER_EB),
        in_specs=[
            pl.BlockSpec((PEER_TT, D_MODEL), lambda i, e: (i, 0)),
            pl.BlockSpec((None, PEER_EB, D_MODEL), lambda i, e: (l, e, 0)),
            pl.BlockSpec((None, PEER_EB, D_MODEL), lambda i, e: (l, e, 0)),
            rspec, rspec, rspec, rspec,
            pl.BlockSpec((PEER_TT, D_MODEL), lambda i, e: (i, 0)),
            pl.BlockSpec((None, None, 6, D_MODEL), lambda i, e: (l, _mod_set(i * PEER_TT), 0, 0)),
        ],
        out_specs=pl.BlockSpec((PEER_TT, D_MODEL), lambda i, e: (i, 0)),
        out_shape=jax.ShapeDtypeStruct((NT, D_MODEL), F32),
        compiler_params=_cparams("parallel", "arbitrary"),
        name="peer_experts",
    )(h2, u_tab, v_tab, rank2, e2, cnt1, e1, x1, mod4)


FIN_TM = 512


def _final_kernel(x_ref, g_ref, o_ref):
    o_ref[...] = _rms(x_ref[...], g_ref[...])


def _final_norm(x, g):
    return pl.pallas_call(
        _final_kernel,
        grid=(NT // FIN_TM,),
        in_specs=[pl.BlockSpec((FIN_TM, D_MODEL), lambda i: (i, 0)),
                  pl.BlockSpec((1, D_MODEL), lambda i: (0, 0))],
        out_specs=pl.BlockSpec((FIN_TM, D_MODEL), lambda i: (i, 0)),
        out_shape=jax.ShapeDtypeStruct((NT, D_MODEL), F32),
        compiler_params=_cparams("parallel"),
        name="final_norm",
    )(x, g)


def _permute_w_in(w_in):
    sizes = (GROUP_W, GROUP_W, GROUP_W, GROUP_W, 4 * ML_HEADS, MLA_Q_RANK, MLA_KV_RANK, MLA_D_ROPE,
             SWA_HEADS * SWA_DH, SWA_KV_HEADS * SWA_DH, SWA_KV_HEADS * SWA_DH, GROUP_W, GROUP_W, GROUP_W)
    offs = [0]
    for s in sizes:
        offs.append(offs[-1] + s)
    part = lambda i: w_in[:, :, offs[i]:offs[i + 1]]
    order = [0, 1, 2, 3, 5, 6, 8, 9, 10, 11, 12, 13, 7, 4]
    cols = [part(i) for i in order]
    used = sum(sizes)
    cols.append(jnp.zeros(w_in.shape[:2] + (PROJ_W - used,), w_in.dtype))
    return jnp.concatenate(cols, axis=-1).astype(BF16)


def _permute_w_uq(w_uq):
    w = w_uq.reshape(DEPTH, MLA_Q_RANK, MLA_HEADS, MLA_D_NOPE + MLA_D_ROPE)
    nope = w[..., :MLA_D_NOPE].reshape(DEPTH, MLA_Q_RANK, MLA_HEADS * MLA_D_NOPE)
    rope = w[..., MLA_D_NOPE:].reshape(DEPTH, MLA_Q_RANK, MLA_HEADS * MLA_D_ROPE)
    return jnp.concatenate([nope, rope], axis=-1).astype(BF16)


def kernel(x_prompt, x_sample, c, cache_mla_ckv, cache_mla_krope, cache_swa_k, cache_swa_v, cache_diff_k,
           cache_diff_v, state_mlstm_C, state_mlstm_n, state_mlstm_m, c_ctx, w_mod, b_mod, norm1_g, w_in,
           mlstm_i_bias, mlstm_f_bias, mlstm_norm_g, mla_qnorm_g, mla_w_uq, mla_kvnorm_g, mla_w_ukv, swa_sink,
           diff_lq1, diff_lk1, diff_lq2, diff_lk2, diff_norm_g, w_out, norm2_g, peer_w_q, peer_sub_keys,
           peer_u, peer_v, final_norm_g):
    x = jnp.concatenate([x_prompt.reshape(N_CTX, D_MODEL), x_sample.reshape(N_LAT, D_MODEL)], axis=0)

    cvec = jnp.concatenate([c_ctx[None, :], c], axis=0)
    cvec_t = jnp.pad(cvec.T, ((0, 0), (0, 8 - N_SETS)))
    mod4 = _modulation(cvec_t, w_mod, b_mod).reshape(DEPTH, 8, 6, D_MODEL)

    w_in_p = _permute_w_in(w_in)
    w_uq_p = _permute_w_uq(mla_w_uq)
    w_ukv_b = mla_w_ukv.astype(BF16)
    w_out_b = w_out.astype(BF16)
    w_q_b = peer_w_q.astype(BF16)
    keys_b = peer_sub_keys.astype(BF16)
    u_b = peer_u.astype(BF16)
    v_b = peer_v.astype(BF16)

    vec3 = lambda a: a.reshape(DEPTH, 1, a.shape[-1])
    norm1_3, norm2_3 = vec3(norm1_g), vec3(norm2_g)
    mlng_3, qg_3, kg_3, dng_3 = vec3(mlstm_norm_g), vec3(mla_qnorm_g), vec3(mla_kvnorm_g), vec3(diff_norm_g)
    lq1_3, lk1_3, lq2_3, lk2_3 = vec3(diff_lq1), vec3(diff_lk1), vec3(diff_lq2), vec3(diff_lk2)

    n_chain = 2 * ML_HEADS
    st_c = state_mlstm_C.reshape(DEC_BATCH, DEPTH, n_chain, ML_DH, ML_DH)
    st_n = state_mlstm_n.reshape(DEC_BATCH, DEPTH, n_chain, ML_DH)
    st_m = jnp.broadcast_to(state_mlstm_m.reshape(DEC_BATCH, DEPTH, n_chain, 1), (DEC_BATCH, DEPTH, n_chain, ML_DH))
    swk_c = cache_swa_k.reshape(DEC_BATCH, DEPTH, PAST_LEN, SWA_KV_HEADS * SWA_DH)
    swv_c = cache_swa_v.reshape(DEC_BATCH, DEPTH, PAST_LEN, SWA_KV_HEADS * SWA_DH)
    dfk_c = cache_diff_k.reshape(DEC_BATCH, DEPTH, PAST_LEN, GROUP_W)
    dfv_c = cache_diff_v.reshape(DEC_BATCH, DEPTH, PAST_LEN, GROUP_W)

    cos64, sin64 = _rope_tables(DEC_SEQ, 64)
    cos128, sin128 = _rope_tables(DEC_SEQ, 128)

    outs = [[] for _ in range(9)]
    for l in range(DEPTH):
        proj = _project(x, mod4, norm1_3, w_in_p, l)

        gates = proj[:, C_TAIL + TAIL_G:C_TAIL + TAIL_G + 16]
        gt = gates.reshape(NT // ML_CHUNK, ML_CHUNK, 16).transpose(0, 2, 1)
        bias = jnp.concatenate([mlstm_i_bias[l].reshape(-1), mlstm_f_bias[l].reshape(-1)])
        bcol = bias.reshape(16, 1)
        brow = jnp.zeros((1, 128), F32).at[0, TAIL_G:TAIL_G + 16].set(bias)

        ml_ctx, c_st, n_st, m_st = _mlstm(proj, gt, bcol, brow, mlng_3, l, latent=False)
        ml_lat = _mlstm(proj, gt, bcol, brow, mlng_3, l, latent=True, states=(st_c, st_n, st_m))
        mla_ctx, ckv_n = _mla_ctx(proj, qg_3, kg_3, w_uq_p, w_ukv_b, l)
        mla_lat = _mla_lat(proj, qg_3, kg_3, w_uq_p, w_ukv_b, cache_mla_ckv, cache_mla_krope, cos64, sin64, l)
        sink = swa_sink[l].reshape(1, SWA_HEADS)
        swa_ctx = _swa_ctx(proj, sink, l)
        swa_lat = _swa_lat(proj, sink, swk_c, swv_c, cos128, sin128, l)
        df_ctx = _diff_ctx(proj, lq1_3, lk1_3, lq2_3, lk2_3, dng_3, l)
        df_lat = _diff_lat(proj, lq1_3, lk1_3, lq2_3, lk2_3, dng_3, dfk_c, dfv_c, cos64, sin64, l)

        mixed = [jnp.concatenate([a, b], axis=0) for a, b in
                 ((ml_ctx, ml_lat), (mla_ctx, mla_lat), (swa_ctx, swa_lat), (df_ctx, df_lat))]
        x1 = _out_proj(mixed, x, mod4, w_out_b, l)
        h2, q = _peer_query(x1, mod4, norm2_3, w_q_b, l)
        routing = _peer_route(q, keys_b, l)
        x = _peer(h2, u_b, v_b, routing, x1, mod4, l)

        pc = proj[:N_CTX]
        outs[0].append(ckv_n.reshape(BATCH, SEQ, MLA_KV_RANK))
        outs[1].append(pc[:, C_TAIL:C_TAIL + MLA_D_ROPE].reshape(BATCH, SEQ, MLA_D_ROPE))
        outs[2].append(pc[:, C_SK:C_SK + 256].reshape(BATCH, SEQ, SWA_KV_HEADS, SWA_DH))
        outs[3].append(pc[:, C_SV:C_SV + 256].reshape(BATCH, SEQ, SWA_KV_HEADS, SWA_DH))
        outs[4].append(pc[:, C_DK:C_DK + 512].reshape(BATCH, SEQ, DIFF_HEADS, 2 * DIFF_DH))
        outs[5].append(pc[:, C_DV:C_DV + 512].reshape(BATCH, SEQ, DIFF_HEADS, 2 * DIFF_DH))
        outs[6].append(c_st.reshape(BATCH, 2, ML_HEADS, ML_DH, ML_DH))
        outs[7].append(n_st.reshape(BATCH, 2, ML_HEADS, ML_DH))
        outs[8].append(m_st[:, :, 0].reshape(BATCH, 2, ML_HEADS))

    y = _final_norm(x, final_norm_g.reshape(1, D_MODEL))
    y_prompt = y[:N_CTX].reshape(BATCH, SEQ, D_MODEL)
    y_sample = y[N_CTX:].reshape(DEC_BATCH, DEC_SEQ, D_MODEL)
    return (y_prompt, y_sample) + tuple(jnp.stack(o, axis=1) for o in outs)
```

```python
import functools
import math

import jax
import jax.numpy as jnp
from jax import lax
from jax.experimental import pallas as pl
from jax.experimental.pallas import tpu as pltpu

F32 = jnp.float32
BF16 = jnp.bfloat16

D_MODEL = 2048
BATCH = 32
SEQ = 256
DEPTH = 4
DEC_BATCH = 2
DEC_SEQ = 1024
PAST_LEN = 256
GRID_W = 64
GROUP_W = D_MODEL // 4
ML_HEADS = 4
ML_DH = GROUP_W // ML_HEADS
ML_CHUNK = 64
MLA_HEADS = 4
MLA_D_NOPE = GROUP_W // MLA_HEADS
MLA_D_ROPE = 64
MLA_Q_RANK = D_MODEL // 8
MLA_KV_RANK = D_MODEL // 8
MLA_SCALE = (MLA_D_NOPE + MLA_D_ROPE) ** -0.5
SWA_HEADS = 4
SWA_KV_HEADS = 2
SWA_DH = GROUP_W // SWA_HEADS
WINDOW = 128
DIFF_HEADS = 4
DIFF_DH = GROUP_W // (2 * DIFF_HEADS)
PEER_HEADS = 8
PEER_QDIM = 256
PEER_NKEYS = 128
PEER_N = PEER_NKEYS * PEER_NKEYS
PEER_TOPK = 16
ROPE_BASE = 10000.0
EPS = 1e-6
NEG = -1e30

N_CTX = BATCH * SEQ
N_LAT = DEC_BATCH * DEC_SEQ
NT = N_CTX + N_LAT
N_SETS = 1 + DEC_BATCH

C_MLQ, C_MLK, C_MLV, C_MLO = 0, 512, 1024, 1536
C_CQ, C_CKV = 2048, 2304
C_SQ, C_SK, C_SV = 2560, 3072, 3328
C_DQ, C_DK, C_DV = 3584, 4096, 4608
C_TAIL = 5120
TAIL_G = 64
PROJ_W = 5376
PROJ_TN = 896

VMEM_LIMIT = 56 * 1024 * 1024


def _cparams(*sem):
    return pltpu.CompilerParams(dimension_semantics=sem, vmem_limit_bytes=VMEM_LIMIT)


def _mod_set(row_start):
    return jnp.where(row_start >= N_CTX, (row_start - N_CTX) // DEC_SEQ + 1, 0)


def _rms(x, g):
    return x * lax.rsqrt(jnp.mean(x * x, axis=-1, keepdims=True) + EPS) * g


def _dot(a, b):
    return jnp.dot(a, b, preferred_element_type=F32)


def _dot_nt(a, b):
    return lax.dot_general(a, b, (((1,), (1,)), ((), ())), preferred_element_type=F32)


def _dot_tn(a, b):
    return lax.dot_general(a, b, (((0,), (0,)), ((), ())), preferred_element_type=F32)


def _dot_hi(a, b):
    return jnp.dot(a, b, preferred_element_type=F32, precision=lax.Precision.HIGHEST)


MOD_TK = 512
MOD_TN = 2048


def _mod_kernel(c_ref, w_ref, b_ref, o_ref, acc_ref):
    k = pl.program_id(2)

    @pl.when(k == 0)
    def _():
        acc_ref[...] = jnp.zeros_like(acc_ref)

    w = w_ref[...]
    for r in range(N_SETS):
        cv = c_ref[:, r:r + 1]
        sv = cv * jax.nn.sigmoid(cv)
        acc_ref[r] += (sv * w).reshape(MOD_TK // 8, 8, MOD_TN).sum(axis=0)

    @pl.when(k == pl.num_programs(2) - 1)
    def _():
        o_ref[...] = jnp.zeros_like(o_ref)
        for r in range(N_SETS):
            o_ref[r:r + 1, :] = acc_ref[r].sum(axis=0, keepdims=True) + b_ref[...]


def _modulation(cvec_t, w_mod, b_mod):
    return pl.pallas_call(
        _mod_kernel,
        grid=(DEPTH, 6 * D_MODEL // MOD_TN, D_MODEL // MOD_TK),
        in_specs=[
            pl.BlockSpec((MOD_TK, 8), lambda l, n, k: (k, 0)),
            pl.BlockSpec((None, MOD_TK, MOD_TN), lambda l, n, k: (l, k, n)),
            pl.BlockSpec((None, 1, MOD_TN), lambda l, n, k: (l, 0, n)),
        ],
        out_specs=pl.BlockSpec((None, 8, MOD_TN), lambda l, n, k: (l, 0, n)),
        out_shape=jax.ShapeDtypeStruct((DEPTH, 8, 6 * D_MODEL), F32),
        scratch_shapes=[pltpu.VMEM((N_SETS, 8, MOD_TN), F32)],
        compiler_params=_cparams("parallel", "parallel", "arbitrary"),
        name="modulation",
    )(cvec_t, w_mod, b_mod.reshape(DEPTH, 1, 6 * D_MODEL))


PROJ_TM = 512


def _proj_kernel(x_ref, m_ref, g_ref, w_ref, o_ref, h_ref):
    @pl.when(pl.program_id(1) == 0)
    def _():
        h = _rms(x_ref[...], g_ref[...]) * (1.0 + m_ref[1:2, :]) + m_ref[0:1, :]
        h_ref[...] = h.astype(BF16)

    o_ref[...] = _dot(h_ref[...], w_ref[...])


def _project(x, mod4, norm_g, w_in_p, l):
    return pl.pallas_call(
        _proj_kernel,
        grid=(NT // PROJ_TM, PROJ_W // PROJ_TN),
        in_specs=[
            pl.BlockSpec((PROJ_TM, D_MODEL), lambda i, j: (i, 0)),
            pl.BlockSpec((None, None, 6, D_MODEL), lambda i, j: (l, _mod_set(i * PROJ_TM), 0, 0)),
            pl.BlockSpec((None, 1, D_MODEL), lambda i, j: (l, 0, 0)),
            pl.BlockSpec((None, D_MODEL, PROJ_TN), lambda i, j: (l, 0, j)),
        ],
        out_specs=pl.BlockSpec((PROJ_TM, PROJ_TN), lambda i, j: (i, j)),
        out_shape=jax.ShapeDtypeStruct((NT, PROJ_W), F32),
        scratch_shapes=[pltpu.VMEM((PROJ_TM, D_MODEL), BF16)],
        compiler_params=_cparams("parallel", "arbitrary"),
        name="adaln_in_proj",
    )(x, mod4, norm_g, w_in_p)


def _rope(x, cos, sin, quarter):
    width = x.shape[-1]
    lane = lax.broadcasted_iota(jnp.int32, x.shape, 1)
    first = (lane % (2 * quarter)) < quarter
    partner = jnp.where(first, pltpu.roll(x, width - quarter, 1), pltpu.roll(x, quarter, 1))
    return x * cos + partner * sin


def _rope_tables(n_tok, rot_dim):
    half = rot_dim // 2
    pos = jnp.arange(n_tok)
    row = (pos // GRID_W).astype(F32)
    col = (pos % GRID_W).astype(F32)
    inv = ROPE_BASE ** (-jnp.arange(0, half, 2, dtype=F32) / half)
    a_row = row[:, None] * inv[None, :]
    a_col = col[:, None] * inv[None, :]
    ang = jnp.concatenate([a_row, a_row, a_col, a_col], axis=-1)
    sign = jnp.tile(jnp.concatenate([-jnp.ones(half // 2, F32), jnp.ones(half // 2, F32)]), 2)
    return jnp.cos(ang), jnp.sin(ang) * sign[None, :]


def _log_sigmoid(x):
    return jnp.minimum(x, 0.0) - jnp.log(1.0 + jnp.exp(-jnp.abs(x)))


def _mlstm_kernel(*refs, n_tok, has_state):
    if has_state:
        (q_ref, k_ref, v_ref, o_ref, tail_ref, gt_ref, bcol_ref, brow_ref, ng_ref, c0_ref, n0_ref, m0_ref,
         out_ref, hf_ref, hb_ref, cst_ref, nst_ref, mst_ref) = refs
    else:
        (q_ref, k_ref, v_ref, o_ref, tail_ref, gt_ref, bcol_ref, brow_ref, ng_ref,
         out_ref, cs_ref, ns_ref, ms_ref, hf_ref, hb_ref, cst_ref, nst_ref, mst_ref) = refs
    n_chunks = n_tok // ML_CHUNK
    scale = ML_DH ** -0.5

    if has_state:
        cst_ref[...] = c0_ref[...]
        nst_ref[...] = n0_ref[...]
        mst_ref[...] = m0_ref[...]
    else:
        cst_ref[...] = jnp.zeros_like(cst_ref)
        nst_ref[...] = jnp.zeros_like(nst_ref)
        mst_ref[...] = jnp.zeros_like(mst_ref)

    row = lax.broadcasted_iota(jnp.int32, (ML_CHUNK, ML_CHUNK), 0)
    col = lax.broadcasted_iota(jnp.int32, (ML_CHUNK, ML_CHUNK), 1)
    lower = (col <= row)
    upper = (col >= row)
    lower_f = lower.astype(F32)
    upper_f = upper.astype(F32)

    def chunk_step(c, carry):
        for d in range(2):
            cc = c if d == 0 else n_chunks - 1 - c
            t0 = pl.multiple_of(cc * ML_CHUNK, ML_CHUNK)
            rows = pl.ds(t0, ML_CHUNK)
            g_col = tail_ref[rows, :] + brow_ref[...]
            g_row = gt_ref[cc] + bcol_ref[...]
            lf_col = _log_sigmoid(g_col)
            lf_row = _log_sigmoid(g_row)
            if d == 0:
                bcum_col = _dot_hi(lower_f, lf_col)
                bcum_row = _dot_hi(lf_row, upper_f)
                mask = lower
            else:
                bcum_col = _dot_hi(upper_f, lf_col)
                bcum_row = _dot_hi(lf_row, lower_f)
                mask = upper
            for h in range(ML_HEADS):
                r = d * ML_HEADS + h
                lanes = slice(h * ML_DH, (h + 1) * ML_DH)
                ci = TAIL_G + r
                cf = TAIL_G + 2 * ML_HEADS + r
                ig_c = g_col[:, ci:ci + 1]
                b_c = bcum_col[:, cf:cf + 1]
                ig_r = g_row[r:r + 1, :]
                b_r = bcum_row[2 * ML_HEADS + r:2 * ML_HEADS + r + 1, :]
                tot = jnp.sum(lf_row[2 * ML_HEADS + r:2 * ML_HEADS + r + 1, :], axis=1, keepdims=True)
                m_prev = mst_ref[r:r + 1, 0:1]
                n_prev = nst_ref[r:r + 1, :]
                c_prev = cst_ref[r]
                q = q_ref[rows, lanes]
                k = k_ref[rows, lanes]
                v = v_ref[rows, lanes].astype(BF16)
                qb = q.astype(BF16)

                dmat = jnp.where(mask, b_c - b_r + ig_r, NEG)
                inter = b_c + m_prev
                mt = jnp.maximum(inter, jnp.max(dmat, axis=1, keepdims=True))
                w = jnp.exp(dmat - mt)
                s = _dot_nt(qb, k.astype(BF16)) * scale * w
                a = jnp.exp(inter - mt)
                num = _dot(s.astype(BF16), v) + a * _dot(qb, c_prev.astype(BF16))
                den = jnp.sum(s, axis=1, keepdims=True) + a * jnp.sum(q * n_prev, axis=1, keepdims=True)
                hc = num / jnp.maximum(jnp.abs(den), jnp.exp(-mt))
                if d == 0:
                    hf_ref[rows, lanes] = hc
                else:
                    hb_ref[rows, lanes] = hc

                wlog_c = tot - b_c + ig_c
                wlog_r = tot - b_r + ig_r
                m_new = jnp.maximum(tot + m_prev, jnp.max(wlog_r, axis=1, keepdims=True))
                decay = jnp.exp(tot + m_prev - m_new)
                kw = k * (scale * jnp.exp(wlog_c - m_new))
                cst_ref[r] = decay * c_prev + _dot_tn(kw.astype(BF16), v)
                nst_ref[r:r + 1, :] = decay * n_prev + jnp.sum(kw, axis=0, keepdims=True)
                mst_ref[r:r + 1, :] = jnp.broadcast_to(m_new, (1, ML_DH))
        return carry

    lax.fori_loop(0, n_chunks, chunk_step, 0)

    for h in range(ML_HEADS):
        lanes = slice(h * ML_DH, (h + 1) * ML_DH)
        hs = hf_ref[:, lanes] + hb_ref[:, lanes]
        out_ref[:, lanes] = _rms(hs, ng_ref[:, lanes]) * jax.nn.sigmoid(o_ref[:, lanes])

    if not has_state:
        cs_ref[...] = cst_ref[...]
        ns_ref[...] = nst_ref[...]
        ms_ref[...] = mst_ref[...]


def _mlstm(proj, gt, bcol, brow, norm_g, l, *, latent, states=None):
    n_tok = DEC_SEQ if latent else SEQ
    n_b = DEC_BATCH if latent else BATCH
    blk0 = N_CTX // n_tok if latent else 0
    n_chunks = n_tok // ML_CHUNK

    def col_spec(c0):
        return pl.BlockSpec((n_tok, 512), lambda b: (blk0 + b, c0 // 512))

    in_specs = [
        col_spec(C_MLQ), col_spec(C_MLK), col_spec(C_MLV), col_spec(C_MLO),
        pl.BlockSpec((n_tok, 128), lambda b: (blk0 + b, C_TAIL // 128)),
        pl.BlockSpec((n_chunks, 16, ML_CHUNK), lambda b: (blk0 + b, 0, 0)),
        pl.BlockSpec((16, 1), lambda b: (0, 0)),
        pl.BlockSpec((1, 128), lambda b: (0, 0)),
        pl.BlockSpec((None, 1, GROUP_W), lambda b: (l, 0, 0)),
    ]
    args = [proj, proj, proj, proj, proj, gt, bcol, brow, norm_g]
    n_chain = 2 * ML_HEADS
    scratch = [pltpu.VMEM((n_tok, GROUP_W), F32), pltpu.VMEM((n_tok, GROUP_W), F32),
               pltpu.VMEM((n_chain, ML_DH, ML_DH), F32), pltpu.VMEM((n_chain, ML_DH), F32),
               pltpu.VMEM((n_chain, ML_DH), F32)]
    if latent:
        c0, n0, m0 = states
        in_specs += [
            pl.BlockSpec((None, None, n_chain, ML_DH, ML_DH), lambda b: (b, l, 0, 0, 0)),
            pl.BlockSpec((None, None, n_chain, ML_DH), lambda b: (b, l, 0, 0)),
            pl.BlockSpec((None, None, n_chain, ML_DH), lambda b: (b, l, 0, 0)),
        ]
        args += [c0, n0, m0]
        out_shape = jax.ShapeDtypeStruct((N_LAT, GROUP_W), F32)
        out_specs = pl.BlockSpec((n_tok, GROUP_W), lambda b: (b, 0))
    else:
        out_shape = (jax.ShapeDtypeStruct((N_CTX, GROUP_W), F32),
                     jax.ShapeDtypeStruct((BATCH, n_chain, ML_DH, ML_DH), F32),
                     jax.ShapeDtypeStruct((BATCH, n_chain, ML_DH), F32),
                     jax.ShapeDtypeStruct((BATCH, n_chain, ML_DH), F32))
        out_specs = (pl.BlockSpec((n_tok, GROUP_W), lambda b: (b, 0)),
                     pl.BlockSpec((None, n_chain, ML_DH, ML_DH), lambda b: (b, 0, 0, 0)),
                     pl.BlockSpec((None, n_chain, ML_DH), lambda b: (b, 0, 0)),
                     pl.BlockSpec((None, n_chain, ML_DH), lambda b: (b, 0, 0)))
    return pl.pallas_call(
        functools.partial(_mlstm_kernel, n_tok=n_tok, has_state=latent),
        grid=(n_b,),
        in_specs=in_specs,
        out_specs=out_specs,
        out_shape=out_shape,
        scratch_shapes=scratch,
        compiler_params=_cparams("parallel"),
        name="mlstm_latent" if latent else "mlstm_context",
    )(*args)


def _softmax_parts(scores, sink=None):
    m = jnp.max(scores[0], axis=1, keepdims=True)
    for s in scores[1:]:
        m = jnp.maximum(m, jnp.max(s, axis=1, keepdims=True))
    if sink is not None:
        m = jnp.maximum(m, sink)
    es = [jnp.exp(s - m) for s in scores]
    den = jnp.sum(es[0], axis=1, keepdims=True)
    for e in es[1:]:
        den = den + jnp.sum(e, axis=1, keepdims=True)
    if sink is not None:
        den = den + jnp.exp(sink - m)
    return es, den


ATT_TQ = 256
LAT_QB = DEC_SEQ // ATT_TQ
N_KEYS_LAT = DEC_SEQ + PAST_LEN


def _mla_q(cq_ref, qg_ref, wuq_ref):
    return _dot(_rms(cq_ref[...], qg_ref[...]).astype(BF16), wuq_ref[...])


def _mla_heads(q, kv, kr, n_keys):
    outs = []
    for h in range(MLA_HEADS):
        qn = q[:, h * MLA_D_NOPE:(h + 1) * MLA_D_NOPE].astype(BF16)
        r0 = MLA_HEADS * MLA_D_NOPE + h * MLA_D_ROPE
        qr = q[:, r0:r0 + MLA_D_ROPE].astype(BF16)
        kn = kv[:, h * 256:h * 256 + MLA_D_NOPE]
        v = kv[:, h * 256 + MLA_D_NOPE:(h + 1) * 256]
        s = (_dot_nt(qn, kn) + _dot_nt(qr, kr)) * MLA_SCALE
        (e,), den = _softmax_parts([s])
        outs.append(_dot(e.astype(BF16), v) / den)
    return outs


def _mla_ctx_kernel(cq_ref, ckv_ref, tail_ref, qg_ref, kg_ref, wuq_ref, wukv_ref, out_ref, ckvn_ref):
    q = _mla_q(cq_ref, qg_ref, wuq_ref)
    ckvn = _rms(ckv_ref[...], kg_ref[...])
    ckvn_ref[...] = ckvn
    kv = _dot(ckvn.astype(BF16), wukv_ref[...]).astype(BF16)
    kr = tail_ref[:, 0:MLA_D_ROPE].astype(BF16)
    outs = _mla_heads(q, kv, kr, SEQ)
    for h in range(MLA_HEADS):
        out_ref[:, h * MLA_D_V:(h + 1) * MLA_D_V] = outs[h]


MLA_D_V = GROUP_W // MLA_HEADS


def _mla_lat_kernel(cq_ref, ckv_ref, tail_ref, qg_ref, kg_ref, wuq_ref, wukv_ref, ckvc_ref, krc_ref,
                    cosq_ref, sinq_ref, cosk_ref, sink_ref, out_ref, kv_ref, kr_ref):
    @pl.when(pl.program_id(1) == 0)
    def _():
        ckvn = _rms(ckv_ref[...], kg_ref[...])
        kv_ref[0:DEC_SEQ, :] = _dot(ckvn.astype(BF16), wukv_ref[...]).astype(BF16)
        kv_ref[DEC_SEQ:N_KEYS_LAT, :] = _dot(ckvc_ref[...].astype(BF16), wukv_ref[...]).astype(BF16)
        kr = _rope(tail_ref[...], cosk_ref[...], sink_ref[...], MLA_D_ROPE // 4)
        kr_ref[0:DEC_SEQ, :] = kr[:, 0:MLA_D_ROPE].astype(BF16)
        kr_ref[DEC_SEQ:N_KEYS_LAT, :] = krc_ref[...].astype(BF16)

    q = _mla_q(cq_ref, qg_ref, wuq_ref)
    n0 = MLA_HEADS * MLA_D_NOPE
    q_rope = _rope(q[:, n0:], cosq_ref[...], sinq_ref[...], MLA_D_ROPE // 4)
    q = jnp.concatenate([q[:, :n0], q_rope], axis=1)
    outs = _mla_heads(q, kv_ref[...], kr_ref[...], N_KEYS_LAT)
    for h in range(MLA_HEADS):
        out_ref[:, h * MLA_D_V:(h + 1) * MLA_D_V] = outs[h]


def _w_specs2(shape_a, shape_b, l):
    return [pl.BlockSpec((None,) + shape_a, lambda *_: (l,) + (0,) * len(shape_a)),
            pl.BlockSpec((None,) + shape_b, lambda *_: (l,) + (0,) * len(shape_b))]


def _mla_ctx(proj, qg, kg, wuq, wukv, l):
    return pl.pallas_call(
        _mla_ctx_kernel,
        grid=(BATCH,),
        in_specs=[
            pl.BlockSpec((SEQ, 256), lambda b: (b, C_CQ // 256)),
            pl.BlockSpec((SEQ, 256), lambda b: (b, C_CKV // 256)),
            pl.BlockSpec((SEQ, 128), lambda b: (b, C_TAIL // 128)),
            pl.BlockSpec((None, 1, MLA_Q_RANK), lambda b: (l, 0, 0)),
            pl.BlockSpec((None, 1, MLA_KV_RANK), lambda b: (l, 0, 0)),
            pl.BlockSpec((None, MLA_Q_RANK, 768), lambda b: (l, 0, 0)),
            pl.BlockSpec((None, MLA_KV_RANK, 1024), lambda b: (l, 0, 0)),
        ],
        out_specs=(pl.BlockSpec((SEQ, GROUP_W), lambda b: (b, 0)),
                   pl.BlockSpec((SEQ, MLA_KV_RANK), lambda b: (b, 0))),
        out_shape=(jax.ShapeDtypeStruct((N_CTX, GROUP_W), F32),
                   jax.ShapeDtypeStruct((N_CTX, MLA_KV_RANK), F32)),
        compiler_params=_cparams("parallel"),
        name="mla_context",
    )(proj, proj, proj, qg, kg, wuq, wukv)


def _mla_lat(proj, qg, kg, wuq, wukv, cache_ckv, cache_kr, cos64, sin64, l):
    qb0 = N_CTX // ATT_TQ
    bb0 = N_CTX // DEC_SEQ
    cosq = jnp.tile(cos64, (1, MLA_HEADS))
    sinq = jnp.tile(sin64, (1, MLA_HEADS))
    cosk = jnp.tile(cos64, (1, 2))
    sink = jnp.tile(sin64, (1, 2))
    return pl.pallas_call(
        _mla_lat_kernel,
        grid=(DEC_BATCH, LAT_QB),
        in_specs=[
            pl.BlockSpec((ATT_TQ, 256), lambda b, i: (qb0 + b * LAT_QB + i, C_CQ // 256)),
            pl.BlockSpec((DEC_SEQ, 256), lambda b, i: (bb0 + b, C_CKV // 256)),
            pl.BlockSpec((DEC_SEQ, 128), lambda b, i: (bb0 + b, C_TAIL // 128)),
            pl.BlockSpec((None, 1, MLA_Q_RANK), lambda b, i: (l, 0, 0)),
            pl.BlockSpec((None, 1, MLA_KV_RANK), lambda b, i: (l, 0, 0)),
            pl.BlockSpec((None, MLA_Q_RANK, 768), lambda b, i: (l, 0, 0)),
            pl.BlockSpec((None, MLA_KV_RANK, 1024), lambda b, i: (l, 0, 0)),
            pl.BlockSpec((None, None, PAST_LEN, MLA_KV_RANK), lambda b, i: (b, l, 0, 0)),
            pl.BlockSpec((None, None, PAST_LEN, MLA_D_ROPE), lambda b, i: (b, l, 0, 0)),
            pl.BlockSpec((ATT_TQ, 256), lambda b, i: (i, 0)),
            pl.BlockSpec((ATT_TQ, 256), lambda b, i: (i, 0)),
            pl.BlockSpec((DEC_SEQ, 128), lambda b, i: (0, 0)),
            pl.BlockSpec((DEC_SEQ, 128), lambda b, i: (0, 0)),
        ],
        out_specs=pl.BlockSpec((ATT_TQ, GROUP_W), lambda b, i: (b * LAT_QB + i, 0)),
        out_shape=jax.ShapeDtypeStruct((N_LAT, GROUP_W), F32),
        scratch_shapes=[pltpu.VMEM((N_KEYS_LAT, 1024), BF16), pltpu.VMEM((N_KEYS_LAT, MLA_D_ROPE), BF16)],
        compiler_params=_cparams("parallel", "arbitrary"),
        name="mla_latent",
    )(proj, proj, proj, qg, kg, wuq, wukv, cache_ckv, cache_kr, cosq, sinq, cosk, sink)


SWA_SCALE = SWA_DH ** -0.5
SWA_REP = SWA_HEADS // SWA_KV_HEADS
SWA_KWIN = ATT_TQ + 2 * WINDOW


def _swa_ctx_kernel(sink_ref, q_ref, k_ref, v_ref, out_ref):
    kb = k_ref[...].astype(BF16)
    vb = v_ref[...].astype(BF16)
    for h in range(SWA_HEADS):
        g = h // SWA_REP
        q = q_ref[:, h * SWA_DH:(h + 1) * SWA_DH].astype(BF16)
        s = _dot_nt(q, kb[:, g * SWA_DH:(g + 1) * SWA_DH]) * SWA_SCALE
        (e,), den = _softmax_parts([s], sink=sink_ref[0, h])
        out_ref[:, h * SWA_DH:(h + 1) * SWA_DH] = _dot(e.astype(BF16), vb[:, g * SWA_DH:(g + 1) * SWA_DH]) / den


def _swa_lat_kernel(sink_ref, q_ref, k_ref, v_ref, kc_ref, vc_ref, cosq_ref, sinq_ref, cosk_ref, sink_t_ref,
                    out_ref, kr_ref):
    i = pl.program_id(1)

    @pl.when(i == 0)
    def _():
        kr_ref[...] = _rope(k_ref[...], cosk_ref[...], sink_t_ref[...], SWA_DH // 4).astype(BF16)

    q_all = _rope(q_ref[...], cosq_ref[...], sinq_ref[...], SWA_DH // 4)
    k0 = pl.multiple_of(jnp.clip(i * ATT_TQ - WINDOW, 0, DEC_SEQ - SWA_KWIN), WINDOW)
    kwin = kr_ref[pl.ds(k0, SWA_KWIN), :]
    vwin = v_ref[pl.ds(k0, SWA_KWIN), :].astype(BF16)
    kc = kc_ref[...].astype(BF16)
    vc = vc_ref[...].astype(BF16)
    qpos = i * ATT_TQ + lax.broadcasted_iota(jnp.int32, (ATT_TQ, SWA_KWIN), 0)
    kpos = k0 + lax.broadcasted_iota(jnp.int32, (ATT_TQ, SWA_KWIN), 1)
    band = jnp.abs(qpos - kpos) <= WINDOW
    for h in range(SWA_HEADS):
        g = h // SWA_REP
        gl = slice(g * SWA_DH, (g + 1) * SWA_DH)
        q = q_all[:, h * SWA_DH:(h + 1) * SWA_DH].astype(BF16)
        s_loc = jnp.where(band, _dot_nt(q, kwin[:, gl]) * SWA_SCALE, NEG)
        s_ctx = _dot_nt(q, kc[:, gl]) * SWA_SCALE
        (e_loc, e_ctx), den = _softmax_parts([s_loc, s_ctx], sink=sink_ref[0, h])
        o = _dot(e_loc.astype(BF16), vwin[:, gl]) + _dot(e_ctx.astype(BF16), vc[:, gl])
        out_ref[:, h * SWA_DH:(h + 1) * SWA_DH] = o / den


def _smem_spec():
    return pl.BlockSpec(memory_space=pltpu.SMEM)


def _swa_ctx(proj, sink, l):
    return pl.pallas_call(
        _swa_ctx_kernel,
        grid=(BATCH,),
        in_specs=[
            _smem_spec(),
            pl.BlockSpec((SEQ, 512), lambda b: (b, C_SQ // 512)),
            pl.BlockSpec((SEQ, 256), lambda b: (b, C_SK // 256)),
            pl.BlockSpec((SEQ, 256), lambda b: (b, C_SV // 256)),
        ],
        out_specs=pl.BlockSpec((SEQ, GROUP_W), lambda b: (b, 0)),
        out_shape=jax.ShapeDtypeStruct((N_CTX, GROUP_W), F32),
        compiler_params=_cparams("parallel"),
        name="swa_context",
    )(sink, proj, proj, proj)


def _swa_lat(proj, sink, cache_k, cache_v, cos128, sin128, l):
    qb0 = N_CTX // ATT_TQ
    bb0 = N_CTX // DEC_SEQ
    kvw = SWA_KV_HEADS * SWA_DH
    return pl.pallas_call(
        _swa_lat_kernel,
        grid=(DEC_BATCH, LAT_QB),
        in_specs=[
            _smem_spec(),
            pl.BlockSpec((ATT_TQ, 512), lambda b, i: (qb0 + b * LAT_QB + i, C_SQ // 512)),
            pl.BlockSpec((DEC_SEQ, 256), lambda b, i: (bb0 + b, C_SK // 256)),
            pl.BlockSpec((DEC_SEQ, 256), lambda b, i: (bb0 + b, C_SV // 256)),
            pl.BlockSpec((None, None, PAST_LEN, kvw), lambda b, i: (b, l, 0, 0)),
            pl.BlockSpec((None, None, PAST_LEN, kvw), lambda b, i: (b, l, 0, 0)),
            pl.BlockSpec((ATT_TQ, 512), lambda b, i: (i, 0)),
            pl.BlockSpec((ATT_TQ, 512), lambda b, i: (i, 0)),
            pl.BlockSpec((DEC_SEQ, 256), lambda b, i: (0, 0)),
            pl.BlockSpec((DEC_SEQ, 256), lambda b, i: (0, 0)),
        ],
        out_specs=pl.BlockSpec((ATT_TQ, GROUP_W), lambda b, i: (b * LAT_QB + i, 0)),
        out_shape=jax.ShapeDtypeStruct((N_LAT, GROUP_W), F32),
        scratch_shapes=[pltpu.VMEM((DEC_SEQ, kvw), BF16)],
        compiler_params=_cparams("parallel", "arbitrary"),
        name="swa_latent",
    )(sink, proj, proj, proj, cache_k, cache_v,
      jnp.tile(cos128, (1, SWA_HEADS)), jnp.tile(sin128, (1, SWA_HEADS)),
      jnp.tile(cos128, (1, SWA_KV_HEADS)), jnp.tile(sin128, (1, SWA_KV_HEADS)))


DIFF_SCALE = DIFF_DH ** -0.5


def _diff_lambda(lq1_ref, lk1_ref, lq2_ref, lk2_ref):
    a = jnp.sum(lq1_ref[...] * lk1_ref[...], axis=1, keepdims=True)
    b = jnp.sum(lq2_ref[...] * lk2_ref[...], axis=1, keepdims=True)
    return jnp.exp(a) - jnp.exp(b)


def _diff_heads(q, k_parts, v_parts, lam, lam_init, ng_ref, out_ref):
    for h in range(DIFF_HEADS):
        ps = []
        for c in range(2):
            sl = slice(h * 2 * DIFF_DH + c * DIFF_DH, h * 2 * DIFF_DH + (c + 1) * DIFF_DH)
            qc = q[:, sl].astype(BF16)
            es, den = _softmax_parts([_dot_nt(qc, kp[:, sl]) * DIFF_SCALE for kp in k_parts])
            ps.append([e / den for e in es])
        vl = slice(h * 2 * DIFF_DH, (h + 1) * 2 * DIFF_DH)
        o = None
        for p1, p2, vp in zip(ps[0], ps[1], v_parts):
            t = _dot((p1 - lam * p2).astype(BF16), vp[:, vl])
            o = t if o is None else o + t
        out_ref[:, vl] = _rms(o, ng_ref[...]) * (1.0 - lam_init)


def _diff_ctx_kernel(q_ref, k_ref, v_ref, lq1_ref, lk1_ref, lq2_ref, lk2_ref, ng_ref, out_ref, *, lam_init):
    lam = _diff_lambda(lq1_ref, lk1_ref, lq2_ref, lk2_ref) + lam_init
    _diff_heads(q_ref[...], [k_ref[...].astype(BF16)], [v_ref[...].astype(BF16)], lam, lam_init, ng_ref, out_ref)


def _diff_lat_kernel(q_ref, k_ref, v_ref, kc_ref, vc_ref, lq1_ref, lk1_ref, lq2_ref, lk2_ref, ng_ref,
                     cosq_ref, sinq_ref, cosk_ref, sink_ref, out_ref, kr_ref, *, lam_init):
    @pl.when(pl.program_id(1) == 0)
    def _():
        kr_ref[...] = _rope(k_ref[...], cosk_ref[...], sink_ref[...], DIFF_DH // 4).astype(BF16)

    lam = _diff_lambda(lq1_ref, lk1_ref, lq2_ref, lk2_ref) + lam_init
    q = _rope(q_ref[...], cosq_ref[...], sinq_ref[...], DIFF_DH // 4)
    _diff_heads(q, [kr_ref[...], kc_ref[...].astype(BF16)], [v_ref[...].astype(BF16), vc_ref[...].astype(BF16)],
                lam, lam_init, ng_ref, out_ref)


def _vec_specs(n, width, l, nargs):
    return [pl.BlockSpec((None, 1, width), lambda *_: (l, 0, 0)) for _ in range(n)]


def _diff_ctx(proj, lq1, lk1, lq2, lk2, ng, l):
    lam_init = 0.8 - 0.6 * math.exp(-0.3 * l)
    return pl.pallas_call(
        functools.partial(_diff_ctx_kernel, lam_init=lam_init),
        grid=(BATCH,),
        in_specs=[
            pl.BlockSpec((SEQ, 512), lambda b: (b, C_DQ // 512)),
            pl.BlockSpec((SEQ, 512), lambda b: (b, C_DK // 512)),
            pl.BlockSpec((SEQ, 512), lambda b: (b, C_DV // 512)),
        ] + _vec_specs(4, DIFF_DH, l, 1) + _vec_specs(1, 2 * DIFF_DH, l, 1),
        out_specs=pl.BlockSpec((SEQ, GROUP_W), lambda b: (b, 0)),
        out_shape=jax.ShapeDtypeStruct((N_CTX, GROUP_W), F32),
        compiler_params=_cparams("parallel"),
        name="diff_context",
    )(proj, proj, proj, lq1, lk1, lq2, lk2, ng)


def _diff_lat(proj, lq1, lk1, lq2, lk2, ng, cache_k, cache_v, cos64, sin64, l):
    lam_init = 0.8 - 0.6 * math.exp(-0.3 * l)
    qb0 = N_CTX // ATT_TQ
    bb0 = N_CTX // DEC_SEQ
    cos_t = jnp.tile(cos64, (1, 2 * DIFF_HEADS))
    sin_t = jnp.tile(sin64, (1, 2 * DIFF_HEADS))
    return pl.pallas_call(
        functools.partial(_diff_lat_kernel, lam_init=lam_init),
        grid=(DEC_BATCH, LAT_QB),
        in_specs=[
            pl.BlockSpec((ATT_TQ, 512), lambda b, i: (qb0 + b * LAT_QB + i, C_DQ // 512)),
            pl.BlockSpec((DEC_SEQ, 512), lambda b, i: (bb0 + b, C_DK // 512)),
            pl.BlockSpec((DEC_SEQ, 512), lambda b, i: (bb0 + b, C_DV // 512)),
            pl.BlockSpec((None, None, PAST_LEN, GROUP_W), lambda b, i: (b, l, 0, 0)),
            pl.BlockSpec((None, None, PAST_LEN, GROUP_W), lambda b, i: (b, l, 0, 0)),
        ] + _vec_specs(4, DIFF_DH, l, 2) + _vec_specs(1, 2 * DIFF_DH, l, 2) + [
            pl.BlockSpec((ATT_TQ, 512), lambda b, i: (i, 0)),
            pl.BlockSpec((ATT_TQ, 512), lambda b, i: (i, 0)),
            pl.BlockSpec((DEC_SEQ, 512), lambda b, i: (0, 0)),
            pl.BlockSpec((DEC_SEQ, 512), lambda b, i: (0, 0)),
        ],
        out_specs=pl.BlockSpec((ATT_TQ, GROUP_W), lambda b, i: (b * LAT_QB + i, 0)),
        out_shape=jax.ShapeDtypeStruct((N_LAT, GROUP_W), F32),
        scratch_shapes=[pltpu.VMEM((DEC_SEQ, GROUP_W), BF16)],
        compiler_params=_cparams("parallel", "arbitrary"),
        name="diff_latent",
    )(proj, proj, proj, cache_k, cache_v, lq1, lk1, lq2, lk2, ng, cos_t, sin_t, cos_t, sin_t)


OUT_TM = 256


def _out_kernel(m0_ref, m1_ref, m2_ref, m3_ref, x_ref, mod_ref, w_ref, x1_ref):
    acc = None
    for g, m_ref in enumerate((m0_ref, m1_ref, m2_ref, m3_ref)):
        t = _dot(m_ref[...].astype(BF16), w_ref[g * GROUP_W:(g + 1) * GROUP_W, :])
        acc = t if acc is None else acc + t
    x1_ref[...] = x_ref[...] + mod_ref[2:3, :] * acc


def _out_proj(mixed, x, mod4, w_out, l):
    row = lambda i: (i, 0)
    return pl.pallas_call(
        _out_kernel,
        grid=(NT // OUT_TM,),
        in_specs=[pl.BlockSpec((OUT_TM, GROUP_W), row) for _ in range(4)] + [
            pl.BlockSpec((OUT_TM, D_MODEL), row),
            pl.BlockSpec((None, None, 6, D_MODEL), lambda i: (l, _mod_set(i * OUT_TM), 0, 0)),
            pl.BlockSpec((None, D_MODEL, D_MODEL), lambda i: (l, 0, 0)),
        ],
        out_specs=pl.BlockSpec((OUT_TM, D_MODEL), row),
        out_shape=jax.ShapeDtypeStruct((NT, D_MODEL), F32),
        compiler_params=_cparams("parallel"),
        name="out_proj_residual",
    )(*mixed, x, mod4, w_out)


def _peerq_kernel(x_ref, mod_ref, g_ref, w_ref, ht_ref, q_ref):
    h = _rms(x_ref[...], g_ref[...]) * (1.0 + mod_ref[4:5, :]) + mod_ref[3:4, :]
    ht_ref[...] = h.T.astype(BF16)
    q_ref[...] = _dot(h.astype(BF16), w_ref[...])


def _peer_query(x1, mod4, norm_g, w_q, l):
    row = lambda i: (i, 0)
    return pl.pallas_call(
        _peerq_kernel,
        grid=(NT // OUT_TM,),
        in_specs=[
            pl.BlockSpec((OUT_TM, D_MODEL), row),
            pl.BlockSpec((None, None, 6, D_MODEL), lambda i: (l, _mod_set(i * OUT_TM), 0, 0)),
            pl.BlockSpec((None, 1, D_MODEL), lambda i: (l, 0, 0)),
            pl.BlockSpec((None, D_MODEL, PEER_HEADS * PEER_QDIM), lambda i: (l, 0, 0)),
        ],
        out_specs=(pl.BlockSpec((D_MODEL, OUT_TM), lambda i: (0, i)),
                   pl.BlockSpec((OUT_TM, PEER_HEADS * PEER_QDIM), row)),
        out_shape=(jax.ShapeDtypeStruct((D_MODEL, NT), BF16),
                   jax.ShapeDtypeStruct((NT, PEER_HEADS * PEER_QDIM), F32)),
        compiler_params=_cparams("parallel"),
        name="adaln_peer_query",
    )(x1, mod4, norm_g, w_q)


ROUTE_TL = 256
NOT_SEL = float(PEER_TOPK)


def _top16(s):
    idx = lax.broadcasted_iota(jnp.int32, s.shape, 0).astype(F32)
    slot = lax.broadcasted_iota(jnp.int32, (PEER_TOPK, s.shape[1]), 0)
    rank = jnp.full(s.shape, NOT_SEL, F32)
    vals = jnp.zeros((PEER_TOPK, s.shape[1]), F32)
    for k in range(PEER_TOPK):
        m = jnp.max(s, axis=0, keepdims=True)
        first = jnp.min(jnp.where(s == m, idx, float(PEER_NKEYS)), axis=0, keepdims=True)
        sel = idx == first
        rank = jnp.where(sel, float(k), rank)
        s = jnp.where(sel, NEG, s)
        vals = jnp.where(slot == k, m, vals)
    return rank, vals


CAND_HALF = PEER_TOPK // 2
CAND_ROWS = PEER_TOPK + (CAND_HALF - 1) * CAND_HALF + CAND_HALF
FLAT_NONE = float(PEER_TOPK * PEER_TOPK)


def _cand_flat(tl):
    r = lax.broadcasted_iota(jnp.int32, (CAND_ROWS, tl), 0)
    mid = r - PEER_TOPK
    mid_flat = (1 + mid // CAND_HALF) * PEER_TOPK + mid % CAND_HALF
    last_flat = (CAND_HALF + r - (CAND_ROWS - CAND_HALF)) * PEER_TOPK
    flat = jnp.where(r < PEER_TOPK, r, jnp.where(r < CAND_ROWS - CAND_HALF, mid_flat, last_flat))
    return flat.astype(F32)


def _route_kernel(q_ref, k1_ref, k2_ref, rank2_ref, e2_ref, cnt1_ref, e1_ref):
    tl = q_ref.shape[0]
    flat = _cand_flat(tl)
    for h in range(PEER_HEADS):
        half = PEER_QDIM // 2
        q1 = q_ref[:, h * PEER_QDIM:h * PEER_QDIM + half].astype(BF16)
        q2 = q_ref[:, h * PEER_QDIM + half:(h + 1) * PEER_QDIM].astype(BF16)
        s1 = _dot_nt(k1_ref[...], q1)
        s2 = _dot_nt(k2_ref[...], q2)
        rank1, v1 = _top16(s1)
        rank2, v2 = _top16(s2)
        slabs = [v1[0:1, :] + v2]
        slabs += [v1[a:a + 1, :] + v2[0:CAND_HALF, :] for a in range(1, CAND_HALF)]
        slabs.append(v1[CAND_HALF:, :] + v2[0:1, :])
        cand = jnp.concatenate(slabs, axis=0)
        top = v1[0:1, :] + v2[0:1, :]
        cnt1 = jnp.zeros(s1.shape, F32)
        z = jnp.zeros((1, tl), F32)
        for k in range(PEER_TOPK):
            m = jnp.max(cand, axis=0, keepdims=True)
            first = jnp.min(jnp.where(cand == m, flat, FLAT_NONE), axis=0, keepdims=True)
            cand = jnp.where(flat == first, NEG, cand)
            a_sel = jnp.floor(first * (1.0 / PEER_TOPK))
            cnt1 = cnt1 + jnp.where(rank1 == a_sel, 1.0, 0.0)
            z = z + jnp.exp(m - top)
        rank2_ref[h] = rank2.astype(BF16)
        cnt1_ref[h] = cnt1
        e1_ref[h] = jnp.exp(s1 - v1[0:1, :]) / z
        e2_ref[h] = jnp.exp(s2 - v2[0:1, :]).astype(BF16)


def _peer_route(q, keys, l):
    shp = jax.ShapeDtypeStruct((PEER_HEADS, PEER_NKEYS, NT), F32)
    shp_b = jax.ShapeDtypeStruct((PEER_HEADS, PEER_NKEYS, NT), BF16)
    spec = pl.BlockSpec((PEER_HEADS, PEER_NKEYS, ROUTE_TL), lambda i: (0, 0, i))
    half = PEER_QDIM // 2
    return pl.pallas_call(
        _route_kernel,
        grid=(NT // ROUTE_TL,),
        in_specs=[
            pl.BlockSpec((ROUTE_TL, PEER_HEADS * PEER_QDIM), lambda i: (i, 0)),
            pl.BlockSpec((None, None, PEER_NKEYS, half), lambda i: (l, 0, 0, 0)),
            pl.BlockSpec((None, None, PEER_NKEYS, half), lambda i: (l, 1, 0, 0)),
        ],
        out_specs=(spec, spec, spec, spec),
        out_shape=(shp_b, shp_b, shp, shp),
        compiler_params=_cparams("parallel"),
        name="peer_route",
    )(q, keys, keys)


PEER_TT = 512
PEER_EB = 512
GELU_C = math.sqrt(2.0 / math.pi)


def _gelu_tanh(x):
    return 0.5 * x * (1.0 + jnp.tanh(GELU_C * (x + 0.044715 * (x * x * x))))


PEER_NE = PEER_N // PEER_EB
PEER_MT = 256


def _peer_kernel(ht_ref, u_ref, vt_ref, rank2_ref, e2_ref, cnt1_ref, e1_ref, x_ref, mod_ref, o_ref,
                 ga_ref, gb_ref, acc_ref):
    e = pl.program_id(1)

    @pl.when(e == 0)
    def _():
        acc_ref[...] = jnp.zeros_like(acc_ref)
        gb_ref[...] = jnp.zeros_like(gb_ref)

    def step(g_prev_ref, g_next_ref):
        blk = jnp.maximum(e - 1, 0)
        n_i = PEER_EB // PEER_NKEYS
        per_tile = PEER_MT // PEER_NKEYS
        for tc in range(PEER_TT // PEER_MT):
            tok = slice(tc * PEER_MT, (tc + 1) * PEER_MT)
            g_next_ref[:, tok] = _gelu_tanh(_dot(u_ref[...], ht_ref[:, tok])).astype(BF16)
        for tc in range(PEER_TT // PEER_MT):
            tok = slice(tc * PEER_MT, (tc + 1) * PEER_MT)
            acc = None
            for kc in range(PEER_EB // PEER_MT):
                tiles = []
                for ii in range(kc * per_tile, (kc + 1) * per_tile):
                    key1 = blk * n_i + ii
                    w = jnp.zeros((PEER_NKEYS, PEER_MT), BF16)
                    for h in range(PEER_HEADS):
                        cnt = cnt1_ref[h, pl.ds(key1, 1), tok].astype(BF16)
                        g1 = e1_ref[h, pl.ds(key1, 1), tok].astype(BF16)
                        w = w + jnp.where(rank2_ref[h, :, tok] < cnt, e2_ref[h, :, tok] * g1, 0.0)
                    tiles.append(w * g_prev_ref[ii * PEER_NKEYS:(ii + 1) * PEER_NKEYS, tok])
                t = _dot(vt_ref[:, kc * PEER_MT:(kc + 1) * PEER_MT], jnp.concatenate(tiles, axis=0))
                acc = t if acc is None else acc + t
            acc_ref[:, tok] += acc

    @pl.when(e % 2 == 0)
    def _():
        step(gb_ref, ga_ref)

    @pl.when(e % 2 == 1)
    def _():
        step(ga_ref, gb_ref)

    @pl.when(e == PEER_NE)
    def _():
        o_ref[...] = x_ref[...] + mod_ref[5:6, :] * acc_ref[...].T


def _peer(ht, u_tab, vt_tab, routing, x1, mod4, l):
    rank2, e2, cnt1, e1 = routing
    rspec = pl.BlockSpec((PEER_HEADS, PEER_NKEYS, PEER_TT), lambda i, e: (0, 0, i))
    return pl.pallas_call(
        _peer_kernel,
        grid=(NT // PEER_TT, PEER_NE + 1),
        in_specs=[
            pl.BlockSpec((D_MODEL, PEER_TT), lambda i, e: (0, i)),
            pl.BlockSpec((None, PEER_EB, D_MODEL), lambda i, e: (l, jnp.minimum(e, PEER_NE - 1), 0)),
            pl.BlockSpec((None, D_MODEL, PEER_EB), lambda i, e: (l, 0, jnp.maximum(e - 1, 0))),
            rspec, rspec, rspec, rspec,
            pl.BlockSpec((PEER_TT, D_MODEL), lambda i, e: (i, 0)),
            pl.BlockSpec((None, None, 6, D_MODEL), lambda i, e: (l, _mod_set(i * PEER_TT), 0, 0)),
        ],
        out_specs=pl.BlockSpec((PEER_TT, D_MODEL), lambda i, e: (i, 0)),
        out_shape=jax.ShapeDtypeStruct((NT, D_MODEL), F32),
        scratch_shapes=[pltpu.VMEM((PEER_EB, PEER_TT), BF16), pltpu.VMEM((PEER_EB, PEER_TT), BF16),
                        pltpu.VMEM((D_MODEL, PEER_TT), F32)],
        compiler_params=_cparams("parallel", "arbitrary"),
        name="peer_experts",
    )(ht, u_tab, vt_tab, rank2, e2, cnt1, e1, x1, mod4)


FIN_TM = 512


def _final_kernel(x_ref, g_ref, o_ref):
    o_ref[...] = _rms(x_ref[...], g_ref[...])


def _final_norm(x, g):
    return pl.pallas_call(
        _final_kernel,
        grid=(NT // FIN_TM,),
        in_specs=[pl.BlockSpec((FIN_TM, D_MODEL), lambda i: (i, 0)),
                  pl.BlockSpec((1, D_MODEL), lambda i: (0, 0))],
        out_specs=pl.BlockSpec((FIN_TM, D_MODEL), lambda i: (i, 0)),
        out_shape=jax.ShapeDtypeStruct((NT, D_MODEL), F32),
        compiler_params=_cparams("parallel"),
        name="final_norm",
    )(x, g)


def _permute_w_in(w_in):
    sizes = (GROUP_W, GROUP_W, GROUP_W, GROUP_W, 4 * ML_HEADS, MLA_Q_RANK, MLA_KV_RANK, MLA_D_ROPE,
             SWA_HEADS * SWA_DH, SWA_KV_HEADS * SWA_DH, SWA_KV_HEADS * SWA_DH, GROUP_W, GROUP_W, GROUP_W)
    offs = [0]
    for s in sizes:
        offs.append(offs[-1] + s)
    part = lambda i: w_in[:, :, offs[i]:offs[i + 1]]
    order = [0, 1, 2, 3, 5, 6, 8, 9, 10, 11, 12, 13, 7, 4]
    cols = [part(i) for i in order]
    used = sum(sizes)
    cols.append(jnp.zeros(w_in.shape[:2] + (PROJ_W - used,), w_in.dtype))
    return jnp.concatenate(cols, axis=-1).astype(BF16)


def _permute_w_uq(w_uq):
    w = w_uq.reshape(DEPTH, MLA_Q_RANK, MLA_HEADS, MLA_D_NOPE + MLA_D_ROPE)
    nope = w[..., :MLA_D_NOPE].reshape(DEPTH, MLA_Q_RANK, MLA_HEADS * MLA_D_NOPE)
    rope = w[..., MLA_D_NOPE:].reshape(DEPTH, MLA_Q_RANK, MLA_HEADS * MLA_D_ROPE)
    return jnp.concatenate([nope, rope], axis=-1).astype(BF16)


def kernel(x_prompt, x_sample, c, cache_mla_ckv, cache_mla_krope, cache_swa_k, cache_swa_v, cache_diff_k,
           cache_diff_v, state_mlstm_C, state_mlstm_n, state_mlstm_m, c_ctx, w_mod, b_mod, norm1_g, w_in,
           mlstm_i_bias, mlstm_f_bias, mlstm_norm_g, mla_qnorm_g, mla_w_uq, mla_kvnorm_g, mla_w_ukv, swa_sink,
           diff_lq1, diff_lk1, diff_lq2, diff_lk2, diff_norm_g, w_out, norm2_g, peer_w_q, peer_sub_keys,
           peer_u, peer_v, final_norm_g):
    x = jnp.concatenate([x_prompt.reshape(N_CTX, D_MODEL), x_sample.reshape(N_LAT, D_MODEL)], axis=0)

    cvec = jnp.concatenate([c_ctx[None, :], c], axis=0)
    cvec_t = jnp.pad(cvec.T, ((0, 0), (0, 8 - N_SETS)))
    mod4 = _modulation(cvec_t, w_mod, b_mod).reshape(DEPTH, 8, 6, D_MODEL)

    w_in_p = _permute_w_in(w_in)
    w_uq_p = _permute_w_uq(mla_w_uq)
    w_ukv_b = mla_w_ukv.astype(BF16)
    w_out_b = w_out.astype(BF16)
    w_q_b = peer_w_q.astype(BF16)
    keys_b = peer_sub_keys.astype(BF16)
    u_b = peer_u.astype(BF16)
    vt_b = jnp.swapaxes(peer_v, 1, 2).astype(BF16)

    vec3 = lambda a: a.reshape(DEPTH, 1, a.shape[-1])
    norm1_3, norm2_3 = vec3(norm1_g), vec3(norm2_g)
    mlng_3, qg_3, kg_3, dng_3 = vec3(mlstm_norm_g), vec3(mla_qnorm_g), vec3(mla_kvnorm_g), vec3(diff_norm_g)
    lq1_3, lk1_3, lq2_3, lk2_3 = vec3(diff_lq1), vec3(diff_lk1), vec3(diff_lq2), vec3(diff_lk2)

    n_chain = 2 * ML_HEADS
    st_c = state_mlstm_C.reshape(DEC_BATCH, DEPTH, n_chain, ML_DH, ML_DH)
    st_n = state_mlstm_n.reshape(DEC_BATCH, DEPTH, n_chain, ML_DH)
    st_m = jnp.broadcast_to(state_mlstm_m.reshape(DEC_BATCH, DEPTH, n_chain, 1), (DEC_BATCH, DEPTH, n_chain, ML_DH))
    swk_c = cache_swa_k.reshape(DEC_BATCH, DEPTH, PAST_LEN, SWA_KV_HEADS * SWA_DH)
    swv_c = cache_swa_v.reshape(DEC_BATCH, DEPTH, PAST_LEN, SWA_KV_HEADS * SWA_DH)
    dfk_c = cache_diff_k.reshape(DEC_BATCH, DEPTH, PAST_LEN, GROUP_W)
    dfv_c = cache_diff_v.reshape(DEC_BATCH, DEPTH, PAST_LEN, GROUP_W)

    cos64, sin64 = _rope_tables(DEC_SEQ, 64)
    cos128, sin128 = _rope_tables(DEC_SEQ, 128)

    outs = [[] for _ in range(9)]
    for l in range(DEPTH):
        proj = _project(x, mod4, norm1_3, w_in_p, l)

        gates = proj[:, C_TAIL + TAIL_G:C_TAIL + TAIL_G + 16]
        gt = gates.reshape(NT // ML_CHUNK, ML_CHUNK, 16).transpose(0, 2, 1)
        bias = jnp.concatenate([mlstm_i_bias[l].reshape(-1), mlstm_f_bias[l].reshape(-1)])
        bcol = bias.reshape(16, 1)
        brow = jnp.zeros((1, 128), F32).at[0, TAIL_G:TAIL_G + 16].set(bias)

        ml_ctx, c_st, n_st, m_st = _mlstm(proj, gt, bcol, brow, mlng_3, l, latent=False)
        ml_lat = _mlstm(proj, gt, bcol, brow, mlng_3, l, latent=True, states=(st_c, st_n, st_m))
        mla_ctx, ckv_n = _mla_ctx(proj, qg_3, kg_3, w_uq_p, w_ukv_b, l)
        mla_lat = _mla_lat(proj, qg_3, kg_3, w_uq_p, w_ukv_b, cache_mla_ckv, cache_mla_krope, cos64, sin64, l)
        sink = swa_sink[l].reshape(1, SWA_HEADS)
        swa_ctx = _swa_ctx(proj, sink, l)
        swa_lat = _swa_lat(proj, sink, swk_c, swv_c, cos128, sin128, l)
        df_ctx = _diff_ctx(proj, lq1_3, lk1_3, lq2_3, lk2_3, dng_3, l)
        df_lat = _diff_lat(proj, lq1_3, lk1_3, lq2_3, lk2_3, dng_3, dfk_c, dfv_c, cos64, sin64, l)

        mixed = [jnp.concatenate([a, b], axis=0) for a, b in
                 ((ml_ctx, ml_lat), (mla_ctx, mla_lat), (swa_ctx, swa_lat), (df_ctx, df_lat))]
        x1 = _out_proj(mixed, x, mod4, w_out_b, l)
        h2t, q = _peer_query(x1, mod4, norm2_3, w_q_b, l)
        routing = _peer_route(q, keys_b, l)
        x = _peer(h2t, u_b, vt_b, routing, x1, mod4, l)

        pc = proj[:N_CTX]
        outs[0].append(ckv_n.reshape(BATCH, SEQ, MLA_KV_RANK))
        outs[1].append(pc[:, C_TAIL:C_TAIL + MLA_D_ROPE].reshape(BATCH, SEQ, MLA_D_ROPE))
        outs[2].append(pc[:, C_SK:C_SK + 256].reshape(BATCH, SEQ, SWA_KV_HEADS, SWA_DH))
        outs[3].append(pc[:, C_SV:C_SV + 256].reshape(BATCH, SEQ, SWA_KV_HEADS, SWA_DH))
        outs[4].append(pc[:, C_DK:C_DK + 512].reshape(BATCH, SEQ, DIFF_HEADS, 2 * DIFF_DH))
        outs[5].append(pc[:, C_DV:C_DV + 512].reshape(BATCH, SEQ, DIFF_HEADS, 2 * DIFF_DH))
        outs[6].append(c_st.reshape(BATCH, 2, ML_HEADS, ML_DH, ML_DH))
        outs[7].append(n_st.reshape(BATCH, 2, ML_HEADS, ML_DH))
        outs[8].append(m_st[:, :, 0].reshape(BATCH, 2, ML_HEADS))

    y = _final_norm(x, final_norm_g.reshape(1, D_MODEL))
    y_prompt = y[:N_CTX].reshape(BATCH, SEQ, D_MODEL)
    y_sample = y[N_CTX:].reshape(DEC_BATCH, DEC_SEQ, D_MODEL)
    return (y_prompt, y_sample) + tuple(jnp.stack(o, axis=1) for o in outs)
```

```python
import functools
import math

import jax
import jax.numpy as jnp
from jax import lax
from jax.experimental import pallas as pl
from jax.experimental.pallas import tpu as pltpu

F32 = jnp.float32
BF16 = jnp.bfloat16

D_MODEL = 2048
BATCH = 32
SEQ = 256
DEPTH = 4
DEC_BATCH = 2
DEC_SEQ = 1024
PAST_LEN = 256
GRID_W = 64
GROUP_W = D_MODEL // 4
ML_HEADS = 4
ML_DH = GROUP_W // ML_HEADS
ML_CHUNK = 64
MLA_HEADS = 4
MLA_D_NOPE = GROUP_W // MLA_HEADS
MLA_D_ROPE = 64
MLA_Q_RANK = D_MODEL // 8
MLA_KV_RANK = D_MODEL // 8
MLA_SCALE = (MLA_D_NOPE + MLA_D_ROPE) ** -0.5
SWA_HEADS = 4
SWA_KV_HEADS = 2
SWA_DH = GROUP_W // SWA_HEADS
WINDOW = 128
DIFF_HEADS = 4
DIFF_DH = GROUP_W // (2 * DIFF_HEADS)
PEER_HEADS = 8
PEER_QDIM = 256
PEER_NKEYS = 128
PEER_N = PEER_NKEYS * PEER_NKEYS
PEER_TOPK = 16
ROPE_BASE = 10000.0
EPS = 1e-6
NEG = -1e30

N_CTX = BATCH * SEQ
N_LAT = DEC_BATCH * DEC_SEQ
NT = N_CTX + N_LAT
N_SETS = 1 + DEC_BATCH

C_MLQ, C_MLK, C_MLV, C_MLO = 0, 512, 1024, 1536
C_CQ, C_CKV = 2048, 2304
C_SQ, C_SK, C_SV = 2560, 3072, 3328
C_DQ, C_DK, C_DV = 3584, 4096, 4608
C_TAIL = 5120
TAIL_G = 64
PROJ_W = 5376
PROJ_TN = 896

VMEM_LIMIT = 56 * 1024 * 1024


def _cparams(*sem):
    return pltpu.CompilerParams(dimension_semantics=sem, vmem_limit_bytes=VMEM_LIMIT)


def _mod_set(row_start):
    return jnp.where(row_start >= N_CTX, (row_start - N_CTX) // DEC_SEQ + 1, 0)


def _rms(x, g):
    return x * lax.rsqrt(jnp.mean(x * x, axis=-1, keepdims=True) + EPS) * g


def _dot(a, b):
    return jnp.dot(a, b, preferred_element_type=F32)


def _dot_nt(a, b):
    return lax.dot_general(a, b, (((1,), (1,)), ((), ())), preferred_element_type=F32)


def _dot_tn(a, b):
    return lax.dot_general(a, b, (((0,), (0,)), ((), ())), preferred_element_type=F32)


def _dot_hi(a, b):
    return jnp.dot(a, b, preferred_element_type=F32, precision=lax.Precision.HIGHEST)


MOD_TK = 512
MOD_TN = 2048


def _mod_kernel(c_ref, w_ref, b_ref, o_ref, acc_ref):
    k = pl.program_id(2)

    @pl.when(k == 0)
    def _():
        acc_ref[...] = jnp.zeros_like(acc_ref)

    w = w_ref[...]
    for r in range(N_SETS):
        cv = c_ref[:, r:r + 1]
        sv = cv * jax.nn.sigmoid(cv)
        acc_ref[r] += (sv * w).reshape(MOD_TK // 8, 8, MOD_TN).sum(axis=0)

    @pl.when(k == pl.num_programs(2) - 1)
    def _():
        o_ref[...] = jnp.zeros_like(o_ref)
        for r in range(N_SETS):
            o_ref[r:r + 1, :] = acc_ref[r].sum(axis=0, keepdims=True) + b_ref[...]


def _modulation(cvec_t, w_mod, b_mod):
    return pl.pallas_call(
        _mod_kernel,
        grid=(DEPTH, 6 * D_MODEL // MOD_TN, D_MODEL // MOD_TK),
        in_specs=[
            pl.BlockSpec((MOD_TK, 8), lambda l, n, k: (k, 0)),
            pl.BlockSpec((None, MOD_TK, MOD_TN), lambda l, n, k: (l, k, n)),
            pl.BlockSpec((None, 1, MOD_TN), lambda l, n, k: (l, 0, n)),
        ],
        out_specs=pl.BlockSpec((None, 8, MOD_TN), lambda l, n, k: (l, 0, n)),
        out_shape=jax.ShapeDtypeStruct((DEPTH, 8, 6 * D_MODEL), F32),
        scratch_shapes=[pltpu.VMEM((N_SETS, 8, MOD_TN), F32)],
        compiler_params=_cparams("parallel", "parallel", "arbitrary"),
        name="modulation",
    )(cvec_t, w_mod, b_mod.reshape(DEPTH, 1, 6 * D_MODEL))


PROJ_TM = 1024


def _proj_kernel(x_ref, m_ref, g_ref, w_ref, o_ref, h_ref):
    @pl.when(pl.program_id(1) == 0)
    def _():
        h = _rms(x_ref[...], g_ref[...]) * (1.0 + m_ref[1:2, :]) + m_ref[0:1, :]
        h_ref[...] = h.astype(BF16)

    o_ref[...] = _dot(h_ref[...], w_ref[...])


def _project(x, mod4, norm_g, w_in_p, l):
    return pl.pallas_call(
        _proj_kernel,
        grid=(NT // PROJ_TM, PROJ_W // PROJ_TN),
        in_specs=[
            pl.BlockSpec((PROJ_TM, D_MODEL), lambda i, j: (i, 0)),
            pl.BlockSpec((None, None, 6, D_MODEL), lambda i, j: (l, _mod_set(i * PROJ_TM), 0, 0)),
            pl.BlockSpec((None, 1, D_MODEL), lambda i, j: (l, 0, 0)),
            pl.BlockSpec((None, D_MODEL, PROJ_TN), lambda i, j: (l, 0, j)),
        ],
        out_specs=pl.BlockSpec((PROJ_TM, PROJ_TN), lambda i, j: (i, j)),
        out_shape=jax.ShapeDtypeStruct((NT, PROJ_W), F32),
        scratch_shapes=[pltpu.VMEM((PROJ_TM, D_MODEL), BF16)],
        compiler_params=_cparams("parallel", "arbitrary"),
        name="adaln_in_proj",
    )(x, mod4, norm_g, w_in_p)


def _rope(x, cos, sin, quarter):
    width = x.shape[-1]
    lane = lax.broadcasted_iota(jnp.int32, x.shape, 1)
    first = (lane % (2 * quarter)) < quarter
    partner = jnp.where(first, pltpu.roll(x, width - quarter, 1), pltpu.roll(x, quarter, 1))
    return x * cos + partner * sin


def _rope_tables(n_tok, rot_dim):
    half = rot_dim // 2
    pos = jnp.arange(n_tok)
    row = (pos // GRID_W).astype(F32)
    col = (pos % GRID_W).astype(F32)
    inv = ROPE_BASE ** (-jnp.arange(0, half, 2, dtype=F32) / half)
    a_row = row[:, None] * inv[None, :]
    a_col = col[:, None] * inv[None, :]
    ang = jnp.concatenate([a_row, a_row, a_col, a_col], axis=-1)
    sign = jnp.tile(jnp.concatenate([-jnp.ones(half // 2, F32), jnp.ones(half // 2, F32)]), 2)
    return jnp.cos(ang), jnp.sin(ang) * sign[None, :]


def _log_sigmoid(x):
    return jnp.minimum(x, 0.0) - jnp.log(1.0 + jnp.exp(-jnp.abs(x)))


def _mlstm_kernel(*refs, n_tok, has_state):
    if has_state:
        (q_ref, k_ref, v_ref, o_ref, tail_ref, gt_ref, bcol_ref, brow_ref, ng_ref, c0_ref, n0_ref, m0_ref,
         out_ref, hf_ref, hb_ref, cst_ref, nst_ref, mst_ref) = refs
    else:
        (q_ref, k_ref, v_ref, o_ref, tail_ref, gt_ref, bcol_ref, brow_ref, ng_ref,
         out_ref, cs_ref, ns_ref, ms_ref, hf_ref, hb_ref, cst_ref, nst_ref, mst_ref) = refs
    n_chunks = n_tok // ML_CHUNK
    scale = ML_DH ** -0.5

    if has_state:
        cst_ref[...] = c0_ref[...]
        nst_ref[...] = n0_ref[...]
        mst_ref[...] = m0_ref[...]
    else:
        cst_ref[...] = jnp.zeros_like(cst_ref)
        nst_ref[...] = jnp.zeros_like(nst_ref)
        mst_ref[...] = jnp.zeros_like(mst_ref)

    row = lax.broadcasted_iota(jnp.int32, (ML_CHUNK, ML_CHUNK), 0)
    col = lax.broadcasted_iota(jnp.int32, (ML_CHUNK, ML_CHUNK), 1)
    lower = (col <= row)
    upper = (col >= row)
    lower_f = lower.astype(F32)
    upper_f = upper.astype(F32)

    def chunk_step(c, carry):
        for d in range(2):
            cc = c if d == 0 else n_chunks - 1 - c
            t0 = pl.multiple_of(cc * ML_CHUNK, ML_CHUNK)
            rows = pl.ds(t0, ML_CHUNK)
            g_col = tail_ref[rows, :] + brow_ref[...]
            g_row = gt_ref[cc] + bcol_ref[...]
            lf_col = _log_sigmoid(g_col)
            lf_row = _log_sigmoid(g_row)
            if d == 0:
                bcum_col = _dot_hi(lower_f, lf_col)
                bcum_row = _dot_hi(lf_row, upper_f)
                mask = lower
            else:
                bcum_col = _dot_hi(upper_f, lf_col)
                bcum_row = _dot_hi(lf_row, lower_f)
                mask = upper
            for h in range(ML_HEADS):
                r = d * ML_HEADS + h
                lanes = slice(h * ML_DH, (h + 1) * ML_DH)
                ci = TAIL_G + r
                cf = TAIL_G + 2 * ML_HEADS + r
                ig_c = g_col[:, ci:ci + 1]
                b_c = bcum_col[:, cf:cf + 1]
                ig_r = g_row[r:r + 1, :]
                b_r = bcum_row[2 * ML_HEADS + r:2 * ML_HEADS + r + 1, :]
                tot = jnp.sum(lf_row[2 * ML_HEADS + r:2 * ML_HEADS + r + 1, :], axis=1, keepdims=True)
                m_prev = mst_ref[r:r + 1, 0:1]
                n_prev = nst_ref[r:r + 1, :]
                c_prev = cst_ref[r]
                q = q_ref[rows, lanes]
                k = k_ref[rows, lanes]
                v = v_ref[rows, lanes].astype(BF16)
                qb = q.astype(BF16)

                dmat = jnp.where(mask, b_c - b_r + ig_r, NEG)
                inter = b_c + m_prev
                mt = jnp.maximum(inter, jnp.max(dmat, axis=1, keepdims=True))
                w = jnp.exp(dmat - mt)
                s = _dot_nt(qb, k.astype(BF16)) * scale * w
                a = jnp.exp(inter - mt)
                num = _dot(s.astype(BF16), v) + a * _dot(qb, c_prev.astype(BF16))
                den = jnp.sum(s, axis=1, keepdims=True) + a * jnp.sum(q * n_prev, axis=1, keepdims=True)
                hc = num / jnp.maximum(jnp.abs(den), jnp.exp(-mt))
                if d == 0:
                    hf_ref[rows, lanes] = hc
                else:
                    hb_ref[rows, lanes] = hc

                wlog_c = tot - b_c + ig_c
                wlog_r = tot - b_r + ig_r
                m_new = jnp.maximum(tot + m_prev, jnp.max(wlog_r, axis=1, keepdims=True))
                decay = jnp.exp(tot + m_prev - m_new)
                kw = k * (scale * jnp.exp(wlog_c - m_new))
                cst_ref[r] = decay * c_prev + _dot_tn(kw.astype(BF16), v)
                nst_ref[r:r + 1, :] = decay * n_prev + jnp.sum(kw, axis=0, keepdims=True)
                mst_ref[r:r + 1, :] = jnp.broadcast_to(m_new, (1, ML_DH))
        return carry

    lax.fori_loop(0, n_chunks, chunk_step, 0)

    for h in range(ML_HEADS):
        lanes = slice(h * ML_DH, (h + 1) * ML_DH)
        hs = hf_ref[:, lanes] + hb_ref[:, lanes]
        out_ref[:, lanes] = _rms(hs, ng_ref[:, lanes]) * jax.nn.sigmoid(o_ref[:, lanes])

    if not has_state:
        cs_ref[...] = cst_ref[...]
        ns_ref[...] = nst_ref[...]
        ms_ref[...] = mst_ref[...]


def _mlstm(proj, gt, bcol, brow, norm_g, l, *, latent, states=None):
    n_tok = DEC_SEQ if latent else SEQ
    n_b = DEC_BATCH if latent else BATCH
    blk0 = N_CTX // n_tok if latent else 0
    n_chunks = n_tok // ML_CHUNK

    def col_spec(c0):
        return pl.BlockSpec((n_tok, 512), lambda b: (blk0 + b, c0 // 512))

    in_specs = [
        col_spec(C_MLQ), col_spec(C_MLK), col_spec(C_MLV), col_spec(C_MLO),
        pl.BlockSpec((n_tok, 128), lambda b: (blk0 + b, C_TAIL // 128)),
        pl.BlockSpec((n_chunks, 16, ML_CHUNK), lambda b: (blk0 + b, 0, 0)),
        pl.BlockSpec((16, 1), lambda b: (0, 0)),
        pl.BlockSpec((1, 128), lambda b: (0, 0)),
        pl.BlockSpec((None, 1, GROUP_W), lambda b: (l, 0, 0)),
    ]
    args = [proj, proj, proj, proj, proj, gt, bcol, brow, norm_g]
    n_chain = 2 * ML_HEADS
    scratch = [pltpu.VMEM((n_tok, GROUP_W), F32), pltpu.VMEM((n_tok, GROUP_W), F32),
               pltpu.VMEM((n_chain, ML_DH, ML_DH), F32), pltpu.VMEM((n_chain, ML_DH), F32),
               pltpu.VMEM((n_chain, ML_DH), F32)]
    if latent:
        c0, n0, m0 = states
        in_specs += [
            pl.BlockSpec((None, None, n_chain, ML_DH, ML_DH), lambda b: (b, l, 0, 0, 0)),
            pl.BlockSpec((None, None, n_chain, ML_DH), lambda b: (b, l, 0, 0)),
            pl.BlockSpec((None, None, n_chain, ML_DH), lambda b: (b, l, 0, 0)),
        ]
        args += [c0, n0, m0]
        out_shape = jax.ShapeDtypeStruct((N_LAT, GROUP_W), F32)
        out_specs = pl.BlockSpec((n_tok, GROUP_W), lambda b: (b, 0))
    else:
        out_shape = (jax.ShapeDtypeStruct((N_CTX, GROUP_W), F32),
                     jax.ShapeDtypeStruct((BATCH, n_chain, ML_DH, ML_DH), F32),
                     jax.ShapeDtypeStruct((BATCH, n_chain, ML_DH), F32),
                     jax.ShapeDtypeStruct((BATCH, n_chain, ML_DH), F32))
        out_specs = (pl.BlockSpec((n_tok, GROUP_W), lambda b: (b, 0)),
                     pl.BlockSpec((None, n_chain, ML_DH, ML_DH), lambda b: (b, 0, 0, 0)),
                     pl.BlockSpec((None, n_chain, ML_DH), lambda b: (b, 0, 0)),
                     pl.BlockSpec((None, n_chain, ML_DH), lambda b: (b, 0, 0)))
    return pl.pallas_call(
        functools.partial(_mlstm_kernel, n_tok=n_tok, has_state=latent),
        grid=(n_b,),
        in_specs=in_specs,
        out_specs=out_specs,
        out_shape=out_shape,
        scratch_shapes=scratch,
        compiler_params=_cparams("parallel"),
        name="mlstm_latent" if latent else "mlstm_context",
    )(*args)


def _softmax_parts(scores, sink=None):
    m = jnp.max(scores[0], axis=1, keepdims=True)
    for s in scores[1:]:
        m = jnp.maximum(m, jnp.max(s, axis=1, keepdims=True))
    if sink is not None:
        m = jnp.maximum(m, sink)
    es = [jnp.exp(s - m) for s in scores]
    den = jnp.sum(es[0], axis=1, keepdims=True)
    for e in es[1:]:
        den = den + jnp.sum(e, axis=1, keepdims=True)
    if sink is not None:
        den = den + jnp.exp(sink - m)
    return es, den


ATT_TQ = 256
LAT_QB = DEC_SEQ // ATT_TQ
N_KEYS_LAT = DEC_SEQ + PAST_LEN


def _mla_q(cq_ref, qg_ref, wuq_ref):
    return _dot(_rms(cq_ref[...], qg_ref[...]).astype(BF16), wuq_ref[...])


def _mla_heads(q, kv, kr, n_keys):
    outs = []
    for h in range(MLA_HEADS):
        qn = q[:, h * MLA_D_NOPE:(h + 1) * MLA_D_NOPE].astype(BF16)
        r0 = MLA_HEADS * MLA_D_NOPE + h * MLA_D_ROPE
        qr = q[:, r0:r0 + MLA_D_ROPE].astype(BF16)
        kn = kv[:, h * 256:h * 256 + MLA_D_NOPE]
        v = kv[:, h * 256 + MLA_D_NOPE:(h + 1) * 256]
        s = (_dot_nt(qn, kn) + _dot_nt(qr, kr)) * MLA_SCALE
        (e,), den = _softmax_parts([s])
        outs.append(_dot(e.astype(BF16), v) / den)
    return outs


def _mla_ctx_kernel(cq_ref, ckv_ref, tail_ref, qg_ref, kg_ref, wuq_ref, wukv_ref, out_ref, ckvn_ref):
    q = _mla_q(cq_ref, qg_ref, wuq_ref)
    ckvn = _rms(ckv_ref[...], kg_ref[...])
    ckvn_ref[...] = ckvn
    kv = _dot(ckvn.astype(BF16), wukv_ref[...]).astype(BF16)
    kr = tail_ref[:, 0:MLA_D_ROPE].astype(BF16)
    outs = _mla_heads(q, kv, kr, SEQ)
    for h in range(MLA_HEADS):
        out_ref[:, h * MLA_D_V:(h + 1) * MLA_D_V] = outs[h]


MLA_D_V = GROUP_W // MLA_HEADS


def _mla_lat_kernel(cq_ref, ckv_ref, tail_ref, qg_ref, kg_ref, wuq_ref, wukv_ref, ckvc_ref, krc_ref,
                    cosq_ref, sinq_ref, cosk_ref, sink_ref, out_ref, kv_ref, kr_ref):
    @pl.when(pl.program_id(1) == 0)
    def _():
        ckvn = _rms(ckv_ref[...], kg_ref[...])
        kv_ref[0:DEC_SEQ, :] = _dot(ckvn.astype(BF16), wukv_ref[...]).astype(BF16)
        kv_ref[DEC_SEQ:N_KEYS_LAT, :] = _dot(ckvc_ref[...].astype(BF16), wukv_ref[...]).astype(BF16)
        kr = _rope(tail_ref[...], cosk_ref[...], sink_ref[...], MLA_D_ROPE // 4)
        kr_ref[0:DEC_SEQ, :] = kr[:, 0:MLA_D_ROPE].astype(BF16)
        kr_ref[DEC_SEQ:N_KEYS_LAT, :] = krc_ref[...].astype(BF16)

    q = _mla_q(cq_ref, qg_ref, wuq_ref)
    n0 = MLA_HEADS * MLA_D_NOPE
    q_rope = _rope(q[:, n0:], cosq_ref[...], sinq_ref[...], MLA_D_ROPE // 4)
    q = jnp.concatenate([q[:, :n0], q_rope], axis=1)
    outs = _mla_heads(q, kv_ref[...], kr_ref[...], N_KEYS_LAT)
    for h in range(MLA_HEADS):
        out_ref[:, h * MLA_D_V:(h + 1) * MLA_D_V] = outs[h]


def _w_specs2(shape_a, shape_b, l):
    return [pl.BlockSpec((None,) + shape_a, lambda *_: (l,) + (0,) * len(shape_a)),
            pl.BlockSpec((None,) + shape_b, lambda *_: (l,) + (0,) * len(shape_b))]


def _mla_ctx(proj, qg, kg, wuq, wukv, l):
    return pl.pallas_call(
        _mla_ctx_kernel,
        grid=(BATCH,),
        in_specs=[
            pl.BlockSpec((SEQ, 256), lambda b: (b, C_CQ // 256)),
            pl.BlockSpec((SEQ, 256), lambda b: (b, C_CKV // 256)),
            pl.BlockSpec((SEQ, 128), lambda b: (b, C_TAIL // 128)),
            pl.BlockSpec((None, 1, MLA_Q_RANK), lambda b: (l, 0, 0)),
            pl.BlockSpec((None, 1, MLA_KV_RANK), lambda b: (l, 0, 0)),
            pl.BlockSpec((None, MLA_Q_RANK, 768), lambda b: (l, 0, 0)),
            pl.BlockSpec((None, MLA_KV_RANK, 1024), lambda b: (l, 0, 0)),
        ],
        out_specs=(pl.BlockSpec((SEQ, GROUP_W), lambda b: (b, 0)),
                   pl.BlockSpec((SEQ, MLA_KV_RANK), lambda b: (b, 0))),
        out_shape=(jax.ShapeDtypeStruct((N_CTX, GROUP_W), F32),
                   jax.ShapeDtypeStruct((N_CTX, MLA_KV_RANK), F32)),
        compiler_params=_cparams("parallel"),
        name="mla_context",
    )(proj, proj, proj, qg, kg, wuq, wukv)


def _mla_lat(proj, qg, kg, wuq, wukv, cache_ckv, cache_kr, cos64, sin64, l):
    qb0 = N_CTX // ATT_TQ
    bb0 = N_CTX // DEC_SEQ
    cosq = jnp.tile(cos64, (1, MLA_HEADS))
    sinq = jnp.tile(sin64, (1, MLA_HEADS))
    cosk = jnp.tile(cos64, (1, 2))
    sink = jnp.tile(sin64, (1, 2))
    return pl.pallas_call(
        _mla_lat_kernel,
        grid=(DEC_BATCH, LAT_QB),
        in_specs=[
            pl.BlockSpec((ATT_TQ, 256), lambda b, i: (qb0 + b * LAT_QB + i, C_CQ // 256)),
            pl.BlockSpec((DEC_SEQ, 256), lambda b, i: (bb0 + b, C_CKV // 256)),
            pl.BlockSpec((DEC_SEQ, 128), lambda b, i: (bb0 + b, C_TAIL // 128)),
            pl.BlockSpec((None, 1, MLA_Q_RANK), lambda b, i: (l, 0, 0)),
            pl.BlockSpec((None, 1, MLA_KV_RANK), lambda b, i: (l, 0, 0)),
            pl.BlockSpec((None, MLA_Q_RANK, 768), lambda b, i: (l, 0, 0)),
            pl.BlockSpec((None, MLA_KV_RANK, 1024), lambda b, i: (l, 0, 0)),
            pl.BlockSpec((None, None, PAST_LEN, MLA_KV_RANK), lambda b, i: (b, l, 0, 0)),
            pl.BlockSpec((None, None, PAST_LEN, MLA_D_ROPE), lambda b, i: (b, l, 0, 0)),
            pl.BlockSpec((ATT_TQ, 256), lambda b, i: (i, 0)),
            pl.BlockSpec((ATT_TQ, 256), lambda b, i: (i, 0)),
            pl.BlockSpec((DEC_SEQ, 128), lambda b, i: (0, 0)),
            pl.BlockSpec((DEC_SEQ, 128), lambda b, i: (0, 0)),
        ],
        out_specs=pl.BlockSpec((ATT_TQ, GROUP_W), lambda b, i: (b * LAT_QB + i, 0)),
        out_shape=jax.ShapeDtypeStruct((N_LAT, GROUP_W), F32),
        scratch_shapes=[pltpu.VMEM((N_KEYS_LAT, 1024), BF16), pltpu.VMEM((N_KEYS_LAT, MLA_D_ROPE), BF16)],
        compiler_params=_cparams("parallel", "arbitrary"),
        name="mla_latent",
    )(proj, proj, proj, qg, kg, wuq, wukv, cache_ckv, cache_kr, cosq, sinq, cosk, sink)


SWA_SCALE = SWA_DH ** -0.5
SWA_REP = SWA_HEADS // SWA_KV_HEADS
SWA_KWIN = ATT_TQ + 2 * WINDOW


def _swa_ctx_kernel(sink_ref, q_ref, k_ref, v_ref, out_ref):
    kb = k_ref[...].astype(BF16)
    vb = v_ref[...].astype(BF16)
    for h in range(SWA_HEADS):
        g = h // SWA_REP
        q = q_ref[:, h * SWA_DH:(h + 1) * SWA_DH].astype(BF16)
        s = _dot_nt(q, kb[:, g * SWA_DH:(g + 1) * SWA_DH]) * SWA_SCALE
        (e,), den = _softmax_parts([s], sink=sink_ref[0, h])
        out_ref[:, h * SWA_DH:(h + 1) * SWA_DH] = _dot(e.astype(BF16), vb[:, g * SWA_DH:(g + 1) * SWA_DH]) / den


def _swa_lat_kernel(sink_ref, q_ref, k_ref, v_ref, kc_ref, vc_ref, cosq_ref, sinq_ref, cosk_ref, sink_t_ref,
                    out_ref, kr_ref):
    i = pl.program_id(1)

    @pl.when(i == 0)
    def _():
        kr_ref[...] = _rope(k_ref[...], cosk_ref[...], sink_t_ref[...], SWA_DH // 4).astype(BF16)

    q_all = _rope(q_ref[...], cosq_ref[...], sinq_ref[...], SWA_DH // 4)
    k0 = pl.multiple_of(jnp.clip(i * ATT_TQ - WINDOW, 0, DEC_SEQ - SWA_KWIN), WINDOW)
    kwin = kr_ref[pl.ds(k0, SWA_KWIN), :]
    vwin = v_ref[pl.ds(k0, SWA_KWIN), :].astype(BF16)
    kc = kc_ref[...].astype(BF16)
    vc = vc_ref[...].astype(BF16)
    qpos = i * ATT_TQ + lax.broadcasted_iota(jnp.int32, (ATT_TQ, SWA_KWIN), 0)
    kpos = k0 + lax.broadcasted_iota(jnp.int32, (ATT_TQ, SWA_KWIN), 1)
    band = jnp.abs(qpos - kpos) <= WINDOW
    for h in range(SWA_HEADS):
        g = h // SWA_REP
        gl = slice(g * SWA_DH, (g + 1) * SWA_DH)
        q = q_all[:, h * SWA_DH:(h + 1) * SWA_DH].astype(BF16)
        s_loc = jnp.where(band, _dot_nt(q, kwin[:, gl]) * SWA_SCALE, NEG)
        s_ctx = _dot_nt(q, kc[:, gl]) * SWA_SCALE
        (e_loc, e_ctx), den = _softmax_parts([s_loc, s_ctx], sink=sink_ref[0, h])
        o = _dot(e_loc.astype(BF16), vwin[:, gl]) + _dot(e_ctx.astype(BF16), vc[:, gl])
        out_ref[:, h * SWA_DH:(h + 1) * SWA_DH] = o / den


def _smem_spec():
    return pl.BlockSpec(memory_space=pltpu.SMEM)


def _swa_ctx(proj, sink, l):
    return pl.pallas_call(
        _swa_ctx_kernel,
        grid=(BATCH,),
        in_specs=[
            _smem_spec(),
            pl.BlockSpec((SEQ, 512), lambda b: (b, C_SQ // 512)),
            pl.BlockSpec((SEQ, 256), lambda b: (b, C_SK // 256)),
            pl.BlockSpec((SEQ, 256), lambda b: (b, C_SV // 256)),
        ],
        out_specs=pl.BlockSpec((SEQ, GROUP_W), lambda b: (b, 0)),
        out_shape=jax.ShapeDtypeStruct((N_CTX, GROUP_W), F32),
        compiler_params=_cparams("parallel"),
        name="swa_context",
    )(sink, proj, proj, proj)


def _swa_lat(proj, sink, cache_k, cache_v, cos128, sin128, l):
    qb0 = N_CTX // ATT_TQ
    bb0 = N_CTX // DEC_SEQ
    kvw = SWA_KV_HEADS * SWA_DH
    return pl.pallas_call(
        _swa_lat_kernel,
        grid=(DEC_BATCH, LAT_QB),
        in_specs=[
            _smem_spec(),
            pl.BlockSpec((ATT_TQ, 512), lambda b, i: (qb0 + b * LAT_QB + i, C_SQ // 512)),
            pl.BlockSpec((DEC_SEQ, 256), lambda b, i: (bb0 + b, C_SK // 256)),
            pl.BlockSpec((DEC_SEQ, 256), lambda b, i: (bb0 + b, C_SV // 256)),
            pl.BlockSpec((None, None, PAST_LEN, kvw), lambda b, i: (b, l, 0, 0)),
            pl.BlockSpec((None, None, PAST_LEN, kvw), lambda b, i: (b, l, 0, 0)),
            pl.BlockSpec((ATT_TQ, 512), lambda b, i: (i, 0)),
            pl.BlockSpec((ATT_TQ, 512), lambda b, i: (i, 0)),
            pl.BlockSpec((DEC_SEQ, 256), lambda b, i: (0, 0)),
            pl.BlockSpec((DEC_SEQ, 256), lambda b, i: (0, 0)),
        ],
        out_specs=pl.BlockSpec((ATT_TQ, GROUP_W), lambda b, i: (b * LAT_QB + i, 0)),
        out_shape=jax.ShapeDtypeStruct((N_LAT, GROUP_W), F32),
        scratch_shapes=[pltpu.VMEM((DEC_SEQ, kvw), BF16)],
        compiler_params=_cparams("parallel", "arbitrary"),
        name="swa_latent",
    )(sink, proj, proj, proj, cache_k, cache_v,
      jnp.tile(cos128, (1, SWA_HEADS)), jnp.tile(sin128, (1, SWA_HEADS)),
      jnp.tile(cos128, (1, SWA_KV_HEADS)), jnp.tile(sin128, (1, SWA_KV_HEADS)))


DIFF_SCALE = DIFF_DH ** -0.5


def _diff_lambda(lq1_ref, lk1_ref, lq2_ref, lk2_ref):
    a = jnp.sum(lq1_ref[...] * lk1_ref[...], axis=1, keepdims=True)
    b = jnp.sum(lq2_ref[...] * lk2_ref[...], axis=1, keepdims=True)
    return jnp.exp(a) - jnp.exp(b)


def _diff_heads(q, k_parts, v_parts, lam, lam_init, ng_ref, out_ref):
    for h in range(DIFF_HEADS):
        ps = []
        for c in range(2):
            sl = slice(h * 2 * DIFF_DH + c * DIFF_DH, h * 2 * DIFF_DH + (c + 1) * DIFF_DH)
            qc = q[:, sl].astype(BF16)
            es, den = _softmax_parts([_dot_nt(qc, kp[:, sl]) * DIFF_SCALE for kp in k_parts])
            ps.append([e / den for e in es])
        vl = slice(h * 2 * DIFF_DH, (h + 1) * 2 * DIFF_DH)
        o = None
        for p1, p2, vp in zip(ps[0], ps[1], v_parts):
            t = _dot((p1 - lam * p2).astype(BF16), vp[:, vl])
            o = t if o is None else o + t
        out_ref[:, vl] = _rms(o, ng_ref[...]) * (1.0 - lam_init)


def _diff_ctx_kernel(q_ref, k_ref, v_ref, lq1_ref, lk1_ref, lq2_ref, lk2_ref, ng_ref, out_ref, *, lam_init):
    lam = _diff_lambda(lq1_ref, lk1_ref, lq2_ref, lk2_ref) + lam_init
    _diff_heads(q_ref[...], [k_ref[...].astype(BF16)], [v_ref[...].astype(BF16)], lam, lam_init, ng_ref, out_ref)


def _diff_lat_kernel(q_ref, k_ref, v_ref, kc_ref, vc_ref, lq1_ref, lk1_ref, lq2_ref, lk2_ref, ng_ref,
                     cosq_ref, sinq_ref, cosk_ref, sink_ref, out_ref, kr_ref, *, lam_init):
    @pl.when(pl.program_id(1) == 0)
    def _():
        kr_ref[...] = _rope(k_ref[...], cosk_ref[...], sink_ref[...], DIFF_DH // 4).astype(BF16)

    lam = _diff_lambda(lq1_ref, lk1_ref, lq2_ref, lk2_ref) + lam_init
    q = _rope(q_ref[...], cosq_ref[...], sinq_ref[...], DIFF_DH // 4)
    _diff_heads(q, [kr_ref[...], kc_ref[...].astype(BF16)], [v_ref[...].astype(BF16), vc_ref[...].astype(BF16)],
                lam, lam_init, ng_ref, out_ref)


def _vec_specs(n, width, l, nargs):
    return [pl.BlockSpec((None, 1, width), lambda *_: (l, 0, 0)) for _ in range(n)]


def _diff_ctx(proj, lq1, lk1, lq2, lk2, ng, l):
    lam_init = 0.8 - 0.6 * math.exp(-0.3 * l)
    return pl.pallas_call(
        functools.partial(_diff_ctx_kernel, lam_init=lam_init),
        grid=(BATCH,),
        in_specs=[
            pl.BlockSpec((SEQ, 512), lambda b: (b, C_DQ // 512)),
            pl.BlockSpec((SEQ, 512), lambda b: (b, C_DK // 512)),
            pl.BlockSpec((SEQ, 512), lambda b: (b, C_DV // 512)),
        ] + _vec_specs(4, DIFF_DH, l, 1) + _vec_specs(1, 2 * DIFF_DH, l, 1),
        out_specs=pl.BlockSpec((SEQ, GROUP_W), lambda b: (b, 0)),
        out_shape=jax.ShapeDtypeStruct((N_CTX, GROUP_W), F32),
        compiler_params=_cparams("parallel"),
        name="diff_context",
    )(proj, proj, proj, lq1, lk1, lq2, lk2, ng)


def _diff_lat(proj, lq1, lk1, lq2, lk2, ng, cache_k, cache_v, cos64, sin64, l):
    lam_init = 0.8 - 0.6 * math.exp(-0.3 * l)
    qb0 = N_CTX // ATT_TQ
    bb0 = N_CTX // DEC_SEQ
    cos_t = jnp.tile(cos64, (1, 2 * DIFF_HEADS))
    sin_t = jnp.tile(sin64, (1, 2 * DIFF_HEADS))
    return pl.pallas_call(
        functools.partial(_diff_lat_kernel, lam_init=lam_init),
        grid=(DEC_BATCH, LAT_QB),
        in_specs=[
            pl.BlockSpec((ATT_TQ, 512), lambda b, i: (qb0 + b * LAT_QB + i, C_DQ // 512)),
            pl.BlockSpec((DEC_SEQ, 512), lambda b, i: (bb0 + b, C_DK // 512)),
            pl.BlockSpec((DEC_SEQ, 512), lambda b, i: (bb0 + b, C_DV // 512)),
            pl.BlockSpec((None, None, PAST_LEN, GROUP_W), lambda b, i: (b, l, 0, 0)),
            pl.BlockSpec((None, None, PAST_LEN, GROUP_W), lambda b, i: (b, l, 0, 0)),
        ] + _vec_specs(4, DIFF_DH, l, 2) + _vec_specs(1, 2 * DIFF_DH, l, 2) + [
            pl.BlockSpec((ATT_TQ, 512), lambda b, i: (i, 0)),
            pl.BlockSpec((ATT_TQ, 512), lambda b, i: (i, 0)),
            pl.BlockSpec((DEC_SEQ, 512), lambda b, i: (0, 0)),
            pl.BlockSpec((DEC_SEQ, 512), lambda b, i: (0, 0)),
        ],
        out_specs=pl.BlockSpec((ATT_TQ, GROUP_W), lambda b, i: (b * LAT_QB + i, 0)),
        out_shape=jax.ShapeDtypeStruct((N_LAT, GROUP_W), F32),
        scratch_shapes=[pltpu.VMEM((DEC_SEQ, GROUP_W), BF16)],
        compiler_params=_cparams("parallel", "arbitrary"),
        name="diff_latent",
    )(proj, proj, proj, cache_k, cache_v, lq1, lk1, lq2, lk2, ng, cos_t, sin_t, cos_t, sin_t)


OUT_TM = 256


OUT_CTX_BLOCKS = N_CTX // OUT_TM


def _out_kernel(*refs):
    ctx_refs, lat_refs = refs[0:4], refs[4:8]
    x_ref, mod_ref, w_ref, x1_ref = refs[8:]
    is_ctx = pl.program_id(0) < OUT_CTX_BLOCKS
    acc = None
    for g in range(4):
        m = jnp.where(is_ctx, ctx_refs[g][...], lat_refs[g][...])
        t = _dot(m.astype(BF16), w_ref[g * GROUP_W:(g + 1) * GROUP_W, :])
        acc = t if acc is None else acc + t
    x1_ref[...] = x_ref[...] + mod_ref[2:3, :] * acc


def _out_proj(mixed_ctx, mixed_lat, x, mod4, w_out, l):
    row = lambda i: (i, 0)
    ctx_row = lambda i: (jnp.minimum(i, OUT_CTX_BLOCKS - 1), 0)
    lat_row = lambda i: (jnp.maximum(i - OUT_CTX_BLOCKS, 0), 0)
    return pl.pallas_call(
        _out_kernel,
        grid=(NT // OUT_TM,),
        in_specs=[pl.BlockSpec((OUT_TM, GROUP_W), ctx_row) for _ in range(4)]
        + [pl.BlockSpec((OUT_TM, GROUP_W), lat_row) for _ in range(4)] + [
            pl.BlockSpec((OUT_TM, D_MODEL), row),
            pl.BlockSpec((None, None, 6, D_MODEL), lambda i: (l, _mod_set(i * OUT_TM), 0, 0)),
            pl.BlockSpec((None, D_MODEL, D_MODEL), lambda i: (l, 0, 0)),
        ],
        out_specs=pl.BlockSpec((OUT_TM, D_MODEL), row),
        out_shape=jax.ShapeDtypeStruct((NT, D_MODEL), F32),
        compiler_params=_cparams("parallel"),
        name="out_proj_residual",
    )(*mixed_ctx, *mixed_lat, x, mod4, w_out)


def _peerq_kernel(x_ref, mod_ref, g_ref, w_ref, ht_ref, q_ref):
    h = _rms(x_ref[...], g_ref[...]) * (1.0 + mod_ref[4:5, :]) + mod_ref[3:4, :]
    ht_ref[...] = h.T.astype(BF16)
    q_ref[...] = _dot(h.astype(BF16), w_ref[...])


def _peer_query(x1, mod4, norm_g, w_q, l):
    row = lambda i: (i, 0)
    return pl.pallas_call(
        _peerq_kernel,
        grid=(NT // OUT_TM,),
        in_specs=[
            pl.BlockSpec((OUT_TM, D_MODEL), row),
            pl.BlockSpec((None, None, 6, D_MODEL), lambda i: (l, _mod_set(i * OUT_TM), 0, 0)),
            pl.BlockSpec((None, 1, D_MODEL), lambda i: (l, 0, 0)),
            pl.BlockSpec((None, D_MODEL, PEER_HEADS * PEER_QDIM), lambda i: (l, 0, 0)),
        ],
        out_specs=(pl.BlockSpec((D_MODEL, OUT_TM), lambda i: (0, i)),
                   pl.BlockSpec((OUT_TM, PEER_HEADS * PEER_QDIM), row)),
        out_shape=(jax.ShapeDtypeStruct((D_MODEL, NT), BF16),
                   jax.ShapeDtypeStruct((NT, PEER_HEADS * PEER_QDIM), F32)),
        compiler_params=_cparams("parallel"),
        name="adaln_peer_query",
    )(x1, mod4, norm_g, w_q)


ROUTE_TL = 256
ROUTE_LANES = 128
NOT_SEL = float(PEER_TOPK)


def _top16(s, index_ties):
    idx = lax.broadcasted_iota(jnp.int32, s.shape, 0).astype(F32)
    slot = lax.broadcasted_iota(jnp.int32, (PEER_TOPK, s.shape[1]), 0)
    rank = jnp.full(s.shape, NOT_SEL, F32)
    vals = jnp.zeros((PEER_TOPK, s.shape[1]), F32)
    for k in range(PEER_TOPK):
        m = jnp.max(s, axis=0, keepdims=True)
        sel = s == m
        if index_ties:
            sel = idx == jnp.min(jnp.where(sel, idx, float(PEER_NKEYS)), axis=0, keepdims=True)
        rank = jnp.where(sel, float(k), rank)
        s = jnp.where(sel, NEG, s)
        vals = jnp.where(slot == k, m, vals)
    return rank, vals


CAND_HALF = PEER_TOPK // 2
CAND_ROWS = PEER_TOPK + (CAND_HALF - 1) * CAND_HALF + CAND_HALF
FLAT_NONE = float(PEER_TOPK * PEER_TOPK)


def _cand_flat(tl):
    r = lax.broadcasted_iota(jnp.int32, (CAND_ROWS, tl), 0)
    mid = r - PEER_TOPK
    mid_flat = (1 + mid // CAND_HALF) * PEER_TOPK + mid % CAND_HALF
    last_flat = (CAND_HALF + r - (CAND_ROWS - CAND_HALF)) * PEER_TOPK
    flat = jnp.where(r < PEER_TOPK, r, jnp.where(r < CAND_ROWS - CAND_HALF, mid_flat, last_flat))
    return flat.astype(F32)


def _count(mask):
    return jnp.sum(jnp.where(mask, 1.0, 0.0), axis=0, keepdims=True)


def _route_head(s1, s2, flat, index_ties):
    tl = s1.shape[1]
    rank1, v1 = _top16(s1, index_ties)
    rank2, v2 = _top16(s2, index_ties)
    slabs = [v1[0:1, :] + v2]
    slabs += [v1[a:a + 1, :] + v2[0:CAND_HALF, :] for a in range(1, CAND_HALF)]
    slabs.append(v1[CAND_HALF:, :] + v2[0:1, :])
    cand = jnp.concatenate(slabs, axis=0)
    a_row = jnp.floor(flat * (1.0 / PEER_TOPK))
    top = v1[0:1, :] + v2[0:1, :]
    cnt1 = jnp.zeros(s1.shape, F32)
    z = jnp.zeros((1, tl), F32)
    for k in range(PEER_TOPK):
        m = jnp.max(cand, axis=0, keepdims=True)
        sel = cand == m
        if index_ties:
            first = jnp.min(jnp.where(sel, flat, FLAT_NONE), axis=0, keepdims=True)
            sel = flat == first
            a_sel = jnp.floor(first * (1.0 / PEER_TOPK))
        else:
            a_sel = jnp.max(jnp.where(sel, a_row, -1.0), axis=0, keepdims=True)
        cand = jnp.where(sel, NEG, cand)
        cnt1 = cnt1 + jnp.where(rank1 == a_sel, 1.0, 0.0)
        z = z + jnp.exp(m - top)
    e1 = jnp.exp(s1 - v1[0:1, :]) / z
    e2 = jnp.exp(s2 - v2[0:1, :])
    if index_ties:
        return rank2, cnt1, e1, e2, None
    full = float(PEER_TOPK)
    clean = ((_count(rank1 < NOT_SEL) == full) & (_count(rank2 < NOT_SEL) == full)
             & (_count(cand == NEG) == full))
    return rank2, cnt1, e1, e2, jnp.max(jnp.where(clean, 0.0, 1.0))


def _route_kernel(q_ref, k1_ref, k2_ref, rank2_ref, e2_ref, cnt1_ref, e1_ref):
    tl = q_ref.shape[0]
    half = PEER_QDIM // 2

    def all_heads(index_ties):
        flat = _cand_flat(ROUTE_LANES)
        tied = jnp.float32(0.0)
        for h in range(PEER_HEADS):
            q1 = q_ref[:, h * PEER_QDIM:h * PEER_QDIM + half].astype(BF16)
            q2 = q_ref[:, h * PEER_QDIM + half:(h + 1) * PEER_QDIM].astype(BF16)
            s1 = _dot_nt(k1_ref[...], q1)
            s2 = _dot_nt(k2_ref[...], q2)
            for c in range(tl // ROUTE_LANES):
                lanes = slice(c * ROUTE_LANES, (c + 1) * ROUTE_LANES)
                rank2, cnt1, e1, e2, t = _route_head(s1[:, lanes], s2[:, lanes], flat, index_ties)
                rank2_ref[h, :, lanes] = rank2.astype(BF16)
                cnt1_ref[h, :, lanes] = cnt1
                e1_ref[h, :, lanes] = e1
                e2_ref[h, :, lanes] = e2.astype(BF16)
                if not index_ties:
                    tied = jnp.maximum(tied, t)
        return tied

    tied = all_heads(index_ties=False)

    @pl.when(tied > 0.0)
    def _():
        all_heads(index_ties=True)


def _peer_route(q, keys, l):
    shp = jax.ShapeDtypeStruct((PEER_HEADS, PEER_NKEYS, NT), F32)
    shp_b = jax.ShapeDtypeStruct((PEER_HEADS, PEER_NKEYS, NT), BF16)
    spec = pl.BlockSpec((PEER_HEADS, PEER_NKEYS, ROUTE_TL), lambda i: (0, 0, i))
    half = PEER_QDIM // 2
    return pl.pallas_call(
        _route_kernel,
        grid=(NT // ROUTE_TL,),
        in_specs=[
            pl.BlockSpec((ROUTE_TL, PEER_HEADS * PEER_QDIM), lambda i: (i, 0)),
            pl.BlockSpec((None, None, PEER_NKEYS, half), lambda i: (l, 0, 0, 0)),
            pl.BlockSpec((None, None, PEER_NKEYS, half), lambda i: (l, 1, 0, 0)),
        ],
        out_specs=(spec, spec, spec, spec),
        out_shape=(shp_b, shp_b, shp, shp),
        compiler_params=_cparams("parallel"),
        name="peer_route",
    )(q, keys, keys)


PEER_TT = 512
PEER_EB = 512
GELU_C = math.sqrt(2.0 / math.pi)


def _gelu_tanh(x):
    return 0.5 * x * (1.0 + jnp.tanh(GELU_C * (x + 0.044715 * (x * x * x))))


PEER_NE = PEER_N // PEER_EB
PEER_MT = 256


def _peer_kernel(ht_ref, u_ref, vt_ref, rank2_ref, e2_ref, cnt1_ref, e1_ref, x_ref, mod_ref, o_ref,
                 ga_ref, gb_ref, acc_ref):
    e = pl.program_id(1)

    @pl.when(e == 0)
    def _():
        acc_ref[...] = jnp.zeros_like(acc_ref)
        gb_ref[...] = jnp.zeros_like(gb_ref)

    def step(g_prev_ref, g_next_ref):
        blk = jnp.maximum(e - 1, 0)
        n_i = PEER_EB // PEER_NKEYS
        per_tile = PEER_MT // PEER_NKEYS
        for tc in range(PEER_TT // PEER_MT):
            tok = slice(tc * PEER_MT, (tc + 1) * PEER_MT)
            g_next_ref[:, tok] = _gelu_tanh(_dot(u_ref[...], ht_ref[:, tok])).astype(BF16)
        for tc in range(PEER_TT // PEER_MT):
            tok = slice(tc * PEER_MT, (tc + 1) * PEER_MT)
            acc = None
            for kc in range(PEER_EB // PEER_MT):
                tiles = []
                for ii in range(kc * per_tile, (kc + 1) * per_tile):
                    key1 = blk * n_i + ii
                    w = jnp.zeros((PEER_NKEYS, PEER_MT), BF16)
                    for h in range(PEER_HEADS):
                        cnt = cnt1_ref[h, pl.ds(key1, 1), tok].astype(BF16)
                        g1 = e1_ref[h, pl.ds(key1, 1), tok].astype(BF16)
                        w = w + jnp.where(rank2_ref[h, :, tok] < cnt, e2_ref[h, :, tok] * g1, 0.0)
                    tiles.append(w * g_prev_ref[ii * PEER_NKEYS:(ii + 1) * PEER_NKEYS, tok])
                t = _dot(vt_ref[:, kc * PEER_MT:(kc + 1) * PEER_MT], jnp.concatenate(tiles, axis=0))
                acc = t if acc is None else acc + t
            acc_ref[:, tok] += acc

    @pl.when(e % 2 == 0)
    def _():
        step(gb_ref, ga_ref)

    @pl.when(e % 2 == 1)
    def _():
        step(ga_ref, gb_ref)

    @pl.when(e == PEER_NE)
    def _():
        o_ref[...] = x_ref[...] + mod_ref[5:6, :] * acc_ref[...].T


def _peer(ht, u_tab, vt_tab, routing, x1, mod4, l):
    rank2, e2, cnt1, e1 = routing
    rspec = pl.BlockSpec((PEER_HEADS, PEER_NKEYS, PEER_TT), lambda i, e: (0, 0, i))
    return pl.pallas_call(
        _peer_kernel,
        grid=(NT // PEER_TT, PEER_NE + 1),
        in_specs=[
            pl.BlockSpec((D_MODEL, PEER_TT), lambda i, e: (0, i)),
            pl.BlockSpec((None, PEER_EB, D_MODEL), lambda i, e: (l, jnp.minimum(e, PEER_NE - 1), 0)),
            pl.BlockSpec((None, D_MODEL, PEER_EB), lambda i, e: (l, 0, jnp.maximum(e - 1, 0))),
            rspec, rspec, rspec, rspec,
            pl.BlockSpec((PEER_TT, D_MODEL), lambda i, e: (i, 0)),
            pl.BlockSpec((None, None, 6, D_MODEL), lambda i, e: (l, _mod_set(i * PEER_TT), 0, 0)),
        ],
        out_specs=pl.BlockSpec((PEER_TT, D_MODEL), lambda i, e: (i, 0)),
        out_shape=jax.ShapeDtypeStruct((NT, D_MODEL), F32),
        scratch_shapes=[pltpu.VMEM((PEER_EB, PEER_TT), BF16), pltpu.VMEM((PEER_EB, PEER_TT), BF16),
                        pltpu.VMEM((D_MODEL, PEER_TT), F32)],
        compiler_params=_cparams("parallel", "arbitrary"),
        name="peer_experts",
    )(ht, u_tab, vt_tab, rank2, e2, cnt1, e1, x1, mod4)


FIN_TM = 512


def _final_kernel(x_ref, g_ref, o_ref):
    o_ref[...] = _rms(x_ref[...], g_ref[...])


def _final_norm(x, g):
    return pl.pallas_call(
        _final_kernel,
        grid=(NT // FIN_TM,),
        in_specs=[pl.BlockSpec((FIN_TM, D_MODEL), lambda i: (i, 0)),
                  pl.BlockSpec((1, D_MODEL), lambda i: (0, 0))],
        out_specs=pl.BlockSpec((FIN_TM, D_MODEL), lambda i: (i, 0)),
        out_shape=jax.ShapeDtypeStruct((NT, D_MODEL), F32),
        compiler_params=_cparams("parallel"),
        name="final_norm",
    )(x, g)


def _permute_w_in(w_in):
    sizes = (GROUP_W, GROUP_W, GROUP_W, GROUP_W, 4 * ML_HEADS, MLA_Q_RANK, MLA_KV_RANK, MLA_D_ROPE,
             SWA_HEADS * SWA_DH, SWA_KV_HEADS * SWA_DH, SWA_KV_HEADS * SWA_DH, GROUP_W, GROUP_W, GROUP_W)
    offs = [0]
    for s in sizes:
        offs.append(offs[-1] + s)
    part = lambda i: w_in[:, :, offs[i]:offs[i + 1]]
    order = [0, 1, 2, 3, 5, 6, 8, 9, 10, 11, 12, 13, 7, 4]
    cols = [part(i) for i in order]
    used = sum(sizes)
    cols.append(jnp.zeros(w_in.shape[:2] + (PROJ_W - used,), w_in.dtype))
    return jnp.concatenate(cols, axis=-1).astype(BF16)


def _permute_w_uq(w_uq):
    w = w_uq.reshape(DEPTH, MLA_Q_RANK, MLA_HEADS, MLA_D_NOPE + MLA_D_ROPE)
    nope = w[..., :MLA_D_NOPE].reshape(DEPTH, MLA_Q_RANK, MLA_HEADS * MLA_D_NOPE)
    rope = w[..., MLA_D_NOPE:].reshape(DEPTH, MLA_Q_RANK, MLA_HEADS * MLA_D_ROPE)
    return jnp.concatenate([nope, rope], axis=-1).astype(BF16)


def kernel(x_prompt, x_sample, c, cache_mla_ckv, cache_mla_krope, cache_swa_k, cache_swa_v, cache_diff_k,
           cache_diff_v, state_mlstm_C, state_mlstm_n, state_mlstm_m, c_ctx, w_mod, b_mod, norm1_g, w_in,
           mlstm_i_bias, mlstm_f_bias, mlstm_norm_g, mla_qnorm_g, mla_w_uq, mla_kvnorm_g, mla_w_ukv, swa_sink,
           diff_lq1, diff_lk1, diff_lq2, diff_lk2, diff_norm_g, w_out, norm2_g, peer_w_q, peer_sub_keys,
           peer_u, peer_v, final_norm_g):
    x = jnp.concatenate([x_prompt.reshape(N_CTX, D_MODEL), x_sample.reshape(N_LAT, D_MODEL)], axis=0)

    cvec = jnp.concatenate([c_ctx[None, :], c], axis=0)
    cvec_t = jnp.pad(cvec.T, ((0, 0), (0, 8 - N_SETS)))
    mod4 = _modulation(cvec_t, w_mod, b_mod).reshape(DEPTH, 8, 6, D_MODEL)

    w_in_p = _permute_w_in(w_in)
    w_uq_p = _permute_w_uq(mla_w_uq)
    w_ukv_b = mla_w_ukv.astype(BF16)
    w_out_b = w_out.astype(BF16)
    w_q_b = peer_w_q.astype(BF16)
    keys_b = peer_sub_keys.astype(BF16)
    u_b = peer_u.astype(BF16)
    vt_b = jnp.swapaxes(peer_v, 1, 2).astype(BF16)

    vec3 = lambda a: a.reshape(DEPTH, 1, a.shape[-1])
    norm1_3, norm2_3 = vec3(norm1_g), vec3(norm2_g)
    mlng_3, qg_3, kg_3, dng_3 = vec3(mlstm_norm_g), vec3(mla_qnorm_g), vec3(mla_kvnorm_g), vec3(diff_norm_g)
    lq1_3, lk1_3, lq2_3, lk2_3 = vec3(diff_lq1), vec3(diff_lk1), vec3(diff_lq2), vec3(diff_lk2)

    n_chain = 2 * ML_HEADS
    st_c = state_mlstm_C.reshape(DEC_BATCH, DEPTH, n_chain, ML_DH, ML_DH)
    st_n = state_mlstm_n.reshape(DEC_BATCH, DEPTH, n_chain, ML_DH)
    st_m = jnp.broadcast_to(state_mlstm_m.reshape(DEC_BATCH, DEPTH, n_chain, 1), (DEC_BATCH, DEPTH, n_chain, ML_DH))
    swk_c = cache_swa_k.reshape(DEC_BATCH, DEPTH, PAST_LEN, SWA_KV_HEADS * SWA_DH)
    swv_c = cache_swa_v.reshape(DEC_BATCH, DEPTH, PAST_LEN, SWA_KV_HEADS * SWA_DH)
    dfk_c = cache_diff_k.reshape(DEC_BATCH, DEPTH, PAST_LEN, GROUP_W)
    dfv_c = cache_diff_v.reshape(DEC_BATCH, DEPTH, PAST_LEN, GROUP_W)

    cos64, sin64 = _rope_tables(DEC_SEQ, 64)
    cos128, sin128 = _rope_tables(DEC_SEQ, 128)

    outs = [[] for _ in range(9)]
    for l in range(DEPTH):
        proj = _project(x, mod4, norm1_3, w_in_p, l)

        gates = proj[:, C_TAIL + TAIL_G:C_TAIL + TAIL_G + 16]
        gt = gates.reshape(NT // ML_CHUNK, ML_CHUNK, 16).transpose(0, 2, 1)
        bias = jnp.concatenate([mlstm_i_bias[l].reshape(-1), mlstm_f_bias[l].reshape(-1)])
        bcol = bias.reshape(16, 1)
        brow = jnp.zeros((1, 128), F32).at[0, TAIL_G:TAIL_G + 16].set(bias)

        ml_ctx, c_st, n_st, m_st = _mlstm(proj, gt, bcol, brow, mlng_3, l, latent=False)
        ml_lat = _mlstm(proj, gt, bcol, brow, mlng_3, l, latent=True, states=(st_c, st_n, st_m))
        mla_ctx, ckv_n = _mla_ctx(proj, qg_3, kg_3, w_uq_p, w_ukv_b, l)
        mla_lat = _mla_lat(proj, qg_3, kg_3, w_uq_p, w_ukv_b, cache_mla_ckv, cache_mla_krope, cos64, sin64, l)
        sink = swa_sink[l].reshape(1, SWA_HEADS)
        swa_ctx = _swa_ctx(proj, sink, l)
        swa_lat = _swa_lat(proj, sink, swk_c, swv_c, cos128, sin128, l)
        df_ctx = _diff_ctx(proj, lq1_3, lk1_3, lq2_3, lk2_3, dng_3, l)
        df_lat = _diff_lat(proj, lq1_3, lk1_3, lq2_3, lk2_3, dng_3, dfk_c, dfv_c, cos64, sin64, l)

        x1 = _out_proj((ml_ctx, mla_ctx, swa_ctx, df_ctx), (ml_lat, mla_lat, swa_lat, df_lat), x, mod4, w_out_b, l)
        h2t, q = _peer_query(x1, mod4, norm2_3, w_q_b, l)
        routing = _peer_route(q, keys_b, l)
        x = _peer(h2t, u_b, vt_b, routing, x1, mod4, l)

        pc = proj[:N_CTX]
        outs[0].append(ckv_n.reshape(BATCH, SEQ, MLA_KV_RANK))
        outs[1].append(pc[:, C_TAIL:C_TAIL + MLA_D_ROPE].reshape(BATCH, SEQ, MLA_D_ROPE))
        outs[2].append(pc[:, C_SK:C_SK + 256].reshape(BATCH, SEQ, SWA_KV_HEADS, SWA_DH))
        outs[3].append(pc[:, C_SV:C_SV + 256].reshape(BATCH, SEQ, SWA_KV_HEADS, SWA_DH))
        outs[4].append(pc[:, C_DK:C_DK + 512].reshape(BATCH, SEQ, DIFF_HEADS, 2 * DIFF_DH))
        outs[5].append(pc[:, C_DV:C_DV + 512].reshape(BATCH, SEQ, DIFF_HEADS, 2 * DIFF_DH))
        outs[6].append(c_st.reshape(BATCH, 2, ML_HEADS, ML_DH, ML_DH))
        outs[7].append(n_st.reshape(BATCH, 2, ML_HEADS, ML_DH))
        outs[8].append(m_st[:, :, 0].reshape(BATCH, 2, ML_HEADS))

    y = _final_norm(x, final_norm_g.reshape(1, D_MODEL))
    y_prompt = y[:N_CTX].reshape(BATCH, SEQ, D_MODEL)
    y_sample = y[N_CTX:].reshape(DEC_BATCH, DEC_SEQ, D_MODEL)
    return (y_prompt, y_sample) + tuple(jnp.stack(o, axis=1) for o in outs)
```

```python
import functools
import math

import jax
import jax.numpy as jnp
from jax import lax
from jax.experimental import pallas as pl
from jax.experimental.pallas import tpu as pltpu

F32 = jnp.float32
BF16 = jnp.bfloat16

D_MODEL = 2048
BATCH = 32
SEQ = 256
DEPTH = 4
DEC_BATCH = 2
DEC_SEQ = 1024
PAST_LEN = 256
GRID_W = 64
GROUP_W = D_MODEL // 4
ML_HEADS = 4
ML_DH = GROUP_W // ML_HEADS
ML_CHUNK = 64
MLA_HEADS = 4
MLA_D_NOPE = GROUP_W // MLA_HEADS
MLA_D_ROPE = 64
MLA_Q_RANK = D_MODEL // 8
MLA_KV_RANK = D_MODEL // 8
MLA_SCALE = (MLA_D_NOPE + MLA_D_ROPE) ** -0.5
SWA_HEADS = 4
SWA_KV_HEADS = 2
SWA_DH = GROUP_W // SWA_HEADS
WINDOW = 128
DIFF_HEADS = 4
DIFF_DH = GROUP_W // (2 * DIFF_HEADS)
PEER_HEADS = 8
PEER_QDIM = 256
PEER_NKEYS = 128
PEER_N = PEER_NKEYS * PEER_NKEYS
PEER_TOPK = 16
ROPE_BASE = 10000.0
EPS = 1e-6
NEG = -1e30

N_CTX = BATCH * SEQ
N_LAT = DEC_BATCH * DEC_SEQ
NT = N_CTX + N_LAT
N_SETS = 1 + DEC_BATCH

C_MLQ, C_MLK, C_MLV, C_MLO = 0, 512, 1024, 1536
C_CQ, C_CKV = 2048, 2304
C_SQ, C_SK, C_SV = 2560, 3072, 3328
C_DQ, C_DK, C_DV = 3584, 4096, 4608
C_TAIL = 5120
TAIL_G = 64
PROJ_W = 5376
PROJ_TN = 896

VMEM_LIMIT = 56 * 1024 * 1024


def _cparams(*sem):
    return pltpu.CompilerParams(dimension_semantics=sem, vmem_limit_bytes=VMEM_LIMIT)


def _mod_set(row_start):
    return jnp.where(row_start >= N_CTX, (row_start - N_CTX) // DEC_SEQ + 1, 0)


def _rms(x, g):
    return x * lax.rsqrt(jnp.mean(x * x, axis=-1, keepdims=True) + EPS) * g


def _dot(a, b):
    return jnp.dot(a, b, preferred_element_type=F32)


def _dot_nt(a, b):
    return lax.dot_general(a, b, (((1,), (1,)), ((), ())), preferred_element_type=F32)


def _dot_tn(a, b):
    return lax.dot_general(a, b, (((0,), (0,)), ((), ())), preferred_element_type=F32)


def _dot_hi(a, b):
    return jnp.dot(a, b, preferred_element_type=F32, precision=lax.Precision.HIGHEST)


MOD_TK = 512
MOD_TN = 2048


def _mod_kernel(c_ref, w_ref, b_ref, o_ref, acc_ref):
    k = pl.program_id(2)

    @pl.when(k == 0)
    def _():
        acc_ref[...] = jnp.zeros_like(acc_ref)

    w = w_ref[...]
    for r in range(N_SETS):
        cv = c_ref[:, r:r + 1]
        sv = cv * jax.nn.sigmoid(cv)
        acc_ref[r] += (sv * w).reshape(MOD_TK // 8, 8, MOD_TN).sum(axis=0)

    @pl.when(k == pl.num_programs(2) - 1)
    def _():
        o_ref[...] = jnp.zeros_like(o_ref)
        for r in range(N_SETS):
            o_ref[r:r + 1, :] = acc_ref[r].sum(axis=0, keepdims=True) + b_ref[...]


def _modulation(cvec_t, w_mod, b_mod):
    return pl.pallas_call(
        _mod_kernel,
        grid=(DEPTH, 6 * D_MODEL // MOD_TN, D_MODEL // MOD_TK),
        in_specs=[
            pl.BlockSpec((MOD_TK, 8), lambda l, n, k: (k, 0)),
            pl.BlockSpec((None, MOD_TK, MOD_TN), lambda l, n, k: (l, k, n)),
            pl.BlockSpec((None, 1, MOD_TN), lambda l, n, k: (l, 0, n)),
        ],
        out_specs=pl.BlockSpec((None, 8, MOD_TN), lambda l, n, k: (l, 0, n)),
        out_shape=jax.ShapeDtypeStruct((DEPTH, 8, 6 * D_MODEL), F32),
        scratch_shapes=[pltpu.VMEM((N_SETS, 8, MOD_TN), F32)],
        compiler_params=_cparams("parallel", "parallel", "arbitrary"),
        name="modulation",
    )(cvec_t, w_mod, b_mod.reshape(DEPTH, 1, 6 * D_MODEL))


PROJ_TM = 1024


def _proj_kernel(x_ref, m_ref, g_ref, w_ref, o_ref, h_ref):
    @pl.when(pl.program_id(1) == 0)
    def _():
        h = _rms(x_ref[...], g_ref[...]) * (1.0 + m_ref[1:2, :]) + m_ref[0:1, :]
        h_ref[...] = h.astype(BF16)

    o_ref[...] = _dot(h_ref[...], w_ref[...])


def _project(x, mod4, norm_g, w_in_p, l):
    return pl.pallas_call(
        _proj_kernel,
        grid=(NT // PROJ_TM, PROJ_W // PROJ_TN),
        in_specs=[
            pl.BlockSpec((PROJ_TM, D_MODEL), lambda i, j: (i, 0)),
            pl.BlockSpec((None, None, 6, D_MODEL), lambda i, j: (l, _mod_set(i * PROJ_TM), 0, 0)),
            pl.BlockSpec((None, 1, D_MODEL), lambda i, j: (l, 0, 0)),
            pl.BlockSpec((None, D_MODEL, PROJ_TN), lambda i, j: (l, 0, j)),
        ],
        out_specs=pl.BlockSpec((PROJ_TM, PROJ_TN), lambda i, j: (i, j)),
        out_shape=jax.ShapeDtypeStruct((NT, PROJ_W), F32),
        scratch_shapes=[pltpu.VMEM((PROJ_TM, D_MODEL), BF16)],
        compiler_params=_cparams("parallel", "arbitrary"),
        name="adaln_in_proj",
    )(x, mod4, norm_g, w_in_p)


def _rope(x, cos, sin, quarter):
    width = x.shape[-1]
    lane = lax.broadcasted_iota(jnp.int32, x.shape, 1)
    first = (lane % (2 * quarter)) < quarter
    partner = jnp.where(first, pltpu.roll(x, width - quarter, 1), pltpu.roll(x, quarter, 1))
    return x * cos + partner * sin


def _rope_tables(n_tok, rot_dim):
    half = rot_dim // 2
    pos = jnp.arange(n_tok)
    row = (pos // GRID_W).astype(F32)
    col = (pos % GRID_W).astype(F32)
    inv = ROPE_BASE ** (-jnp.arange(0, half, 2, dtype=F32) / half)
    a_row = row[:, None] * inv[None, :]
    a_col = col[:, None] * inv[None, :]
    ang = jnp.concatenate([a_row, a_row, a_col, a_col], axis=-1)
    sign = jnp.tile(jnp.concatenate([-jnp.ones(half // 2, F32), jnp.ones(half // 2, F32)]), 2)
    return jnp.cos(ang), jnp.sin(ang) * sign[None, :]


def _log_sigmoid(x):
    return jnp.minimum(x, 0.0) - jnp.log(1.0 + jnp.exp(-jnp.abs(x)))


def _mlstm_kernel(*refs, n_tok, n_seq, has_state):
    if has_state:
        (q_ref, k_ref, v_ref, o_ref, tail_ref, gt_ref, bcol_ref, brow_ref, ng_ref, c0_ref, n0_ref, m0_ref,
         out_ref, hf_ref, hb_ref, cst_ref, nst_ref, mst_ref) = refs
    else:
        (q_ref, k_ref, v_ref, o_ref, tail_ref, gt_ref, bcol_ref, brow_ref, ng_ref,
         out_ref, cs_ref, ns_ref, ms_ref, hf_ref, hb_ref, cst_ref, nst_ref, mst_ref) = refs
    n_chunks = n_tok // ML_CHUNK
    scale = ML_DH ** -0.5

    if has_state:
        cst_ref[...] = c0_ref[...]
        nst_ref[...] = n0_ref[...]
        mst_ref[...] = m0_ref[...]
    else:
        cst_ref[...] = jnp.zeros_like(cst_ref)
        nst_ref[...] = jnp.zeros_like(nst_ref)
        mst_ref[...] = jnp.zeros_like(mst_ref)

    row = lax.broadcasted_iota(jnp.int32, (ML_CHUNK, ML_CHUNK), 0)
    col = lax.broadcasted_iota(jnp.int32, (ML_CHUNK, ML_CHUNK), 1)
    lower = (col <= row)
    upper = (col >= row)
    lower_f = lower.astype(F32)
    upper_f = upper.astype(F32)

    def scan_seq(c, b):
        for d in range(2):
            cc = c if d == 0 else n_chunks - 1 - c
            t0 = pl.multiple_of(b * n_tok + cc * ML_CHUNK, ML_CHUNK)
            rows = pl.ds(t0, ML_CHUNK)
            g_col = tail_ref[rows, :] + brow_ref[...]
            g_row = gt_ref[b * n_chunks + cc] + bcol_ref[...]
            lf_col = _log_sigmoid(g_col)
            lf_row = _log_sigmoid(g_row)
            if d == 0:
                bcum_col = _dot_hi(lower_f, lf_col)
                bcum_row = _dot_hi(lf_row, upper_f)
                mask = lower
            else:
                bcum_col = _dot_hi(upper_f, lf_col)
                bcum_row = _dot_hi(lf_row, lower_f)
                mask = upper
            for h in range(ML_HEADS):
                r = d * ML_HEADS + h
                lanes = slice(h * ML_DH, (h + 1) * ML_DH)
                ci = TAIL_G + r
                cf = TAIL_G + 2 * ML_HEADS + r
                ig_c = g_col[:, ci:ci + 1]
                b_c = bcum_col[:, cf:cf + 1]
                ig_r = g_row[r:r + 1, :]
                b_r = bcum_row[2 * ML_HEADS + r:2 * ML_HEADS + r + 1, :]
                tot = jnp.sum(lf_row[2 * ML_HEADS + r:2 * ML_HEADS + r + 1, :], axis=1, keepdims=True)
                m_prev = mst_ref[b, r:r + 1, 0:1]
                n_prev = nst_ref[b, r:r + 1, :]
                c_prev = cst_ref[b, r]
                q = q_ref[rows, lanes]
                k = k_ref[rows, lanes]
                v = v_ref[rows, lanes].astype(BF16)
                qb = q.astype(BF16)

                dmat = jnp.where(mask, b_c - b_r + ig_r, NEG)
                inter = b_c + m_prev
                mt = jnp.maximum(inter, jnp.max(dmat, axis=1, keepdims=True))
                w = jnp.exp(dmat - mt)
                s = _dot_nt(qb, k.astype(BF16)) * scale * w
                a = jnp.exp(inter - mt)
                num = _dot(s.astype(BF16), v) + a * _dot(qb, c_prev.astype(BF16))
                den = jnp.sum(s, axis=1, keepdims=True) + a * jnp.sum(q * n_prev, axis=1, keepdims=True)
                hc = num / jnp.maximum(jnp.abs(den), jnp.exp(-mt))
                if d == 0:
                    hf_ref[rows, lanes] = hc
                else:
                    hb_ref[rows, lanes] = hc

                wlog_c = tot - b_c + ig_c
                wlog_r = tot - b_r + ig_r
                m_new = jnp.maximum(tot + m_prev, jnp.max(wlog_r, axis=1, keepdims=True))
                decay = jnp.exp(tot + m_prev - m_new)
                kw = k * (scale * jnp.exp(wlog_c - m_new))
                cst_ref[b, r] = decay * c_prev + _dot_tn(kw.astype(BF16), v)
                nst_ref[b, r:r + 1, :] = decay * n_prev + jnp.sum(kw, axis=0, keepdims=True)
                mst_ref[b, r:r + 1, :] = jnp.broadcast_to(m_new, (1, ML_DH))

    def chunk_step(c, carry):
        for b in range(n_seq):
            scan_seq(c, b)
        return carry

    lax.fori_loop(0, n_chunks, chunk_step, 0)

    for h in range(ML_HEADS):
        lanes = slice(h * ML_DH, (h + 1) * ML_DH)
        hs = hf_ref[:, lanes] + hb_ref[:, lanes]
        out_ref[:, lanes] = _rms(hs, ng_ref[:, lanes]) * jax.nn.sigmoid(o_ref[:, lanes])

    if not has_state:
        cs_ref[...] = cst_ref[...]
        ns_ref[...] = nst_ref[...]
        ms_ref[...] = mst_ref[...]


ML_SEQ_PER_STEP = 4


def _mlstm(proj, gt, bcol, brow, norm_g, l, *, latent, states=None):
    n_tok = DEC_SEQ if latent else SEQ
    n_seq = 1 if latent else ML_SEQ_PER_STEP
    n_b = DEC_BATCH if latent else BATCH
    rows = n_seq * n_tok
    blk0 = N_CTX // rows if latent else 0
    n_chunks = n_tok // ML_CHUNK

    def col_spec(c0):
        return pl.BlockSpec((rows, 512), lambda b: (blk0 + b, c0 // 512))

    in_specs = [
        col_spec(C_MLQ), col_spec(C_MLK), col_spec(C_MLV), col_spec(C_MLO),
        pl.BlockSpec((rows, 128), lambda b: (blk0 + b, C_TAIL // 128)),
        pl.BlockSpec((n_seq * n_chunks, 16, ML_CHUNK), lambda b: (blk0 + b, 0, 0)),
        pl.BlockSpec((16, 1), lambda b: (0, 0)),
        pl.BlockSpec((1, 128), lambda b: (0, 0)),
        pl.BlockSpec((None, 1, GROUP_W), lambda b: (l, 0, 0)),
    ]
    args = [proj, proj, proj, proj, proj, gt, bcol, brow, norm_g]
    n_chain = 2 * ML_HEADS
    scratch = [pltpu.VMEM((rows, GROUP_W), F32), pltpu.VMEM((rows, GROUP_W), F32),
               pltpu.VMEM((n_seq, n_chain, ML_DH, ML_DH), F32), pltpu.VMEM((n_seq, n_chain, ML_DH), F32),
               pltpu.VMEM((n_seq, n_chain, ML_DH), F32)]
    if latent:
        c0, n0, m0 = states
        in_specs += [
            pl.BlockSpec((n_seq, None, n_chain, ML_DH, ML_DH), lambda b: (b, l, 0, 0, 0)),
            pl.BlockSpec((n_seq, None, n_chain, ML_DH), lambda b: (b, l, 0, 0)),
            pl.BlockSpec((n_seq, None, n_chain, ML_DH), lambda b: (b, l, 0, 0)),
        ]
        args += [c0, n0, m0]
        out_shape = jax.ShapeDtypeStruct((N_LAT, GROUP_W), F32)
        out_specs = pl.BlockSpec((rows, GROUP_W), lambda b: (b, 0))
    else:
        out_shape = (jax.ShapeDtypeStruct((N_CTX, GROUP_W), F32),
                     jax.ShapeDtypeStruct((BATCH, n_chain, ML_DH, ML_DH), F32),
                     jax.ShapeDtypeStruct((BATCH, n_chain, ML_DH), F32),
                     jax.ShapeDtypeStruct((BATCH, n_chain, ML_DH), F32))
        out_specs = (pl.BlockSpec((rows, GROUP_W), lambda b: (b, 0)),
                     pl.BlockSpec((n_seq, n_chain, ML_DH, ML_DH), lambda b: (b, 0, 0, 0)),
                     pl.BlockSpec((n_seq, n_chain, ML_DH), lambda b: (b, 0, 0)),
                     pl.BlockSpec((n_seq, n_chain, ML_DH), lambda b: (b, 0, 0)))
    return pl.pallas_call(
        functools.partial(_mlstm_kernel, n_tok=n_tok, n_seq=n_seq, has_state=latent),
        grid=(n_b // n_seq,),
        in_specs=in_specs,
        out_specs=out_specs,
        out_shape=out_shape,
        scratch_shapes=scratch,
        compiler_params=_cparams("parallel"),
        name="mlstm_latent" if latent else "mlstm_context",
    )(*args)


def _softmax_parts(scores, sink=None):
    m = jnp.max(scores[0], axis=1, keepdims=True)
    for s in scores[1:]:
        m = jnp.maximum(m, jnp.max(s, axis=1, keepdims=True))
    if sink is not None:
        m = jnp.maximum(m, sink)
    es = [jnp.exp(s - m) for s in scores]
    den = jnp.sum(es[0], axis=1, keepdims=True)
    for e in es[1:]:
        den = den + jnp.sum(e, axis=1, keepdims=True)
    if sink is not None:
        den = den + jnp.exp(sink - m)
    return es, den


ATT_TQ = 256
LAT_QB = DEC_SEQ // ATT_TQ
N_KEYS_LAT = DEC_SEQ + PAST_LEN


def _mla_q(cq_ref, qg_ref, wuq_ref):
    return _dot(_rms(cq_ref[...], qg_ref[...]).astype(BF16), wuq_ref[...])


def _mla_heads(q, kv, kr, n_keys):
    outs = []
    for h in range(MLA_HEADS):
        qn = q[:, h * MLA_D_NOPE:(h + 1) * MLA_D_NOPE].astype(BF16)
        r0 = MLA_HEADS * MLA_D_NOPE + h * MLA_D_ROPE
        qr = q[:, r0:r0 + MLA_D_ROPE].astype(BF16)
        kn = kv[:, h * 256:h * 256 + MLA_D_NOPE]
        v = kv[:, h * 256 + MLA_D_NOPE:(h + 1) * 256]
        s = (_dot_nt(qn, kn) + _dot_nt(qr, kr)) * MLA_SCALE
        (e,), den = _softmax_parts([s])
        outs.append(_dot(e.astype(BF16), v) / den)
    return outs


def _mla_ctx_kernel(cq_ref, ckv_ref, tail_ref, qg_ref, kg_ref, wuq_ref, wukv_ref, out_ref, ckvn_ref):
    q = _mla_q(cq_ref, qg_ref, wuq_ref)
    ckvn = _rms(ckv_ref[...], kg_ref[...])
    ckvn_ref[...] = ckvn
    kv = _dot(ckvn.astype(BF16), wukv_ref[...]).astype(BF16)
    kr = tail_ref[:, 0:MLA_D_ROPE].astype(BF16)
    outs = _mla_heads(q, kv, kr, SEQ)
    for h in range(MLA_HEADS):
        out_ref[:, h * MLA_D_V:(h + 1) * MLA_D_V] = outs[h]


MLA_D_V = GROUP_W // MLA_HEADS


def _mla_lat_kernel(cq_ref, ckv_ref, tail_ref, qg_ref, kg_ref, wuq_ref, wukv_ref, ckvc_ref, krc_ref,
                    cosq_ref, sinq_ref, cosk_ref, sink_ref, out_ref, kv_ref, kr_ref):
    @pl.when(pl.program_id(1) == 0)
    def _():
        ckvn = _rms(ckv_ref[...], kg_ref[...])
        kv_ref[0:DEC_SEQ, :] = _dot(ckvn.astype(BF16), wukv_ref[...]).astype(BF16)
        kv_ref[DEC_SEQ:N_KEYS_LAT, :] = _dot(ckvc_ref[...].astype(BF16), wukv_ref[...]).astype(BF16)
        kr = _rope(tail_ref[...], cosk_ref[...], sink_ref[...], MLA_D_ROPE // 4)
        kr_ref[0:DEC_SEQ, :] = kr[:, 0:MLA_D_ROPE].astype(BF16)
        kr_ref[DEC_SEQ:N_KEYS_LAT, :] = krc_ref[...].astype(BF16)

    q = _mla_q(cq_ref, qg_ref, wuq_ref)
    n0 = MLA_HEADS * MLA_D_NOPE
    q_rope = _rope(q[:, n0:], cosq_ref[...], sinq_ref[...], MLA_D_ROPE // 4)
    q = jnp.concatenate([q[:, :n0], q_rope], axis=1)
    outs = _mla_heads(q, kv_ref[...], kr_ref[...], N_KEYS_LAT)
    for h in range(MLA_HEADS):
        out_ref[:, h * MLA_D_V:(h + 1) * MLA_D_V] = outs[h]


def _w_specs2(shape_a, shape_b, l):
    return [pl.BlockSpec((None,) + shape_a, lambda *_: (l,) + (0,) * len(shape_a)),
            pl.BlockSpec((None,) + shape_b, lambda *_: (l,) + (0,) * len(shape_b))]


def _mla_ctx(proj, qg, kg, wuq, wukv, l):
    return pl.pallas_call(
        _mla_ctx_kernel,
        grid=(BATCH,),
        in_specs=[
            pl.BlockSpec((SEQ, 256), lambda b: (b, C_CQ // 256)),
            pl.BlockSpec((SEQ, 256), lambda b: (b, C_CKV // 256)),
            pl.BlockSpec((SEQ, 128), lambda b: (b, C_TAIL // 128)),
            pl.BlockSpec((None, 1, MLA_Q_RANK), lambda b: (l, 0, 0)),
            pl.BlockSpec((None, 1, MLA_KV_RANK), lambda b: (l, 0, 0)),
            pl.BlockSpec((None, MLA_Q_RANK, 768), lambda b: (l, 0, 0)),
            pl.BlockSpec((None, MLA_KV_RANK, 1024), lambda b: (l, 0, 0)),
        ],
        out_specs=(pl.BlockSpec((SEQ, GROUP_W), lambda b: (b, 0)),
                   pl.BlockSpec((SEQ, MLA_KV_RANK), lambda b: (b, 0))),
        out_shape=(jax.ShapeDtypeStruct((N_CTX, GROUP_W), F32),
                   jax.ShapeDtypeStruct((N_CTX, MLA_KV_RANK), F32)),
        compiler_params=_cparams("parallel"),
        name="mla_context",
    )(proj, proj, proj, qg, kg, wuq, wukv)


def _mla_lat(proj, qg, kg, wuq, wukv, cache_ckv, cache_kr, cos64, sin64, l):
    qb0 = N_CTX // ATT_TQ
    bb0 = N_CTX // DEC_SEQ
    cosq = jnp.tile(cos64, (1, MLA_HEADS))
    sinq = jnp.tile(sin64, (1, MLA_HEADS))
    cosk = jnp.tile(cos64, (1, 2))
    sink = jnp.tile(sin64, (1, 2))
    return pl.pallas_call(
        _mla_lat_kernel,
        grid=(DEC_BATCH, LAT_QB),
        in_specs=[
            pl.BlockSpec((ATT_TQ, 256), lambda b, i: (qb0 + b * LAT_QB + i, C_CQ // 256)),
            pl.BlockSpec((DEC_SEQ, 256), lambda b, i: (bb0 + b, C_CKV // 256)),
            pl.BlockSpec((DEC_SEQ, 128), lambda b, i: (bb0 + b, C_TAIL // 128)),
            pl.BlockSpec((None, 1, MLA_Q_RANK), lambda b, i: (l, 0, 0)),
            pl.BlockSpec((None, 1, MLA_KV_RANK), lambda b, i: (l, 0, 0)),
            pl.BlockSpec((None, MLA_Q_RANK, 768), lambda b, i: (l, 0, 0)),
            pl.BlockSpec((None, MLA_KV_RANK, 1024), lambda b, i: (l, 0, 0)),
            pl.BlockSpec((None, None, PAST_LEN, MLA_KV_RANK), lambda b, i: (b, l, 0, 0)),
            pl.BlockSpec((None, None, PAST_LEN, MLA_D_ROPE), lambda b, i: (b, l, 0, 0)),
            pl.BlockSpec((ATT_TQ, 256), lambda b, i: (i, 0)),
            pl.BlockSpec((ATT_TQ, 256), lambda b, i: (i, 0)),
            pl.BlockSpec((DEC_SEQ, 128), lambda b, i: (0, 0)),
            pl.BlockSpec((DEC_SEQ, 128), lambda b, i: (0, 0)),
        ],
        out_specs=pl.BlockSpec((ATT_TQ, GROUP_W), lambda b, i: (b * LAT_QB + i, 0)),
        out_shape=jax.ShapeDtypeStruct((N_LAT, GROUP_W), F32),
        scratch_shapes=[pltpu.VMEM((N_KEYS_LAT, 1024), BF16), pltpu.VMEM((N_KEYS_LAT, MLA_D_ROPE), BF16)],
        compiler_params=_cparams("parallel", "arbitrary"),
        name="mla_latent",
    )(proj, proj, proj, qg, kg, wuq, wukv, cache_ckv, cache_kr, cosq, sinq, cosk, sink)


SWA_SCALE = SWA_DH ** -0.5
SWA_REP = SWA_HEADS // SWA_KV_HEADS
SWA_KWIN = ATT_TQ + 2 * WINDOW


def _swa_ctx_kernel(sink_ref, q_ref, k_ref, v_ref, out_ref):
    kb = k_ref[...].astype(BF16)
    vb = v_ref[...].astype(BF16)
    for h in range(SWA_HEADS):
        g = h // SWA_REP
        q = q_ref[:, h * SWA_DH:(h + 1) * SWA_DH].astype(BF16)
        s = _dot_nt(q, kb[:, g * SWA_DH:(g + 1) * SWA_DH]) * SWA_SCALE
        (e,), den = _softmax_parts([s], sink=sink_ref[0, h])
        out_ref[:, h * SWA_DH:(h + 1) * SWA_DH] = _dot(e.astype(BF16), vb[:, g * SWA_DH:(g + 1) * SWA_DH]) / den


def _swa_lat_kernel(sink_ref, q_ref, k_ref, v_ref, kc_ref, vc_ref, cosq_ref, sinq_ref, cosk_ref, sink_t_ref,
                    out_ref, kr_ref):
    i = pl.program_id(1)

    @pl.when(i == 0)
    def _():
        kr_ref[...] = _rope(k_ref[...], cosk_ref[...], sink_t_ref[...], SWA_DH // 4).astype(BF16)

    q_all = _rope(q_ref[...], cosq_ref[...], sinq_ref[...], SWA_DH // 4)
    k0 = pl.multiple_of(jnp.clip(i * ATT_TQ - WINDOW, 0, DEC_SEQ - SWA_KWIN), WINDOW)
    kwin = kr_ref[pl.ds(k0, SWA_KWIN), :]
    vwin = v_ref[pl.ds(k0, SWA_KWIN), :].astype(BF16)
    kc = kc_ref[...].astype(BF16)
    vc = vc_ref[...].astype(BF16)
    qpos = i * ATT_TQ + lax.broadcasted_iota(jnp.int32, (ATT_TQ, SWA_KWIN), 0)
    kpos = k0 + lax.broadcasted_iota(jnp.int32, (ATT_TQ, SWA_KWIN), 1)
    band = jnp.abs(qpos - kpos) <= WINDOW
    for h in range(SWA_HEADS):
        g = h // SWA_REP
        gl = slice(g * SWA_DH, (g + 1) * SWA_DH)
        q = q_all[:, h * SWA_DH:(h + 1) * SWA_DH].astype(BF16)
        s_loc = jnp.where(band, _dot_nt(q, kwin[:, gl]) * SWA_SCALE, NEG)
        s_ctx = _dot_nt(q, kc[:, gl]) * SWA_SCALE
        (e_loc, e_ctx), den = _softmax_parts([s_loc, s_ctx], sink=sink_ref[0, h])
        o = _dot(e_loc.astype(BF16), vwin[:, gl]) + _dot(e_ctx.astype(BF16), vc[:, gl])
        out_ref[:, h * SWA_DH:(h + 1) * SWA_DH] = o / den


def _smem_spec():
    return pl.BlockSpec(memory_space=pltpu.SMEM)


def _swa_ctx(proj, sink, l):
    return pl.pallas_call(
        _swa_ctx_kernel,
        grid=(BATCH,),
        in_specs=[
            _smem_spec(),
            pl.BlockSpec((SEQ, 512), lambda b: (b, C_SQ // 512)),
            pl.BlockSpec((SEQ, 256), lambda b: (b, C_SK // 256)),
            pl.BlockSpec((SEQ, 256), lambda b: (b, C_SV // 256)),
        ],
        out_specs=pl.BlockSpec((SEQ, GROUP_W), lambda b: (b, 0)),
        out_shape=jax.ShapeDtypeStruct((N_CTX, GROUP_W), F32),
        compiler_params=_cparams("parallel"),
        name="swa_context",
    )(sink, proj, proj, proj)


def _swa_lat(proj, sink, cache_k, cache_v, cos128, sin128, l):
    qb0 = N_CTX // ATT_TQ
    bb0 = N_CTX // DEC_SEQ
    kvw = SWA_KV_HEADS * SWA_DH
    return pl.pallas_call(
        _swa_lat_kernel,
        grid=(DEC_BATCH, LAT_QB),
        in_specs=[
            _smem_spec(),
            pl.BlockSpec((ATT_TQ, 512), lambda b, i: (qb0 + b * LAT_QB + i, C_SQ // 512)),
            pl.BlockSpec((DEC_SEQ, 256), lambda b, i: (bb0 + b, C_SK // 256)),
            pl.BlockSpec((DEC_SEQ, 256), lambda b, i: (bb0 + b, C_SV // 256)),
            pl.BlockSpec((None, None, PAST_LEN, kvw), lambda b, i: (b, l, 0, 0)),
            pl.BlockSpec((None, None, PAST_LEN, kvw), lambda b, i: (b, l, 0, 0)),
            pl.BlockSpec((ATT_TQ, 512), lambda b, i: (i, 0)),
            pl.BlockSpec((ATT_TQ, 512), lambda b, i: (i, 0)),
            pl.BlockSpec((DEC_SEQ, 256), lambda b, i: (0, 0)),
            pl.BlockSpec((DEC_SEQ, 256), lambda b, i: (0, 0)),
        ],
        out_specs=pl.BlockSpec((ATT_TQ, GROUP_W), lambda b, i: (b * LAT_QB + i, 0)),
        out_shape=jax.ShapeDtypeStruct((N_LAT, GROUP_W), F32),
        scratch_shapes=[pltpu.VMEM((DEC_SEQ, kvw), BF16)],
        compiler_params=_cparams("parallel", "arbitrary"),
        name="swa_latent",
    )(sink, proj, proj, proj, cache_k, cache_v,
      jnp.tile(cos128, (1, SWA_HEADS)), jnp.tile(sin128, (1, SWA_HEADS)),
      jnp.tile(cos128, (1, SWA_KV_HEADS)), jnp.tile(sin128, (1, SWA_KV_HEADS)))


DIFF_SCALE = DIFF_DH ** -0.5


def _diff_lambda(lq1_ref, lk1_ref, lq2_ref, lk2_ref):
    a = jnp.sum(lq1_ref[...] * lk1_ref[...], axis=1, keepdims=True)
    b = jnp.sum(lq2_ref[...] * lk2_ref[...], axis=1, keepdims=True)
    return jnp.exp(a) - jnp.exp(b)


def _diff_heads(q, k_parts, v_parts, lam, lam_init, ng_ref, out_ref):
    for h in range(DIFF_HEADS):
        ps = []
        for c in range(2):
            sl = slice(h * 2 * DIFF_DH + c * DIFF_DH, h * 2 * DIFF_DH + (c + 1) * DIFF_DH)
            qc = q[:, sl].astype(BF16)
            es, den = _softmax_parts([_dot_nt(qc, kp[:, sl]) * DIFF_SCALE for kp in k_parts])
            ps.append([e / den for e in es])
        vl = slice(h * 2 * DIFF_DH, (h + 1) * 2 * DIFF_DH)
        o = None
        for p1, p2, vp in zip(ps[0], ps[1], v_parts):
            t = _dot((p1 - lam * p2).astype(BF16), vp[:, vl])
            o = t if o is None else o + t
        out_ref[:, vl] = _rms(o, ng_ref[...]) * (1.0 - lam_init)


def _diff_ctx_kernel(q_ref, k_ref, v_ref, lq1_ref, lk1_ref, lq2_ref, lk2_ref, ng_ref, out_ref, *, lam_init):
    lam = _diff_lambda(lq1_ref, lk1_ref, lq2_ref, lk2_ref) + lam_init
    _diff_heads(q_ref[...], [k_ref[...].astype(BF16)], [v_ref[...].astype(BF16)], lam, lam_init, ng_ref, out_ref)


def _diff_lat_kernel(q_ref, k_ref, v_ref, kc_ref, vc_ref, lq1_ref, lk1_ref, lq2_ref, lk2_ref, ng_ref,
                     cosq_ref, sinq_ref, cosk_ref, sink_ref, out_ref, kr_ref, *, lam_init):
    @pl.when(pl.program_id(1) == 0)
    def _():
        kr_ref[...] = _rope(k_ref[...], cosk_ref[...], sink_ref[...], DIFF_DH // 4).astype(BF16)

    lam = _diff_lambda(lq1_ref, lk1_ref, lq2_ref, lk2_ref) + lam_init
    q = _rope(q_ref[...], cosq_ref[...], sinq_ref[...], DIFF_DH // 4)
    _diff_heads(q, [kr_ref[...], kc_ref[...].astype(BF16)], [v_ref[...].astype(BF16), vc_ref[...].astype(BF16)],
                lam, lam_init, ng_ref, out_ref)


def _vec_specs(n, width, l, nargs):
    return [pl.BlockSpec((None, 1, width), lambda *_: (l, 0, 0)) for _ in range(n)]


def _diff_ctx(proj, lq1, lk1, lq2, lk2, ng, l):
    lam_init = 0.8 - 0.6 * math.exp(-0.3 * l)
    return pl.pallas_call(
        functools.partial(_diff_ctx_kernel, lam_init=lam_init),
        grid=(BATCH,),
        in_specs=[
            pl.BlockSpec((SEQ, 512), lambda b: (b, C_DQ // 512)),
            pl.BlockSpec((SEQ, 512), lambda b: (b, C_DK // 512)),
            pl.BlockSpec((SEQ, 512), lambda b: (b, C_DV // 512)),
        ] + _vec_specs(4, DIFF_DH, l, 1) + _vec_specs(1, 2 * DIFF_DH, l, 1),
        out_specs=pl.BlockSpec((SEQ, GROUP_W), lambda b: (b, 0)),
        out_shape=jax.ShapeDtypeStruct((N_CTX, GROUP_W), F32),
        compiler_params=_cparams("parallel"),
        name="diff_context",
    )(proj, proj, proj, lq1, lk1, lq2, lk2, ng)


def _diff_lat(proj, lq1, lk1, lq2, lk2, ng, cache_k, cache_v, cos64, sin64, l):
    lam_init = 0.8 - 0.6 * math.exp(-0.3 * l)
    qb0 = N_CTX // ATT_TQ
    bb0 = N_CTX // DEC_SEQ
    cos_t = jnp.tile(cos64, (1, 2 * DIFF_HEADS))
    sin_t = jnp.tile(sin64, (1, 2 * DIFF_HEADS))
    return pl.pallas_call(
        functools.partial(_diff_lat_kernel, lam_init=lam_init),
        grid=(DEC_BATCH, LAT_QB),
        in_specs=[
            pl.BlockSpec((ATT_TQ, 512), lambda b, i: (qb0 + b * LAT_QB + i, C_DQ // 512)),
            pl.BlockSpec((DEC_SEQ, 512), lambda b, i: (bb0 + b, C_DK // 512)),
            pl.BlockSpec((DEC_SEQ, 512), lambda b, i: (bb0 + b, C_DV // 512)),
            pl.BlockSpec((None, None, PAST_LEN, GROUP_W), lambda b, i: (b, l, 0, 0)),
            pl.BlockSpec((None, None, PAST_LEN, GROUP_W), lambda b, i: (b, l, 0, 0)),
        ] + _vec_specs(4, DIFF_DH, l, 2) + _vec_specs(1, 2 * DIFF_DH, l, 2) + [
            pl.BlockSpec((ATT_TQ, 512), lambda b, i: (i, 0)),
            pl.BlockSpec((ATT_TQ, 512), lambda b, i: (i, 0)),
            pl.BlockSpec((DEC_SEQ, 512), lambda b, i: (0, 0)),
            pl.BlockSpec((DEC_SEQ, 512), lambda b, i: (0, 0)),
        ],
        out_specs=pl.BlockSpec((ATT_TQ, GROUP_W), lambda b, i: (b * LAT_QB + i, 0)),
        out_shape=jax.ShapeDtypeStruct((N_LAT, GROUP_W), F32),
        scratch_shapes=[pltpu.VMEM((DEC_SEQ, GROUP_W), BF16)],
        compiler_params=_cparams("parallel", "arbitrary"),
        name="diff_latent",
    )(proj, proj, proj, cache_k, cache_v, lq1, lk1, lq2, lk2, ng, cos_t, sin_t, cos_t, sin_t)


OUT_TM = 256


OUT_CTX_BLOCKS = N_CTX // OUT_TM


def _out_kernel(*refs):
    ctx_refs, lat_refs = refs[0:4], refs[4:8]
    x_ref, mod_ref, w_ref, x1_ref = refs[8:]
    is_ctx = pl.program_id(0) < OUT_CTX_BLOCKS
    acc = None
    for g in range(4):
        m = jnp.where(is_ctx, ctx_refs[g][...], lat_refs[g][...])
        t = _dot(m.astype(BF16), w_ref[g * GROUP_W:(g + 1) * GROUP_W, :])
        acc = t if acc is None else acc + t
    x1_ref[...] = x_ref[...] + mod_ref[2:3, :] * acc


def _out_proj(mixed_ctx, mixed_lat, x, mod4, w_out, l):
    row = lambda i: (i, 0)
    ctx_row = lambda i: (jnp.minimum(i, OUT_CTX_BLOCKS - 1), 0)
    lat_row = lambda i: (jnp.maximum(i - OUT_CTX_BLOCKS, 0), 0)
    return pl.pallas_call(
        _out_kernel,
        grid=(NT // OUT_TM,),
        in_specs=[pl.BlockSpec((OUT_TM, GROUP_W), ctx_row) for _ in range(4)]
        + [pl.BlockSpec((OUT_TM, GROUP_W), lat_row) for _ in range(4)] + [
            pl.BlockSpec((OUT_TM, D_MODEL), row),
            pl.BlockSpec((None, None, 6, D_MODEL), lambda i: (l, _mod_set(i * OUT_TM), 0, 0)),
            pl.BlockSpec((None, D_MODEL, D_MODEL), lambda i: (l, 0, 0)),
        ],
        out_specs=pl.BlockSpec((OUT_TM, D_MODEL), row),
        out_shape=jax.ShapeDtypeStruct((NT, D_MODEL), F32),
        compiler_params=_cparams("parallel"),
        name="out_proj_residual",
    )(*mixed_ctx, *mixed_lat, x, mod4, w_out)


def _peerq_kernel(x_ref, mod_ref, g_ref, w_ref, ht_ref, q_ref):
    h = _rms(x_ref[...], g_ref[...]) * (1.0 + mod_ref[4:5, :]) + mod_ref[3:4, :]
    ht_ref[...] = h.T.astype(BF16)
    q_ref[...] = _dot(h.astype(BF16), w_ref[...])


def _peer_query(x1, mod4, norm_g, w_q, l):
    row = lambda i: (i, 0)
    return pl.pallas_call(
        _peerq_kernel,
        grid=(NT // OUT_TM,),
        in_specs=[
            pl.BlockSpec((OUT_TM, D_MODEL), row),
            pl.BlockSpec((None, None, 6, D_MODEL), lambda i: (l, _mod_set(i * OUT_TM), 0, 0)),
            pl.BlockSpec((None, 1, D_MODEL), lambda i: (l, 0, 0)),
            pl.BlockSpec((None, D_MODEL, PEER_HEADS * PEER_QDIM), lambda i: (l, 0, 0)),
        ],
        out_specs=(pl.BlockSpec((D_MODEL, OUT_TM), lambda i: (0, i)),
                   pl.BlockSpec((OUT_TM, PEER_HEADS * PEER_QDIM), row)),
        out_shape=(jax.ShapeDtypeStruct((D_MODEL, NT), BF16),
                   jax.ShapeDtypeStruct((NT, PEER_HEADS * PEER_QDIM), F32)),
        compiler_params=_cparams("parallel"),
        name="adaln_peer_query",
    )(x1, mod4, norm_g, w_q)


ROUTE_TL = 256
ROUTE_LANES = 128
NOT_SEL = float(PEER_TOPK)


def _top16(s, index_ties):
    idx = lax.broadcasted_iota(jnp.int32, s.shape, 0).astype(F32)
    slot = lax.broadcasted_iota(jnp.int32, (PEER_TOPK, s.shape[1]), 0)
    rank = jnp.full(s.shape, NOT_SEL, F32)
    vals = jnp.zeros((PEER_TOPK, s.shape[1]), F32)
    for k in range(PEER_TOPK):
        m = jnp.max(s, axis=0, keepdims=True)
        sel = s == m
        if index_ties:
            sel = idx == jnp.min(jnp.where(sel, idx, float(PEER_NKEYS)), axis=0, keepdims=True)
        rank = jnp.where(sel, float(k), rank)
        s = jnp.where(sel, NEG, s)
        vals = jnp.where(slot == k, m, vals)
    return rank, vals


CAND_HALF = PEER_TOPK // 2
CAND_ROWS = PEER_TOPK + (CAND_HALF - 1) * CAND_HALF + CAND_HALF
FLAT_NONE = float(PEER_TOPK * PEER_TOPK)


def _cand_flat(tl):
    r = lax.broadcasted_iota(jnp.int32, (CAND_ROWS, tl), 0)
    mid = r - PEER_TOPK
    mid_flat = (1 + mid // CAND_HALF) * PEER_TOPK + mid % CAND_HALF
    last_flat = (CAND_HALF + r - (CAND_ROWS - CAND_HALF)) * PEER_TOPK
    flat = jnp.where(r < PEER_TOPK, r, jnp.where(r < CAND_ROWS - CAND_HALF, mid_flat, last_flat))
    return flat.astype(F32)


def _count(mask):
    return jnp.sum(jnp.where(mask, 1.0, 0.0), axis=0, keepdims=True)


def _route_head(s1, s2, flat, index_ties):
    tl = s1.shape[1]
    rank1, v1 = _top16(s1, index_ties)
    rank2, v2 = _top16(s2, index_ties)
    slabs = [v1[0:1, :] + v2]
    slabs += [v1[a:a + 1, :] + v2[0:CAND_HALF, :] for a in range(1, CAND_HALF)]
    slabs.append(v1[CAND_HALF:, :] + v2[0:1, :])
    cand = jnp.concatenate(slabs, axis=0)
    a_row = jnp.floor(flat * (1.0 / PEER_TOPK))
    top = v1[0:1, :] + v2[0:1, :]
    cnt1 = jnp.zeros(s1.shape, F32)
    z = jnp.zeros((1, tl), F32)
    for k in range(PEER_TOPK):
        m = jnp.max(cand, axis=0, keepdims=True)
        sel = cand == m
        if index_ties:
            first = jnp.min(jnp.where(sel, flat, FLAT_NONE), axis=0, keepdims=True)
            sel = flat == first
            a_sel = jnp.floor(first * (1.0 / PEER_TOPK))
        else:
            a_sel = jnp.max(jnp.where(sel, a_row, -1.0), axis=0, keepdims=True)
        cand = jnp.where(sel, NEG, cand)
        cnt1 = cnt1 + jnp.where(rank1 == a_sel, 1.0, 0.0)
        z = z + jnp.exp(m - top)
    e1 = jnp.exp(s1 - v1[0:1, :]) / z
    e2 = jnp.exp(s2 - v2[0:1, :])
    if index_ties:
        return rank2, cnt1, e1, e2, None
    full = float(PEER_TOPK)
    clean = ((_count(rank1 < NOT_SEL) == full) & (_count(rank2 < NOT_SEL) == full)
             & (_count(cand == NEG) == full))
    return rank2, cnt1, e1, e2, jnp.max(jnp.where(clean, 0.0, 1.0))


def _route_kernel(q_ref, k1_ref, k2_ref, rank2_ref, e2_ref, cnt1_ref, e1_ref):
    tl = q_ref.shape[0]
    half = PEER_QDIM // 2

    n_chunks = tl // ROUTE_LANES

    def scores(h):
        q1 = q_ref[:, h * PEER_QDIM:h * PEER_QDIM + half].astype(BF16)
        q2 = q_ref[:, h * PEER_QDIM + half:(h + 1) * PEER_QDIM].astype(BF16)
        return _dot_nt(k1_ref[...], q1), _dot_nt(k2_ref[...], q2)

    def route(h, c, s1, s2, index_ties):
        lanes = slice(c * ROUTE_LANES, (c + 1) * ROUTE_LANES)
        rank2, cnt1, e1, e2, tied = _route_head(s1[:, lanes], s2[:, lanes], _cand_flat(ROUTE_LANES), index_ties)
        rank2_ref[h, :, lanes] = rank2.astype(BF16)
        cnt1_ref[h, :, lanes] = cnt1
        e1_ref[h, :, lanes] = e1
        e2_ref[h, :, lanes] = e2.astype(BF16)
        return tied

    tied = {}
    for h in range(PEER_HEADS):
        s1, s2 = scores(h)
        for c in range(n_chunks):
            tied[h, c] = route(h, c, s1, s2, index_ties=False)

    for h in range(PEER_HEADS):
        for c in range(n_chunks):
            @pl.when(tied[h, c] > 0.0)
            def _():
                s1, s2 = scores(h)
                route(h, c, s1, s2, index_ties=True)


def _peer_route(q, keys, l):
    shp = jax.ShapeDtypeStruct((PEER_HEADS, PEER_NKEYS, NT), F32)
    shp_b = jax.ShapeDtypeStruct((PEER_HEADS, PEER_NKEYS, NT), BF16)
    spec = pl.BlockSpec((PEER_HEADS, PEER_NKEYS, ROUTE_TL), lambda i: (0, 0, i))
    half = PEER_QDIM // 2
    return pl.pallas_call(
        _route_kernel,
        grid=(NT // ROUTE_TL,),
        in_specs=[
            pl.BlockSpec((ROUTE_TL, PEER_HEADS * PEER_QDIM), lambda i: (i, 0)),
            pl.BlockSpec((None, None, PEER_NKEYS, half), lambda i: (l, 0, 0, 0)),
            pl.BlockSpec((None, None, PEER_NKEYS, half), lambda i: (l, 1, 0, 0)),
        ],
        out_specs=(spec, spec, spec, spec),
        out_shape=(shp_b, shp_b, shp, shp),
        compiler_params=_cparams("parallel"),
        name="peer_route",
    )(q, keys, keys)


PEER_TT = 512
PEER_EB = 512
GELU_C = math.sqrt(2.0 / math.pi)


def _gelu_tanh(x):
    return 0.5 * x * (1.0 + jnp.tanh(GELU_C * (x + 0.044715 * (x * x * x))))


PEER_NE = PEER_N // PEER_EB
PEER_MT = 256


def _peer_kernel(ht_ref, u_ref, vt_ref, rank2_ref, e2_ref, cnt1_ref, e1_ref, x_ref, mod_ref, o_ref,
                 ga_ref, gb_ref, acc_ref):
    e = pl.program_id(1)

    @pl.when(e == 0)
    def _():
        acc_ref[...] = jnp.zeros_like(acc_ref)
        gb_ref[...] = jnp.zeros_like(gb_ref)

    def step(g_prev_ref, g_next_ref):
        blk = jnp.maximum(e - 1, 0)
        n_i = PEER_EB // PEER_NKEYS
        per_tile = PEER_MT // PEER_NKEYS
        for tc in range(PEER_TT // PEER_MT):
            tok = slice(tc * PEER_MT, (tc + 1) * PEER_MT)
            g_next_ref[:, tok] = _gelu_tanh(_dot(u_ref[...], ht_ref[:, tok])).astype(BF16)
        for tc in range(PEER_TT // PEER_MT):
            tok = slice(tc * PEER_MT, (tc + 1) * PEER_MT)
            acc = None
            for kc in range(PEER_EB // PEER_MT):
                tiles = []
                for ii in range(kc * per_tile, (kc + 1) * per_tile):
                    key1 = blk * n_i + ii
                    w = jnp.zeros((PEER_NKEYS, PEER_MT), BF16)
                    for h in range(PEER_HEADS):
                        cnt = cnt1_ref[h, pl.ds(key1, 1), tok].astype(BF16)
                        g1 = e1_ref[h, pl.ds(key1, 1), tok].astype(BF16)
                        w = w + jnp.where(rank2_ref[h, :, tok] < cnt, e2_ref[h, :, tok] * g1, 0.0)
                    tiles.append(w * g_prev_ref[ii * PEER_NKEYS:(ii + 1) * PEER_NKEYS, tok])
                t = _dot(vt_ref[:, kc * PEER_MT:(kc + 1) * PEER_MT], jnp.concatenate(tiles, axis=0))
                acc = t if acc is None else acc + t
            acc_ref[:, tok] += acc

    @pl.when(e % 2 == 0)
    def _():
        step(gb_ref, ga_ref)

    @pl.when(e % 2 == 1)
    def _():
        step(ga_ref, gb_ref)

    @pl.when(e == PEER_NE)
    def _():
        o_ref[...] = x_ref[...] + mod_ref[5:6, :] * acc_ref[...].T


def _peer(ht, u_tab, vt_tab, routing, x1, mod4, l):
    rank2, e2, cnt1, e1 = routing
    rspec = pl.BlockSpec((PEER_HEADS, PEER_NKEYS, PEER_TT), lambda i, e: (0, 0, i))
    return pl.pallas_call(
        _peer_kernel,
        grid=(NT // PEER_TT, PEER_NE + 1),
        in_specs=[
            pl.BlockSpec((D_MODEL, PEER_TT), lambda i, e: (0, i)),
            pl.BlockSpec((None, PEER_EB, D_MODEL), lambda i, e: (l, jnp.minimum(e, PEER_NE - 1), 0)),
            pl.BlockSpec((None, D_MODEL, PEER_EB), lambda i, e: (l, 0, jnp.maximum(e - 1, 0))),
            rspec, rspec, rspec, rspec,
            pl.BlockSpec((PEER_TT, D_MODEL), lambda i, e: (i, 0)),
            pl.BlockSpec((None, None, 6, D_MODEL), lambda i, e: (l, _mod_set(i * PEER_TT), 0, 0)),
        ],
        out_specs=pl.BlockSpec((PEER_TT, D_MODEL), lambda i, e: (i, 0)),
        out_shape=jax.ShapeDtypeStruct((NT, D_MODEL), F32),
        scratch_shapes=[pltpu.VMEM((PEER_EB, PEER_TT), BF16), pltpu.VMEM((PEER_EB, PEER_TT), BF16),
                        pltpu.VMEM((D_MODEL, PEER_TT), F32)],
        compiler_params=_cparams("parallel", "arbitrary"),
        name="peer_experts",
    )(ht, u_tab, vt_tab, rank2, e2, cnt1, e1, x1, mod4)


FIN_TM = 512


def _final_kernel(x_ref, g_ref, o_ref):
    o_ref[...] = _rms(x_ref[...], g_ref[...])


def _final_norm(x, g):
    return pl.pallas_call(
        _final_kernel,
        grid=(NT // FIN_TM,),
        in_specs=[pl.BlockSpec((FIN_TM, D_MODEL), lambda i: (i, 0)),
                  pl.BlockSpec((1, D_MODEL), lambda i: (0, 0))],
        out_specs=pl.BlockSpec((FIN_TM, D_MODEL), lambda i: (i, 0)),
        out_shape=jax.ShapeDtypeStruct((NT, D_MODEL), F32),
        compiler_params=_cparams("parallel"),
        name="final_norm",
    )(x, g)


def _permute_w_in(w_in):
    sizes = (GROUP_W, GROUP_W, GROUP_W, GROUP_W, 4 * ML_HEADS, MLA_Q_RANK, MLA_KV_RANK, MLA_D_ROPE,
             SWA_HEADS * SWA_DH, SWA_KV_HEADS * SWA_DH, SWA_KV_HEADS * SWA_DH, GROUP_W, GROUP_W, GROUP_W)
    offs = [0]
    for s in sizes:
        offs.append(offs[-1] + s)
    part = lambda i: w_in[:, :, offs[i]:offs[i + 1]]
    order = [0, 1, 2, 3, 5, 6, 8, 9, 10, 11, 12, 13, 7, 4]
    cols = [part(i) for i in order]
    used = sum(sizes)
    cols.append(jnp.zeros(w_in.shape[:2] + (PROJ_W - used,), w_in.dtype))
    return jnp.concatenate(cols, axis=-1).astype(BF16)


def _permute_w_uq(w_uq):
    w = w_uq.reshape(DEPTH, MLA_Q_RANK, MLA_HEADS, MLA_D_NOPE + MLA_D_ROPE)
    nope = w[..., :MLA_D_NOPE].reshape(DEPTH, MLA_Q_RANK, MLA_HEADS * MLA_D_NOPE)
    rope = w[..., MLA_D_NOPE:].reshape(DEPTH, MLA_Q_RANK, MLA_HEADS * MLA_D_ROPE)
    return jnp.concatenate([nope, rope], axis=-1).astype(BF16)


def kernel(x_prompt, x_sample, c, cache_mla_ckv, cache_mla_krope, cache_swa_k, cache_swa_v, cache_diff_k,
           cache_diff_v, state_mlstm_C, state_mlstm_n, state_mlstm_m, c_ctx, w_mod, b_mod, norm1_g, w_in,
           mlstm_i_bias, mlstm_f_bias, mlstm_norm_g, mla_qnorm_g, mla_w_uq, mla_kvnorm_g, mla_w_ukv, swa_sink,
           diff_lq1, diff_lk1, diff_lq2, diff_lk2, diff_norm_g, w_out, norm2_g, peer_w_q, peer_sub_keys,
           peer_u, peer_v, final_norm_g):
    x = jnp.concatenate([x_prompt.reshape(N_CTX, D_MODEL), x_sample.reshape(N_LAT, D_MODEL)], axis=0)

    cvec = jnp.concatenate([c_ctx[None, :], c], axis=0)
    cvec_t = jnp.pad(cvec.T, ((0, 0), (0, 8 - N_SETS)))
    mod4 = _modulation(cvec_t, w_mod, b_mod).reshape(DEPTH, 8, 6, D_MODEL)

    w_in_p = _permute_w_in(w_in)
    w_uq_p = _permute_w_uq(mla_w_uq)
    w_ukv_b = mla_w_ukv.astype(BF16)
    w_out_b = w_out.astype(BF16)
    w_q_b = peer_w_q.astype(BF16)
    keys_b = peer_sub_keys.astype(BF16)
    u_b = peer_u.astype(BF16)
    vt_b = jnp.swapaxes(peer_v, 1, 2).astype(BF16)

    vec3 = lambda a: a.reshape(DEPTH, 1, a.shape[-1])
    norm1_3, norm2_3 = vec3(norm1_g), vec3(norm2_g)
    mlng_3, qg_3, kg_3, dng_3 = vec3(mlstm_norm_g), vec3(mla_qnorm_g), vec3(mla_kvnorm_g), vec3(diff_norm_g)
    lq1_3, lk1_3, lq2_3, lk2_3 = vec3(diff_lq1), vec3(diff_lk1), vec3(diff_lq2), vec3(diff_lk2)

    n_chain = 2 * ML_HEADS
    st_c = state_mlstm_C.reshape(DEC_BATCH, DEPTH, n_chain, ML_DH, ML_DH)
    st_n = state_mlstm_n.reshape(DEC_BATCH, DEPTH, n_chain, ML_DH)
    st_m = jnp.broadcast_to(state_mlstm_m.reshape(DEC_BATCH, DEPTH, n_chain, 1), (DEC_BATCH, DEPTH, n_chain, ML_DH))
    swk_c = cache_swa_k.reshape(DEC_BATCH, DEPTH, PAST_LEN, SWA_KV_HEADS * SWA_DH)
    swv_c = cache_swa_v.reshape(DEC_BATCH, DEPTH, PAST_LEN, SWA_KV_HEADS * SWA_DH)
    dfk_c = cache_diff_k.reshape(DEC_BATCH, DEPTH, PAST_LEN, GROUP_W)
    dfv_c = cache_diff_v.reshape(DEC_BATCH, DEPTH, PAST_LEN, GROUP_W)

    cos64, sin64 = _rope_tables(DEC_SEQ, 64)
    cos128, sin128 = _rope_tables(DEC_SEQ, 128)

    outs = [[] for _ in range(9)]
    for l in range(DEPTH):
        proj = _project(x, mod4, norm1_3, w_in_p, l)

        gates = proj[:, C_TAIL + TAIL_G:C_TAIL + TAIL_G + 16]
        gt = gates.reshape(NT // ML_CHUNK, ML_CHUNK, 16).transpose(0, 2, 1)
        bias = jnp.concatenate([mlstm_i_bias[l].reshape(-1), mlstm_f_bias[l].reshape(-1)])
        bcol = bias.reshape(16, 1)
        brow = jnp.zeros((1, 128), F32).at[0, TAIL_G:TAIL_G + 16].set(bias)

        ml_ctx, c_st, n_st, m_st = _mlstm(proj, gt, bcol, brow, mlng_3, l, latent=False)
        ml_lat = _mlstm(proj, gt, bcol, brow, mlng_3, l, latent=True, states=(st_c, st_n, st_m))
        mla_ctx, ckv_n = _mla_ctx(proj, qg_3, kg_3, w_uq_p, w_ukv_b, l)
        mla_lat = _mla_lat(proj, qg_3, kg_3, w_uq_p, w_ukv_b, cache_mla_ckv, cache_mla_krope, cos64, sin64, l)
        sink = swa_sink[l].reshape(1, SWA_HEADS)
        swa_ctx = _swa_ctx(proj, sink, l)
        swa_lat = _swa_lat(proj, sink, swk_c, swv_c, cos128, sin128, l)
        df_ctx = _diff_ctx(proj, lq1_3, lk1_3, lq2_3, lk2_3, dng_3, l)
        df_lat = _diff_lat(proj, lq1_3, lk1_3, lq2_3, lk2_3, dng_3, dfk_c, dfv_c, cos64, sin64, l)

        x1 = _out_proj((ml_ctx, mla_ctx, swa_ctx, df_ctx), (ml_lat, mla_lat, swa_lat, df_lat), x, mod4, w_out_b, l)
        h2t, q = _peer_query(x1, mod4, norm2_3, w_q_b, l)
        routing = _peer_route(q, keys_b, l)
        x = _peer(h2t, u_b, vt_b, routing, x1, mod4, l)

        pc = proj[:N_CTX]
        outs[0].append(ckv_n.reshape(BATCH, SEQ, MLA_KV_RANK))
        outs[1].append(pc[:, C_TAIL:C_TAIL + MLA_D_ROPE].reshape(BATCH, SEQ, MLA_D_ROPE))
        outs[2].append(pc[:, C_SK:C_SK + 256].reshape(BATCH, SEQ, SWA_KV_HEADS, SWA_DH))
        outs[3].append(pc[:, C_SV:C_SV + 256].reshape(BATCH, SEQ, SWA_KV_HEADS, SWA_DH))
        outs[4].append(pc[:, C_DK:C_DK + 512].reshape(BATCH, SEQ, DIFF_HEADS, 2 * DIFF_DH))
        outs[5].append(pc[:, C_DV:C_DV + 512].reshape(BATCH, SEQ, DIFF_HEADS, 2 * DIFF_DH))
        outs[6].append(c_st.reshape(BATCH, 2, ML_HEADS, ML_DH, ML_DH))
        outs[7].append(n_st.reshape(BATCH, 2, ML_HEADS, ML_DH))
        outs[8].append(m_st[:, :, 0].reshape(BATCH, 2, ML_HEADS))

    y = _final_norm(x, final_norm_g.reshape(1, D_MODEL))
    y_prompt = y[:N_CTX].reshape(BATCH, SEQ, D_MODEL)
    y_sample = y[N_CTX:].reshape(DEC_BATCH, DEC_SEQ, D_MODEL)
    return (y_prompt, y_sample) + tuple(jnp.stack(o, axis=1) for o in outs)
```

```python
import functools
import math

import jax
import jax.numpy as jnp
from jax import lax
from jax.experimental import pallas as pl
from jax.experimental.pallas import tpu as pltpu

F32 = jnp.float32
BF16 = jnp.bfloat16

D_MODEL = 2048
BATCH = 32
SEQ = 256
DEPTH = 4
DEC_BATCH = 2
DEC_SEQ = 1024
PAST_LEN = 256
GRID_W = 64
GROUP_W = D_MODEL // 4
ML_HEADS = 4
ML_DH = GROUP_W // ML_HEADS
ML_CHUNK = 64
MLA_HEADS = 4
MLA_D_NOPE = GROUP_W // MLA_HEADS
MLA_D_ROPE = 64
MLA_Q_RANK = D_MODEL // 8
MLA_KV_RANK = D_MODEL // 8
MLA_SCALE = (MLA_D_NOPE + MLA_D_ROPE) ** -0.5
SWA_HEADS = 4
SWA_KV_HEADS = 2
SWA_DH = GROUP_W // SWA_HEADS
WINDOW = 128
DIFF_HEADS = 4
DIFF_DH = GROUP_W // (2 * DIFF_HEADS)
PEER_HEADS = 8
PEER_QDIM = 256
PEER_NKEYS = 128
PEER_N = PEER_NKEYS * PEER_NKEYS
PEER_TOPK = 16
ROPE_BASE = 10000.0
EPS = 1e-6
NEG = -1e30

N_CTX = BATCH * SEQ
N_LAT = DEC_BATCH * DEC_SEQ
NT = N_CTX + N_LAT
N_SETS = 1 + DEC_BATCH

C_MLQ, C_MLK, C_MLV, C_MLO = 0, 512, 1024, 1536
C_CQ, C_CKV = 2048, 2304
C_SQ, C_SK, C_SV = 2560, 3072, 3328
C_DQ, C_DK, C_DV = 3584, 4096, 4608
C_TAIL = 5120
TAIL_G = 64
PROJ_W = 5376
PROJ_TN = 896

VMEM_LIMIT = 56 * 1024 * 1024


def _cparams(*sem):
    return pltpu.CompilerParams(dimension_semantics=sem, vmem_limit_bytes=VMEM_LIMIT)


def _mod_set(row_start):
    return jnp.where(row_start >= N_CTX, (row_start - N_CTX) // DEC_SEQ + 1, 0)


def _rms(x, g):
    return x * lax.rsqrt(jnp.mean(x * x, axis=-1, keepdims=True) + EPS) * g


def _dot(a, b):
    return jnp.dot(a, b, preferred_element_type=F32)


def _dot_nt(a, b):
    return lax.dot_general(a, b, (((1,), (1,)), ((), ())), preferred_element_type=F32)


def _dot_tn(a, b):
    return lax.dot_general(a, b, (((0,), (0,)), ((), ())), preferred_element_type=F32)


def _dot_hi(a, b):
    return jnp.dot(a, b, preferred_element_type=F32, precision=lax.Precision.HIGHEST)


MOD_TK = 512
MOD_TN = 2048


def _mod_kernel(c_ref, w_ref, b_ref, o_ref, acc_ref):
    k = pl.program_id(2)

    @pl.when(k == 0)
    def _():
        acc_ref[...] = jnp.zeros_like(acc_ref)

    w = w_ref[...]
    for r in range(N_SETS):
        cv = c_ref[:, r:r + 1]
        sv = cv * jax.nn.sigmoid(cv)
        acc_ref[r] += (sv * w).reshape(MOD_TK // 8, 8, MOD_TN).sum(axis=0)

    @pl.when(k == pl.num_programs(2) - 1)
    def _():
        o_ref[...] = jnp.zeros_like(o_ref)
        for r in range(N_SETS):
            o_ref[r:r + 1, :] = acc_ref[r].sum(axis=0, keepdims=True) + b_ref[...]


def _modulation(cvec_t, w_mod, b_mod):
    return pl.pallas_call(
        _mod_kernel,
        grid=(DEPTH, 6 * D_MODEL // MOD_TN, D_MODEL // MOD_TK),
        in_specs=[
            pl.BlockSpec((MOD_TK, 8), lambda l, n, k: (k, 0)),
            pl.BlockSpec((None, MOD_TK, MOD_TN), lambda l, n, k: (l, k, n)),
            pl.BlockSpec((None, 1, MOD_TN), lambda l, n, k: (l, 0, n)),
        ],
        out_specs=pl.BlockSpec((None, 8, MOD_TN), lambda l, n, k: (l, 0, n)),
        out_shape=jax.ShapeDtypeStruct((DEPTH, 8, 6 * D_MODEL), F32),
        scratch_shapes=[pltpu.VMEM((N_SETS, 8, MOD_TN), F32)],
        compiler_params=_cparams("parallel", "parallel", "arbitrary"),
        name="modulation",
    )(cvec_t, w_mod, b_mod.reshape(DEPTH, 1, 6 * D_MODEL))


PROJ_TM = 1024


def _proj_kernel(x_ref, m_ref, g_ref, w_ref, o_ref, h_ref):
    @pl.when(pl.program_id(1) == 0)
    def _():
        h = _rms(x_ref[...], g_ref[...]) * (1.0 + m_ref[1:2, :]) + m_ref[0:1, :]
        h_ref[...] = h.astype(BF16)

    o_ref[...] = _dot(h_ref[...], w_ref[...])


def _project(x, mod4, norm_g, w_in_p, l):
    return pl.pallas_call(
        _proj_kernel,
        grid=(NT // PROJ_TM, PROJ_W // PROJ_TN),
        in_specs=[
            pl.BlockSpec((PROJ_TM, D_MODEL), lambda i, j: (i, 0)),
            pl.BlockSpec((None, None, 6, D_MODEL), lambda i, j: (l, _mod_set(i * PROJ_TM), 0, 0)),
            pl.BlockSpec((None, 1, D_MODEL), lambda i, j: (l, 0, 0)),
            pl.BlockSpec((None, D_MODEL, PROJ_TN), lambda i, j: (l, 0, j)),
        ],
        out_specs=pl.BlockSpec((PROJ_TM, PROJ_TN), lambda i, j: (i, j)),
        out_shape=jax.ShapeDtypeStruct((NT, PROJ_W), F32),
        scratch_shapes=[pltpu.VMEM((PROJ_TM, D_MODEL), BF16)],
        compiler_params=_cparams("parallel", "arbitrary"),
        name="adaln_in_proj",
    )(x, mod4, norm_g, w_in_p)


def _rope(x, cos, sin, quarter):
    width = x.shape[-1]
    lane = lax.broadcasted_iota(jnp.int32, x.shape, 1)
    first = (lane % (2 * quarter)) < quarter
    partner = jnp.where(first, pltpu.roll(x, width - quarter, 1), pltpu.roll(x, quarter, 1))
    return x * cos + partner * sin


def _rope_tables(n_tok, rot_dim):
    half = rot_dim // 2
    pos = jnp.arange(n_tok)
    row = (pos // GRID_W).astype(F32)
    col = (pos % GRID_W).astype(F32)
    inv = ROPE_BASE ** (-jnp.arange(0, half, 2, dtype=F32) / half)
    a_row = row[:, None] * inv[None, :]
    a_col = col[:, None] * inv[None, :]
    ang = jnp.concatenate([a_row, a_row, a_col, a_col], axis=-1)
    sign = jnp.tile(jnp.concatenate([-jnp.ones(half // 2, F32), jnp.ones(half // 2, F32)]), 2)
    return jnp.cos(ang), jnp.sin(ang) * sign[None, :]


def _log_sigmoid(x):
    return jnp.minimum(x, 0.0) - jnp.log(1.0 + jnp.exp(-jnp.abs(x)))


def _mlstm_kernel(*refs, n_tok, n_seq, has_state):
    if has_state:
        (q_ref, k_ref, v_ref, o_ref, tail_ref, gt_ref, bcol_ref, brow_ref, ng_ref, c0_ref, n0_ref, m0_ref,
         out_ref, hf_ref, hb_ref, cst_ref, nst_ref, mst_ref) = refs
    else:
        (q_ref, k_ref, v_ref, o_ref, tail_ref, gt_ref, bcol_ref, brow_ref, ng_ref,
         out_ref, cs_ref, ns_ref, ms_ref, hf_ref, hb_ref, cst_ref, nst_ref, mst_ref) = refs
    n_chunks = n_tok // ML_CHUNK
    scale = ML_DH ** -0.5

    if has_state:
        cst_ref[...] = c0_ref[...]
        nst_ref[...] = n0_ref[...]
        mst_ref[...] = m0_ref[...]
    else:
        cst_ref[...] = jnp.zeros_like(cst_ref)
        nst_ref[...] = jnp.zeros_like(nst_ref)
        mst_ref[...] = jnp.zeros_like(mst_ref)

    row = lax.broadcasted_iota(jnp.int32, (ML_CHUNK, ML_CHUNK), 0)
    col = lax.broadcasted_iota(jnp.int32, (ML_CHUNK, ML_CHUNK), 1)
    lower = (col <= row)
    upper = (col >= row)
    lower_f = lower.astype(F32)
    upper_f = upper.astype(F32)

    def scan_seq(c, b):
        for d in range(2):
            cc = c if d == 0 else n_chunks - 1 - c
            t0 = pl.multiple_of(b * n_tok + cc * ML_CHUNK, ML_CHUNK)
            rows = pl.ds(t0, ML_CHUNK)
            g_col = tail_ref[rows, :] + brow_ref[...]
            g_row = gt_ref[b * n_chunks + cc] + bcol_ref[...]
            lf_col = _log_sigmoid(g_col)
            lf_row = _log_sigmoid(g_row)
            if d == 0:
                bcum_col = _dot_hi(lower_f, lf_col)
                bcum_row = _dot_hi(lf_row, upper_f)
                mask = lower
            else:
                bcum_col = _dot_hi(upper_f, lf_col)
                bcum_row = _dot_hi(lf_row, lower_f)
                mask = upper
            for h in range(ML_HEADS):
                r = d * ML_HEADS + h
                lanes = slice(h * ML_DH, (h + 1) * ML_DH)
                ci = TAIL_G + r
                cf = TAIL_G + 2 * ML_HEADS + r
                ig_c = g_col[:, ci:ci + 1]
                b_c = bcum_col[:, cf:cf + 1]
                ig_r = g_row[r:r + 1, :]
                b_r = bcum_row[2 * ML_HEADS + r:2 * ML_HEADS + r + 1, :]
                tot = jnp.sum(lf_row[2 * ML_HEADS + r:2 * ML_HEADS + r + 1, :], axis=1, keepdims=True)
                m_prev = mst_ref[b, r:r + 1, 0:1]
                n_prev = nst_ref[b, r:r + 1, :]
                c_prev = cst_ref[b, r]
                q = q_ref[rows, lanes]
                k = k_ref[rows, lanes]
                v = v_ref[rows, lanes].astype(BF16)
                qb = q.astype(BF16)

                dmat = jnp.where(mask, b_c - b_r + ig_r, NEG)
                inter = b_c + m_prev
                mt = jnp.maximum(inter, jnp.max(dmat, axis=1, keepdims=True))
                w = jnp.exp(dmat - mt)
                s = _dot_nt(qb, k.astype(BF16)) * scale * w
                a = jnp.exp(inter - mt)
                num = _dot(s.astype(BF16), v) + a * _dot(qb, c_prev.astype(BF16))
                den = jnp.sum(s, axis=1, keepdims=True) + a * jnp.sum(q * n_prev, axis=1, keepdims=True)
                hc = num / jnp.maximum(jnp.abs(den), jnp.exp(-mt))
                if d == 0:
                    hf_ref[rows, lanes] = hc
                else:
                    hb_ref[rows, lanes] = hc

                wlog_c = tot - b_c + ig_c
                wlog_r = tot - b_r + ig_r
                m_new = jnp.maximum(tot + m_prev, jnp.max(wlog_r, axis=1, keepdims=True))
                decay = jnp.exp(tot + m_prev - m_new)
                kw = k * (scale * jnp.exp(wlog_c - m_new))
                cst_ref[b, r] = decay * c_prev + _dot_tn(kw.astype(BF16), v)
                nst_ref[b, r:r + 1, :] = decay * n_prev + jnp.sum(kw, axis=0, keepdims=True)
                mst_ref[b, r:r + 1, :] = jnp.broadcast_to(m_new, (1, ML_DH))

    def chunk_step(c, carry):
        for b in range(n_seq):
            scan_seq(c, b)
        return carry

    lax.fori_loop(0, n_chunks, chunk_step, 0)

    for h in range(ML_HEADS):
        lanes = slice(h * ML_DH, (h + 1) * ML_DH)
        hs = hf_ref[:, lanes] + hb_ref[:, lanes]
        out_ref[:, lanes] = _rms(hs, ng_ref[:, lanes]) * jax.nn.sigmoid(o_ref[:, lanes])

    if not has_state:
        cs_ref[...] = cst_ref[...]
        ns_ref[...] = nst_ref[...]
        ms_ref[...] = mst_ref[...]


ML_SEQ_PER_STEP = 4


def _mlstm(proj, gt, bcol, brow, norm_g, l, *, latent, states=None):
    n_tok = DEC_SEQ if latent else SEQ
    n_seq = 1 if latent else ML_SEQ_PER_STEP
    n_b = DEC_BATCH if latent else BATCH
    rows = n_seq * n_tok
    blk0 = N_CTX // rows if latent else 0
    n_chunks = n_tok // ML_CHUNK

    def col_spec(c0):
        return pl.BlockSpec((rows, 512), lambda b: (blk0 + b, c0 // 512))

    in_specs = [
        col_spec(C_MLQ), col_spec(C_MLK), col_spec(C_MLV), col_spec(C_MLO),
        pl.BlockSpec((rows, 128), lambda b: (blk0 + b, C_TAIL // 128)),
        pl.BlockSpec((n_seq * n_chunks, 16, ML_CHUNK), lambda b: (blk0 + b, 0, 0)),
        pl.BlockSpec((16, 1), lambda b: (0, 0)),
        pl.BlockSpec((1, 128), lambda b: (0, 0)),
        pl.BlockSpec((None, 1, GROUP_W), lambda b: (l, 0, 0)),
    ]
    args = [proj, proj, proj, proj, proj, gt, bcol, brow, norm_g]
    n_chain = 2 * ML_HEADS
    scratch = [pltpu.VMEM((rows, GROUP_W), F32), pltpu.VMEM((rows, GROUP_W), F32),
               pltpu.VMEM((n_seq, n_chain, ML_DH, ML_DH), F32), pltpu.VMEM((n_seq, n_chain, ML_DH), F32),
               pltpu.VMEM((n_seq, n_chain, ML_DH), F32)]
    if latent:
        c0, n0, m0 = states
        in_specs += [
            pl.BlockSpec((n_seq, None, n_chain, ML_DH, ML_DH), lambda b: (b, l, 0, 0, 0)),
            pl.BlockSpec((n_seq, None, n_chain, ML_DH), lambda b: (b, l, 0, 0)),
            pl.BlockSpec((n_seq, None, n_chain, ML_DH), lambda b: (b, l, 0, 0)),
        ]
        args += [c0, n0, m0]
        out_shape = jax.ShapeDtypeStruct((N_LAT, GROUP_W), F32)
        out_specs = pl.BlockSpec((rows, GROUP_W), lambda b: (b, 0))
    else:
        out_shape = (jax.ShapeDtypeStruct((N_CTX, GROUP_W), F32),
                     jax.ShapeDtypeStruct((BATCH, n_chain, ML_DH, ML_DH), F32),
                     jax.ShapeDtypeStruct((BATCH, n_chain, ML_DH), F32),
                     jax.ShapeDtypeStruct((BATCH, n_chain, ML_DH), F32))
        out_specs = (pl.BlockSpec((rows, GROUP_W), lambda b: (b, 0)),
                     pl.BlockSpec((n_seq, n_chain, ML_DH, ML_DH), lambda b: (b, 0, 0, 0)),
                     pl.BlockSpec((n_seq, n_chain, ML_DH), lambda b: (b, 0, 0)),
                     pl.BlockSpec((n_seq, n_chain, ML_DH), lambda b: (b, 0, 0)))
    return pl.pallas_call(
        functools.partial(_mlstm_kernel, n_tok=n_tok, n_seq=n_seq, has_state=latent),
        grid=(n_b // n_seq,),
        in_specs=in_specs,
        out_specs=out_specs,
        out_shape=out_shape,
        scratch_shapes=scratch,
        compiler_params=_cparams("parallel"),
        name="mlstm_latent" if latent else "mlstm_context",
    )(*args)


def _softmax_parts(scores, sink=None):
    m = jnp.max(scores[0], axis=1, keepdims=True)
    for s in scores[1:]:
        m = jnp.maximum(m, jnp.max(s, axis=1, keepdims=True))
    if sink is not None:
        m = jnp.maximum(m, sink)
    es = [jnp.exp(s - m) for s in scores]
    den = jnp.sum(es[0], axis=1, keepdims=True)
    for e in es[1:]:
        den = den + jnp.sum(e, axis=1, keepdims=True)
    if sink is not None:
        den = den + jnp.exp(sink - m)
    return es, den


ATT_TQ = 256
LAT_QB = DEC_SEQ // ATT_TQ
N_KEYS_LAT = DEC_SEQ + PAST_LEN


def _mla_q(cq_ref, qg_ref, wuq_ref):
    return _dot(_rms(cq_ref[...], qg_ref[...]).astype(BF16), wuq_ref[...])


def _mla_heads(q, kv, kr, n_keys):
    outs = []
    for h in range(MLA_HEADS):
        qn = q[:, h * MLA_D_NOPE:(h + 1) * MLA_D_NOPE].astype(BF16)
        r0 = MLA_HEADS * MLA_D_NOPE + h * MLA_D_ROPE
        qr = q[:, r0:r0 + MLA_D_ROPE].astype(BF16)
        kn = kv[:, h * 256:h * 256 + MLA_D_NOPE]
        v = kv[:, h * 256 + MLA_D_NOPE:(h + 1) * 256]
        s = (_dot_nt(qn, kn) + _dot_nt(qr, kr)) * MLA_SCALE
        (e,), den = _softmax_parts([s])
        outs.append(_dot(e.astype(BF16), v) / den)
    return outs


def _mla_ctx_kernel(cq_ref, ckv_ref, tail_ref, qg_ref, kg_ref, wuq_ref, wukv_ref, out_ref, ckvn_ref):
    q = _mla_q(cq_ref, qg_ref, wuq_ref)
    ckvn = _rms(ckv_ref[...], kg_ref[...])
    ckvn_ref[...] = ckvn
    kv = _dot(ckvn.astype(BF16), wukv_ref[...]).astype(BF16)
    kr = tail_ref[:, 0:MLA_D_ROPE].astype(BF16)
    outs = _mla_heads(q, kv, kr, SEQ)
    for h in range(MLA_HEADS):
        out_ref[:, h * MLA_D_V:(h + 1) * MLA_D_V] = outs[h]


MLA_D_V = GROUP_W // MLA_HEADS


def _mla_lat_kernel(cq_ref, ckv_ref, tail_ref, qg_ref, kg_ref, wuq_ref, wukv_ref, ckvc_ref, krc_ref,
                    cosq_ref, sinq_ref, cosk_ref, sink_ref, out_ref, kv_ref, kr_ref):
    @pl.when(pl.program_id(1) == 0)
    def _():
        ckvn = _rms(ckv_ref[...], kg_ref[...])
        kv_ref[0:DEC_SEQ, :] = _dot(ckvn.astype(BF16), wukv_ref[...]).astype(BF16)
        kv_ref[DEC_SEQ:N_KEYS_LAT, :] = _dot(ckvc_ref[...].astype(BF16), wukv_ref[...]).astype(BF16)
        kr = _rope(tail_ref[...], cosk_ref[...], sink_ref[...], MLA_D_ROPE // 4)
        kr_ref[0:DEC_SEQ, :] = kr[:, 0:MLA_D_ROPE].astype(BF16)
        kr_ref[DEC_SEQ:N_KEYS_LAT, :] = krc_ref[...].astype(BF16)

    q = _mla_q(cq_ref, qg_ref, wuq_ref)
    n0 = MLA_HEADS * MLA_D_NOPE
    q_rope = _rope(q[:, n0:], cosq_ref[...], sinq_ref[...], MLA_D_ROPE // 4)
    q = jnp.concatenate([q[:, :n0], q_rope], axis=1)
    outs = _mla_heads(q, kv_ref[...], kr_ref[...], N_KEYS_LAT)
    for h in range(MLA_HEADS):
        out_ref[:, h * MLA_D_V:(h + 1) * MLA_D_V] = outs[h]


def _w_specs2(shape_a, shape_b, l):
    return [pl.BlockSpec((None,) + shape_a, lambda *_: (l,) + (0,) * len(shape_a)),
            pl.BlockSpec((None,) + shape_b, lambda *_: (l,) + (0,) * len(shape_b))]


def _mla_ctx(proj, qg, kg, wuq, wukv, l):
    return pl.pallas_call(
        _mla_ctx_kernel,
        grid=(BATCH,),
        in_specs=[
            pl.BlockSpec((SEQ, 256), lambda b: (b, C_CQ // 256)),
            pl.BlockSpec((SEQ, 256), lambda b: (b, C_CKV // 256)),
            pl.BlockSpec((SEQ, 128), lambda b: (b, C_TAIL // 128)),
            pl.BlockSpec((None, 1, MLA_Q_RANK), lambda b: (l, 0, 0)),
            pl.BlockSpec((None, 1, MLA_KV_RANK), lambda b: (l, 0, 0)),
            pl.BlockSpec((None, MLA_Q_RANK, 768), lambda b: (l, 0, 0)),
            pl.BlockSpec((None, MLA_KV_RANK, 1024), lambda b: (l, 0, 0)),
        ],
        out_specs=(pl.BlockSpec((SEQ, GROUP_W), lambda b: (b, 0)),
                   pl.BlockSpec((SEQ, MLA_KV_RANK), lambda b: (b, 0))),
        out_shape=(jax.ShapeDtypeStruct((N_CTX, GROUP_W), F32),
                   jax.ShapeDtypeStruct((N_CTX, MLA_KV_RANK), F32)),
        compiler_params=_cparams("parallel"),
        name="mla_context",
    )(proj, proj, proj, qg, kg, wuq, wukv)


def _mla_lat(proj, qg, kg, wuq, wukv, cache_ckv, cache_kr, cos64, sin64, l):
    qb0 = N_CTX // ATT_TQ
    bb0 = N_CTX // DEC_SEQ
    cosq = jnp.tile(cos64, (1, MLA_HEADS))
    sinq = jnp.tile(sin64, (1, MLA_HEADS))
    cosk = jnp.tile(cos64, (1, 2))
    sink = jnp.tile(sin64, (1, 2))
    return pl.pallas_call(
        _mla_lat_kernel,
        grid=(DEC_BATCH, LAT_QB),
        in_specs=[
            pl.BlockSpec((ATT_TQ, 256), lambda b, i: (qb0 + b * LAT_QB + i, C_CQ // 256)),
            pl.BlockSpec((DEC_SEQ, 256), lambda b, i: (bb0 + b, C_CKV // 256)),
            pl.BlockSpec((DEC_SEQ, 128), lambda b, i: (bb0 + b, C_TAIL // 128)),
            pl.BlockSpec((None, 1, MLA_Q_RANK), lambda b, i: (l, 0, 0)),
            pl.BlockSpec((None, 1, MLA_KV_RANK), lambda b, i: (l, 0, 0)),
            pl.BlockSpec((None, MLA_Q_RANK, 768), lambda b, i: (l, 0, 0)),
            pl.BlockSpec((None, MLA_KV_RANK, 1024), lambda b, i: (l, 0, 0)),
            pl.BlockSpec((None, None, PAST_LEN, MLA_KV_RANK), lambda b, i: (b, l, 0, 0)),
            pl.BlockSpec((None, None, PAST_LEN, MLA_D_ROPE), lambda b, i: (b, l, 0, 0)),
            pl.BlockSpec((ATT_TQ, 256), lambda b, i: (i, 0)),
            pl.BlockSpec((ATT_TQ, 256), lambda b, i: (i, 0)),
            pl.BlockSpec((DEC_SEQ, 128), lambda b, i: (0, 0)),
            pl.BlockSpec((DEC_SEQ, 128), lambda b, i: (0, 0)),
        ],
        out_specs=pl.BlockSpec((ATT_TQ, GROUP_W), lambda b, i: (b * LAT_QB + i, 0)),
        out_shape=jax.ShapeDtypeStruct((N_LAT, GROUP_W), F32),
        scratch_shapes=[pltpu.VMEM((N_KEYS_LAT, 1024), BF16), pltpu.VMEM((N_KEYS_LAT, MLA_D_ROPE), BF16)],
        compiler_params=_cparams("parallel", "arbitrary"),
        name="mla_latent",
    )(proj, proj, proj, qg, kg, wuq, wukv, cache_ckv, cache_kr, cosq, sinq, cosk, sink)


SWA_SCALE = SWA_DH ** -0.5
SWA_REP = SWA_HEADS // SWA_KV_HEADS
SWA_KWIN = ATT_TQ + 2 * WINDOW


def _swa_ctx_kernel(sink_ref, q_ref, k_ref, v_ref, out_ref):
    kb = k_ref[...].astype(BF16)
    vb = v_ref[...].astype(BF16)
    for h in range(SWA_HEADS):
        g = h // SWA_REP
        q = q_ref[:, h * SWA_DH:(h + 1) * SWA_DH].astype(BF16)
        s = _dot_nt(q, kb[:, g * SWA_DH:(g + 1) * SWA_DH]) * SWA_SCALE
        (e,), den = _softmax_parts([s], sink=sink_ref[0, h])
        out_ref[:, h * SWA_DH:(h + 1) * SWA_DH] = _dot(e.astype(BF16), vb[:, g * SWA_DH:(g + 1) * SWA_DH]) / den


def _swa_lat_kernel(sink_ref, q_ref, k_ref, v_ref, kc_ref, vc_ref, cosq_ref, sinq_ref, cosk_ref, sink_t_ref,
                    out_ref, kr_ref):
    i = pl.program_id(1)

    @pl.when(i == 0)
    def _():
        kr_ref[...] = _rope(k_ref[...], cosk_ref[...], sink_t_ref[...], SWA_DH // 4).astype(BF16)

    q_all = _rope(q_ref[...], cosq_ref[...], sinq_ref[...], SWA_DH // 4)
    k0 = pl.multiple_of(jnp.clip(i * ATT_TQ - WINDOW, 0, DEC_SEQ - SWA_KWIN), WINDOW)
    kwin = kr_ref[pl.ds(k0, SWA_KWIN), :]
    vwin = v_ref[pl.ds(k0, SWA_KWIN), :].astype(BF16)
    kc = kc_ref[...].astype(BF16)
    vc = vc_ref[...].astype(BF16)
    qpos = i * ATT_TQ + lax.broadcasted_iota(jnp.int32, (ATT_TQ, SWA_KWIN), 0)
    kpos = k0 + lax.broadcasted_iota(jnp.int32, (ATT_TQ, SWA_KWIN), 1)
    band = jnp.abs(qpos - kpos) <= WINDOW
    for h in range(SWA_HEADS):
        g = h // SWA_REP
        gl = slice(g * SWA_DH, (g + 1) * SWA_DH)
        q = q_all[:, h * SWA_DH:(h + 1) * SWA_DH].astype(BF16)
        s_loc = jnp.where(band, _dot_nt(q, kwin[:, gl]) * SWA_SCALE, NEG)
        s_ctx = _dot_nt(q, kc[:, gl]) * SWA_SCALE
        (e_loc, e_ctx), den = _softmax_parts([s_loc, s_ctx], sink=sink_ref[0, h])
        o = _dot(e_loc.astype(BF16), vwin[:, gl]) + _dot(e_ctx.astype(BF16), vc[:, gl])
        out_ref[:, h * SWA_DH:(h + 1) * SWA_DH] = o / den


def _smem_spec():
    return pl.BlockSpec(memory_space=pltpu.SMEM)


def _swa_ctx(proj, sink, l):
    return pl.pallas_call(
        _swa_ctx_kernel,
        grid=(BATCH,),
        in_specs=[
            _smem_spec(),
            pl.BlockSpec((SEQ, 512), lambda b: (b, C_SQ // 512)),
            pl.BlockSpec((SEQ, 256), lambda b: (b, C_SK // 256)),
            pl.BlockSpec((SEQ, 256), lambda b: (b, C_SV // 256)),
        ],
        out_specs=pl.BlockSpec((SEQ, GROUP_W), lambda b: (b, 0)),
        out_shape=jax.ShapeDtypeStruct((N_CTX, GROUP_W), F32),
        compiler_params=_cparams("parallel"),
        name="swa_context",
    )(sink, proj, proj, proj)


def _swa_lat(proj, sink, cache_k, cache_v, cos128, sin128, l):
    qb0 = N_CTX // ATT_TQ
    bb0 = N_CTX // DEC_SEQ
    kvw = SWA_KV_HEADS * SWA_DH
    return pl.pallas_call(
        _swa_lat_kernel,
        grid=(DEC_BATCH, LAT_QB),
        in_specs=[
            _smem_spec(),
            pl.BlockSpec((ATT_TQ, 512), lambda b, i: (qb0 + b * LAT_QB + i, C_SQ // 512)),
            pl.BlockSpec((DEC_SEQ, 256), lambda b, i: (bb0 + b, C_SK // 256)),
            pl.BlockSpec((DEC_SEQ, 256), lambda b, i: (bb0 + b, C_SV // 256)),
            pl.BlockSpec((None, None, PAST_LEN, kvw), lambda b, i: (b, l, 0, 0)),
            pl.BlockSpec((None, None, PAST_LEN, kvw), lambda b, i: (b, l, 0, 0)),
            pl.BlockSpec((ATT_TQ, 512), lambda b, i: (i, 0)),
            pl.BlockSpec((ATT_TQ, 512), lambda b, i: (i, 0)),
            pl.BlockSpec((DEC_SEQ, 256), lambda b, i: (0, 0)),
            pl.BlockSpec((DEC_SEQ, 256), lambda b, i: (0, 0)),
        ],
        out_specs=pl.BlockSpec((ATT_TQ, GROUP_W), lambda b, i: (b * LAT_QB + i, 0)),
        out_shape=jax.ShapeDtypeStruct((N_LAT, GROUP_W), F32),
        scratch_shapes=[pltpu.VMEM((DEC_SEQ, kvw), BF16)],
        compiler_params=_cparams("parallel", "arbitrary"),
        name="swa_latent",
    )(sink, proj, proj, proj, cache_k, cache_v,
      jnp.tile(cos128, (1, SWA_HEADS)), jnp.tile(sin128, (1, SWA_HEADS)),
      jnp.tile(cos128, (1, SWA_KV_HEADS)), jnp.tile(sin128, (1, SWA_KV_HEADS)))


DIFF_SCALE = DIFF_DH ** -0.5


def _diff_lambda(lq1_ref, lk1_ref, lq2_ref, lk2_ref):
    a = jnp.sum(lq1_ref[...] * lk1_ref[...], axis=1, keepdims=True)
    b = jnp.sum(lq2_ref[...] * lk2_ref[...], axis=1, keepdims=True)
    return jnp.exp(a) - jnp.exp(b)


def _diff_heads(q, k_parts, v_parts, lam, lam_init, ng_ref, out_ref):
    for h in range(DIFF_HEADS):
        ps = []
        for c in range(2):
            sl = slice(h * 2 * DIFF_DH + c * DIFF_DH, h * 2 * DIFF_DH + (c + 1) * DIFF_DH)
            qc = q[:, sl].astype(BF16)
            es, den = _softmax_parts([_dot_nt(qc, kp[:, sl]) * DIFF_SCALE for kp in k_parts])
            ps.append([e / den for e in es])
        vl = slice(h * 2 * DIFF_DH, (h + 1) * 2 * DIFF_DH)
        o = None
        for p1, p2, vp in zip(ps[0], ps[1], v_parts):
            t = _dot((p1 - lam * p2).astype(BF16), vp[:, vl])
            o = t if o is None else o + t
        out_ref[:, vl] = _rms(o, ng_ref[...]) * (1.0 - lam_init)


def _diff_ctx_kernel(q_ref, k_ref, v_ref, lq1_ref, lk1_ref, lq2_ref, lk2_ref, ng_ref, out_ref, *, lam_init):
    lam = _diff_lambda(lq1_ref, lk1_ref, lq2_ref, lk2_ref) + lam_init
    _diff_heads(q_ref[...], [k_ref[...].astype(BF16)], [v_ref[...].astype(BF16)], lam, lam_init, ng_ref, out_ref)


def _diff_lat_kernel(q_ref, k_ref, v_ref, kc_ref, vc_ref, lq1_ref, lk1_ref, lq2_ref, lk2_ref, ng_ref,
                     cosq_ref, sinq_ref, cosk_ref, sink_ref, out_ref, kr_ref, *, lam_init):
    @pl.when(pl.program_id(1) == 0)
    def _():
        kr_ref[...] = _rope(k_ref[...], cosk_ref[...], sink_ref[...], DIFF_DH // 4).astype(BF16)

    lam = _diff_lambda(lq1_ref, lk1_ref, lq2_ref, lk2_ref) + lam_init
    q = _rope(q_ref[...], cosq_ref[...], sinq_ref[...], DIFF_DH // 4)
    _diff_heads(q, [kr_ref[...], kc_ref[...].astype(BF16)], [v_ref[...].astype(BF16), vc_ref[...].astype(BF16)],
                lam, lam_init, ng_ref, out_ref)


def _vec_specs(n, width, l, nargs):
    return [pl.BlockSpec((None, 1, width), lambda *_: (l, 0, 0)) for _ in range(n)]


def _diff_ctx(proj, lq1, lk1, lq2, lk2, ng, l):
    lam_init = 0.8 - 0.6 * math.exp(-0.3 * l)
    return pl.pallas_call(
        functools.partial(_diff_ctx_kernel, lam_init=lam_init),
        grid=(BATCH,),
        in_specs=[
            pl.BlockSpec((SEQ, 512), lambda b: (b, C_DQ // 512)),
            pl.BlockSpec((SEQ, 512), lambda b: (b, C_DK // 512)),
            pl.BlockSpec((SEQ, 512), lambda b: (b, C_DV // 512)),
        ] + _vec_specs(4, DIFF_DH, l, 1) + _vec_specs(1, 2 * DIFF_DH, l, 1),
        out_specs=pl.BlockSpec((SEQ, GROUP_W), lambda b: (b, 0)),
        out_shape=jax.ShapeDtypeStruct((N_CTX, GROUP_W), F32),
        compiler_params=_cparams("parallel"),
        name="diff_context",
    )(proj, proj, proj, lq1, lk1, lq2, lk2, ng)


def _diff_lat(proj, lq1, lk1, lq2, lk2, ng, cache_k, cache_v, cos64, sin64, l):
    lam_init = 0.8 - 0.6 * math.exp(-0.3 * l)
    qb0 = N_CTX // ATT_TQ
    bb0 = N_CTX // DEC_SEQ
    cos_t = jnp.tile(cos64, (1, 2 * DIFF_HEADS))
    sin_t = jnp.tile(sin64, (1, 2 * DIFF_HEADS))
    return pl.pallas_call(
        functools.partial(_diff_lat_kernel, lam_init=lam_init),
        grid=(DEC_BATCH, LAT_QB),
        in_specs=[
            pl.BlockSpec((ATT_TQ, 512), lambda b, i: (qb0 + b * LAT_QB + i, C_DQ // 512)),
            pl.BlockSpec((DEC_SEQ, 512), lambda b, i: (bb0 + b, C_DK // 512)),
            pl.BlockSpec((DEC_SEQ, 512), lambda b, i: (bb0 + b, C_DV // 512)),
            pl.BlockSpec((None, None, PAST_LEN, GROUP_W), lambda b, i: (b, l, 0, 0)),
            pl.BlockSpec((None, None, PAST_LEN, GROUP_W), lambda b, i: (b, l, 0, 0)),
        ] + _vec_specs(4, DIFF_DH, l, 2) + _vec_specs(1, 2 * DIFF_DH, l, 2) + [
            pl.BlockSpec((ATT_TQ, 512), lambda b, i: (i, 0)),
            pl.BlockSpec((ATT_TQ, 512), lambda b, i: (i, 0)),
            pl.BlockSpec((DEC_SEQ, 512), lambda b, i: (0, 0)),
            pl.BlockSpec((DEC_SEQ, 512), lambda b, i: (0, 0)),
        ],
        out_specs=pl.BlockSpec((ATT_TQ, GROUP_W), lambda b, i: (b * LAT_QB + i, 0)),
        out_shape=jax.ShapeDtypeStruct((N_LAT, GROUP_W), F32),
        scratch_shapes=[pltpu.VMEM((DEC_SEQ, GROUP_W), BF16)],
        compiler_params=_cparams("parallel", "arbitrary"),
        name="diff_latent",
    )(proj, proj, proj, cache_k, cache_v, lq1, lk1, lq2, lk2, ng, cos_t, sin_t, cos_t, sin_t)


OUT_TM = 512


OUT_CTX_BLOCKS = N_CTX // OUT_TM


def _out_kernel(*refs):
    ctx_refs, lat_refs = refs[0:4], refs[4:8]
    x_ref, mod_ref, w_ref, x1_ref = refs[8:]
    is_ctx = pl.program_id(0) < OUT_CTX_BLOCKS
    acc = None
    for g in range(4):
        m = jnp.where(is_ctx, ctx_refs[g][...], lat_refs[g][...])
        t = _dot(m.astype(BF16), w_ref[g * GROUP_W:(g + 1) * GROUP_W, :])
        acc = t if acc is None else acc + t
    x1_ref[...] = x_ref[...] + mod_ref[2:3, :] * acc


def _out_proj(mixed_ctx, mixed_lat, x, mod4, w_out, l):
    row = lambda i: (i, 0)
    ctx_row = lambda i: (jnp.minimum(i, OUT_CTX_BLOCKS - 1), 0)
    lat_row = lambda i: (jnp.maximum(i - OUT_CTX_BLOCKS, 0), 0)
    return pl.pallas_call(
        _out_kernel,
        grid=(NT // OUT_TM,),
        in_specs=[pl.BlockSpec((OUT_TM, GROUP_W), ctx_row) for _ in range(4)]
        + [pl.BlockSpec((OUT_TM, GROUP_W), lat_row) for _ in range(4)] + [
            pl.BlockSpec((OUT_TM, D_MODEL), row),
            pl.BlockSpec((None, None, 6, D_MODEL), lambda i: (l, _mod_set(i * OUT_TM), 0, 0)),
            pl.BlockSpec((None, D_MODEL, D_MODEL), lambda i: (l, 0, 0)),
        ],
        out_specs=pl.BlockSpec((OUT_TM, D_MODEL), row),
        out_shape=jax.ShapeDtypeStruct((NT, D_MODEL), F32),
        compiler_params=_cparams("parallel"),
        name="out_proj_residual",
    )(*mixed_ctx, *mixed_lat, x, mod4, w_out)


def _peerq_kernel(x_ref, mod_ref, g_ref, w_ref, ht_ref, q_ref):
    h = _rms(x_ref[...], g_ref[...]) * (1.0 + mod_ref[4:5, :]) + mod_ref[3:4, :]
    ht_ref[...] = h.T.astype(BF16)
    q_ref[...] = _dot(h.astype(BF16), w_ref[...])


def _peer_query(x1, mod4, norm_g, w_q, l):
    row = lambda i: (i, 0)
    return pl.pallas_call(
        _peerq_kernel,
        grid=(NT // OUT_TM,),
        in_specs=[
            pl.BlockSpec((OUT_TM, D_MODEL), row),
            pl.BlockSpec((None, None, 6, D_MODEL), lambda i: (l, _mod_set(i * OUT_TM), 0, 0)),
            pl.BlockSpec((None, 1, D_MODEL), lambda i: (l, 0, 0)),
            pl.BlockSpec((None, D_MODEL, PEER_HEADS * PEER_QDIM), lambda i: (l, 0, 0)),
        ],
        out_specs=(pl.BlockSpec((D_MODEL, OUT_TM), lambda i: (0, i)),
                   pl.BlockSpec((OUT_TM, PEER_HEADS * PEER_QDIM), row)),
        out_shape=(jax.ShapeDtypeStruct((D_MODEL, NT), BF16),
                   jax.ShapeDtypeStruct((NT, PEER_HEADS * PEER_QDIM), F32)),
        compiler_params=_cparams("parallel"),
        name="adaln_peer_query",
    )(x1, mod4, norm_g, w_q)


ROUTE_TL = 256
ROUTE_LANES = 128
NOT_SEL = float(PEER_TOPK)


def _top16(s, index_ties):
    idx = lax.broadcasted_iota(jnp.int32, s.shape, 0).astype(F32)
    slot = lax.broadcasted_iota(jnp.int32, (PEER_TOPK, s.shape[1]), 0)
    rank = jnp.full(s.shape, NOT_SEL, F32)
    vals = jnp.zeros((PEER_TOPK, s.shape[1]), F32)
    for k in range(PEER_TOPK):
        m = jnp.max(s, axis=0, keepdims=True)
        sel = s == m
        if index_ties:
            sel = idx == jnp.min(jnp.where(sel, idx, float(PEER_NKEYS)), axis=0, keepdims=True)
        rank = jnp.where(sel, float(k), rank)
        s = jnp.where(sel, NEG, s)
        vals = jnp.where(slot == k, m, vals)
    return rank, vals


CAND_HALF = PEER_TOPK // 2
CAND_ROWS = PEER_TOPK + (CAND_HALF - 1) * CAND_HALF + CAND_HALF
FLAT_NONE = float(PEER_TOPK * PEER_TOPK)


def _cand_flat(tl):
    r = lax.broadcasted_iota(jnp.int32, (CAND_ROWS, tl), 0)
    mid = r - PEER_TOPK
    mid_flat = (1 + mid // CAND_HALF) * PEER_TOPK + mid % CAND_HALF
    last_flat = (CAND_HALF + r - (CAND_ROWS - CAND_HALF)) * PEER_TOPK
    flat = jnp.where(r < PEER_TOPK, r, jnp.where(r < CAND_ROWS - CAND_HALF, mid_flat, last_flat))
    return flat.astype(F32)


def _count(mask):
    return jnp.sum(jnp.where(mask, 1.0, 0.0), axis=0, keepdims=True)


def _route_head(s1, s2, flat, index_ties):
    tl = s1.shape[1]
    rank1, v1 = _top16(s1, index_ties)
    rank2, v2 = _top16(s2, index_ties)
    slabs = [v1[0:1, :] + v2]
    slabs += [v1[a:a + 1, :] + v2[0:CAND_HALF, :] for a in range(1, CAND_HALF)]
    slabs.append(v1[CAND_HALF:, :] + v2[0:1, :])
    cand = jnp.concatenate(slabs, axis=0)
    a_row = jnp.floor(flat * (1.0 / PEER_TOPK))
    top = v1[0:1, :] + v2[0:1, :]
    cnt1 = jnp.zeros(s1.shape, F32)
    z = jnp.zeros((1, tl), F32)
    for k in range(PEER_TOPK):
        m = jnp.max(cand, axis=0, keepdims=True)
        sel = cand == m
        if index_ties:
            first = jnp.min(jnp.where(sel, flat, FLAT_NONE), axis=0, keepdims=True)
            sel = flat == first
            a_sel = jnp.floor(first * (1.0 / PEER_TOPK))
        else:
            a_sel = jnp.max(jnp.where(sel, a_row, -1.0), axis=0, keepdims=True)
        cand = jnp.where(sel, NEG, cand)
        cnt1 = cnt1 + jnp.where(rank1 == a_sel, 1.0, 0.0)
        z = z + jnp.exp(m - top)
    e1 = jnp.exp(s1 - v1[0:1, :]) / z
    e2 = jnp.exp(s2 - v2[0:1, :])
    if index_ties:
        return rank2, cnt1, e1, e2, None
    full = float(PEER_TOPK)
    clean = ((_count(rank1 < NOT_SEL) == full) & (_count(rank2 < NOT_SEL) == full)
             & (_count(cand == NEG) == full))
    return rank2, cnt1, e1, e2, jnp.max(jnp.where(clean, 0.0, 1.0))


def _route_kernel(q_ref, k1_ref, k2_ref, rank2_ref, e2_ref, cnt1_ref, e1_ref):
    tl = q_ref.shape[0]
    half = PEER_QDIM // 2

    n_chunks = tl // ROUTE_LANES

    def scores(h):
        q1 = q_ref[:, h * PEER_QDIM:h * PEER_QDIM + half].astype(BF16)
        q2 = q_ref[:, h * PEER_QDIM + half:(h + 1) * PEER_QDIM].astype(BF16)
        return _dot_nt(k1_ref[...], q1), _dot_nt(k2_ref[...], q2)

    def route(h, c, s1, s2, index_ties):
        lanes = slice(c * ROUTE_LANES, (c + 1) * ROUTE_LANES)
        rank2, cnt1, e1, e2, tied = _route_head(s1[:, lanes], s2[:, lanes], _cand_flat(ROUTE_LANES), index_ties)
        rank2_ref[h, :, lanes] = rank2.astype(BF16)
        cnt1_ref[h, :, lanes] = cnt1
        e1_ref[h, :, lanes] = e1
        e2_ref[h, :, lanes] = e2.astype(BF16)
        return tied

    tied = {}
    for h in range(PEER_HEADS):
        s1, s2 = scores(h)
        for c in range(n_chunks):
            tied[h, c] = route(h, c, s1, s2, index_ties=False)

    for h in range(PEER_HEADS):
        for c in range(n_chunks):
            @pl.when(tied[h, c] > 0.0)
            def _():
                s1, s2 = scores(h)
                route(h, c, s1, s2, index_ties=True)


def _peer_route(q, keys, l):
    shp = jax.ShapeDtypeStruct((PEER_HEADS, PEER_NKEYS, NT), F32)
    shp_b = jax.ShapeDtypeStruct((PEER_HEADS, PEER_NKEYS, NT), BF16)
    spec = pl.BlockSpec((PEER_HEADS, PEER_NKEYS, ROUTE_TL), lambda i: (0, 0, i))
    half = PEER_QDIM // 2
    return pl.pallas_call(
        _route_kernel,
        grid=(NT // ROUTE_TL,),
        in_specs=[
            pl.BlockSpec((ROUTE_TL, PEER_HEADS * PEER_QDIM), lambda i: (i, 0)),
            pl.BlockSpec((None, None, PEER_NKEYS, half), lambda i: (l, 0, 0, 0)),
            pl.BlockSpec((None, None, PEER_NKEYS, half), lambda i: (l, 1, 0, 0)),
        ],
        out_specs=(spec, spec, spec, spec),
        out_shape=(shp_b, shp_b, shp, shp),
        compiler_params=_cparams("parallel"),
        name="peer_route",
    )(q, keys, keys)


PEER_TT = 512
PEER_EB = 512
GELU_C = math.sqrt(2.0 / math.pi)


def _gelu_tanh(x):
    return 0.5 * x * (1.0 + jnp.tanh(GELU_C * (x + 0.044715 * (x * x * x))))


PEER_NE = PEER_N // PEER_EB
PEER_MT = 256


def _peer_kernel(ht_ref, u_ref, vt_ref, rank2_ref, e2_ref, cnt1_ref, e1_ref, x_ref, mod_ref, o_ref,
                 ga_ref, gb_ref, acc_ref):
    e = pl.program_id(1)

    @pl.when(e == 0)
    def _():
        acc_ref[...] = jnp.zeros_like(acc_ref)
        gb_ref[...] = jnp.zeros_like(gb_ref)

    def step(g_prev_ref, g_next_ref):
        blk = jnp.maximum(e - 1, 0)
        n_i = PEER_EB // PEER_NKEYS
        per_tile = PEER_MT // PEER_NKEYS

        def weighted(tok, kc):
            tiles = []
            for ii in range(kc * per_tile, (kc + 1) * per_tile):
                key1 = blk * n_i + ii
                w = jnp.zeros((PEER_NKEYS, PEER_MT), BF16)
                for h in range(PEER_HEADS):
                    cnt = cnt1_ref[h, pl.ds(key1, 1), tok].astype(BF16)
                    g1 = e1_ref[h, pl.ds(key1, 1), tok].astype(BF16)
                    w = w + jnp.where(rank2_ref[h, :, tok] < cnt, e2_ref[h, :, tok] * g1, 0.0)
                tiles.append(w * g_prev_ref[ii * PEER_NKEYS:(ii + 1) * PEER_NKEYS, tok])
            return jnp.concatenate(tiles, axis=0)

        for tc in range(PEER_TT // PEER_MT):
            tok = slice(tc * PEER_MT, (tc + 1) * PEER_MT)
            p0 = weighted(tok, 0)
            act = _dot(u_ref[...], ht_ref[:, tok])
            p1 = weighted(tok, 1)
            t0 = _dot(vt_ref[:, 0:PEER_MT], p0)
            g_next_ref[:, tok] = _gelu_tanh(act).astype(BF16)
            acc_ref[:, tok] += t0 + _dot(vt_ref[:, PEER_MT:PEER_EB], p1)

    @pl.when(e % 2 == 0)
    def _():
        step(gb_ref, ga_ref)

    @pl.when(e % 2 == 1)
    def _():
        step(ga_ref, gb_ref)

    @pl.when(e == PEER_NE)
    def _():
        o_ref[...] = x_ref[...] + mod_ref[5:6, :] * acc_ref[...].T


def _peer(ht, u_tab, vt_tab, routing, x1, mod4, l):
    rank2, e2, cnt1, e1 = routing
    rspec = pl.BlockSpec((PEER_HEADS, PEER_NKEYS, PEER_TT), lambda i, e: (0, 0, i))
    return pl.pallas_call(
        _peer_kernel,
        grid=(NT // PEER_TT, PEER_NE + 1),
        in_specs=[
            pl.BlockSpec((D_MODEL, PEER_TT), lambda i, e: (0, i)),
            pl.BlockSpec((None, PEER_EB, D_MODEL), lambda i, e: (l, jnp.minimum(e, PEER_NE - 1), 0)),
            pl.BlockSpec((None, D_MODEL, PEER_EB), lambda i, e: (l, 0, jnp.maximum(e - 1, 0))),
            rspec, rspec, rspec, rspec,
            pl.BlockSpec((PEER_TT, D_MODEL), lambda i, e: (i, 0)),
            pl.BlockSpec((None, None, 6, D_MODEL), lambda i, e: (l, _mod_set(i * PEER_TT), 0, 0)),
        ],
        out_specs=pl.BlockSpec((PEER_TT, D_MODEL), lambda i, e: (i, 0)),
        out_shape=jax.ShapeDtypeStruct((NT, D_MODEL), F32),
        scratch_shapes=[pltpu.VMEM((PEER_EB, PEER_TT), BF16), pltpu.VMEM((PEER_EB, PEER_TT), BF16),
                        pltpu.VMEM((D_MODEL, PEER_TT), F32)],
        compiler_params=_cparams("parallel", "arbitrary"),
        name="peer_experts",
    )(ht, u_tab, vt_tab, rank2, e2, cnt1, e1, x1, mod4)


FIN_TM = 512


def _final_kernel(x_ref, g_ref, o_ref):
    o_ref[...] = _rms(x_ref[...], g_ref[...])


def _final_norm(x, g):
    return pl.pallas_call(
        _final_kernel,
        grid=(NT // FIN_TM,),
        in_specs=[pl.BlockSpec((FIN_TM, D_MODEL), lambda i: (i, 0)),
                  pl.BlockSpec((1, D_MODEL), lambda i: (0, 0))],
        out_specs=pl.BlockSpec((FIN_TM, D_MODEL), lambda i: (i, 0)),
        out_shape=jax.ShapeDtypeStruct((NT, D_MODEL), F32),
        compiler_params=_cparams("parallel"),
        name="final_norm",
    )(x, g)


def _permute_w_in(w_in):
    sizes = (GROUP_W, GROUP_W, GROUP_W, GROUP_W, 4 * ML_HEADS, MLA_Q_RANK, MLA_KV_RANK, MLA_D_ROPE,
             SWA_HEADS * SWA_DH, SWA_KV_HEADS * SWA_DH, SWA_KV_HEADS * SWA_DH, GROUP_W, GROUP_W, GROUP_W)
    offs = [0]
    for s in sizes:
        offs.append(offs[-1] + s)
    part = lambda i: w_in[:, :, offs[i]:offs[i + 1]]
    order = [0, 1, 2, 3, 5, 6, 8, 9, 10, 11, 12, 13, 7, 4]
    cols = [part(i) for i in order]
    used = sum(sizes)
    cols.append(jnp.zeros(w_in.shape[:2] + (PROJ_W - used,), w_in.dtype))
    return jnp.concatenate(cols, axis=-1).astype(BF16)


def _permute_w_uq(w_uq):
    w = w_uq.reshape(DEPTH, MLA_Q_RANK, MLA_HEADS, MLA_D_NOPE + MLA_D_ROPE)
    nope = w[..., :MLA_D_NOPE].reshape(DEPTH, MLA_Q_RANK, MLA_HEADS * MLA_D_NOPE)
    rope = w[..., MLA_D_NOPE:].reshape(DEPTH, MLA_Q_RANK, MLA_HEADS * MLA_D_ROPE)
    return jnp.concatenate([nope, rope], axis=-1).astype(BF16)


def kernel(x_prompt, x_sample, c, cache_mla_ckv, cache_mla_krope, cache_swa_k, cache_swa_v, cache_diff_k,
           cache_diff_v, state_mlstm_C, state_mlstm_n, state_mlstm_m, c_ctx, w_mod, b_mod, norm1_g, w_in,
           mlstm_i_bias, mlstm_f_bias, mlstm_norm_g, mla_qnorm_g, mla_w_uq, mla_kvnorm_g, mla_w_ukv, swa_sink,
           diff_lq1, diff_lk1, diff_lq2, diff_lk2, diff_norm_g, w_out, norm2_g, peer_w_q, peer_sub_keys,
           peer_u, peer_v, final_norm_g):
    x = jnp.concatenate([x_prompt.reshape(N_CTX, D_MODEL), x_sample.reshape(N_LAT, D_MODEL)], axis=0)

    cvec = jnp.concatenate([c_ctx[None, :], c], axis=0)
    cvec_t = jnp.pad(cvec.T, ((0, 0), (0, 8 - N_SETS)))
    mod4 = _modulation(cvec_t, w_mod, b_mod).reshape(DEPTH, 8, 6, D_MODEL)

    w_in_p = _permute_w_in(w_in)
    w_uq_p = _permute_w_uq(mla_w_uq)
    w_ukv_b = mla_w_ukv.astype(BF16)
    w_out_b = w_out.astype(BF16)
    w_q_b = peer_w_q.astype(BF16)
    keys_b = peer_sub_keys.astype(BF16)
    u_b = peer_u.astype(BF16)
    vt_b = jnp.swapaxes(peer_v, 1, 2).astype(BF16)

    vec3 = lambda a: a.reshape(DEPTH, 1, a.shape[-1])
    norm1_3, norm2_3 = vec3(norm1_g), vec3(norm2_g)
    mlng_3, qg_3, kg_3, dng_3 = vec3(mlstm_norm_g), vec3(mla_qnorm_g), vec3(mla_kvnorm_g), vec3(diff_norm_g)
    lq1_3, lk1_3, lq2_3, lk2_3 = vec3(diff_lq1), vec3(diff_lk1), vec3(diff_lq2), vec3(diff_lk2)

    n_chain = 2 * ML_HEADS
    st_c = state_mlstm_C.reshape(DEC_BATCH, DEPTH, n_chain, ML_DH, ML_DH)
    st_n = state_mlstm_n.reshape(DEC_BATCH, DEPTH, n_chain, ML_DH)
    st_m = jnp.broadcast_to(state_mlstm_m.reshape(DEC_BATCH, DEPTH, n_chain, 1), (DEC_BATCH, DEPTH, n_chain, ML_DH))
    swk_c = cache_swa_k.reshape(DEC_BATCH, DEPTH, PAST_LEN, SWA_KV_HEADS * SWA_DH)
    swv_c = cache_swa_v.reshape(DEC_BATCH, DEPTH, PAST_LEN, SWA_KV_HEADS * SWA_DH)
    dfk_c = cache_diff_k.reshape(DEC_BATCH, DEPTH, PAST_LEN, GROUP_W)
    dfv_c = cache_diff_v.reshape(DEC_BATCH, DEPTH, PAST_LEN, GROUP_W)

    cos64, sin64 = _rope_tables(DEC_SEQ, 64)
    cos128, sin128 = _rope_tables(DEC_SEQ, 128)

    outs = [[] for _ in range(9)]
    for l in range(DEPTH):
        proj = _project(x, mod4, norm1_3, w_in_p, l)

        gates = proj[:, C_TAIL + TAIL_G:C_TAIL + TAIL_G + 16]
        gt = gates.reshape(NT // ML_CHUNK, ML_CHUNK, 16).transpose(0, 2, 1)
        bias = jnp.concatenate([mlstm_i_bias[l].reshape(-1), mlstm_f_bias[l].reshape(-1)])
        bcol = bias.reshape(16, 1)
        brow = jnp.zeros((1, 128), F32).at[0, TAIL_G:TAIL_G + 16].set(bias)

        ml_ctx, c_st, n_st, m_st = _mlstm(proj, gt, bcol, brow, mlng_3, l, latent=False)
        ml_lat = _mlstm(proj, gt, bcol, brow, mlng_3, l, latent=True, states=(st_c, st_n, st_m))
        mla_ctx, ckv_n = _mla_ctx(proj, qg_3, kg_3, w_uq_p, w_ukv_b, l)
        mla_lat = _mla_lat(proj, qg_3, kg_3, w_uq_p, w_ukv_b, cache_mla_ckv, cache_mla_krope, cos64, sin64, l)
        sink = swa_sink[l].reshape(1, SWA_HEADS)
        swa_ctx = _swa_ctx(proj, sink, l)
        swa_lat = _swa_lat(proj, sink, swk_c, swv_c, cos128, sin128, l)
        df_ctx = _diff_ctx(proj, lq1_3, lk1_3, lq2_3, lk2_3, dng_3, l)
        df_lat = _diff_lat(proj, lq1_3, lk1_3, lq2_3, lk2_3, dng_3, dfk_c, dfv_c, cos64, sin64, l)

        x1 = _out_proj((ml_ctx, mla_ctx, swa_ctx, df_ctx), (ml_lat, mla_lat, swa_lat, df_lat), x, mod4, w_out_b, l)
        h2t, q = _peer_query(x1, mod4, norm2_3, w_q_b, l)
        routing = _peer_route(q, keys_b, l)
        x = _peer(h2t, u_b, vt_b, routing, x1, mod4, l)

        pc = proj[:N_CTX]
        outs[0].append(ckv_n.reshape(BATCH, SEQ, MLA_KV_RANK))
        outs[1].append(pc[:, C_TAIL:C_TAIL + MLA_D_ROPE].reshape(BATCH, SEQ, MLA_D_ROPE))
        outs[2].append(pc[:, C_SK:C_SK + 256].reshape(BATCH, SEQ, SWA_KV_HEADS, SWA_DH))
        outs[3].append(pc[:, C_SV:C_SV + 256].reshape(BATCH, SEQ, SWA_KV_HEADS, SWA_DH))
        outs[4].append(pc[:, C_DK:C_DK + 512].reshape(BATCH, SEQ, DIFF_HEADS, 2 * DIFF_DH))
        outs[5].append(pc[:, C_DV:C_DV + 512].reshape(BATCH, SEQ, DIFF_HEADS, 2 * DIFF_DH))
        outs[6].append(c_st.reshape(BATCH, 2, ML_HEADS, ML_DH, ML_DH))
        outs[7].append(n_st.reshape(BATCH, 2, ML_HEADS, ML_DH))
        outs[8].append(m_st[:, :, 0].reshape(BATCH, 2, ML_HEADS))

    y = _final_norm(x, final_norm_g.reshape(1, D_MODEL))
    y_prompt = y[:N_CTX].reshape(BATCH, SEQ, D_MODEL)
    y_sample = y[N_CTX:].reshape(DEC_BATCH, DEC_SEQ, D_MODEL)
    return (y_prompt, y_sample) + tuple(jnp.stack(o, axis=1) for o in outs)
```

```python
import functools
import math

import jax
import jax.numpy as jnp
from jax import lax
from jax.experimental import pallas as pl
from jax.experimental.pallas import tpu as pltpu

F32 = jnp.float32
BF16 = jnp.bfloat16

D_MODEL = 2048
BATCH = 32
SEQ = 256
DEPTH = 4
DEC_BATCH = 2
DEC_SEQ = 1024
PAST_LEN = 256
GRID_W = 64
GROUP_W = D_MODEL // 4
ML_HEADS = 4
ML_DH = GROUP_W // ML_HEADS
ML_CHUNK = 64
MLA_HEADS = 4
MLA_D_NOPE = GROUP_W // MLA_HEADS
MLA_D_ROPE = 64
MLA_Q_RANK = D_MODEL // 8
MLA_KV_RANK = D_MODEL // 8
MLA_SCALE = (MLA_D_NOPE + MLA_D_ROPE) ** -0.5
SWA_HEADS = 4
SWA_KV_HEADS = 2
SWA_DH = GROUP_W // SWA_HEADS
WINDOW = 128
DIFF_HEADS = 4
DIFF_DH = GROUP_W // (2 * DIFF_HEADS)
PEER_HEADS = 8
PEER_QDIM = 256
PEER_NKEYS = 128
PEER_N = PEER_NKEYS * PEER_NKEYS
PEER_TOPK = 16
ROPE_BASE = 10000.0
EPS = 1e-6
NEG = -1e30

N_CTX = BATCH * SEQ
N_LAT = DEC_BATCH * DEC_SEQ
NT = N_CTX + N_LAT
N_SETS = 1 + DEC_BATCH

C_MLQ, C_MLK, C_MLV, C_MLO = 0, 512, 1024, 1536
C_CQ, C_CKV = 2048, 2304
C_SQ, C_SK, C_SV = 2560, 3072, 3328
C_DQ, C_DK, C_DV = 3584, 4096, 4608
C_TAIL = 5120
TAIL_G = 64
PROJ_W = 5376
PROJ_TN = 896

VMEM_LIMIT = 56 * 1024 * 1024


def _cparams(*sem):
    return pltpu.CompilerParams(dimension_semantics=sem, vmem_limit_bytes=VMEM_LIMIT)


def _mod_set(row_start):
    return jnp.where(row_start >= N_CTX, (row_start - N_CTX) // DEC_SEQ + 1, 0)


def _rms(x, g):
    return x * lax.rsqrt(jnp.mean(x * x, axis=-1, keepdims=True) + EPS) * g


def _dot(a, b):
    return jnp.dot(a, b, preferred_element_type=F32)


def _dot_nt(a, b):
    return lax.dot_general(a, b, (((1,), (1,)), ((), ())), preferred_element_type=F32)


def _dot_tn(a, b):
    return lax.dot_general(a, b, (((0,), (0,)), ((), ())), preferred_element_type=F32)


def _dot_hi(a, b):
    return jnp.dot(a, b, preferred_element_type=F32, precision=lax.Precision.HIGHEST)


MOD_TK = 512
MOD_TN = 2048


def _mod_kernel(c_ref, w_ref, b_ref, o_ref, acc_ref):
    k = pl.program_id(2)

    @pl.when(k == 0)
    def _():
        acc_ref[...] = jnp.zeros_like(acc_ref)

    w = w_ref[...]
    for r in range(N_SETS):
        cv = c_ref[:, r:r + 1]
        sv = cv * jax.nn.sigmoid(cv)
        acc_ref[r] += (sv * w).reshape(MOD_TK // 8, 8, MOD_TN).sum(axis=0)

    @pl.when(k == pl.num_programs(2) - 1)
    def _():
        o_ref[...] = jnp.zeros_like(o_ref)
        for r in range(N_SETS):
            o_ref[r:r + 1, :] = acc_ref[r].sum(axis=0, keepdims=True) + b_ref[...]


def _modulation(cvec_t, w_mod, b_mod):
    return pl.pallas_call(
        _mod_kernel,
        grid=(DEPTH, 6 * D_MODEL // MOD_TN, D_MODEL // MOD_TK),
        in_specs=[
            pl.BlockSpec((MOD_TK, 8), lambda l, n, k: (k, 0)),
            pl.BlockSpec((None, MOD_TK, MOD_TN), lambda l, n, k: (l, k, n)),
            pl.BlockSpec((None, 1, MOD_TN), lambda l, n, k: (l, 0, n)),
        ],
        out_specs=pl.BlockSpec((None, 8, MOD_TN), lambda l, n, k: (l, 0, n)),
        out_shape=jax.ShapeDtypeStruct((DEPTH, 8, 6 * D_MODEL), F32),
        scratch_shapes=[pltpu.VMEM((N_SETS, 8, MOD_TN), F32)],
        compiler_params=_cparams("parallel", "parallel", "arbitrary"),
        name="modulation",
    )(cvec_t, w_mod, b_mod.reshape(DEPTH, 1, 6 * D_MODEL))


PROJ_TM = 1024


def _proj_kernel(x_ref, m_ref, g_ref, w_ref, o_ref, h_ref):
    @pl.when(pl.program_id(1) == 0)
    def _():
        h = _rms(x_ref[...], g_ref[...]) * (1.0 + m_ref[1:2, :]) + m_ref[0:1, :]
        h_ref[...] = h.astype(BF16)

    o_ref[...] = _dot(h_ref[...], w_ref[...])


def _project(x, mod4, norm_g, w_in_p, l):
    return pl.pallas_call(
        _proj_kernel,
        grid=(NT // PROJ_TM, PROJ_W // PROJ_TN),
        in_specs=[
            pl.BlockSpec((PROJ_TM, D_MODEL), lambda i, j: (i, 0)),
            pl.BlockSpec((None, None, 6, D_MODEL), lambda i, j: (l, _mod_set(i * PROJ_TM), 0, 0)),
            pl.BlockSpec((None, 1, D_MODEL), lambda i, j: (l, 0, 0)),
            pl.BlockSpec((None, D_MODEL, PROJ_TN), lambda i, j: (l, 0, j)),
        ],
        out_specs=pl.BlockSpec((PROJ_TM, PROJ_TN), lambda i, j: (i, j)),
        out_shape=jax.ShapeDtypeStruct((NT, PROJ_W), F32),
        scratch_shapes=[pltpu.VMEM((PROJ_TM, D_MODEL), BF16)],
        compiler_params=_cparams("parallel", "arbitrary"),
        name="adaln_in_proj",
    )(x, mod4, norm_g, w_in_p)


def _rope(x, cos, sin, quarter):
    width = x.shape[-1]
    lane = lax.broadcasted_iota(jnp.int32, x.shape, 1)
    first = (lane % (2 * quarter)) < quarter
    partner = jnp.where(first, pltpu.roll(x, width - quarter, 1), pltpu.roll(x, quarter, 1))
    return x * cos + partner * sin


def _rope_tables(n_tok, rot_dim):
    half = rot_dim // 2
    pos = jnp.arange(n_tok)
    row = (pos // GRID_W).astype(F32)
    col = (pos % GRID_W).astype(F32)
    inv = ROPE_BASE ** (-jnp.arange(0, half, 2, dtype=F32) / half)
    a_row = row[:, None] * inv[None, :]
    a_col = col[:, None] * inv[None, :]
    ang = jnp.concatenate([a_row, a_row, a_col, a_col], axis=-1)
    sign = jnp.tile(jnp.concatenate([-jnp.ones(half // 2, F32), jnp.ones(half // 2, F32)]), 2)
    return jnp.cos(ang), jnp.sin(ang) * sign[None, :]


def _log_sigmoid(x):
    return jnp.minimum(x, 0.0) - jnp.log(1.0 + jnp.exp(-jnp.abs(x)))


def _mlstm_kernel(*refs, n_tok, n_seq, has_state):
    if has_state:
        (q_ref, k_ref, v_ref, o_ref, tail_ref, gt_ref, bcol_ref, brow_ref, ng_ref, c0_ref, n0_ref, m0_ref,
         out_ref, hf_ref, hb_ref, cst_ref, nst_ref, mst_ref) = refs
    else:
        (q_ref, k_ref, v_ref, o_ref, tail_ref, gt_ref, bcol_ref, brow_ref, ng_ref,
         out_ref, cs_ref, ns_ref, ms_ref, hf_ref, hb_ref, cst_ref, nst_ref, mst_ref) = refs
    n_chunks = n_tok // ML_CHUNK
    scale = ML_DH ** -0.5

    if has_state:
        cst_ref[...] = c0_ref[...]
        nst_ref[...] = n0_ref[...]
        mst_ref[...] = m0_ref[...]
    else:
        cst_ref[...] = jnp.zeros_like(cst_ref)
        nst_ref[...] = jnp.zeros_like(nst_ref)
        mst_ref[...] = jnp.zeros_like(mst_ref)

    row = lax.broadcasted_iota(jnp.int32, (ML_CHUNK, ML_CHUNK), 0)
    col = lax.broadcasted_iota(jnp.int32, (ML_CHUNK, ML_CHUNK), 1)
    lower = (col <= row)
    upper = (col >= row)
    lower_f = lower.astype(F32)
    upper_f = upper.astype(F32)

    def scan_group(c, group):
        ch = []
        for b, d in group:
            cc = c if d == 0 else n_chunks - 1 - c
            t0 = pl.multiple_of(b * n_tok + cc * ML_CHUNK, ML_CHUNK)
            rows = pl.ds(t0, ML_CHUNK)
            g_col = tail_ref[rows, :] + brow_ref[...]
            g_row = gt_ref[b * n_chunks + cc] + bcol_ref[...]
            lf_col = _log_sigmoid(g_col)
            lf_row = _log_sigmoid(g_row)
            if d == 0:
                bcum_col = _dot_hi(lower_f, lf_col)
                bcum_row = _dot_hi(lf_row, upper_f)
                mask = lower
            else:
                bcum_col = _dot_hi(upper_f, lf_col)
                bcum_row = _dot_hi(lf_row, lower_f)
                mask = upper
            for h in range(ML_HEADS):
                r = d * ML_HEADS + h
                fr = 2 * ML_HEADS + r
                ch.append(dict(
                    b=b, d=d, r=r, rows=rows, lanes=slice(h * ML_DH, (h + 1) * ML_DH), mask=mask,
                    ig_c=g_col[:, TAIL_G + r:TAIL_G + r + 1],
                    b_c=bcum_col[:, TAIL_G + fr:TAIL_G + fr + 1],
                    ig_r=g_row[r:r + 1, :],
                    b_r=bcum_row[fr:fr + 1, :],
                    tot=jnp.sum(lf_row[fr:fr + 1, :], axis=1, keepdims=True)))
        n = range(len(ch))
        m_prev = [mst_ref[x["b"], x["r"]:x["r"] + 1, 0:1] for x in ch]
        n_prev = [nst_ref[x["b"], x["r"]:x["r"] + 1, :] for x in ch]
        c_prev = [cst_ref[x["b"], x["r"]] for x in ch]
        q = [q_ref[x["rows"], x["lanes"]] for x in ch]
        k = [k_ref[x["rows"], x["lanes"]] for x in ch]
        v = [v_ref[x["rows"], x["lanes"]].astype(BF16) for x in ch]
        qb = [x.astype(BF16) for x in q]

        dmat = [jnp.where(ch[i]["mask"], ch[i]["b_c"] - ch[i]["b_r"] + ch[i]["ig_r"], NEG) for i in n]
        inter = [ch[i]["b_c"] + m_prev[i] for i in n]
        mt = [jnp.maximum(inter[i], jnp.max(dmat[i], axis=1, keepdims=True)) for i in n]
        w = [jnp.exp(dmat[i] - mt[i]) for i in n]
        qk = [_dot_nt(qb[i], k[i].astype(BF16)) for i in n]
        qc = [_dot(qb[i], c_prev[i].astype(BF16)) for i in n]
        s = [qk[i] * scale * w[i] for i in n]
        a = [jnp.exp(inter[i] - mt[i]) for i in n]
        sv = [_dot(s[i].astype(BF16), v[i]) for i in n]
        den = [jnp.sum(s[i], axis=1, keepdims=True) + a[i] * jnp.sum(q[i] * n_prev[i], axis=1, keepdims=True)
               for i in n]
        hc = [(sv[i] + a[i] * qc[i]) / jnp.maximum(jnp.abs(den[i]), jnp.exp(-mt[i])) for i in n]
        for i in n:
            (hf_ref if ch[i]["d"] == 0 else hb_ref)[ch[i]["rows"], ch[i]["lanes"]] = hc[i]

        wlog_c = [ch[i]["tot"] - ch[i]["b_c"] + ch[i]["ig_c"] for i in n]
        wlog_r = [ch[i]["tot"] - ch[i]["b_r"] + ch[i]["ig_r"] for i in n]
        m_new = [jnp.maximum(ch[i]["tot"] + m_prev[i], jnp.max(wlog_r[i], axis=1, keepdims=True)) for i in n]
        decay = [jnp.exp(ch[i]["tot"] + m_prev[i] - m_new[i]) for i in n]
        kw = [k[i] * (scale * jnp.exp(wlog_c[i] - m_new[i])) for i in n]
        kv = [_dot_tn(kw[i].astype(BF16), v[i]) for i in n]
        for i in n:
            b, r = ch[i]["b"], ch[i]["r"]
            cst_ref[b, r] = decay[i] * c_prev[i] + kv[i]
            nst_ref[b, r:r + 1, :] = decay[i] * n_prev[i] + jnp.sum(kw[i], axis=0, keepdims=True)
            mst_ref[b, r:r + 1, :] = jnp.broadcast_to(m_new[i], (1, ML_DH))

    def chunk_step(c, carry):
        for b in range(n_seq):
            scan_group(c, [(b, 0)])
            scan_group(c, [(b, 1)])
        return carry

    lax.fori_loop(0, n_chunks, chunk_step, 0)

    for h in range(ML_HEADS):
        lanes = slice(h * ML_DH, (h + 1) * ML_DH)
        hs = hf_ref[:, lanes] + hb_ref[:, lanes]
        out_ref[:, lanes] = _rms(hs, ng_ref[:, lanes]) * jax.nn.sigmoid(o_ref[:, lanes])

    if not has_state:
        cs_ref[...] = cst_ref[...]
        ns_ref[...] = nst_ref[...]
        ms_ref[...] = mst_ref[...]


ML_SEQ_PER_STEP = 4


def _mlstm(proj, gt, bcol, brow, norm_g, l, *, latent, states=None):
    n_tok = DEC_SEQ if latent else SEQ
    n_seq = 1 if latent else ML_SEQ_PER_STEP
    n_b = DEC_BATCH if latent else BATCH
    rows = n_seq * n_tok
    blk0 = N_CTX // rows if latent else 0
    n_chunks = n_tok // ML_CHUNK

    def col_spec(c0):
        return pl.BlockSpec((rows, 512), lambda b: (blk0 + b, c0 // 512))

    in_specs = [
        col_spec(C_MLQ), col_spec(C_MLK), col_spec(C_MLV), col_spec(C_MLO),
        pl.BlockSpec((rows, 128), lambda b: (blk0 + b, C_TAIL // 128)),
        pl.BlockSpec((n_seq * n_chunks, 16, ML_CHUNK), lambda b: (blk0 + b, 0, 0)),
        pl.BlockSpec((16, 1), lambda b: (0, 0)),
        pl.BlockSpec((1, 128), lambda b: (0, 0)),
        pl.BlockSpec((None, 1, GROUP_W), lambda b: (l, 0, 0)),
    ]
    args = [proj, proj, proj, proj, proj, gt, bcol, brow, norm_g]
    n_chain = 2 * ML_HEADS
    scratch = [pltpu.VMEM((rows, GROUP_W), F32), pltpu.VMEM((rows, GROUP_W), F32),
               pltpu.VMEM((n_seq, n_chain, ML_DH, ML_DH), F32), pltpu.VMEM((n_seq, n_chain, ML_DH), F32),
               pltpu.VMEM((n_seq, n_chain, ML_DH), F32)]
    if latent:
        c0, n0, m0 = states
        in_specs += [
            pl.BlockSpec((n_seq, None, n_chain, ML_DH, ML_DH), lambda b: (b, l, 0, 0, 0)),
            pl.BlockSpec((n_seq, None, n_chain, ML_DH), lambda b: (b, l, 0, 0)),
            pl.BlockSpec((n_seq, None, n_chain, ML_DH), lambda b: (b, l, 0, 0)),
        ]
        args += [c0, n0, m0]
        out_shape = jax.ShapeDtypeStruct((N_LAT, GROUP_W), F32)
        out_specs = pl.BlockSpec((rows, GROUP_W), lambda b: (b, 0))
    else:
        out_shape = (jax.ShapeDtypeStruct((N_CTX, GROUP_W), F32),
                     jax.ShapeDtypeStruct((BATCH, n_chain, ML_DH, ML_DH), F32),
                     jax.ShapeDtypeStruct((BATCH, n_chain, ML_DH), F32),
                     jax.ShapeDtypeStruct((BATCH, n_chain, ML_DH), F32))
        out_specs = (pl.BlockSpec((rows, GROUP_W), lambda b: (b, 0)),
                     pl.BlockSpec((n_seq, n_chain, ML_DH, ML_DH), lambda b: (b, 0, 0, 0)),
                     pl.BlockSpec((n_seq, n_chain, ML_DH), lambda b: (b, 0, 0)),
                     pl.BlockSpec((n_seq, n_chain, ML_DH), lambda b: (b, 0, 0)))
    return pl.pallas_call(
        functools.partial(_mlstm_kernel, n_tok=n_tok, n_seq=n_seq, has_state=latent),
        grid=(n_b // n_seq,),
        in_specs=in_specs,
        out_specs=out_specs,
        out_shape=out_shape,
        scratch_shapes=scratch,
        compiler_params=_cparams("parallel"),
        name="mlstm_latent" if latent else "mlstm_context",
    )(*args)


def _softmax_parts(scores, sink=None):
    m = jnp.max(scores[0], axis=1, keepdims=True)
    for s in scores[1:]:
        m = jnp.maximum(m, jnp.max(s, axis=1, keepdims=True))
    if sink is not None:
        m = jnp.maximum(m, sink)
    es = [jnp.exp(s - m) for s in scores]
    den = jnp.sum(es[0], axis=1, keepdims=True)
    for e in es[1:]:
        den = den + jnp.sum(e, axis=1, keepdims=True)
    if sink is not None:
        den = den + jnp.exp(sink - m)
    return es, den


ATT_TQ = 256
LAT_QB = DEC_SEQ // ATT_TQ
N_KEYS_LAT = DEC_SEQ + PAST_LEN


def _mla_q(cq_ref, qg_ref, wuq_ref):
    return _dot(_rms(cq_ref[...], qg_ref[...]).astype(BF16), wuq_ref[...])


def _mla_heads(q, kv, kr, n_keys):
    outs = []
    for h in range(MLA_HEADS):
        qn = q[:, h * MLA_D_NOPE:(h + 1) * MLA_D_NOPE].astype(BF16)
        r0 = MLA_HEADS * MLA_D_NOPE + h * MLA_D_ROPE
        qr = q[:, r0:r0 + MLA_D_ROPE].astype(BF16)
        kn = kv[:, h * 256:h * 256 + MLA_D_NOPE]
        v = kv[:, h * 256 + MLA_D_NOPE:(h + 1) * 256]
        s = (_dot_nt(qn, kn) + _dot_nt(qr, kr)) * MLA_SCALE
        (e,), den = _softmax_parts([s])
        outs.append(_dot(e.astype(BF16), v) / den)
    return outs


def _mla_ctx_kernel(cq_ref, ckv_ref, tail_ref, qg_ref, kg_ref, wuq_ref, wukv_ref, out_ref, ckvn_ref):
    q = _mla_q(cq_ref, qg_ref, wuq_ref)
    ckvn = _rms(ckv_ref[...], kg_ref[...])
    ckvn_ref[...] = ckvn
    kv = _dot(ckvn.astype(BF16), wukv_ref[...]).astype(BF16)
    kr = tail_ref[:, 0:MLA_D_ROPE].astype(BF16)
    outs = _mla_heads(q, kv, kr, SEQ)
    for h in range(MLA_HEADS):
        out_ref[:, h * MLA_D_V:(h + 1) * MLA_D_V] = outs[h]


MLA_D_V = GROUP_W // MLA_HEADS


def _mla_lat_kernel(cq_ref, ckv_ref, tail_ref, qg_ref, kg_ref, wuq_ref, wukv_ref, ckvc_ref, krc_ref,
                    cosq_ref, sinq_ref, cosk_ref, sink_ref, out_ref, kv_ref, kr_ref):
    @pl.when(pl.program_id(1) == 0)
    def _():
        ckvn = _rms(ckv_ref[...], kg_ref[...])
        kv_ref[0:DEC_SEQ, :] = _dot(ckvn.astype(BF16), wukv_ref[...]).astype(BF16)
        kv_ref[DEC_SEQ:N_KEYS_LAT, :] = _dot(ckvc_ref[...].astype(BF16), wukv_ref[...]).astype(BF16)
        kr = _rope(tail_ref[...], cosk_ref[...], sink_ref[...], MLA_D_ROPE // 4)
        kr_ref[0:DEC_SEQ, :] = kr[:, 0:MLA_D_ROPE].astype(BF16)
        kr_ref[DEC_SEQ:N_KEYS_LAT, :] = krc_ref[...].astype(BF16)

    q = _mla_q(cq_ref, qg_ref, wuq_ref)
    n0 = MLA_HEADS * MLA_D_NOPE
    q_rope = _rope(q[:, n0:], cosq_ref[...], sinq_ref[...], MLA_D_ROPE // 4)
    q = jnp.concatenate([q[:, :n0], q_rope], axis=1)
    outs = _mla_heads(q, kv_ref[...], kr_ref[...], N_KEYS_LAT)
    for h in range(MLA_HEADS):
        out_ref[:, h * MLA_D_V:(h + 1) * MLA_D_V] = outs[h]


def _w_specs2(shape_a, shape_b, l):
    return [pl.BlockSpec((None,) + shape_a, lambda *_: (l,) + (0,) * len(shape_a)),
            pl.BlockSpec((None,) + shape_b, lambda *_: (l,) + (0,) * len(shape_b))]


def _mla_ctx(proj, qg, kg, wuq, wukv, l):
    return pl.pallas_call(
        _mla_ctx_kernel,
        grid=(BATCH,),
        in_specs=[
            pl.BlockSpec((SEQ, 256), lambda b: (b, C_CQ // 256)),
            pl.BlockSpec((SEQ, 256), lambda b: (b, C_CKV // 256)),
            pl.BlockSpec((SEQ, 128), lambda b: (b, C_TAIL // 128)),
            pl.BlockSpec((None, 1, MLA_Q_RANK), lambda b: (l, 0, 0)),
            pl.BlockSpec((None, 1, MLA_KV_RANK), lambda b: (l, 0, 0)),
            pl.BlockSpec((None, MLA_Q_RANK, 768), lambda b: (l, 0, 0)),
            pl.BlockSpec((None, MLA_KV_RANK, 1024), lambda b: (l, 0, 0)),
        ],
        out_specs=(pl.BlockSpec((SEQ, GROUP_W), lambda b: (b, 0)),
                   pl.BlockSpec((SEQ, MLA_KV_RANK), lambda b: (b, 0))),
        out_shape=(jax.ShapeDtypeStruct((N_CTX, GROUP_W), F32),
                   jax.ShapeDtypeStruct((N_CTX, MLA_KV_RANK), F32)),
        compiler_params=_cparams("parallel"),
        name="mla_context",
    )(proj, proj, proj, qg, kg, wuq, wukv)


def _mla_lat(proj, qg, kg, wuq, wukv, cache_ckv, cache_kr, cos64, sin64, l):
    qb0 = N_CTX // ATT_TQ
    bb0 = N_CTX // DEC_SEQ
    cosq = jnp.tile(cos64, (1, MLA_HEADS))
    sinq = jnp.tile(sin64, (1, MLA_HEADS))
    cosk = jnp.tile(cos64, (1, 2))
    sink = jnp.tile(sin64, (1, 2))
    return pl.pallas_call(
        _mla_lat_kernel,
        grid=(DEC_BATCH, LAT_QB),
        in_specs=[
            pl.BlockSpec((ATT_TQ, 256), lambda b, i: (qb0 + b * LAT_QB + i, C_CQ // 256)),
            pl.BlockSpec((DEC_SEQ, 256), lambda b, i: (bb0 + b, C_CKV // 256)),
            pl.BlockSpec((DEC_SEQ, 128), lambda b, i: (bb0 + b, C_TAIL // 128)),
            pl.BlockSpec((None, 1, MLA_Q_RANK), lambda b, i: (l, 0, 0)),
            pl.BlockSpec((None, 1, MLA_KV_RANK), lambda b, i: (l, 0, 0)),
            pl.BlockSpec((None, MLA_Q_RANK, 768), lambda b, i: (l, 0, 0)),
            pl.BlockSpec((None, MLA_KV_RANK, 1024), lambda b, i: (l, 0, 0)),
            pl.BlockSpec((None, None, PAST_LEN, MLA_KV_RANK), lambda b, i: (b, l, 0, 0)),
            pl.BlockSpec((None, None, PAST_LEN, MLA_D_ROPE), lambda b, i: (b, l, 0, 0)),
            pl.BlockSpec((ATT_TQ, 256), lambda b, i: (i, 0)),
            pl.BlockSpec((ATT_TQ, 256), lambda b, i: (i, 0)),
            pl.BlockSpec((DEC_SEQ, 128), lambda b, i: (0, 0)),
            pl.BlockSpec((DEC_SEQ, 128), lambda b, i: (0, 0)),
        ],
        out_specs=pl.BlockSpec((ATT_TQ, GROUP_W), lambda b, i: (b * LAT_QB + i, 0)),
        out_shape=jax.ShapeDtypeStruct((N_LAT, GROUP_W), F32),
        scratch_shapes=[pltpu.VMEM((N_KEYS_LAT, 1024), BF16), pltpu.VMEM((N_KEYS_LAT, MLA_D_ROPE), BF16)],
        compiler_params=_cparams("parallel", "arbitrary"),
        name="mla_latent",
    )(proj, proj, proj, qg, kg, wuq, wukv, cache_ckv, cache_kr, cosq, sinq, cosk, sink)


SWA_SCALE = SWA_DH ** -0.5
SWA_REP = SWA_HEADS // SWA_KV_HEADS
SWA_KWIN = ATT_TQ + 2 * WINDOW


def _swa_ctx_kernel(sink_ref, q_ref, k_ref, v_ref, out_ref):
    kb = k_ref[...].astype(BF16)
    vb = v_ref[...].astype(BF16)
    for h in range(SWA_HEADS):
        g = h // SWA_REP
        q = q_ref[:, h * SWA_DH:(h + 1) * SWA_DH].astype(BF16)
        s = _dot_nt(q, kb[:, g * SWA_DH:(g + 1) * SWA_DH]) * SWA_SCALE
        (e,), den = _softmax_parts([s], sink=sink_ref[0, h])
        out_ref[:, h * SWA_DH:(h + 1) * SWA_DH] = _dot(e.astype(BF16), vb[:, g * SWA_DH:(g + 1) * SWA_DH]) / den


def _swa_lat_kernel(sink_ref, q_ref, k_ref, v_ref, kc_ref, vc_ref, cosq_ref, sinq_ref, cosk_ref, sink_t_ref,
                    out_ref, kr_ref):
    i = pl.program_id(1)

    @pl.when(i == 0)
    def _():
        kr_ref[...] = _rope(k_ref[...], cosk_ref[...], sink_t_ref[...], SWA_DH // 4).astype(BF16)

    q_all = _rope(q_ref[...], cosq_ref[...], sinq_ref[...], SWA_DH // 4)
    k0 = pl.multiple_of(jnp.clip(i * ATT_TQ - WINDOW, 0, DEC_SEQ - SWA_KWIN), WINDOW)
    kwin = kr_ref[pl.ds(k0, SWA_KWIN), :]
    vwin = v_ref[pl.ds(k0, SWA_KWIN), :].astype(BF16)
    kc = kc_ref[...].astype(BF16)
    vc = vc_ref[...].astype(BF16)
    qpos = i * ATT_TQ + lax.broadcasted_iota(jnp.int32, (ATT_TQ, SWA_KWIN), 0)
    kpos = k0 + lax.broadcasted_iota(jnp.int32, (ATT_TQ, SWA_KWIN), 1)
    band = jnp.abs(qpos - kpos) <= WINDOW
    for h in range(SWA_HEADS):
        g = h // SWA_REP
        gl = slice(g * SWA_DH, (g + 1) * SWA_DH)
        q = q_all[:, h * SWA_DH:(h + 1) * SWA_DH].astype(BF16)
        s_loc = jnp.where(band, _dot_nt(q, kwin[:, gl]) * SWA_SCALE, NEG)
        s_ctx = _dot_nt(q, kc[:, gl]) * SWA_SCALE
        (e_loc, e_ctx), den = _softmax_parts([s_loc, s_ctx], sink=sink_ref[0, h])
        o = _dot(e_loc.astype(BF16), vwin[:, gl]) + _dot(e_ctx.astype(BF16), vc[:, gl])
        out_ref[:, h * SWA_DH:(h + 1) * SWA_DH] = o / den


def _smem_spec():
    return pl.BlockSpec(memory_space=pltpu.SMEM)


def _swa_ctx(proj, sink, l):
    return pl.pallas_call(
        _swa_ctx_kernel,
        grid=(BATCH,),
        in_specs=[
            _smem_spec(),
            pl.BlockSpec((SEQ, 512), lambda b: (b, C_SQ // 512)),
            pl.BlockSpec((SEQ, 256), lambda b: (b, C_SK // 256)),
            pl.BlockSpec((SEQ, 256), lambda b: (b, C_SV // 256)),
        ],
        out_specs=pl.BlockSpec((SEQ, GROUP_W), lambda b: (b, 0)),
        out_shape=jax.ShapeDtypeStruct((N_CTX, GROUP_W), F32),
        compiler_params=_cparams("parallel"),
        name="swa_context",
    )(sink, proj, proj, proj)


def _swa_lat(proj, sink, cache_k, cache_v, cos128, sin128, l):
    qb0 = N_CTX // ATT_TQ
    bb0 = N_CTX // DEC_SEQ
    kvw = SWA_KV_HEADS * SWA_DH
    return pl.pallas_call(
        _swa_lat_kernel,
        grid=(DEC_BATCH, LAT_QB),
        in_specs=[
            _smem_spec(),
            pl.BlockSpec((ATT_TQ, 512), lambda b, i: (qb0 + b * LAT_QB + i, C_SQ // 512)),
            pl.BlockSpec((DEC_SEQ, 256), lambda b, i: (bb0 + b, C_SK // 256)),
            pl.BlockSpec((DEC_SEQ, 256), lambda b, i: (bb0 + b, C_SV // 256)),
            pl.BlockSpec((None, None, PAST_LEN, kvw), lambda b, i: (b, l, 0, 0)),
            pl.BlockSpec((None, None, PAST_LEN, kvw), lambda b, i: (b, l, 0, 0)),
            pl.BlockSpec((ATT_TQ, 512), lambda b, i: (i, 0)),
            pl.BlockSpec((ATT_TQ, 512), lambda b, i: (i, 0)),
            pl.BlockSpec((DEC_SEQ, 256), lambda b, i: (0, 0)),
            pl.BlockSpec((DEC_SEQ, 256), lambda b, i: (0, 0)),
        ],
        out_specs=pl.BlockSpec((ATT_TQ, GROUP_W), lambda b, i: (b * LAT_QB + i, 0)),
        out_shape=jax.ShapeDtypeStruct((N_LAT, GROUP_W), F32),
        scratch_shapes=[pltpu.VMEM((DEC_SEQ, kvw), BF16)],
        compiler_params=_cparams("parallel", "arbitrary"),
        name="swa_latent",
    )(sink, proj, proj, proj, cache_k, cache_v,
      jnp.tile(cos128, (1, SWA_HEADS)), jnp.tile(sin128, (1, SWA_HEADS)),
      jnp.tile(cos128, (1, SWA_KV_HEADS)), jnp.tile(sin128, (1, SWA_KV_HEADS)))


DIFF_SCALE = DIFF_DH ** -0.5


def _diff_lambda(lq1_ref, lk1_ref, lq2_ref, lk2_ref):
    a = jnp.sum(lq1_ref[...] * lk1_ref[...], axis=1, keepdims=True)
    b = jnp.sum(lq2_ref[...] * lk2_ref[...], axis=1, keepdims=True)
    return jnp.exp(a) - jnp.exp(b)


def _diff_heads(q, k_parts, v_parts, lam, lam_init, ng_ref, out_ref):
    for h in range(DIFF_HEADS):
        ps = []
        for c in range(2):
            sl = slice(h * 2 * DIFF_DH + c * DIFF_DH, h * 2 * DIFF_DH + (c + 1) * DIFF_DH)
            qc = q[:, sl].astype(BF16)
            es, den = _softmax_parts([_dot_nt(qc, kp[:, sl]) * DIFF_SCALE for kp in k_parts])
            ps.append([e / den for e in es])
        vl = slice(h * 2 * DIFF_DH, (h + 1) * 2 * DIFF_DH)
        o = None
        for p1, p2, vp in zip(ps[0], ps[1], v_parts):
            t = _dot((p1 - lam * p2).astype(BF16), vp[:, vl])
            o = t if o is None else o + t
        out_ref[:, vl] = _rms(o, ng_ref[...]) * (1.0 - lam_init)


def _diff_ctx_kernel(q_ref, k_ref, v_ref, lq1_ref, lk1_ref, lq2_ref, lk2_ref, ng_ref, out_ref, *, lam_init):
    lam = _diff_lambda(lq1_ref, lk1_ref, lq2_ref, lk2_ref) + lam_init
    _diff_heads(q_ref[...], [k_ref[...].astype(BF16)], [v_ref[...].astype(BF16)], lam, lam_init, ng_ref, out_ref)


def _diff_lat_kernel(q_ref, k_ref, v_ref, kc_ref, vc_ref, lq1_ref, lk1_ref, lq2_ref, lk2_ref, ng_ref,
                     cosq_ref, sinq_ref, cosk_ref, sink_ref, out_ref, kr_ref, *, lam_init):
    @pl.when(pl.program_id(1) == 0)
    def _():
        kr_ref[...] = _rope(k_ref[...], cosk_ref[...], sink_ref[...], DIFF_DH // 4).astype(BF16)

    lam = _diff_lambda(lq1_ref, lk1_ref, lq2_ref, lk2_ref) + lam_init
    q = _rope(q_ref[...], cosq_ref[...], sinq_ref[...], DIFF_DH // 4)
    _diff_heads(q, [kr_ref[...], kc_ref[...].astype(BF16)], [v_ref[...].astype(BF16), vc_ref[...].astype(BF16)],
                lam, lam_init, ng_ref, out_ref)


def _vec_specs(n, width, l, nargs):
    return [pl.BlockSpec((None, 1, width), lambda *_: (l, 0, 0)) for _ in range(n)]


def _diff_ctx(proj, lq1, lk1, lq2, lk2, ng, l):
    lam_init = 0.8 - 0.6 * math.exp(-0.3 * l)
    return pl.pallas_call(
        functools.partial(_diff_ctx_kernel, lam_init=lam_init),
        grid=(BATCH,),
        in_specs=[
            pl.BlockSpec((SEQ, 512), lambda b: (b, C_DQ // 512)),
            pl.BlockSpec((SEQ, 512), lambda b: (b, C_DK // 512)),
            pl.BlockSpec((SEQ, 512), lambda b: (b, C_DV // 512)),
        ] + _vec_specs(4, DIFF_DH, l, 1) + _vec_specs(1, 2 * DIFF_DH, l, 1),
        out_specs=pl.BlockSpec((SEQ, GROUP_W), lambda b: (b, 0)),
        out_shape=jax.ShapeDtypeStruct((N_CTX, GROUP_W), F32),
        compiler_params=_cparams("parallel"),
        name="diff_context",
    )(proj, proj, proj, lq1, lk1, lq2, lk2, ng)


def _diff_lat(proj, lq1, lk1, lq2, lk2, ng, cache_k, cache_v, cos64, sin64, l):
    lam_init = 0.8 - 0.6 * math.exp(-0.3 * l)
    qb0 = N_CTX // ATT_TQ
    bb0 = N_CTX // DEC_SEQ
    cos_t = jnp.tile(cos64, (1, 2 * DIFF_HEADS))
    sin_t = jnp.tile(sin64, (1, 2 * DIFF_HEADS))
    return pl.pallas_call(
        functools.partial(_diff_lat_kernel, lam_init=lam_init),
        grid=(DEC_BATCH, LAT_QB),
        in_specs=[
            pl.BlockSpec((ATT_TQ, 512), lambda b, i: (qb0 + b * LAT_QB + i, C_DQ // 512)),
            pl.BlockSpec((DEC_SEQ, 512), lambda b, i: (bb0 + b, C_DK // 512)),
            pl.BlockSpec((DEC_SEQ, 512), lambda b, i: (bb0 + b, C_DV // 512)),
            pl.BlockSpec((None, None, PAST_LEN, GROUP_W), lambda b, i: (b, l, 0, 0)),
            pl.BlockSpec((None, None, PAST_LEN, GROUP_W), lambda b, i: (b, l, 0, 0)),
        ] + _vec_specs(4, DIFF_DH, l, 2) + _vec_specs(1, 2 * DIFF_DH, l, 2) + [
            pl.BlockSpec((ATT_TQ, 512), lambda b, i: (i, 0)),
            pl.BlockSpec((ATT_TQ, 512), lambda b, i: (i, 0)),
            pl.BlockSpec((DEC_SEQ, 512), lambda b, i: (0, 0)),
            pl.BlockSpec((DEC_SEQ, 512), lambda b, i: (0, 0)),
        ],
        out_specs=pl.BlockSpec((ATT_TQ, GROUP_W), lambda b, i: (b * LAT_QB + i, 0)),
        out_shape=jax.ShapeDtypeStruct((N_LAT, GROUP_W), F32),
        scratch_shapes=[pltpu.VMEM((DEC_SEQ, GROUP_W), BF16)],
        compiler_params=_cparams("parallel", "arbitrary"),
        name="diff_latent",
    )(proj, proj, proj, cache_k, cache_v, lq1, lk1, lq2, lk2, ng, cos_t, sin_t, cos_t, sin_t)


OUT_TM = 512


OUT_CTX_BLOCKS = N_CTX // OUT_TM


def _out_kernel(*refs):
    ctx_refs, lat_refs = refs[0:4], refs[4:8]
    x_ref, mod_ref, w_ref, x1_ref = refs[8:]
    is_ctx = pl.program_id(0) < OUT_CTX_BLOCKS
    acc = None
    for g in range(4):
        m = jnp.where(is_ctx, ctx_refs[g][...], lat_refs[g][...])
        t = _dot(m.astype(BF16), w_ref[g * GROUP_W:(g + 1) * GROUP_W, :])
        acc = t if acc is None else acc + t
    x1_ref[...] = x_ref[...] + mod_ref[2:3, :] * acc


def _out_proj(mixed_ctx, mixed_lat, x, mod4, w_out, l):
    row = lambda i: (i, 0)
    ctx_row = lambda i: (jnp.minimum(i, OUT_CTX_BLOCKS - 1), 0)
    lat_row = lambda i: (jnp.maximum(i - OUT_CTX_BLOCKS, 0), 0)
    return pl.pallas_call(
        _out_kernel,
        grid=(NT // OUT_TM,),
        in_specs=[pl.BlockSpec((OUT_TM, GROUP_W), ctx_row) for _ in range(4)]
        + [pl.BlockSpec((OUT_TM, GROUP_W), lat_row) for _ in range(4)] + [
            pl.BlockSpec((OUT_TM, D_MODEL), row),
            pl.BlockSpec((None, None, 6, D_MODEL), lambda i: (l, _mod_set(i * OUT_TM), 0, 0)),
            pl.BlockSpec((None, D_MODEL, D_MODEL), lambda i: (l, 0, 0)),
        ],
        out_specs=pl.BlockSpec((OUT_TM, D_MODEL), row),
        out_shape=jax.ShapeDtypeStruct((NT, D_MODEL), F32),
        compiler_params=_cparams("parallel"),
        name="out_proj_residual",
    )(*mixed_ctx, *mixed_lat, x, mod4, w_out)


def _peerq_kernel(x_ref, mod_ref, g_ref, w_ref, ht_ref, q_ref):
    h = _rms(x_ref[...], g_ref[...]) * (1.0 + mod_ref[4:5, :]) + mod_ref[3:4, :]
    ht_ref[...] = h.T.astype(BF16)
    q_ref[...] = _dot(h.astype(BF16), w_ref[...])


def _peer_query(x1, mod4, norm_g, w_q, l):
    row = lambda i: (i, 0)
    return pl.pallas_call(
        _peerq_kernel,
        grid=(NT // OUT_TM,),
        in_specs=[
            pl.BlockSpec((OUT_TM, D_MODEL), row),
            pl.BlockSpec((None, None, 6, D_MODEL), lambda i: (l, _mod_set(i * OUT_TM), 0, 0)),
            pl.BlockSpec((None, 1, D_MODEL), lambda i: (l, 0, 0)),
            pl.BlockSpec((None, D_MODEL, PEER_HEADS * PEER_QDIM), lambda i: (l, 0, 0)),
        ],
        out_specs=(pl.BlockSpec((D_MODEL, OUT_TM), lambda i: (0, i)),
                   pl.BlockSpec((OUT_TM, PEER_HEADS * PEER_QDIM), row)),
        out_shape=(jax.ShapeDtypeStruct((D_MODEL, NT), BF16),
                   jax.ShapeDtypeStruct((NT, PEER_HEADS * PEER_QDIM), F32)),
        compiler_params=_cparams("parallel"),
        name="adaln_peer_query",
    )(x1, mod4, norm_g, w_q)


ROUTE_TL = 256
ROUTE_LANES = 128
NOT_SEL = float(PEER_TOPK)


def _top16(s, index_ties):
    idx = lax.broadcasted_iota(jnp.int32, s.shape, 0).astype(F32)
    slot = lax.broadcasted_iota(jnp.int32, (PEER_TOPK, s.shape[1]), 0)
    rank = jnp.full(s.shape, NOT_SEL, F32)
    vals = jnp.zeros((PEER_TOPK, s.shape[1]), F32)
    for k in range(PEER_TOPK):
        m = jnp.max(s, axis=0, keepdims=True)
        sel = s == m
        if index_ties:
            sel = idx == jnp.min(jnp.where(sel, idx, float(PEER_NKEYS)), axis=0, keepdims=True)
        rank = jnp.where(sel, float(k), rank)
        s = jnp.where(sel, NEG, s)
        vals = jnp.where(slot == k, m, vals)
    return rank, vals


CAND_HALF = PEER_TOPK // 2
CAND_ROWS = PEER_TOPK + (CAND_HALF - 1) * CAND_HALF + CAND_HALF
FLAT_NONE = float(PEER_TOPK * PEER_TOPK)


def _cand_flat(tl):
    r = lax.broadcasted_iota(jnp.int32, (CAND_ROWS, tl), 0)
    mid = r - PEER_TOPK
    mid_flat = (1 + mid // CAND_HALF) * PEER_TOPK + mid % CAND_HALF
    last_flat = (CAND_HALF + r - (CAND_ROWS - CAND_HALF)) * PEER_TOPK
    flat = jnp.where(r < PEER_TOPK, r, jnp.where(r < CAND_ROWS - CAND_HALF, mid_flat, last_flat))
    return flat.astype(F32)


def _count(mask):
    return jnp.sum(jnp.where(mask, 1.0, 0.0), axis=0, keepdims=True)


def _route_head(s1, s2, flat, index_ties):
    tl = s1.shape[1]
    rank1, v1 = _top16(s1, index_ties)
    rank2, v2 = _top16(s2, index_ties)
    slabs = [v1[0:1, :] + v2]
    slabs += [v1[a:a + 1, :] + v2[0:CAND_HALF, :] for a in range(1, CAND_HALF)]
    slabs.append(v1[CAND_HALF:, :] + v2[0:1, :])
    cand = jnp.concatenate(slabs, axis=0)
    a_row = jnp.floor(flat * (1.0 / PEER_TOPK))
    top = v1[0:1, :] + v2[0:1, :]
    cnt1 = jnp.zeros(s1.shape, F32)
    z = jnp.zeros((1, tl), F32)
    for k in range(PEER_TOPK):
        m = jnp.max(cand, axis=0, keepdims=True)
        sel = cand == m
        if index_ties:
            first = jnp.min(jnp.where(sel, flat, FLAT_NONE), axis=0, keepdims=True)
            sel = flat == first
            a_sel = jnp.floor(first * (1.0 / PEER_TOPK))
        else:
            a_sel = jnp.max(jnp.where(sel, a_row, -1.0), axis=0, keepdims=True)
        cand = jnp.where(sel, NEG, cand)
        cnt1 = cnt1 + jnp.where(rank1 == a_sel, 1.0, 0.0)
        z = z + jnp.exp(m - top)
    e1 = jnp.exp(s1 - v1[0:1, :]) / z
    e2 = jnp.exp(s2 - v2[0:1, :])
    if index_ties:
        return rank2, cnt1, e1, e2, None
    full = float(PEER_TOPK)
    clean = ((_count(rank1 < NOT_SEL) == full) & (_count(rank2 < NOT_SEL) == full)
             & (_count(cand == NEG) == full))
    return rank2, cnt1, e1, e2, jnp.max(jnp.where(clean, 0.0, 1.0))


def _route_kernel(q_ref, k1_ref, k2_ref, rank2_ref, e2_ref, cnt1_ref, e1_ref):
    tl = q_ref.shape[0]
    half = PEER_QDIM // 2

    n_chunks = tl // ROUTE_LANES

    def scores(h):
        q1 = q_ref[:, h * PEER_QDIM:h * PEER_QDIM + half].astype(BF16)
        q2 = q_ref[:, h * PEER_QDIM + half:(h + 1) * PEER_QDIM].astype(BF16)
        return _dot_nt(k1_ref[...], q1), _dot_nt(k2_ref[...], q2)

    def route(h, c, s1, s2, index_ties):
        lanes = slice(c * ROUTE_LANES, (c + 1) * ROUTE_LANES)
        rank2, cnt1, e1, e2, tied = _route_head(s1[:, lanes], s2[:, lanes], _cand_flat(ROUTE_LANES), index_ties)
        rank2_ref[h, :, lanes] = rank2.astype(BF16)
        cnt1_ref[h, :, lanes] = cnt1
        e1_ref[h, :, lanes] = e1
        e2_ref[h, :, lanes] = e2.astype(BF16)
        return tied

    tied = {}
    for h in range(PEER_HEADS):
        s1, s2 = scores(h)
        for c in range(n_chunks):
            tied[h, c] = route(h, c, s1, s2, index_ties=False)

    for h in range(PEER_HEADS):
        for c in range(n_chunks):
            @pl.when(tied[h, c] > 0.0)
            def _():
                s1, s2 = scores(h)
                route(h, c, s1, s2, index_ties=True)


def _peer_route(q, keys, l):
    shp = jax.ShapeDtypeStruct((PEER_HEADS, PEER_NKEYS, NT), F32)
    shp_b = jax.ShapeDtypeStruct((PEER_HEADS, PEER_NKEYS, NT), BF16)
    spec = pl.BlockSpec((PEER_HEADS, PEER_NKEYS, ROUTE_TL), lambda i: (0, 0, i))
    half = PEER_QDIM // 2
    return pl.pallas_call(
        _route_kernel,
        grid=(NT // ROUTE_TL,),
        in_specs=[
            pl.BlockSpec((ROUTE_TL, PEER_HEADS * PEER_QDIM), lambda i: (i, 0)),
            pl.BlockSpec((None, None, PEER_NKEYS, half), lambda i: (l, 0, 0, 0)),
            pl.BlockSpec((None, None, PEER_NKEYS, half), lambda i: (l, 1, 0, 0)),
        ],
        out_specs=(spec, spec, spec, spec),
        out_shape=(shp_b, shp_b, shp, shp),
        compiler_params=_cparams("parallel"),
        name="peer_route",
    )(q, keys, keys)


PEER_TT = 512
PEER_EB = 1024
GELU_C = math.sqrt(2.0 / math.pi)


def _gelu_tanh(x):
    return 0.5 * x * (1.0 + jnp.tanh(GELU_C * (x + 0.044715 * (x * x * x))))


PEER_NE = PEER_N // PEER_EB
PEER_MT = 256


def _peer_kernel(ht_ref, u_ref, vt_ref, rank2_ref, e2_ref, cnt1_ref, e1_ref, x_ref, mod_ref, o_ref,
                 ga_ref, gb_ref, acc_ref):
    e = pl.program_id(1)

    @pl.when(e == 0)
    def _():
        acc_ref[...] = jnp.zeros_like(acc_ref)
        gb_ref[...] = jnp.zeros_like(gb_ref)

    def step(g_prev_ref, g_next_ref):
        blk = jnp.maximum(e - 1, 0)
        n_i = PEER_EB // PEER_NKEYS
        per_tile = PEER_MT // PEER_NKEYS

        def weighted(tok, kc):
            tiles = []
            for ii in range(kc * per_tile, (kc + 1) * per_tile):
                key1 = blk * n_i + ii
                w = jnp.zeros((PEER_NKEYS, PEER_MT), BF16)
                for h in range(PEER_HEADS):
                    cnt = cnt1_ref[h, pl.ds(key1, 1), tok].astype(BF16)
                    g1 = e1_ref[h, pl.ds(key1, 1), tok].astype(BF16)
                    w = w + jnp.where(rank2_ref[h, :, tok] < cnt, e2_ref[h, :, tok] * g1, 0.0)
                tiles.append(w * g_prev_ref[ii * PEER_NKEYS:(ii + 1) * PEER_NKEYS, tok])
            return jnp.concatenate(tiles, axis=0)

        n_kc = PEER_EB // PEER_MT
        for tc in range(PEER_TT // PEER_MT):
            tok = slice(tc * PEER_MT, (tc + 1) * PEER_MT)
            act = _dot(u_ref[...], ht_ref[:, tok])
            p = [weighted(tok, 0), weighted(tok, 1)]
            acc = None
            for kc in range(n_kc):
                if kc + 2 < n_kc:
                    p.append(weighted(tok, kc + 2))
                t = _dot(vt_ref[:, kc * PEER_MT:(kc + 1) * PEER_MT], p[kc])
                acc = t if acc is None else acc + t
                if kc == n_kc - 2:
                    g_next_ref[:, tok] = _gelu_tanh(act).astype(BF16)
            acc_ref[:, tok] += acc

    @pl.when(e % 2 == 0)
    def _():
        step(gb_ref, ga_ref)

    @pl.when(e % 2 == 1)
    def _():
        step(ga_ref, gb_ref)

    @pl.when(e == PEER_NE)
    def _():
        o_ref[...] = x_ref[...] + mod_ref[5:6, :] * acc_ref[...].T


def _peer(ht, u_tab, vt_tab, routing, x1, mod4, l):
    rank2, e2, cnt1, e1 = routing
    once = pl.Buffered(1)
    rspec = pl.BlockSpec((PEER_HEADS, PEER_NKEYS, PEER_TT), lambda i, e: (0, 0, i), pipeline_mode=once)
    return pl.pallas_call(
        _peer_kernel,
        grid=(NT // PEER_TT, PEER_NE + 1),
        in_specs=[
            pl.BlockSpec((D_MODEL, PEER_TT), lambda i, e: (0, i), pipeline_mode=once),
            pl.BlockSpec((None, PEER_EB, D_MODEL), lambda i, e: (l, jnp.minimum(e, PEER_NE - 1), 0)),
            pl.BlockSpec((None, D_MODEL, PEER_EB), lambda i, e: (l, 0, jnp.maximum(e - 1, 0))),
            rspec, rspec, rspec, rspec,
            pl.BlockSpec((PEER_TT, D_MODEL), lambda i, e: (i, 0), pipeline_mode=once),
            pl.BlockSpec((None, None, 6, D_MODEL), lambda i, e: (l, _mod_set(i * PEER_TT), 0, 0)),
        ],
        out_specs=pl.BlockSpec((PEER_TT, D_MODEL), lambda i, e: (i, 0)),
        out_shape=jax.ShapeDtypeStruct((NT, D_MODEL), F32),
        scratch_shapes=[pltpu.VMEM((PEER_EB, PEER_TT), BF16), pltpu.VMEM((PEER_EB, PEER_TT), BF16),
                        pltpu.VMEM((D_MODEL, PEER_TT), F32)],
        compiler_params=_cparams("parallel", "arbitrary"),
        name="peer_experts",
    )(ht, u_tab, vt_tab, rank2, e2, cnt1, e1, x1, mod4)


FIN_TM = 512


def _final_kernel(x_ref, g_ref, o_ref):
    o_ref[...] = _rms(x_ref[...], g_ref[...])


def _final_norm(x, g):
    return pl.pallas_call(
        _final_kernel,
        grid=(NT // FIN_TM,),
        in_specs=[pl.BlockSpec((FIN_TM, D_MODEL), lambda i: (i, 0)),
                  pl.BlockSpec((1, D_MODEL), lambda i: (0, 0))],
        out_specs=pl.BlockSpec((FIN_TM, D_MODEL), lambda i: (i, 0)),
        out_shape=jax.ShapeDtypeStruct((NT, D_MODEL), F32),
        compiler_params=_cparams("parallel"),
        name="final_norm",
    )(x, g)


def _permute_w_in(w_in):
    sizes = (GROUP_W, GROUP_W, GROUP_W, GROUP_W, 4 * ML_HEADS, MLA_Q_RANK, MLA_KV_RANK, MLA_D_ROPE,
             SWA_HEADS * SWA_DH, SWA_KV_HEADS * SWA_DH, SWA_KV_HEADS * SWA_DH, GROUP_W, GROUP_W, GROUP_W)
    offs = [0]
    for s in sizes:
        offs.append(offs[-1] + s)
    part = lambda i: w_in[:, :, offs[i]:offs[i + 1]]
    order = [0, 1, 2, 3, 5, 6, 8, 9, 10, 11, 12, 13, 7, 4]
    cols = [part(i) for i in order]
    used = sum(sizes)
    cols.append(jnp.zeros(w_in.shape[:2] + (PROJ_W - used,), w_in.dtype))
    return jnp.concatenate(cols, axis=-1).astype(BF16)


def _permute_w_uq(w_uq):
    w = w_uq.reshape(DEPTH, MLA_Q_RANK, MLA_HEADS, MLA_D_NOPE + MLA_D_ROPE)
    nope = w[..., :MLA_D_NOPE].reshape(DEPTH, MLA_Q_RANK, MLA_HEADS * MLA_D_NOPE)
    rope = w[..., MLA_D_NOPE:].reshape(DEPTH, MLA_Q_RANK, MLA_HEADS * MLA_D_ROPE)
    return jnp.concatenate([nope, rope], axis=-1).astype(BF16)


def kernel(x_prompt, x_sample, c, cache_mla_ckv, cache_mla_krope, cache_swa_k, cache_swa_v, cache_diff_k,
           cache_diff_v, state_mlstm_C, state_mlstm_n, state_mlstm_m, c_ctx, w_mod, b_mod, norm1_g, w_in,
           mlstm_i_bias, mlstm_f_bias, mlstm_norm_g, mla_qnorm_g, mla_w_uq, mla_kvnorm_g, mla_w_ukv, swa_sink,
           diff_lq1, diff_lk1, diff_lq2, diff_lk2, diff_norm_g, w_out, norm2_g, peer_w_q, peer_sub_keys,
           peer_u, peer_v, final_norm_g):
    x = jnp.concatenate([x_prompt.reshape(N_CTX, D_MODEL), x_sample.reshape(N_LAT, D_MODEL)], axis=0)

    cvec = jnp.concatenate([c_ctx[None, :], c], axis=0)
    cvec_t = jnp.pad(cvec.T, ((0, 0), (0, 8 - N_SETS)))
    mod4 = _modulation(cvec_t, w_mod, b_mod).reshape(DEPTH, 8, 6, D_MODEL)

    w_in_p = _permute_w_in(w_in)
    w_uq_p = _permute_w_uq(mla_w_uq)
    w_ukv_b = mla_w_ukv.astype(BF16)
    w_out_b = w_out.astype(BF16)
    w_q_b = peer_w_q.astype(BF16)
    keys_b = peer_sub_keys.astype(BF16)
    u_b = peer_u.astype(BF16)
    vt_b = jnp.swapaxes(peer_v, 1, 2).astype(BF16)

    vec3 = lambda a: a.reshape(DEPTH, 1, a.shape[-1])
    norm1_3, norm2_3 = vec3(norm1_g), vec3(norm2_g)
    mlng_3, qg_3, kg_3, dng_3 = vec3(mlstm_norm_g), vec3(mla_qnorm_g), vec3(mla_kvnorm_g), vec3(diff_norm_g)
    lq1_3, lk1_3, lq2_3, lk2_3 = vec3(diff_lq1), vec3(diff_lk1), vec3(diff_lq2), vec3(diff_lk2)

    n_chain = 2 * ML_HEADS
    st_c = state_mlstm_C.reshape(DEC_BATCH, DEPTH, n_chain, ML_DH, ML_DH)
    st_n = state_mlstm_n.reshape(DEC_BATCH, DEPTH, n_chain, ML_DH)
    st_m = jnp.broadcast_to(state_mlstm_m.reshape(DEC_BATCH, DEPTH, n_chain, 1), (DEC_BATCH, DEPTH, n_chain, ML_DH))
    swk_c = cache_swa_k.reshape(DEC_BATCH, DEPTH, PAST_LEN, SWA_KV_HEADS * SWA_DH)
    swv_c = cache_swa_v.reshape(DEC_BATCH, DEPTH, PAST_LEN, SWA_KV_HEADS * SWA_DH)
    dfk_c = cache_diff_k.reshape(DEC_BATCH, DEPTH, PAST_LEN, GROUP_W)
    dfv_c = cache_diff_v.reshape(DEC_BATCH, DEPTH, PAST_LEN, GROUP_W)

    cos64, sin64 = _rope_tables(DEC_SEQ, 64)
    cos128, sin128 = _rope_tables(DEC_SEQ, 128)

    outs = [[] for _ in range(9)]
    for l in range(DEPTH):
        proj = _project(x, mod4, norm1_3, w_in_p, l)

        gates = proj[:, C_TAIL + TAIL_G:C_TAIL + TAIL_G + 16]
        gt = gates.reshape(NT // ML_CHUNK, ML_CHUNK, 16).transpose(0, 2, 1)
        bias = jnp.concatenate([mlstm_i_bias[l].reshape(-1), mlstm_f_bias[l].reshape(-1)])
        bcol = bias.reshape(16, 1)
        brow = jnp.zeros((1, 128), F32).at[0, TAIL_G:TAIL_G + 16].set(bias)

        ml_ctx, c_st, n_st, m_st = _mlstm(proj, gt, bcol, brow, mlng_3, l, latent=False)
        ml_lat = _mlstm(proj, gt, bcol, brow, mlng_3, l, latent=True, states=(st_c, st_n, st_m))
        mla_ctx, ckv_n = _mla_ctx(proj, qg_3, kg_3, w_uq_p, w_ukv_b, l)
        mla_lat = _mla_lat(proj, qg_3, kg_3, w_uq_p, w_ukv_b, cache_mla_ckv, cache_mla_krope, cos64, sin64, l)
        sink = swa_sink[l].reshape(1, SWA_HEADS)
        swa_ctx = _swa_ctx(proj, sink, l)
        swa_lat = _swa_lat(proj, sink, swk_c, swv_c, cos128, sin128, l)
        df_ctx = _diff_ctx(proj, lq1_3, lk1_3, lq2_3, lk2_3, dng_3, l)
        df_lat = _diff_lat(proj, lq1_3, lk1_3, lq2_3, lk2_3, dng_3, dfk_c, dfv_c, cos64, sin64, l)

        x1 = _out_proj((ml_ctx, mla_ctx, swa_ctx, df_ctx), (ml_lat, mla_lat, swa_lat, df_lat), x, mod4, w_out_b, l)
        h2t, q = _peer_query(x1, mod4, norm2_3, w_q_b, l)
        routing = _peer_route(q, keys_b, l)
        x = _peer(h2t, u_b, vt_b, routing, x1, mod4, l)

        pc = proj[:N_CTX]
        outs[0].append(ckv_n.reshape(BATCH, SEQ, MLA_KV_RANK))
        outs[1].append(pc[:, C_TAIL:C_TAIL + MLA_D_ROPE].reshape(BATCH, SEQ, MLA_D_ROPE))
        outs[2].append(pc[:, C_SK:C_SK + 256].reshape(BATCH, SEQ, SWA_KV_HEADS, SWA_DH))
        outs[3].append(pc[:, C_SV:C_SV + 256].reshape(BATCH, SEQ, SWA_KV_HEADS, SWA_DH))
        outs[4].append(pc[:, C_DK:C_DK + 512].reshape(BATCH, SEQ, DIFF_HEADS, 2 * DIFF_DH))
        outs[5].append(pc[:, C_DV:C_DV + 512].reshape(BATCH, SEQ, DIFF_HEADS, 2 * DIFF_DH))
        outs[6].append(c_st.reshape(BATCH, 2, ML_HEADS, ML_DH, ML_DH))
        outs[7].append(n_st.reshape(BATCH, 2, ML_HEADS, ML_DH))
        outs[8].append(m_st[:, :, 0].reshape(BATCH, 2, ML_HEADS))

    y = _final_norm(x, final_norm_g.reshape(1, D_MODEL))
    y_prompt = y[:N_CTX].reshape(BATCH, SEQ, D_MODEL)
    y_sample = y[N_CTX:].reshape(DEC_BATCH, DEC_SEQ, D_MODEL)
    return (y_prompt, y_sample) + tuple(jnp.stack(o, axis=1) for o in outs)
```

```python
import functools
import math

import jax
import jax.numpy as jnp
from jax import lax
from jax.experimental import pallas as pl
from jax.experimental.pallas import tpu as pltpu

F32 = jnp.float32
BF16 = jnp.bfloat16

D_MODEL = 2048
BATCH = 32
SEQ = 256
DEPTH = 4
DEC_BATCH = 2
DEC_SEQ = 1024
PAST_LEN = 256
GRID_W = 64
GROUP_W = D_MODEL // 4
ML_HEADS = 4
ML_DH = GROUP_W // ML_HEADS
ML_CHUNK = 64
MLA_HEADS = 4
MLA_D_NOPE = GROUP_W // MLA_HEADS
MLA_D_ROPE = 64
MLA_Q_RANK = D_MODEL // 8
MLA_KV_RANK = D_MODEL // 8
MLA_SCALE = (MLA_D_NOPE + MLA_D_ROPE) ** -0.5
SWA_HEADS = 4
SWA_KV_HEADS = 2
SWA_DH = GROUP_W // SWA_HEADS
WINDOW = 128
DIFF_HEADS = 4
DIFF_DH = GROUP_W // (2 * DIFF_HEADS)
PEER_HEADS = 8
PEER_QDIM = 256
PEER_NKEYS = 128
PEER_N = PEER_NKEYS * PEER_NKEYS
PEER_TOPK = 16
ROPE_BASE = 10000.0
EPS = 1e-6
NEG = -1e30

N_CTX = BATCH * SEQ
N_LAT = DEC_BATCH * DEC_SEQ
NT = N_CTX + N_LAT
N_SETS = 1 + DEC_BATCH

C_MLQ, C_MLK, C_MLV, C_MLO = 0, 512, 1024, 1536
C_CQ, C_CKV = 2048, 2304
C_SQ, C_SK, C_SV = 2560, 3072, 3328
C_DQ, C_DK, C_DV = 3584, 4096, 4608
C_TAIL = 5120
TAIL_G = 64
PROJ_W = 5376
PROJ_TN = 1792

VMEM_LIMIT = 56 * 1024 * 1024


def _cparams(*sem):
    return pltpu.CompilerParams(dimension_semantics=sem, vmem_limit_bytes=VMEM_LIMIT)


def _mod_set(row_start):
    return jnp.where(row_start >= N_CTX, (row_start - N_CTX) // DEC_SEQ + 1, 0)


def _rms(x, g):
    return x * lax.rsqrt(jnp.mean(x * x, axis=-1, keepdims=True) + EPS) * g


def _dot(a, b):
    return jnp.dot(a, b, preferred_element_type=F32)


def _dot_nt(a, b):
    return lax.dot_general(a, b, (((1,), (1,)), ((), ())), preferred_element_type=F32)


def _dot_tn(a, b):
    return lax.dot_general(a, b, (((0,), (0,)), ((), ())), preferred_element_type=F32)


def _dot_hi(a, b):
    return jnp.dot(a, b, preferred_element_type=F32, precision=lax.Precision.HIGHEST)


MOD_TK = 512
MOD_TN = 2048


def _mod_kernel(c_ref, w_ref, b_ref, o_ref, acc_ref):
    k = pl.program_id(2)

    @pl.when(k == 0)
    def _():
        acc_ref[...] = jnp.zeros_like(acc_ref)

    w = w_ref[...]
    for r in range(N_SETS):
        cv = c_ref[:, r:r + 1]
        sv = cv * jax.nn.sigmoid(cv)
        acc_ref[r] += (sv * w).reshape(MOD_TK // 8, 8, MOD_TN).sum(axis=0)

    @pl.when(k == pl.num_programs(2) - 1)
    def _():
        o_ref[...] = jnp.zeros_like(o_ref)
        for r in range(N_SETS):
            o_ref[r:r + 1, :] = acc_ref[r].sum(axis=0, keepdims=True) + b_ref[...]


def _modulation(cvec_t, w_mod, b_mod):
    return pl.pallas_call(
        _mod_kernel,
        grid=(DEPTH, 6 * D_MODEL // MOD_TN, D_MODEL // MOD_TK),
        in_specs=[
            pl.BlockSpec((MOD_TK, 8), lambda l, n, k: (k, 0)),
            pl.BlockSpec((None, MOD_TK, MOD_TN), lambda l, n, k: (l, k, n)),
            pl.BlockSpec((None, 1, MOD_TN), lambda l, n, k: (l, 0, n)),
        ],
        out_specs=pl.BlockSpec((None, 8, MOD_TN), lambda l, n, k: (l, 0, n)),
        out_shape=jax.ShapeDtypeStruct((DEPTH, 8, 6 * D_MODEL), F32),
        scratch_shapes=[pltpu.VMEM((N_SETS, 8, MOD_TN), F32)],
        compiler_params=_cparams("parallel", "parallel", "arbitrary"),
        name="modulation",
    )(cvec_t, w_mod, b_mod.reshape(DEPTH, 1, 6 * D_MODEL))


PROJ_TM = 1024


def _proj_kernel(x_ref, m_ref, g_ref, w_ref, o_ref, h_ref):
    @pl.when(pl.program_id(1) == 0)
    def _():
        h = _rms(x_ref[...], g_ref[...]) * (1.0 + m_ref[1:2, :]) + m_ref[0:1, :]
        h_ref[...] = h.astype(BF16)

    o_ref[...] = _dot(h_ref[...], w_ref[...])


def _project(x, mod4, norm_g, w_in_p, l):
    return pl.pallas_call(
        _proj_kernel,
        grid=(NT // PROJ_TM, PROJ_W // PROJ_TN),
        in_specs=[
            pl.BlockSpec((PROJ_TM, D_MODEL), lambda i, j: (i, 0), pipeline_mode=pl.Buffered(1)),
            pl.BlockSpec((None, None, 6, D_MODEL), lambda i, j: (l, _mod_set(i * PROJ_TM), 0, 0)),
            pl.BlockSpec((None, 1, D_MODEL), lambda i, j: (l, 0, 0)),
            pl.BlockSpec((None, D_MODEL, PROJ_TN), lambda i, j: (l, 0, j)),
        ],
        out_specs=pl.BlockSpec((PROJ_TM, PROJ_TN), lambda i, j: (i, j)),
        out_shape=jax.ShapeDtypeStruct((NT, PROJ_W), F32),
        scratch_shapes=[pltpu.VMEM((PROJ_TM, D_MODEL), BF16)],
        compiler_params=_cparams("parallel", "arbitrary"),
        name="adaln_in_proj",
    )(x, mod4, norm_g, w_in_p)


def _rope(x, cos, sin, quarter):
    width = x.shape[-1]
    lane = lax.broadcasted_iota(jnp.int32, x.shape, 1)
    first = (lane % (2 * quarter)) < quarter
    partner = jnp.where(first, pltpu.roll(x, width - quarter, 1), pltpu.roll(x, quarter, 1))
    return x * cos + partner * sin


def _rope_tables(n_tok, rot_dim):
    half = rot_dim // 2
    pos = jnp.arange(n_tok)
    row = (pos // GRID_W).astype(F32)
    col = (pos % GRID_W).astype(F32)
    inv = ROPE_BASE ** (-jnp.arange(0, half, 2, dtype=F32) / half)
    a_row = row[:, None] * inv[None, :]
    a_col = col[:, None] * inv[None, :]
    ang = jnp.concatenate([a_row, a_row, a_col, a_col], axis=-1)
    sign = jnp.tile(jnp.concatenate([-jnp.ones(half // 2, F32), jnp.ones(half // 2, F32)]), 2)
    return jnp.cos(ang), jnp.sin(ang) * sign[None, :]


def _log_sigmoid(x):
    return jnp.minimum(x, 0.0) - jnp.log(1.0 + jnp.exp(-jnp.abs(x)))


def _mlstm_kernel(*refs, n_tok, n_seq, has_state):
    if has_state:
        (q_ref, k_ref, v_ref, o_ref, tail_ref, gt_ref, bcol_ref, brow_ref, ng_ref, c0_ref, n0_ref, m0_ref,
         out_ref, hf_ref, hb_ref, cst_ref, nst_ref, mst_ref) = refs
    else:
        (q_ref, k_ref, v_ref, o_ref, tail_ref, gt_ref, bcol_ref, brow_ref, ng_ref,
         out_ref, cs_ref, ns_ref, ms_ref, hf_ref, hb_ref, cst_ref, nst_ref, mst_ref) = refs
    n_chunks = n_tok // ML_CHUNK
    scale = ML_DH ** -0.5

    if has_state:
        cst_ref[...] = c0_ref[...]
        nst_ref[...] = n0_ref[...]
        mst_ref[...] = m0_ref[...]
    else:
        cst_ref[...] = jnp.zeros_like(cst_ref)
        nst_ref[...] = jnp.zeros_like(nst_ref)
        mst_ref[...] = jnp.zeros_like(mst_ref)

    row = lax.broadcasted_iota(jnp.int32, (ML_CHUNK, ML_CHUNK), 0)
    col = lax.broadcasted_iota(jnp.int32, (ML_CHUNK, ML_CHUNK), 1)
    lower = (col <= row)
    upper = (col >= row)
    lower_f = lower.astype(F32)
    upper_f = upper.astype(F32)

    def scan_group(c, group):
        ch = []
        for b, d in group:
            cc = c if d == 0 else n_chunks - 1 - c
            t0 = pl.multiple_of(b * n_tok + cc * ML_CHUNK, ML_CHUNK)
            rows = pl.ds(t0, ML_CHUNK)
            g_col = tail_ref[rows, :] + brow_ref[...]
            g_row = gt_ref[b * n_chunks + cc] + bcol_ref[...]
            lf_col = _log_sigmoid(g_col)
            lf_row = _log_sigmoid(g_row)
            if d == 0:
                bcum_col = _dot_hi(lower_f, lf_col)
                bcum_row = _dot_hi(lf_row, upper_f)
                mask = lower
            else:
                bcum_col = _dot_hi(upper_f, lf_col)
                bcum_row = _dot_hi(lf_row, lower_f)
                mask = upper
            for h in range(ML_HEADS):
                r = d * ML_HEADS + h
                fr = 2 * ML_HEADS + r
                ch.append(dict(
                    b=b, d=d, r=r, rows=rows, lanes=slice(h * ML_DH, (h + 1) * ML_DH), mask=mask,
                    ig_c=g_col[:, TAIL_G + r:TAIL_G + r + 1],
                    b_c=bcum_col[:, TAIL_G + fr:TAIL_G + fr + 1],
                    ig_r=g_row[r:r + 1, :],
                    b_r=bcum_row[fr:fr + 1, :],
                    tot=jnp.sum(lf_row[fr:fr + 1, :], axis=1, keepdims=True)))
        n = range(len(ch))
        m_prev = [mst_ref[x["b"], x["r"]:x["r"] + 1, 0:1] for x in ch]
        n_prev = [nst_ref[x["b"], x["r"]:x["r"] + 1, :] for x in ch]
        c_prev = [cst_ref[x["b"], x["r"]] for x in ch]
        q = [q_ref[x["rows"], x["lanes"]] for x in ch]
        k = [k_ref[x["rows"], x["lanes"]] for x in ch]
        v = [v_ref[x["rows"], x["lanes"]].astype(BF16) for x in ch]
        qb = [x.astype(BF16) for x in q]

        dmat = [jnp.where(ch[i]["mask"], ch[i]["b_c"] - ch[i]["b_r"] + ch[i]["ig_r"], NEG) for i in n]
        inter = [ch[i]["b_c"] + m_prev[i] for i in n]
        mt = [jnp.maximum(inter[i], jnp.max(dmat[i], axis=1, keepdims=True)) for i in n]
        w = [jnp.exp(dmat[i] - mt[i]) for i in n]
        qk = [_dot_nt(qb[i], k[i].astype(BF16)) for i in n]
        qc = [_dot(qb[i], c_prev[i].astype(BF16)) for i in n]
        s = [qk[i] * scale * w[i] for i in n]
        a = [jnp.exp(inter[i] - mt[i]) for i in n]
        sv = [_dot(s[i].astype(BF16), v[i]) for i in n]
        den = [jnp.sum(s[i], axis=1, keepdims=True) + a[i] * jnp.sum(q[i] * n_prev[i], axis=1, keepdims=True)
               for i in n]
        hc = [(sv[i] + a[i] * qc[i]) / jnp.maximum(jnp.abs(den[i]), jnp.exp(-mt[i])) for i in n]
        for i in n:
            (hf_ref if ch[i]["d"] == 0 else hb_ref)[ch[i]["rows"], ch[i]["lanes"]] = hc[i]

        wlog_c = [ch[i]["tot"] - ch[i]["b_c"] + ch[i]["ig_c"] for i in n]
        wlog_r = [ch[i]["tot"] - ch[i]["b_r"] + ch[i]["ig_r"] for i in n]
        m_new = [jnp.maximum(ch[i]["tot"] + m_prev[i], jnp.max(wlog_r[i], axis=1, keepdims=True)) for i in n]
        decay = [jnp.exp(ch[i]["tot"] + m_prev[i] - m_new[i]) for i in n]
        kw = [k[i] * (scale * jnp.exp(wlog_c[i] - m_new[i])) for i in n]
        kv = [_dot_tn(kw[i].astype(BF16), v[i]) for i in n]
        for i in n:
            b, r = ch[i]["b"], ch[i]["r"]
            cst_ref[b, r] = decay[i] * c_prev[i] + kv[i]
            nst_ref[b, r:r + 1, :] = decay[i] * n_prev[i] + jnp.sum(kw[i], axis=0, keepdims=True)
            mst_ref[b, r:r + 1, :] = jnp.broadcast_to(m_new[i], (1, ML_DH))

    def chunk_step(c, carry):
        for b in range(n_seq):
            scan_group(c, [(b, 0)])
            scan_group(c, [(b, 1)])
        return carry

    lax.fori_loop(0, n_chunks, chunk_step, 0)

    for h in range(ML_HEADS):
        lanes = slice(h * ML_DH, (h + 1) * ML_DH)
        hs = hf_ref[:, lanes] + hb_ref[:, lanes]
        out_ref[:, lanes] = _rms(hs, ng_ref[:, lanes]) * jax.nn.sigmoid(o_ref[:, lanes])

    if not has_state:
        cs_ref[...] = cst_ref[...]
        ns_ref[...] = nst_ref[...]
        ms_ref[...] = mst_ref[...]


ML_SEQ_PER_STEP = 4


def _mlstm(proj, gt, bcol, brow, norm_g, l, *, latent, states=None):
    n_tok = DEC_SEQ if latent else SEQ
    n_seq = 1 if latent else ML_SEQ_PER_STEP
    n_b = DEC_BATCH if latent else BATCH
    rows = n_seq * n_tok
    blk0 = N_CTX // rows if latent else 0
    n_chunks = n_tok // ML_CHUNK

    def col_spec(c0):
        return pl.BlockSpec((rows, 512), lambda b: (blk0 + b, c0 // 512))

    in_specs = [
        col_spec(C_MLQ), col_spec(C_MLK), col_spec(C_MLV), col_spec(C_MLO),
        pl.BlockSpec((rows, 128), lambda b: (blk0 + b, C_TAIL // 128)),
        pl.BlockSpec((n_seq * n_chunks, 16, ML_CHUNK), lambda b: (blk0 + b, 0, 0)),
        pl.BlockSpec((16, 1), lambda b: (0, 0)),
        pl.BlockSpec((1, 128), lambda b: (0, 0)),
        pl.BlockSpec((None, 1, GROUP_W), lambda b: (l, 0, 0)),
    ]
    args = [proj, proj, proj, proj, proj, gt, bcol, brow, norm_g]
    n_chain = 2 * ML_HEADS
    scratch = [pltpu.VMEM((rows, GROUP_W), F32), pltpu.VMEM((rows, GROUP_W), F32),
               pltpu.VMEM((n_seq, n_chain, ML_DH, ML_DH), F32), pltpu.VMEM((n_seq, n_chain, ML_DH), F32),
               pltpu.VMEM((n_seq, n_chain, ML_DH), F32)]
    if latent:
        c0, n0, m0 = states
        in_specs += [
            pl.BlockSpec((n_seq, None, n_chain, ML_DH, ML_DH), lambda b: (b, l, 0, 0, 0)),
            pl.BlockSpec((n_seq, None, n_chain, ML_DH), lambda b: (b, l, 0, 0)),
            pl.BlockSpec((n_seq, None, n_chain, ML_DH), lambda b: (b, l, 0, 0)),
        ]
        args += [c0, n0, m0]
        out_shape = jax.ShapeDtypeStruct((N_LAT, GROUP_W), F32)
        out_specs = pl.BlockSpec((rows, GROUP_W), lambda b: (b, 0))
    else:
        out_shape = (jax.ShapeDtypeStruct((N_CTX, GROUP_W), F32),
                     jax.ShapeDtypeStruct((BATCH, n_chain, ML_DH, ML_DH), F32),
                     jax.ShapeDtypeStruct((BATCH, n_chain, ML_DH), F32),
                     jax.ShapeDtypeStruct((BATCH, n_chain, ML_DH), F32))
        out_specs = (pl.BlockSpec((rows, GROUP_W), lambda b: (b, 0)),
                     pl.BlockSpec((n_seq, n_chain, ML_DH, ML_DH), lambda b: (b, 0, 0, 0)),
                     pl.BlockSpec((n_seq, n_chain, ML_DH), lambda b: (b, 0, 0)),
                     pl.BlockSpec((n_seq, n_chain, ML_DH), lambda b: (b, 0, 0)))
    return pl.pallas_call(
        functools.partial(_mlstm_kernel, n_tok=n_tok, n_seq=n_seq, has_state=latent),
        grid=(n_b // n_seq,),
        in_specs=in_specs,
        out_specs=out_specs,
        out_shape=out_shape,
        scratch_shapes=scratch,
        compiler_params=_cparams("parallel"),
        name="mlstm_latent" if latent else "mlstm_context",
    )(*args)


def _softmax_parts(scores, sink=None):
    m = jnp.max(scores[0], axis=1, keepdims=True)
    for s in scores[1:]:
        m = jnp.maximum(m, jnp.max(s, axis=1, keepdims=True))
    if sink is not None:
        m = jnp.maximum(m, sink)
    es = [jnp.exp(s - m) for s in scores]
    den = jnp.sum(es[0], axis=1, keepdims=True)
    for e in es[1:]:
        den = den + jnp.sum(e, axis=1, keepdims=True)
    if sink is not None:
        den = den + jnp.exp(sink - m)
    return es, den


def _softmax_parts_all(score_lists, sinks=None):
    n = range(len(score_lists))
    sinks = [None] * len(score_lists) if sinks is None else sinks
    ms = []
    for i in n:
        m = jnp.max(score_lists[i][0], axis=1, keepdims=True)
        for s in score_lists[i][1:]:
            m = jnp.maximum(m, jnp.max(s, axis=1, keepdims=True))
        ms.append(m if sinks[i] is None else jnp.maximum(m, sinks[i]))
    es = [[jnp.exp(s - ms[i]) for s in score_lists[i]] for i in n]
    dens = []
    for i in n:
        den = jnp.sum(es[i][0], axis=1, keepdims=True)
        for e in es[i][1:]:
            den = den + jnp.sum(e, axis=1, keepdims=True)
        dens.append(den if sinks[i] is None else den + jnp.exp(sinks[i] - ms[i]))
    return es, dens


ATT_TQ = 256
LAT_QB = DEC_SEQ // ATT_TQ
N_KEYS_LAT = DEC_SEQ + PAST_LEN


def _mla_q(cq_ref, qg_ref, wuq_ref):
    return _dot(_rms(cq_ref[...], qg_ref[...]).astype(BF16), wuq_ref[...])


def _mla_heads(q, kv, kr, n_keys):
    hs = range(MLA_HEADS)
    r0 = MLA_HEADS * MLA_D_NOPE
    qn = [q[:, h * MLA_D_NOPE:(h + 1) * MLA_D_NOPE].astype(BF16) for h in hs]
    qr = [q[:, r0 + h * MLA_D_ROPE:r0 + (h + 1) * MLA_D_ROPE].astype(BF16) for h in hs]
    kn = [kv[:, h * 256:h * 256 + MLA_D_NOPE] for h in hs]
    v = [kv[:, h * 256 + MLA_D_NOPE:(h + 1) * 256] for h in hs]
    s = [(_dot_nt(qn[h], kn[h]) + _dot_nt(qr[h], kr)) * MLA_SCALE for h in hs]
    es, dens = _softmax_parts_all([[s[h]] for h in hs])
    pv = [_dot(es[h][0].astype(BF16), v[h]) for h in hs]
    return [pv[h] / dens[h] for h in hs]


def _mla_ctx_kernel(cq_ref, ckv_ref, tail_ref, qg_ref, kg_ref, wuq_ref, wukv_ref, out_ref, ckvn_ref):
    q = _mla_q(cq_ref, qg_ref, wuq_ref)
    ckvn = _rms(ckv_ref[...], kg_ref[...])
    ckvn_ref[...] = ckvn
    kv = _dot(ckvn.astype(BF16), wukv_ref[...]).astype(BF16)
    kr = tail_ref[:, 0:MLA_D_ROPE].astype(BF16)
    outs = _mla_heads(q, kv, kr, SEQ)
    for h in range(MLA_HEADS):
        out_ref[:, h * MLA_D_V:(h + 1) * MLA_D_V] = outs[h]


MLA_D_V = GROUP_W // MLA_HEADS


def _mla_lat_kernel(cq_ref, ckv_ref, tail_ref, qg_ref, kg_ref, wuq_ref, wukv_ref, ckvc_ref, krc_ref,
                    cosq_ref, sinq_ref, cosk_ref, sink_ref, out_ref, kv_ref, kr_ref):
    @pl.when(pl.program_id(1) == 0)
    def _():
        ckvn = _rms(ckv_ref[...], kg_ref[...])
        kv_ref[0:DEC_SEQ, :] = _dot(ckvn.astype(BF16), wukv_ref[...]).astype(BF16)
        kv_ref[DEC_SEQ:N_KEYS_LAT, :] = _dot(ckvc_ref[...].astype(BF16), wukv_ref[...]).astype(BF16)
        kr = _rope(tail_ref[...], cosk_ref[...], sink_ref[...], MLA_D_ROPE // 4)
        kr_ref[0:DEC_SEQ, :] = kr[:, 0:MLA_D_ROPE].astype(BF16)
        kr_ref[DEC_SEQ:N_KEYS_LAT, :] = krc_ref[...].astype(BF16)

    q = _mla_q(cq_ref, qg_ref, wuq_ref)
    n0 = MLA_HEADS * MLA_D_NOPE
    q_rope = _rope(q[:, n0:], cosq_ref[...], sinq_ref[...], MLA_D_ROPE // 4)
    q = jnp.concatenate([q[:, :n0], q_rope], axis=1)
    outs = _mla_heads(q, kv_ref[...], kr_ref[...], N_KEYS_LAT)
    for h in range(MLA_HEADS):
        out_ref[:, h * MLA_D_V:(h + 1) * MLA_D_V] = outs[h]


def _w_specs2(shape_a, shape_b, l):
    return [pl.BlockSpec((None,) + shape_a, lambda *_: (l,) + (0,) * len(shape_a)),
            pl.BlockSpec((None,) + shape_b, lambda *_: (l,) + (0,) * len(shape_b))]


def _mla_ctx(proj, qg, kg, wuq, wukv, l):
    return pl.pallas_call(
        _mla_ctx_kernel,
        grid=(BATCH,),
        in_specs=[
            pl.BlockSpec((SEQ, 256), lambda b: (b, C_CQ // 256)),
            pl.BlockSpec((SEQ, 256), lambda b: (b, C_CKV // 256)),
            pl.BlockSpec((SEQ, 128), lambda b: (b, C_TAIL // 128)),
            pl.BlockSpec((None, 1, MLA_Q_RANK), lambda b: (l, 0, 0)),
            pl.BlockSpec((None, 1, MLA_KV_RANK), lambda b: (l, 0, 0)),
            pl.BlockSpec((None, MLA_Q_RANK, 768), lambda b: (l, 0, 0)),
            pl.BlockSpec((None, MLA_KV_RANK, 1024), lambda b: (l, 0, 0)),
        ],
        out_specs=(pl.BlockSpec((SEQ, GROUP_W), lambda b: (b, 0)),
                   pl.BlockSpec((SEQ, MLA_KV_RANK), lambda b: (b, 0))),
        out_shape=(jax.ShapeDtypeStruct((N_CTX, GROUP_W), F32),
                   jax.ShapeDtypeStruct((N_CTX, MLA_KV_RANK), F32)),
        compiler_params=_cparams("parallel"),
        name="mla_context",
    )(proj, proj, proj, qg, kg, wuq, wukv)


def _mla_lat(proj, qg, kg, wuq, wukv, cache_ckv, cache_kr, cos64, sin64, l):
    qb0 = N_CTX // ATT_TQ
    bb0 = N_CTX // DEC_SEQ
    cosq = jnp.tile(cos64, (1, MLA_HEADS))
    sinq = jnp.tile(sin64, (1, MLA_HEADS))
    cosk = jnp.tile(cos64, (1, 2))
    sink = jnp.tile(sin64, (1, 2))
    return pl.pallas_call(
        _mla_lat_kernel,
        grid=(DEC_BATCH, LAT_QB),
        in_specs=[
            pl.BlockSpec((ATT_TQ, 256), lambda b, i: (qb0 + b * LAT_QB + i, C_CQ // 256)),
            pl.BlockSpec((DEC_SEQ, 256), lambda b, i: (bb0 + b, C_CKV // 256)),
            pl.BlockSpec((DEC_SEQ, 128), lambda b, i: (bb0 + b, C_TAIL // 128)),
            pl.BlockSpec((None, 1, MLA_Q_RANK), lambda b, i: (l, 0, 0)),
            pl.BlockSpec((None, 1, MLA_KV_RANK), lambda b, i: (l, 0, 0)),
            pl.BlockSpec((None, MLA_Q_RANK, 768), lambda b, i: (l, 0, 0)),
            pl.BlockSpec((None, MLA_KV_RANK, 1024), lambda b, i: (l, 0, 0)),
            pl.BlockSpec((None, None, PAST_LEN, MLA_KV_RANK), lambda b, i: (b, l, 0, 0)),
            pl.BlockSpec((None, None, PAST_LEN, MLA_D_ROPE), lambda b, i: (b, l, 0, 0)),
            pl.BlockSpec((ATT_TQ, 256), lambda b, i: (i, 0)),
            pl.BlockSpec((ATT_TQ, 256), lambda b, i: (i, 0)),
            pl.BlockSpec((DEC_SEQ, 128), lambda b, i: (0, 0)),
            pl.BlockSpec((DEC_SEQ, 128), lambda b, i: (0, 0)),
        ],
        out_specs=pl.BlockSpec((ATT_TQ, GROUP_W), lambda b, i: (b * LAT_QB + i, 0)),
        out_shape=jax.ShapeDtypeStruct((N_LAT, GROUP_W), F32),
        scratch_shapes=[pltpu.VMEM((N_KEYS_LAT, 1024), BF16), pltpu.VMEM((N_KEYS_LAT, MLA_D_ROPE), BF16)],
        compiler_params=_cparams("parallel", "arbitrary"),
        name="mla_latent",
    )(proj, proj, proj, qg, kg, wuq, wukv, cache_ckv, cache_kr, cosq, sinq, cosk, sink)


SWA_SCALE = SWA_DH ** -0.5
SWA_REP = SWA_HEADS // SWA_KV_HEADS
SWA_KWIN = ATT_TQ + 2 * WINDOW


def _swa_ctx_kernel(sink_ref, q_ref, k_ref, v_ref, out_ref):
    kb = k_ref[...].astype(BF16)
    vb = v_ref[...].astype(BF16)
    hs = range(SWA_HEADS)
    gl = [slice((h // SWA_REP) * SWA_DH, (h // SWA_REP + 1) * SWA_DH) for h in hs]
    q = [q_ref[:, h * SWA_DH:(h + 1) * SWA_DH].astype(BF16) for h in hs]
    s = [_dot_nt(q[h], kb[:, gl[h]]) * SWA_SCALE for h in hs]
    es, dens = _softmax_parts_all([[s[h]] for h in hs], sinks=[sink_ref[0, h] for h in hs])
    pv = [_dot(es[h][0].astype(BF16), vb[:, gl[h]]) for h in hs]
    for h in hs:
        out_ref[:, h * SWA_DH:(h + 1) * SWA_DH] = pv[h] / dens[h]


def _swa_lat_kernel(sink_ref, q_ref, k_ref, v_ref, kc_ref, vc_ref, cosq_ref, sinq_ref, cosk_ref, sink_t_ref,
                    out_ref, kr_ref):
    i = pl.program_id(1)

    @pl.when(i == 0)
    def _():
        kr_ref[...] = _rope(k_ref[...], cosk_ref[...], sink_t_ref[...], SWA_DH // 4).astype(BF16)

    q_all = _rope(q_ref[...], cosq_ref[...], sinq_ref[...], SWA_DH // 4)
    k0 = pl.multiple_of(jnp.clip(i * ATT_TQ - WINDOW, 0, DEC_SEQ - SWA_KWIN), WINDOW)
    kwin = kr_ref[pl.ds(k0, SWA_KWIN), :]
    vwin = v_ref[pl.ds(k0, SWA_KWIN), :].astype(BF16)
    kc = kc_ref[...].astype(BF16)
    vc = vc_ref[...].astype(BF16)
    qpos = i * ATT_TQ + lax.broadcasted_iota(jnp.int32, (ATT_TQ, SWA_KWIN), 0)
    kpos = k0 + lax.broadcasted_iota(jnp.int32, (ATT_TQ, SWA_KWIN), 1)
    band = jnp.abs(qpos - kpos) <= WINDOW
    hs = range(SWA_HEADS)
    gl = [slice((h // SWA_REP) * SWA_DH, (h // SWA_REP + 1) * SWA_DH) for h in hs]
    q = [q_all[:, h * SWA_DH:(h + 1) * SWA_DH].astype(BF16) for h in hs]
    s_loc = [jnp.where(band, _dot_nt(q[h], kwin[:, gl[h]]) * SWA_SCALE, NEG) for h in hs]
    s_ctx = [_dot_nt(q[h], kc[:, gl[h]]) * SWA_SCALE for h in hs]
    es, dens = _softmax_parts_all([[s_loc[h], s_ctx[h]] for h in hs], sinks=[sink_ref[0, h] for h in hs])
    o = [_dot(es[h][0].astype(BF16), vwin[:, gl[h]]) + _dot(es[h][1].astype(BF16), vc[:, gl[h]]) for h in hs]
    for h in hs:
        out_ref[:, h * SWA_DH:(h + 1) * SWA_DH] = o[h] / dens[h]


def _smem_spec():
    return pl.BlockSpec(memory_space=pltpu.SMEM)


def _swa_ctx(proj, sink, l):
    return pl.pallas_call(
        _swa_ctx_kernel,
        grid=(BATCH,),
        in_specs=[
            _smem_spec(),
            pl.BlockSpec((SEQ, 512), lambda b: (b, C_SQ // 512)),
            pl.BlockSpec((SEQ, 256), lambda b: (b, C_SK // 256)),
            pl.BlockSpec((SEQ, 256), lambda b: (b, C_SV // 256)),
        ],
        out_specs=pl.BlockSpec((SEQ, GROUP_W), lambda b: (b, 0)),
        out_shape=jax.ShapeDtypeStruct((N_CTX, GROUP_W), F32),
        compiler_params=_cparams("parallel"),
        name="swa_context",
    )(sink, proj, proj, proj)


def _swa_lat(proj, sink, cache_k, cache_v, cos128, sin128, l):
    qb0 = N_CTX // ATT_TQ
    bb0 = N_CTX // DEC_SEQ
    kvw = SWA_KV_HEADS * SWA_DH
    return pl.pallas_call(
        _swa_lat_kernel,
        grid=(DEC_BATCH, LAT_QB),
        in_specs=[
            _smem_spec(),
            pl.BlockSpec((ATT_TQ, 512), lambda b, i: (qb0 + b * LAT_QB + i, C_SQ // 512)),
            pl.BlockSpec((DEC_SEQ, 256), lambda b, i: (bb0 + b, C_SK // 256)),
            pl.BlockSpec((DEC_SEQ, 256), lambda b, i: (bb0 + b, C_SV // 256)),
            pl.BlockSpec((None, None, PAST_LEN, kvw), lambda b, i: (b, l, 0, 0)),
            pl.BlockSpec((None, None, PAST_LEN, kvw), lambda b, i: (b, l, 0, 0)),
            pl.BlockSpec((ATT_TQ, 512), lambda b, i: (i, 0)),
            pl.BlockSpec((ATT_TQ, 512), lambda b, i: (i, 0)),
            pl.BlockSpec((DEC_SEQ, 256), lambda b, i: (0, 0)),
            pl.BlockSpec((DEC_SEQ, 256), lambda b, i: (0, 0)),
        ],
        out_specs=pl.BlockSpec((ATT_TQ, GROUP_W), lambda b, i: (b * LAT_QB + i, 0)),
        out_shape=jax.ShapeDtypeStruct((N_LAT, GROUP_W), F32),
        scratch_shapes=[pltpu.VMEM((DEC_SEQ, kvw), BF16)],
        compiler_params=_cparams("parallel", "arbitrary"),
        name="swa_latent",
    )(sink, proj, proj, proj, cache_k, cache_v,
      jnp.tile(cos128, (1, SWA_HEADS)), jnp.tile(sin128, (1, SWA_HEADS)),
      jnp.tile(cos128, (1, SWA_KV_HEADS)), jnp.tile(sin128, (1, SWA_KV_HEADS)))


DIFF_SCALE = DIFF_DH ** -0.5


def _diff_lambda(lq1_ref, lk1_ref, lq2_ref, lk2_ref):
    a = jnp.sum(lq1_ref[...] * lk1_ref[...], axis=1, keepdims=True)
    b = jnp.sum(lq2_ref[...] * lk2_ref[...], axis=1, keepdims=True)
    return jnp.exp(a) - jnp.exp(b)


def _diff_heads(q, k_parts, v_parts, lam, lam_init, ng_ref, out_ref, heads_per_group):
    for h0 in range(0, DIFF_HEADS, heads_per_group):
        hs = range(h0, h0 + heads_per_group)
        sls = [slice(h * 2 * DIFF_DH + c * DIFF_DH, h * 2 * DIFF_DH + (c + 1) * DIFF_DH)
               for h in hs for c in range(2)]
        qc = [q[:, sl].astype(BF16) for sl in sls]
        scores = [[_dot_nt(qc[j], kp[:, sls[j]]) * DIFF_SCALE for kp in k_parts] for j in range(len(sls))]
        es, dens = _softmax_parts_all(scores)
        ps = [[e / dens[j] for e in es[j]] for j in range(len(sls))]
        outs = []
        for i, h in enumerate(hs):
            vl = slice(h * 2 * DIFF_DH, (h + 1) * 2 * DIFF_DH)
            o = None
            for p1, p2, vp in zip(ps[2 * i], ps[2 * i + 1], v_parts):
                t = _dot((p1 - lam * p2).astype(BF16), vp[:, vl])
                o = t if o is None else o + t
            outs.append((vl, o))
        for vl, o in outs:
            out_ref[:, vl] = _rms(o, ng_ref[...]) * (1.0 - lam_init)


def _diff_ctx_kernel(q_ref, k_ref, v_ref, lq1_ref, lk1_ref, lq2_ref, lk2_ref, ng_ref, out_ref, *, lam_init):
    lam = _diff_lambda(lq1_ref, lk1_ref, lq2_ref, lk2_ref) + lam_init
    _diff_heads(q_ref[...], [k_ref[...].astype(BF16)], [v_ref[...].astype(BF16)], lam, lam_init, ng_ref, out_ref,
                heads_per_group=DIFF_HEADS)


def _diff_lat_kernel(q_ref, k_ref, v_ref, kc_ref, vc_ref, lq1_ref, lk1_ref, lq2_ref, lk2_ref, ng_ref,
                     cosq_ref, sinq_ref, cosk_ref, sink_ref, out_ref, kr_ref, *, lam_init):
    @pl.when(pl.program_id(1) == 0)
    def _():
        kr_ref[...] = _rope(k_ref[...], cosk_ref[...], sink_ref[...], DIFF_DH // 4).astype(BF16)

    lam = _diff_lambda(lq1_ref, lk1_ref, lq2_ref, lk2_ref) + lam_init
    q = _rope(q_ref[...], cosq_ref[...], sinq_ref[...], DIFF_DH // 4)
    _diff_heads(q, [kr_ref[...], kc_ref[...].astype(BF16)], [v_ref[...].astype(BF16), vc_ref[...].astype(BF16)],
                lam, lam_init, ng_ref, out_ref, heads_per_group=2)


def _vec_specs(n, width, l, nargs):
    return [pl.BlockSpec((None, 1, width), lambda *_: (l, 0, 0)) for _ in range(n)]


def _diff_ctx(proj, lq1, lk1, lq2, lk2, ng, l):
    lam_init = 0.8 - 0.6 * math.exp(-0.3 * l)
    return pl.pallas_call(
        functools.partial(_diff_ctx_kernel, lam_init=lam_init),
        grid=(BATCH,),
        in_specs=[
            pl.BlockSpec((SEQ, 512), lambda b: (b, C_DQ // 512)),
            pl.BlockSpec((SEQ, 512), lambda b: (b, C_DK // 512)),
            pl.BlockSpec((SEQ, 512), lambda b: (b, C_DV // 512)),
        ] + _vec_specs(4, DIFF_DH, l, 1) + _vec_specs(1, 2 * DIFF_DH, l, 1),
        out_specs=pl.BlockSpec((SEQ, GROUP_W), lambda b: (b, 0)),
        out_shape=jax.ShapeDtypeStruct((N_CTX, GROUP_W), F32),
        compiler_params=_cparams("parallel"),
        name="diff_context",
    )(proj, proj, proj, lq1, lk1, lq2, lk2, ng)


def _diff_lat(proj, lq1, lk1, lq2, lk2, ng, cache_k, cache_v, cos64, sin64, l):
    lam_init = 0.8 - 0.6 * math.exp(-0.3 * l)
    qb0 = N_CTX // ATT_TQ
    bb0 = N_CTX // DEC_SEQ
    cos_t = jnp.tile(cos64, (1, 2 * DIFF_HEADS))
    sin_t = jnp.tile(sin64, (1, 2 * DIFF_HEADS))
    return pl.pallas_call(
        functools.partial(_diff_lat_kernel, lam_init=lam_init),
        grid=(DEC_BATCH, LAT_QB),
        in_specs=[
            pl.BlockSpec((ATT_TQ, 512), lambda b, i: (qb0 + b * LAT_QB + i, C_DQ // 512)),
            pl.BlockSpec((DEC_SEQ, 512), lambda b, i: (bb0 + b, C_DK // 512)),
            pl.BlockSpec((DEC_SEQ, 512), lambda b, i: (bb0 + b, C_DV // 512)),
            pl.BlockSpec((None, None, PAST_LEN, GROUP_W), lambda b, i: (b, l, 0, 0)),
            pl.BlockSpec((None, None, PAST_LEN, GROUP_W), lambda b, i: (b, l, 0, 0)),
        ] + _vec_specs(4, DIFF_DH, l, 2) + _vec_specs(1, 2 * DIFF_DH, l, 2) + [
            pl.BlockSpec((ATT_TQ, 512), lambda b, i: (i, 0)),
            pl.BlockSpec((ATT_TQ, 512), lambda b, i: (i, 0)),
            pl.BlockSpec((DEC_SEQ, 512), lambda b, i: (0, 0)),
            pl.BlockSpec((DEC_SEQ, 512), lambda b, i: (0, 0)),
        ],
        out_specs=pl.BlockSpec((ATT_TQ, GROUP_W), lambda b, i: (b * LAT_QB + i, 0)),
        out_shape=jax.ShapeDtypeStruct((N_LAT, GROUP_W), F32),
        scratch_shapes=[pltpu.VMEM((DEC_SEQ, GROUP_W), BF16)],
        compiler_params=_cparams("parallel", "arbitrary"),
        name="diff_latent",
    )(proj, proj, proj, cache_k, cache_v, lq1, lk1, lq2, lk2, ng, cos_t, sin_t, cos_t, sin_t)


OUT_TM = 512


OUT_CTX_BLOCKS = N_CTX // OUT_TM


def _out_kernel(*refs):
    ctx_refs, lat_refs = refs[0:4], refs[4:8]
    x_ref, mod_ref, w_ref, x1_ref = refs[8:]
    is_ctx = pl.program_id(0) < OUT_CTX_BLOCKS
    acc = None
    for g in range(4):
        m = jnp.where(is_ctx, ctx_refs[g][...], lat_refs[g][...])
        t = _dot(m.astype(BF16), w_ref[g * GROUP_W:(g + 1) * GROUP_W, :])
        acc = t if acc is None else acc + t
    x1_ref[...] = x_ref[...] + mod_ref[2:3, :] * acc


def _out_proj(mixed_ctx, mixed_lat, x, mod4, w_out, l):
    row = lambda i: (i, 0)
    ctx_row = lambda i: (jnp.minimum(i, OUT_CTX_BLOCKS - 1), 0)
    lat_row = lambda i: (jnp.maximum(i - OUT_CTX_BLOCKS, 0), 0)
    return pl.pallas_call(
        _out_kernel,
        grid=(NT // OUT_TM,),
        in_specs=[pl.BlockSpec((OUT_TM, GROUP_W), ctx_row) for _ in range(4)]
        + [pl.BlockSpec((OUT_TM, GROUP_W), lat_row) for _ in range(4)] + [
            pl.BlockSpec((OUT_TM, D_MODEL), row),
            pl.BlockSpec((None, None, 6, D_MODEL), lambda i: (l, _mod_set(i * OUT_TM), 0, 0)),
            pl.BlockSpec((None, D_MODEL, D_MODEL), lambda i: (l, 0, 0)),
        ],
        out_specs=pl.BlockSpec((OUT_TM, D_MODEL), row),
        out_shape=jax.ShapeDtypeStruct((NT, D_MODEL), F32),
        compiler_params=_cparams("parallel"),
        name="out_proj_residual",
    )(*mixed_ctx, *mixed_lat, x, mod4, w_out)


def _peerq_kernel(x_ref, mod_ref, g_ref, w_ref, ht_ref, q_ref):
    h = _rms(x_ref[...], g_ref[...]) * (1.0 + mod_ref[4:5, :]) + mod_ref[3:4, :]
    ht_ref[...] = h.T.astype(BF16)
    q_ref[...] = _dot(h.astype(BF16), w_ref[...])


def _peer_query(x1, mod4, norm_g, w_q, l):
    row = lambda i: (i, 0)
    return pl.pallas_call(
        _peerq_kernel,
        grid=(NT // OUT_TM,),
        in_specs=[
            pl.BlockSpec((OUT_TM, D_MODEL), row),
            pl.BlockSpec((None, None, 6, D_MODEL), lambda i: (l, _mod_set(i * OUT_TM), 0, 0)),
            pl.BlockSpec((None, 1, D_MODEL), lambda i: (l, 0, 0)),
            pl.BlockSpec((None, D_MODEL, PEER_HEADS * PEER_QDIM), lambda i: (l, 0, 0)),
        ],
        out_specs=(pl.BlockSpec((D_MODEL, OUT_TM), lambda i: (0, i)),
                   pl.BlockSpec((OUT_TM, PEER_HEADS * PEER_QDIM), row)),
        out_shape=(jax.ShapeDtypeStruct((D_MODEL, NT), BF16),
                   jax.ShapeDtypeStruct((NT, PEER_HEADS * PEER_QDIM), F32)),
        compiler_params=_cparams("parallel"),
        name="adaln_peer_query",
    )(x1, mod4, norm_g, w_q)


ROUTE_TL = 256
ROUTE_LANES = 128
NOT_SEL = float(PEER_TOPK)


def _top16(s, index_ties):
    idx = lax.broadcasted_iota(jnp.int32, s.shape, 0).astype(F32)
    slot = lax.broadcasted_iota(jnp.int32, (PEER_TOPK, s.shape[1]), 0)
    rank = jnp.full(s.shape, NOT_SEL, F32)
    vals = jnp.zeros((PEER_TOPK, s.shape[1]), F32)
    for k in range(PEER_TOPK):
        m = jnp.max(s, axis=0, keepdims=True)
        sel = s == m
        if index_ties:
            sel = idx == jnp.min(jnp.where(sel, idx, float(PEER_NKEYS)), axis=0, keepdims=True)
        rank = jnp.where(sel, float(k), rank)
        s = jnp.where(sel, NEG, s)
        vals = jnp.where(slot == k, m, vals)
    return rank, vals


CAND_HALF = PEER_TOPK // 2
CAND_ROWS = PEER_TOPK + (CAND_HALF - 1) * CAND_HALF + CAND_HALF
FLAT_NONE = float(PEER_TOPK * PEER_TOPK)


def _cand_flat(tl):
    r = lax.broadcasted_iota(jnp.int32, (CAND_ROWS, tl), 0)
    mid = r - PEER_TOPK
    mid_flat = (1 + mid // CAND_HALF) * PEER_TOPK + mid % CAND_HALF
    last_flat = (CAND_HALF + r - (CAND_ROWS - CAND_HALF)) * PEER_TOPK
    flat = jnp.where(r < PEER_TOPK, r, jnp.where(r < CAND_ROWS - CAND_HALF, mid_flat, last_flat))
    return flat.astype(F32)


def _count(mask):
    return jnp.sum(jnp.where(mask, 1.0, 0.0), axis=0, keepdims=True)


def _route_head(s1, s2, flat, index_ties):
    tl = s1.shape[1]
    rank1, v1 = _top16(s1, index_ties)
    rank2, v2 = _top16(s2, index_ties)
    slabs = [v1[0:1, :] + v2]
    slabs += [v1[a:a + 1, :] + v2[0:CAND_HALF, :] for a in range(1, CAND_HALF)]
    slabs.append(v1[CAND_HALF:, :] + v2[0:1, :])
    cand = jnp.concatenate(slabs, axis=0)
    a_row = jnp.floor(flat * (1.0 / PEER_TOPK))
    top = v1[0:1, :] + v2[0:1, :]
    cnt1 = jnp.zeros(s1.shape, F32)
    z = jnp.zeros((1, tl), F32)
    for k in range(PEER_TOPK):
        m = jnp.max(cand, axis=0, keepdims=True)
        sel = cand == m
        if index_ties:
            first = jnp.min(jnp.where(sel, flat, FLAT_NONE), axis=0, keepdims=True)
            sel = flat == first
            a_sel = jnp.floor(first * (1.0 / PEER_TOPK))
        else:
            a_sel = jnp.max(jnp.where(sel, a_row, -1.0), axis=0, keepdims=True)
        cand = jnp.where(sel, NEG, cand)
        cnt1 = cnt1 + jnp.where(rank1 == a_sel, 1.0, 0.0)
        z = z + jnp.exp(m - top)
    e1 = jnp.exp(s1 - v1[0:1, :]) / z
    e2 = jnp.exp(s2 - v2[0:1, :])
    if index_ties:
        return rank2, cnt1, e1, e2, None
    full = float(PEER_TOPK)
    clean = ((_count(rank1 < NOT_SEL) == full) & (_count(rank2 < NOT_SEL) == full)
             & (_count(cand == NEG) == full))
    return rank2, cnt1, e1, e2, jnp.max(jnp.where(clean, 0.0, 1.0))


def _route_kernel(q_ref, k1_ref, k2_ref, rank2_ref, e2_ref, cnt1_ref, e1_ref):
    tl = q_ref.shape[0]
    half = PEER_QDIM // 2

    n_chunks = tl // ROUTE_LANES

    def scores(h):
        q1 = q_ref[:, h * PEER_QDIM:h * PEER_QDIM + half].astype(BF16)
        q2 = q_ref[:, h * PEER_QDIM + half:(h + 1) * PEER_QDIM].astype(BF16)
        return _dot_nt(k1_ref[...], q1), _dot_nt(k2_ref[...], q2)

    def route(h, c, s1, s2, index_ties):
        lanes = slice(c * ROUTE_LANES, (c + 1) * ROUTE_LANES)
        rank2, cnt1, e1, e2, tied = _route_head(s1[:, lanes], s2[:, lanes], _cand_flat(ROUTE_LANES), index_ties)
        rank2_ref[h, :, lanes] = rank2.astype(BF16)
        cnt1_ref[h, :, lanes] = cnt1
        e1_ref[h, :, lanes] = e1
        e2_ref[h, :, lanes] = e2.astype(BF16)
        return tied

    tied = {}
    for h in range(PEER_HEADS):
        s1, s2 = scores(h)
        for c in range(n_chunks):
            tied[h, c] = route(h, c, s1, s2, index_ties=False)

    for h in range(PEER_HEADS):
        for c in range(n_chunks):
            @pl.when(tied[h, c] > 0.0)
            def _():
                s1, s2 = scores(h)
                route(h, c, s1, s2, index_ties=True)


def _peer_route(q, keys, l):
    shp = jax.ShapeDtypeStruct((PEER_HEADS, PEER_NKEYS, NT), F32)
    shp_b = jax.ShapeDtypeStruct((PEER_HEADS, PEER_NKEYS, NT), BF16)
    spec = pl.BlockSpec((PEER_HEADS, PEER_NKEYS, ROUTE_TL), lambda i: (0, 0, i))
    half = PEER_QDIM // 2
    return pl.pallas_call(
        _route_kernel,
        grid=(NT // ROUTE_TL,),
        in_specs=[
            pl.BlockSpec((ROUTE_TL, PEER_HEADS * PEER_QDIM), lambda i: (i, 0)),
            pl.BlockSpec((None, None, PEER_NKEYS, half), lambda i: (l, 0, 0, 0)),
            pl.BlockSpec((None, None, PEER_NKEYS, half), lambda i: (l, 1, 0, 0)),
        ],
        out_specs=(spec, spec, spec, spec),
        out_shape=(shp_b, shp_b, shp, shp),
        compiler_params=_cparams("parallel"),
        name="peer_route",
    )(q, keys, keys)


PEER_TT = 512
PEER_EB = 1024
GELU_C = math.sqrt(2.0 / math.pi)


def _gelu_tanh(x):
    return 0.5 * x * (1.0 + jnp.tanh(GELU_C * (x + 0.044715 * (x * x * x))))


PEER_NE = PEER_N // PEER_EB
PEER_MT = 256


def _peer_kernel(ht_ref, u_ref, vt_ref, rank2_ref, e2_ref, cnt1_ref, e1_ref, x_ref, mod_ref, o_ref,
                 ga_ref, gb_ref, acc_ref):
    e = pl.program_id(1)

    @pl.when(e == 0)
    def _():
        acc_ref[...] = jnp.zeros_like(acc_ref)
        gb_ref[...] = jnp.zeros_like(gb_ref)

    def step(g_prev_ref, g_next_ref):
        blk = jnp.maximum(e - 1, 0)
        n_i = PEER_EB // PEER_NKEYS
        per_tile = PEER_MT // PEER_NKEYS

        def weighted(tok, kc):
            tiles = []
            for ii in range(kc * per_tile, (kc + 1) * per_tile):
                key1 = blk * n_i + ii
                w = jnp.zeros((PEER_NKEYS, PEER_MT), BF16)
                for h in range(PEER_HEADS):
                    cnt = cnt1_ref[h, pl.ds(key1, 1), tok].astype(BF16)
                    g1 = e1_ref[h, pl.ds(key1, 1), tok].astype(BF16)
                    w = w + jnp.where(rank2_ref[h, :, tok] < cnt, e2_ref[h, :, tok] * g1, 0.0)
                tiles.append(w * g_prev_ref[ii * PEER_NKEYS:(ii + 1) * PEER_NKEYS, tok])
            return jnp.concatenate(tiles, axis=0)

        n_kc = PEER_EB // PEER_MT
        for tc in range(PEER_TT // PEER_MT):
            tok = slice(tc * PEER_MT, (tc + 1) * PEER_MT)
            act = _dot(u_ref[...], ht_ref[:, tok])
            p = [weighted(tok, 0), weighted(tok, 1)]
            acc = None
            for kc in range(n_kc):
                if kc + 2 < n_kc:
                    p.append(weighted(tok, kc + 2))
                t = _dot(vt_ref[:, kc * PEER_MT:(kc + 1) * PEER_MT], p[kc])
                acc = t if acc is None else acc + t
                if kc == n_kc - 2:
                    g_next_ref[:, tok] = _gelu_tanh(act).astype(BF16)
            acc_ref[:, tok] += acc

    @pl.when(e % 2 == 0)
    def _():
        step(gb_ref, ga_ref)

    @pl.when(e % 2 == 1)
    def _():
        step(ga_ref, gb_ref)

    @pl.when(e == PEER_NE)
    def _():
        o_ref[...] = x_ref[...] + mod_ref[5:6, :] * acc_ref[...].T


def _peer(ht, u_tab, vt_tab, routing, x1, mod4, l):
    rank2, e2, cnt1, e1 = routing
    once = pl.Buffered(1)
    rspec = pl.BlockSpec((PEER_HEADS, PEER_NKEYS, PEER_TT), lambda i, e: (0, 0, i), pipeline_mode=once)
    return pl.pallas_call(
        _peer_kernel,
        grid=(NT // PEER_TT, PEER_NE + 1),
        in_specs=[
            pl.BlockSpec((D_MODEL, PEER_TT), lambda i, e: (0, i), pipeline_mode=once),
            pl.BlockSpec((None, PEER_EB, D_MODEL), lambda i, e: (l, jnp.minimum(e, PEER_NE - 1), 0)),
            pl.BlockSpec((None, D_MODEL, PEER_EB), lambda i, e: (l, 0, jnp.maximum(e - 1, 0))),
            rspec, rspec, rspec, rspec,
            pl.BlockSpec((PEER_TT, D_MODEL), lambda i, e: (i, 0), pipeline_mode=once),
            pl.BlockSpec((None, None, 6, D_MODEL), lambda i, e: (l, _mod_set(i * PEER_TT), 0, 0)),
        ],
        out_specs=pl.BlockSpec((PEER_TT, D_MODEL), lambda i, e: (i, 0)),
        out_shape=jax.ShapeDtypeStruct((NT, D_MODEL), F32),
        scratch_shapes=[pltpu.VMEM((PEER_EB, PEER_TT), BF16), pltpu.VMEM((PEER_EB, PEER_TT), BF16),
                        pltpu.VMEM((D_MODEL, PEER_TT), F32)],
        compiler_params=_cparams("parallel", "arbitrary"),
        name="peer_experts",
    )(ht, u_tab, vt_tab, rank2, e2, cnt1, e1, x1, mod4)


FIN_TM = 512


def _final_kernel(x_ref, g_ref, o_ref):
    o_ref[...] = _rms(x_ref[...], g_ref[...])


def _final_norm(x, g):
    return pl.pallas_call(
        _final_kernel,
        grid=(NT // FIN_TM,),
        in_specs=[pl.BlockSpec((FIN_TM, D_MODEL), lambda i: (i, 0)),
                  pl.BlockSpec((1, D_MODEL), lambda i: (0, 0))],
        out_specs=pl.BlockSpec((FIN_TM, D_MODEL), lambda i: (i, 0)),
        out_shape=jax.ShapeDtypeStruct((NT, D_MODEL), F32),
        compiler_params=_cparams("parallel"),
        name="final_norm",
    )(x, g)


def _permute_w_in(w_in):
    sizes = (GROUP_W, GROUP_W, GROUP_W, GROUP_W, 4 * ML_HEADS, MLA_Q_RANK, MLA_KV_RANK, MLA_D_ROPE,
             SWA_HEADS * SWA_DH, SWA_KV_HEADS * SWA_DH, SWA_KV_HEADS * SWA_DH, GROUP_W, GROUP_W, GROUP_W)
    offs = [0]
    for s in sizes:
        offs.append(offs[-1] + s)
    part = lambda i: w_in[:, :, offs[i]:offs[i + 1]]
    order = [0, 1, 2, 3, 5, 6, 8, 9, 10, 11, 12, 13, 7, 4]
    cols = [part(i) for i in order]
    used = sum(sizes)
    cols.append(jnp.zeros(w_in.shape[:2] + (PROJ_W - used,), w_in.dtype))
    return jnp.concatenate(cols, axis=-1).astype(BF16)


def _permute_w_uq(w_uq):
    w = w_uq.reshape(DEPTH, MLA_Q_RANK, MLA_HEADS, MLA_D_NOPE + MLA_D_ROPE)
    nope = w[..., :MLA_D_NOPE].reshape(DEPTH, MLA_Q_RANK, MLA_HEADS * MLA_D_NOPE)
    rope = w[..., MLA_D_NOPE:].reshape(DEPTH, MLA_Q_RANK, MLA_HEADS * MLA_D_ROPE)
    return jnp.concatenate([nope, rope], axis=-1).astype(BF16)


def kernel(x_prompt, x_sample, c, cache_mla_ckv, cache_mla_krope, cache_swa_k, cache_swa_v, cache_diff_k,
           cache_diff_v, state_mlstm_C, state_mlstm_n, state_mlstm_m, c_ctx, w_mod, b_mod, norm1_g, w_in,
           mlstm_i_bias, mlstm_f_bias, mlstm_norm_g, mla_qnorm_g, mla_w_uq, mla_kvnorm_g, mla_w_ukv, swa_sink,
           diff_lq1, diff_lk1, diff_lq2, diff_lk2, diff_norm_g, w_out, norm2_g, peer_w_q, peer_sub_keys,
           peer_u, peer_v, final_norm_g):
    x = jnp.concatenate([x_prompt.reshape(N_CTX, D_MODEL), x_sample.reshape(N_LAT, D_MODEL)], axis=0)

    cvec = jnp.concatenate([c_ctx[None, :], c], axis=0)
    cvec_t = jnp.pad(cvec.T, ((0, 0), (0, 8 - N_SETS)))
    mod4 = _modulation(cvec_t, w_mod, b_mod).reshape(DEPTH, 8, 6, D_MODEL)

    w_in_p = _permute_w_in(w_in)
    w_uq_p = _permute_w_uq(mla_w_uq)
    w_ukv_b = mla_w_ukv.astype(BF16)
    w_out_b = w_out.astype(BF16)
    w_q_b = peer_w_q.astype(BF16)
    keys_b = peer_sub_keys.astype(BF16)
    u_b = peer_u.astype(BF16)
    vt_b = jnp.swapaxes(peer_v, 1, 2).astype(BF16)

    vec3 = lambda a: a.reshape(DEPTH, 1, a.shape[-1])
    norm1_3, norm2_3 = vec3(norm1_g), vec3(norm2_g)
    mlng_3, qg_3, kg_3, dng_3 = vec3(mlstm_norm_g), vec3(mla_qnorm_g), vec3(mla_kvnorm_g), vec3(diff_norm_g)
    lq1_3, lk1_3, lq2_3, lk2_3 = vec3(diff_lq1), vec3(diff_lk1), vec3(diff_lq2), vec3(diff_lk2)

    n_chain = 2 * ML_HEADS
    st_c = state_mlstm_C.reshape(DEC_BATCH, DEPTH, n_chain, ML_DH, ML_DH)
    st_n = state_mlstm_n.reshape(DEC_BATCH, DEPTH, n_chain, ML_DH)
    st_m = jnp.broadcast_to(state_mlstm_m.reshape(DEC_BATCH, DEPTH, n_chain, 1), (DEC_BATCH, DEPTH, n_chain, ML_DH))
    swk_c = cache_swa_k.reshape(DEC_BATCH, DEPTH, PAST_LEN, SWA_KV_HEADS * SWA_DH)
    swv_c = cache_swa_v.reshape(DEC_BATCH, DEPTH, PAST_LEN, SWA_KV_HEADS * SWA_DH)
    dfk_c = cache_diff_k.reshape(DEC_BATCH, DEPTH, PAST_LEN, GROUP_W)
    dfv_c = cache_diff_v.reshape(DEC_BATCH, DEPTH, PAST_LEN, GROUP_W)

    cos64, sin64 = _rope_tables(DEC_SEQ, 64)
    cos128, sin128 = _rope_tables(DEC_SEQ, 128)

    outs = [[] for _ in range(9)]
    for l in range(DEPTH):
        proj = _project(x, mod4, norm1_3, w_in_p, l)

        gates = proj[:, C_TAIL + TAIL_G:C_TAIL + TAIL_G + 16]
        gt = gates.reshape(NT // ML_CHUNK, ML_CHUNK, 16).transpose(0, 2, 1)
        bias = jnp.concatenate([mlstm_i_bias[l].reshape(-1), mlstm_f_bias[l].reshape(-1)])
        bcol = bias.reshape(16, 1)
        brow = jnp.zeros((1, 128), F32).at[0, TAIL_G:TAIL_G + 16].set(bias)

        ml_ctx, c_st, n_st, m_st = _mlstm(proj, gt, bcol, brow, mlng_3, l, latent=False)
        ml_lat = _mlstm(proj, gt, bcol, brow, mlng_3, l, latent=True, states=(st_c, st_n, st_m))
        mla_ctx, ckv_n = _mla_ctx(proj, qg_3, kg_3, w_uq_p, w_ukv_b, l)
        mla_lat = _mla_lat(proj, qg_3, kg_3, w_uq_p, w_ukv_b, cache_mla_ckv, cache_mla_krope, cos64, sin64, l)
        sink = swa_sink[l].reshape(1, SWA_HEADS)
        swa_ctx = _swa_ctx(proj, sink, l)
        swa_lat = _swa_lat(proj, sink, swk_c, swv_c, cos128, sin128, l)
        df_ctx = _diff_ctx(proj, lq1_3, lk1_3, lq2_3, lk2_3, dng_3, l)
        df_lat = _diff_lat(proj, lq1_3, lk1_3, lq2_3, lk2_3, dng_3, dfk_c, dfv_c, cos64, sin64, l)

        x1 = _out_proj((ml_ctx, mla_ctx, swa_ctx, df_ctx), (ml_lat, mla_lat, swa_lat, df_lat), x, mod4, w_out_b, l)
        h2t, q = _peer_query(x1, mod4, norm2_3, w_q_b, l)
        routing = _peer_route(q, keys_b, l)
        x = _peer(h2t, u_b, vt_b, routing, x1, mod4, l)

        pc = proj[:N_CTX]
        outs[0].append(ckv_n.reshape(BATCH, SEQ, MLA_KV_RANK))
        outs[1].append(pc[:, C_TAIL:C_TAIL + MLA_D_ROPE].reshape(BATCH, SEQ, MLA_D_ROPE))
        outs[2].append(pc[:, C_SK:C_SK + 256].reshape(BATCH, SEQ, SWA_KV_HEADS, SWA_DH))
        outs[3].append(pc[:, C_SV:C_SV + 256].reshape(BATCH, SEQ, SWA_KV_HEADS, SWA_DH))
        outs[4].append(pc[:, C_DK:C_DK + 512].reshape(BATCH, SEQ, DIFF_HEADS, 2 * DIFF_DH))
        outs[5].append(pc[:, C_DV:C_DV + 512].reshape(BATCH, SEQ, DIFF_HEADS, 2 * DIFF_DH))
        outs[6].append(c_st.reshape(BATCH, 2, ML_HEADS, ML_DH, ML_DH))
        outs[7].append(n_st.reshape(BATCH, 2, ML_HEADS, ML_DH))
        outs[8].append(m_st[:, :, 0].reshape(BATCH, 2, ML_HEADS))

    y = _final_norm(x, final_norm_g.reshape(1, D_MODEL))
    y_prompt = y[:N_CTX].reshape(BATCH, SEQ, D_MODEL)
    y_sample = y[N_CTX:].reshape(DEC_BATCH, DEC_SEQ, D_MODEL)
    return (y_prompt, y_sample) + tuple(jnp.stack(o, axis=1) for o in outs)
```

```python
import functools
import math

import jax
import jax.numpy as jnp
from jax import lax
from jax.experimental import pallas as pl
from jax.experimental.pallas import tpu as pltpu

F32 = jnp.float32
BF16 = jnp.bfloat16

D_MODEL = 2048
BATCH = 32
SEQ = 256
DEPTH = 4
DEC_BATCH = 2
DEC_SEQ = 1024
PAST_LEN = 256
GRID_W = 64
GROUP_W = D_MODEL // 4
ML_HEADS = 4
ML_DH = GROUP_W // ML_HEADS
ML_CHUNK = 64
MLA_HEADS = 4
MLA_D_NOPE = GROUP_W // MLA_HEADS
MLA_D_ROPE = 64
MLA_Q_RANK = D_MODEL // 8
MLA_KV_RANK = D_MODEL // 8
MLA_SCALE = (MLA_D_NOPE + MLA_D_ROPE) ** -0.5
SWA_HEADS = 4
SWA_KV_HEADS = 2
SWA_DH = GROUP_W // SWA_HEADS
WINDOW = 128
DIFF_HEADS = 4
DIFF_DH = GROUP_W // (2 * DIFF_HEADS)
PEER_HEADS = 8
PEER_QDIM = 256
PEER_NKEYS = 128
PEER_N = PEER_NKEYS * PEER_NKEYS
PEER_TOPK = 16
ROPE_BASE = 10000.0
EPS = 1e-6
NEG = -1e30

N_CTX = BATCH * SEQ
N_LAT = DEC_BATCH * DEC_SEQ
NT = N_CTX + N_LAT
N_SETS = 1 + DEC_BATCH

C_MLQ, C_MLK, C_MLV, C_MLO = 0, 512, 1024, 1536
C_CQ, C_CKV = 2048, 2304
C_SQ, C_SK, C_SV = 2560, 3072, 3328
C_DQ, C_DK, C_DV = 3584, 4096, 4608
C_TAIL = 5120
TAIL_G = 64
PROJ_W = 5376
PROJ_TN = 1792

VMEM_LIMIT = 56 * 1024 * 1024


def _cparams(*sem):
    return pltpu.CompilerParams(dimension_semantics=sem, vmem_limit_bytes=VMEM_LIMIT)


def _mod_set(row_start):
    return jnp.where(row_start >= N_CTX, (row_start - N_CTX) // DEC_SEQ + 1, 0)


def _rms(x, g):
    return x * lax.rsqrt(jnp.mean(x * x, axis=-1, keepdims=True) + EPS) * g


def _dot(a, b):
    return jnp.dot(a, b, preferred_element_type=F32)


def _dot_nt(a, b):
    return lax.dot_general(a, b, (((1,), (1,)), ((), ())), preferred_element_type=F32)


def _dot_tn(a, b):
    return lax.dot_general(a, b, (((0,), (0,)), ((), ())), preferred_element_type=F32)


def _dot_hi(a, b):
    return jnp.dot(a, b, preferred_element_type=F32, precision=lax.Precision.HIGHEST)


MOD_TK = 512
MOD_TN = 2048


def _mod_kernel(c_ref, w_ref, b_ref, o_ref, acc_ref):
    k = pl.program_id(2)

    @pl.when(k == 0)
    def _():
        acc_ref[...] = jnp.zeros_like(acc_ref)

    w = w_ref[...]
    for r in range(N_SETS):
        cv = c_ref[:, r:r + 1]
        sv = cv * jax.nn.sigmoid(cv)
        acc_ref[r] += (sv * w).reshape(MOD_TK // 8, 8, MOD_TN).sum(axis=0)

    @pl.when(k == pl.num_programs(2) - 1)
    def _():
        o_ref[...] = jnp.zeros_like(o_ref)
        for r in range(N_SETS):
            o_ref[r:r + 1, :] = acc_ref[r].sum(axis=0, keepdims=True) + b_ref[...]


def _modulation(cvec_t, w_mod, b_mod):
    return pl.pallas_call(
        _mod_kernel,
        grid=(DEPTH, 6 * D_MODEL // MOD_TN, D_MODEL // MOD_TK),
        in_specs=[
            pl.BlockSpec((MOD_TK, 8), lambda l, n, k: (k, 0)),
            pl.BlockSpec((None, MOD_TK, MOD_TN), lambda l, n, k: (l, k, n)),
            pl.BlockSpec((None, 1, MOD_TN), lambda l, n, k: (l, 0, n)),
        ],
        out_specs=pl.BlockSpec((None, 8, MOD_TN), lambda l, n, k: (l, 0, n)),
        out_shape=jax.ShapeDtypeStruct((DEPTH, 8, 6 * D_MODEL), F32),
        scratch_shapes=[pltpu.VMEM((N_SETS, 8, MOD_TN), F32)],
        compiler_params=_cparams("parallel", "parallel", "arbitrary"),
        name="modulation",
    )(cvec_t, w_mod, b_mod.reshape(DEPTH, 1, 6 * D_MODEL))


PROJ_TM = 1024


def _proj_kernel(x_ref, m_ref, g_ref, w_ref, o_ref, h_ref):
    @pl.when(pl.program_id(1) == 0)
    def _():
        h = _rms(x_ref[...], g_ref[...]) * (1.0 + m_ref[1:2, :]) + m_ref[0:1, :]
        h_ref[...] = h.astype(BF16)

    o_ref[...] = _dot(h_ref[...], w_ref[...])


def _project(x, mod4, norm_g, w_in_p, l):
    return pl.pallas_call(
        _proj_kernel,
        grid=(NT // PROJ_TM, PROJ_W // PROJ_TN),
        in_specs=[
            pl.BlockSpec((PROJ_TM, D_MODEL), lambda i, j: (i, 0), pipeline_mode=pl.Buffered(1)),
            pl.BlockSpec((None, None, 6, D_MODEL), lambda i, j: (l, _mod_set(i * PROJ_TM), 0, 0)),
            pl.BlockSpec((None, 1, D_MODEL), lambda i, j: (l, 0, 0)),
            pl.BlockSpec((None, D_MODEL, PROJ_TN), lambda i, j: (l, 0, j)),
        ],
        out_specs=pl.BlockSpec((PROJ_TM, PROJ_TN), lambda i, j: (i, j)),
        out_shape=jax.ShapeDtypeStruct((NT, PROJ_W), F32),
        scratch_shapes=[pltpu.VMEM((PROJ_TM, D_MODEL), BF16)],
        compiler_params=_cparams("parallel", "arbitrary"),
        name="adaln_in_proj",
    )(x, mod4, norm_g, w_in_p)


def _rope(x, cos, sin, quarter):
    width = x.shape[-1]
    lane = lax.broadcasted_iota(jnp.int32, x.shape, 1)
    first = (lane % (2 * quarter)) < quarter
    partner = jnp.where(first, pltpu.roll(x, width - quarter, 1), pltpu.roll(x, quarter, 1))
    return x * cos + partner * sin


def _rope_tables(n_tok, rot_dim):
    half = rot_dim // 2
    pos = jnp.arange(n_tok)
    row = (pos // GRID_W).astype(F32)
    col = (pos % GRID_W).astype(F32)
    inv = ROPE_BASE ** (-jnp.arange(0, half, 2, dtype=F32) / half)
    a_row = row[:, None] * inv[None, :]
    a_col = col[:, None] * inv[None, :]
    ang = jnp.concatenate([a_row, a_row, a_col, a_col], axis=-1)
    sign = jnp.tile(jnp.concatenate([-jnp.ones(half // 2, F32), jnp.ones(half // 2, F32)]), 2)
    return jnp.cos(ang), jnp.sin(ang) * sign[None, :]


def _log_sigmoid(x):
    return jnp.minimum(x, 0.0) - jnp.log(1.0 + jnp.exp(-jnp.abs(x)))


def _mlstm_kernel(*refs, n_tok, n_seq, has_state):
    if has_state:
        (q_ref, k_ref, v_ref, o_ref, tail_ref, gt_ref, bcol_ref, brow_ref, ng_ref, c0_ref, n0_ref, m0_ref,
         out_ref, hf_ref, hb_ref, cst_ref, nst_ref, mst_ref) = refs
    else:
        (q_ref, k_ref, v_ref, o_ref, tail_ref, gt_ref, bcol_ref, brow_ref, ng_ref,
         out_ref, cs_ref, ns_ref, ms_ref, hf_ref, hb_ref, cst_ref, nst_ref, mst_ref) = refs
    n_chunks = n_tok // ML_CHUNK
    scale = ML_DH ** -0.5

    if has_state:
        cst_ref[...] = c0_ref[...]
        nst_ref[...] = n0_ref[...]
        mst_ref[...] = m0_ref[...]
    else:
        cst_ref[...] = jnp.zeros_like(cst_ref)
        nst_ref[...] = jnp.zeros_like(nst_ref)
        mst_ref[...] = jnp.zeros_like(mst_ref)

    row = lax.broadcasted_iota(jnp.int32, (ML_CHUNK, ML_CHUNK), 0)
    col = lax.broadcasted_iota(jnp.int32, (ML_CHUNK, ML_CHUNK), 1)
    lower = (col <= row)
    upper = (col >= row)
    lower_f = lower.astype(F32)
    upper_f = upper.astype(F32)

    def scan_group(c, group):
        ch = []
        for b, d in group:
            cc = c if d == 0 else n_chunks - 1 - c
            t0 = pl.multiple_of(b * n_tok + cc * ML_CHUNK, ML_CHUNK)
            rows = pl.ds(t0, ML_CHUNK)
            g_col = tail_ref[rows, :] + brow_ref[...]
            g_row = gt_ref[b * n_chunks + cc] + bcol_ref[...]
            lf_col = _log_sigmoid(g_col)
            lf_row = _log_sigmoid(g_row)
            if d == 0:
                bcum_col = _dot_hi(lower_f, lf_col)
                bcum_row = _dot_hi(lf_row, upper_f)
                mask = lower
            else:
                bcum_col = _dot_hi(upper_f, lf_col)
                bcum_row = _dot_hi(lf_row, lower_f)
                mask = upper
            for h in range(ML_HEADS):
                r = d * ML_HEADS + h
                fr = 2 * ML_HEADS + r
                ch.append(dict(
                    b=b, d=d, r=r, rows=rows, lanes=slice(h * ML_DH, (h + 1) * ML_DH), mask=mask,
                    ig_c=g_col[:, TAIL_G + r:TAIL_G + r + 1],
                    b_c=bcum_col[:, TAIL_G + fr:TAIL_G + fr + 1],
                    ig_r=g_row[r:r + 1, :],
                    b_r=bcum_row[fr:fr + 1, :],
                    tot=jnp.sum(lf_row[fr:fr + 1, :], axis=1, keepdims=True)))
        n = range(len(ch))
        m_prev = [mst_ref[x["b"], x["r"]:x["r"] + 1, 0:1] for x in ch]
        n_prev = [nst_ref[x["b"], x["r"]:x["r"] + 1, :] for x in ch]
        c_prev = [cst_ref[x["b"], x["r"]] for x in ch]
        q = [q_ref[x["rows"], x["lanes"]] for x in ch]
        k = [k_ref[x["rows"], x["lanes"]] for x in ch]
        v = [v_ref[x["rows"], x["lanes"]].astype(BF16) for x in ch]
        qb = [x.astype(BF16) for x in q]

        dmat = [jnp.where(ch[i]["mask"], ch[i]["b_c"] - ch[i]["b_r"] + ch[i]["ig_r"], NEG) for i in n]
        inter = [ch[i]["b_c"] + m_prev[i] for i in n]
        mt = [jnp.maximum(inter[i], jnp.max(dmat[i], axis=1, keepdims=True)) for i in n]
        w = [jnp.exp(dmat[i] - mt[i]) for i in n]
        qk = [_dot_nt(qb[i], k[i].astype(BF16)) for i in n]
        qc = [_dot(qb[i], c_prev[i].astype(BF16)) for i in n]
        s = [qk[i] * scale * w[i] for i in n]
        a = [jnp.exp(inter[i] - mt[i]) for i in n]
        sv = [_dot(s[i].astype(BF16), v[i]) for i in n]
        den = [jnp.sum(s[i], axis=1, keepdims=True) + a[i] * jnp.sum(q[i] * n_prev[i], axis=1, keepdims=True)
               for i in n]
        hc = [(sv[i] + a[i] * qc[i]) / jnp.maximum(jnp.abs(den[i]), jnp.exp(-mt[i])) for i in n]
        for i in n:
            (hf_ref if ch[i]["d"] == 0 else hb_ref)[ch[i]["rows"], ch[i]["lanes"]] = hc[i]

        wlog_c = [ch[i]["tot"] - ch[i]["b_c"] + ch[i]["ig_c"] for i in n]
        wlog_r = [ch[i]["tot"] - ch[i]["b_r"] + ch[i]["ig_r"] for i in n]
        m_new = [jnp.maximum(ch[i]["tot"] + m_prev[i], jnp.max(wlog_r[i], axis=1, keepdims=True)) for i in n]
        decay = [jnp.exp(ch[i]["tot"] + m_prev[i] - m_new[i]) for i in n]
        kw = [k[i] * (scale * jnp.exp(wlog_c[i] - m_new[i])) for i in n]
        kv = [_dot_tn(kw[i].astype(BF16), v[i]) for i in n]
        for i in n:
            b, r = ch[i]["b"], ch[i]["r"]
            cst_ref[b, r] = decay[i] * c_prev[i] + kv[i]
            nst_ref[b, r:r + 1, :] = decay[i] * n_prev[i] + jnp.sum(kw[i], axis=0, keepdims=True)
            mst_ref[b, r:r + 1, :] = jnp.broadcast_to(m_new[i], (1, ML_DH))

    def chunk_step(c, carry):
        for b in range(n_seq):
            scan_group(c, [(b, 0)])
            scan_group(c, [(b, 1)])
        return carry

    lax.fori_loop(0, n_chunks, chunk_step, 0)

    for h in range(ML_HEADS):
        lanes = slice(h * ML_DH, (h + 1) * ML_DH)
        hs = hf_ref[:, lanes] + hb_ref[:, lanes]
        out_ref[:, lanes] = _rms(hs, ng_ref[:, lanes]) * jax.nn.sigmoid(o_ref[:, lanes])

    if not has_state:
        cs_ref[...] = cst_ref[...]
        ns_ref[...] = nst_ref[...]
        ms_ref[...] = mst_ref[...]


ML_SEQ_PER_STEP = 4


def _mlstm(proj, gt, bcol, brow, norm_g, l, *, latent, states=None):
    n_tok = DEC_SEQ if latent else SEQ
    n_seq = 1 if latent else ML_SEQ_PER_STEP
    n_b = DEC_BATCH if latent else BATCH
    rows = n_seq * n_tok
    blk0 = N_CTX // rows if latent else 0
    n_chunks = n_tok // ML_CHUNK

    def col_spec(c0):
        return pl.BlockSpec((rows, 512), lambda b: (blk0 + b, c0 // 512))

    in_specs = [
        col_spec(C_MLQ), col_spec(C_MLK), col_spec(C_MLV), col_spec(C_MLO),
        pl.BlockSpec((rows, 128), lambda b: (blk0 + b, C_TAIL // 128)),
        pl.BlockSpec((n_seq * n_chunks, 16, ML_CHUNK), lambda b: (blk0 + b, 0, 0)),
        pl.BlockSpec((16, 1), lambda b: (0, 0)),
        pl.BlockSpec((1, 128), lambda b: (0, 0)),
        pl.BlockSpec((None, 1, GROUP_W), lambda b: (l, 0, 0)),
    ]
    args = [proj, proj, proj, proj, proj, gt, bcol, brow, norm_g]
    n_chain = 2 * ML_HEADS
    scratch = [pltpu.VMEM((rows, GROUP_W), F32), pltpu.VMEM((rows, GROUP_W), F32),
               pltpu.VMEM((n_seq, n_chain, ML_DH, ML_DH), F32), pltpu.VMEM((n_seq, n_chain, ML_DH), F32),
               pltpu.VMEM((n_seq, n_chain, ML_DH), F32)]
    if latent:
        c0, n0, m0 = states
        in_specs += [
            pl.BlockSpec((n_seq, None, n_chain, ML_DH, ML_DH), lambda b: (b, l, 0, 0, 0)),
            pl.BlockSpec((n_seq, None, n_chain, ML_DH), lambda b: (b, l, 0, 0)),
            pl.BlockSpec((n_seq, None, n_chain, ML_DH), lambda b: (b, l, 0, 0)),
        ]
        args += [c0, n0, m0]
        out_shape = jax.ShapeDtypeStruct((N_LAT, GROUP_W), F32)
        out_specs = pl.BlockSpec((rows, GROUP_W), lambda b: (b, 0))
    else:
        out_shape = (jax.ShapeDtypeStruct((N_CTX, GROUP_W), F32),
                     jax.ShapeDtypeStruct((BATCH, n_chain, ML_DH, ML_DH), F32),
                     jax.ShapeDtypeStruct((BATCH, n_chain, ML_DH), F32),
                     jax.ShapeDtypeStruct((BATCH, n_chain, ML_DH), F32))
        out_specs = (pl.BlockSpec((rows, GROUP_W), lambda b: (b, 0)),
                     pl.BlockSpec((n_seq, n_chain, ML_DH, ML_DH), lambda b: (b, 0, 0, 0)),
                     pl.BlockSpec((n_seq, n_chain, ML_DH), lambda b: (b, 0, 0)),
                     pl.BlockSpec((n_seq, n_chain, ML_DH), lambda b: (b, 0, 0)))
    return pl.pallas_call(
        functools.partial(_mlstm_kernel, n_tok=n_tok, n_seq=n_seq, has_state=latent),
        grid=(n_b // n_seq,),
        in_specs=in_specs,
        out_specs=out_specs,
        out_shape=out_shape,
        scratch_shapes=scratch,
        compiler_params=_cparams("parallel"),
        name="mlstm_latent" if latent else "mlstm_context",
    )(*args)


def _softmax_parts(scores, sink=None):
    m = jnp.max(scores[0], axis=1, keepdims=True)
    for s in scores[1:]:
        m = jnp.maximum(m, jnp.max(s, axis=1, keepdims=True))
    if sink is not None:
        m = jnp.maximum(m, sink)
    es = [jnp.exp(s - m) for s in scores]
    den = jnp.sum(es[0], axis=1, keepdims=True)
    for e in es[1:]:
        den = den + jnp.sum(e, axis=1, keepdims=True)
    if sink is not None:
        den = den + jnp.exp(sink - m)
    return es, den


def _softmax_parts_all(score_lists, sinks=None):
    n = range(len(score_lists))
    sinks = [None] * len(score_lists) if sinks is None else sinks
    ms = []
    for i in n:
        m = jnp.max(score_lists[i][0], axis=1, keepdims=True)
        for s in score_lists[i][1:]:
            m = jnp.maximum(m, jnp.max(s, axis=1, keepdims=True))
        ms.append(m if sinks[i] is None else jnp.maximum(m, sinks[i]))
    es = [[jnp.exp(s - ms[i]) for s in score_lists[i]] for i in n]
    dens = []
    for i in n:
        den = jnp.sum(es[i][0], axis=1, keepdims=True)
        for e in es[i][1:]:
            den = den + jnp.sum(e, axis=1, keepdims=True)
        dens.append(den if sinks[i] is None else den + jnp.exp(sinks[i] - ms[i]))
    return es, dens


ATT_TQ = 256
LAT_QB = DEC_SEQ // ATT_TQ
N_KEYS_LAT = DEC_SEQ + PAST_LEN


def _mla_q(cq_ref, qg_ref, wuq_ref):
    return _dot(_rms(cq_ref[...], qg_ref[...]).astype(BF16), wuq_ref[...])


def _mla_heads(q, kv, kr, n_keys):
    hs = range(MLA_HEADS)
    r0 = MLA_HEADS * MLA_D_NOPE
    qn = [q[:, h * MLA_D_NOPE:(h + 1) * MLA_D_NOPE].astype(BF16) for h in hs]
    qr = [q[:, r0 + h * MLA_D_ROPE:r0 + (h + 1) * MLA_D_ROPE].astype(BF16) for h in hs]
    kn = [kv[:, h * 256:h * 256 + MLA_D_NOPE] for h in hs]
    v = [kv[:, h * 256 + MLA_D_NOPE:(h + 1) * 256] for h in hs]
    s = [(_dot_nt(qn[h], kn[h]) + _dot_nt(qr[h], kr)) * MLA_SCALE for h in hs]
    es, dens = _softmax_parts_all([[s[h]] for h in hs])
    pv = [_dot(es[h][0].astype(BF16), v[h]) for h in hs]
    return [pv[h] / dens[h] for h in hs]


def _mla_ctx_kernel(cq_ref, ckv_ref, tail_ref, qg_ref, kg_ref, wuq_ref, wukv_ref, out_ref, ckvn_ref):
    q = _mla_q(cq_ref, qg_ref, wuq_ref)
    ckvn = _rms(ckv_ref[...], kg_ref[...])
    ckvn_ref[...] = ckvn
    kv = _dot(ckvn.astype(BF16), wukv_ref[...]).astype(BF16)
    kr = tail_ref[:, 0:MLA_D_ROPE].astype(BF16)
    outs = _mla_heads(q, kv, kr, SEQ)
    for h in range(MLA_HEADS):
        out_ref[:, h * MLA_D_V:(h + 1) * MLA_D_V] = outs[h]


MLA_D_V = GROUP_W // MLA_HEADS


def _mla_lat_kernel(cq_ref, ckv_ref, tail_ref, qg_ref, kg_ref, wuq_ref, wukv_ref, ckvc_ref, krc_ref,
                    cosq_ref, sinq_ref, cosk_ref, sink_ref, out_ref, kv_ref, kr_ref):
    @pl.when(pl.program_id(1) == 0)
    def _():
        ckvn = _rms(ckv_ref[...], kg_ref[...])
        kv_ref[0:DEC_SEQ, :] = _dot(ckvn.astype(BF16), wukv_ref[...]).astype(BF16)
        kv_ref[DEC_SEQ:N_KEYS_LAT, :] = _dot(ckvc_ref[...].astype(BF16), wukv_ref[...]).astype(BF16)
        kr = _rope(tail_ref[...], cosk_ref[...], sink_ref[...], MLA_D_ROPE // 4)
        kr_ref[0:DEC_SEQ, :] = kr[:, 0:MLA_D_ROPE].astype(BF16)
        kr_ref[DEC_SEQ:N_KEYS_LAT, :] = krc_ref[...].astype(BF16)

    q = _mla_q(cq_ref, qg_ref, wuq_ref)
    n0 = MLA_HEADS * MLA_D_NOPE
    q_rope = _rope(q[:, n0:], cosq_ref[...], sinq_ref[...], MLA_D_ROPE // 4)
    q = jnp.concatenate([q[:, :n0], q_rope], axis=1)
    outs = _mla_heads(q, kv_ref[...], kr_ref[...], N_KEYS_LAT)
    for h in range(MLA_HEADS):
        out_ref[:, h * MLA_D_V:(h + 1) * MLA_D_V] = outs[h]


def _w_specs2(shape_a, shape_b, l):
    return [pl.BlockSpec((None,) + shape_a, lambda *_: (l,) + (0,) * len(shape_a)),
            pl.BlockSpec((None,) + shape_b, lambda *_: (l,) + (0,) * len(shape_b))]


def _mla_ctx(proj, qg, kg, wuq, wukv, l):
    return pl.pallas_call(
        _mla_ctx_kernel,
        grid=(BATCH,),
        in_specs=[
            pl.BlockSpec((SEQ, 256), lambda b: (b, C_CQ // 256)),
            pl.BlockSpec((SEQ, 256), lambda b: (b, C_CKV // 256)),
            pl.BlockSpec((SEQ, 128), lambda b: (b, C_TAIL // 128)),
            pl.BlockSpec((None, 1, MLA_Q_RANK), lambda b: (l, 0, 0)),
            pl.BlockSpec((None, 1, MLA_KV_RANK), lambda b: (l, 0, 0)),
            pl.BlockSpec((None, MLA_Q_RANK, 768), lambda b: (l, 0, 0)),
            pl.BlockSpec((None, MLA_KV_RANK, 1024), lambda b: (l, 0, 0)),
        ],
        out_specs=(pl.BlockSpec((SEQ, GROUP_W), lambda b: (b, 0)),
                   pl.BlockSpec((SEQ, MLA_KV_RANK), lambda b: (b, 0))),
        out_shape=(jax.ShapeDtypeStruct((N_CTX, GROUP_W), F32),
                   jax.ShapeDtypeStruct((N_CTX, MLA_KV_RANK), F32)),
        compiler_params=_cparams("parallel"),
        name="mla_context",
    )(proj, proj, proj, qg, kg, wuq, wukv)


def _mla_lat(proj, qg, kg, wuq, wukv, cache_ckv, cache_kr, cos64, sin64, l):
    qb0 = N_CTX // ATT_TQ
    bb0 = N_CTX // DEC_SEQ
    cosq = jnp.tile(cos64, (1, MLA_HEADS))
    sinq = jnp.tile(sin64, (1, MLA_HEADS))
    cosk = jnp.tile(cos64, (1, 2))
    sink = jnp.tile(sin64, (1, 2))
    return pl.pallas_call(
        _mla_lat_kernel,
        grid=(DEC_BATCH, LAT_QB),
        in_specs=[
            pl.BlockSpec((ATT_TQ, 256), lambda b, i: (qb0 + b * LAT_QB + i, C_CQ // 256)),
            pl.BlockSpec((DEC_SEQ, 256), lambda b, i: (bb0 + b, C_CKV // 256)),
            pl.BlockSpec((DEC_SEQ, 128), lambda b, i: (bb0 + b, C_TAIL // 128)),
            pl.BlockSpec((None, 1, MLA_Q_RANK), lambda b, i: (l, 0, 0)),
            pl.BlockSpec((None, 1, MLA_KV_RANK), lambda b, i: (l, 0, 0)),
            pl.BlockSpec((None, MLA_Q_RANK, 768), lambda b, i: (l, 0, 0)),
            pl.BlockSpec((None, MLA_KV_RANK, 1024), lambda b, i: (l, 0, 0)),
            pl.BlockSpec((None, None, PAST_LEN, MLA_KV_RANK), lambda b, i: (b, l, 0, 0)),
            pl.BlockSpec((None, None, PAST_LEN, MLA_D_ROPE), lambda b, i: (b, l, 0, 0)),
            pl.BlockSpec((ATT_TQ, 256), lambda b, i: (i, 0)),
            pl.BlockSpec((ATT_TQ, 256), lambda b, i: (i, 0)),
            pl.BlockSpec((DEC_SEQ, 128), lambda b, i: (0, 0)),
            pl.BlockSpec((DEC_SEQ, 128), lambda b, i: (0, 0)),
        ],
        out_specs=pl.BlockSpec((ATT_TQ, GROUP_W), lambda b, i: (b * LAT_QB + i, 0)),
        out_shape=jax.ShapeDtypeStruct((N_LAT, GROUP_W), F32),
        scratch_shapes=[pltpu.VMEM((N_KEYS_LAT, 1024), BF16), pltpu.VMEM((N_KEYS_LAT, MLA_D_ROPE), BF16)],
        compiler_params=_cparams("parallel", "arbitrary"),
        name="mla_latent",
    )(proj, proj, proj, qg, kg, wuq, wukv, cache_ckv, cache_kr, cosq, sinq, cosk, sink)


SWA_SCALE = SWA_DH ** -0.5
SWA_REP = SWA_HEADS // SWA_KV_HEADS
SWA_KWIN = ATT_TQ + 2 * WINDOW


def _swa_ctx_kernel(sink_ref, q_ref, k_ref, v_ref, out_ref):
    kb = k_ref[...].astype(BF16)
    vb = v_ref[...].astype(BF16)
    hs = range(SWA_HEADS)
    gl = [slice((h // SWA_REP) * SWA_DH, (h // SWA_REP + 1) * SWA_DH) for h in hs]
    q = [q_ref[:, h * SWA_DH:(h + 1) * SWA_DH].astype(BF16) for h in hs]
    s = [_dot_nt(q[h], kb[:, gl[h]]) * SWA_SCALE for h in hs]
    es, dens = _softmax_parts_all([[s[h]] for h in hs], sinks=[sink_ref[0, h] for h in hs])
    pv = [_dot(es[h][0].astype(BF16), vb[:, gl[h]]) for h in hs]
    for h in hs:
        out_ref[:, h * SWA_DH:(h + 1) * SWA_DH] = pv[h] / dens[h]


def _swa_lat_kernel(sink_ref, q_ref, k_ref, v_ref, kc_ref, vc_ref, cosq_ref, sinq_ref, cosk_ref, sink_t_ref,
                    out_ref, kr_ref):
    i = pl.program_id(1)

    @pl.when(i == 0)
    def _():
        kr_ref[...] = _rope(k_ref[...], cosk_ref[...], sink_t_ref[...], SWA_DH // 4).astype(BF16)

    q_all = _rope(q_ref[...], cosq_ref[...], sinq_ref[...], SWA_DH // 4)
    k0 = pl.multiple_of(jnp.clip(i * ATT_TQ - WINDOW, 0, DEC_SEQ - SWA_KWIN), WINDOW)
    kwin = kr_ref[pl.ds(k0, SWA_KWIN), :]
    vwin = v_ref[pl.ds(k0, SWA_KWIN), :].astype(BF16)
    kc = kc_ref[...].astype(BF16)
    vc = vc_ref[...].astype(BF16)
    qpos = i * ATT_TQ + lax.broadcasted_iota(jnp.int32, (ATT_TQ, SWA_KWIN), 0)
    kpos = k0 + lax.broadcasted_iota(jnp.int32, (ATT_TQ, SWA_KWIN), 1)
    band = jnp.abs(qpos - kpos) <= WINDOW
    hs = range(SWA_HEADS)
    gl = [slice((h // SWA_REP) * SWA_DH, (h // SWA_REP + 1) * SWA_DH) for h in hs]
    q = [q_all[:, h * SWA_DH:(h + 1) * SWA_DH].astype(BF16) for h in hs]
    s_loc = [jnp.where(band, _dot_nt(q[h], kwin[:, gl[h]]) * SWA_SCALE, NEG) for h in hs]
    s_ctx = [_dot_nt(q[h], kc[:, gl[h]]) * SWA_SCALE for h in hs]
    es, dens = _softmax_parts_all([[s_loc[h], s_ctx[h]] for h in hs], sinks=[sink_ref[0, h] for h in hs])
    o = [_dot(es[h][0].astype(BF16), vwin[:, gl[h]]) + _dot(es[h][1].astype(BF16), vc[:, gl[h]]) for h in hs]
    for h in hs:
        out_ref[:, h * SWA_DH:(h + 1) * SWA_DH] = o[h] / dens[h]


def _smem_spec():
    return pl.BlockSpec(memory_space=pltpu.SMEM)


def _swa_ctx(proj, sink, l):
    return pl.pallas_call(
        _swa_ctx_kernel,
        grid=(BATCH,),
        in_specs=[
            _smem_spec(),
            pl.BlockSpec((SEQ, 512), lambda b: (b, C_SQ // 512)),
            pl.BlockSpec((SEQ, 256), lambda b: (b, C_SK // 256)),
            pl.BlockSpec((SEQ, 256), lambda b: (b, C_SV // 256)),
        ],
        out_specs=pl.BlockSpec((SEQ, GROUP_W), lambda b: (b, 0)),
        out_shape=jax.ShapeDtypeStruct((N_CTX, GROUP_W), F32),
        compiler_params=_cparams("parallel"),
        name="swa_context",
    )(sink, proj, proj, proj)


def _swa_lat(proj, sink, cache_k, cache_v, cos128, sin128, l):
    qb0 = N_CTX // ATT_TQ
    bb0 = N_CTX // DEC_SEQ
    kvw = SWA_KV_HEADS * SWA_DH
    return pl.pallas_call(
        _swa_lat_kernel,
        grid=(DEC_BATCH, LAT_QB),
        in_specs=[
            _smem_spec(),
            pl.BlockSpec((ATT_TQ, 512), lambda b, i: (qb0 + b * LAT_QB + i, C_SQ // 512)),
            pl.BlockSpec((DEC_SEQ, 256), lambda b, i: (bb0 + b, C_SK // 256)),
            pl.BlockSpec((DEC_SEQ, 256), lambda b, i: (bb0 + b, C_SV // 256)),
            pl.BlockSpec((None, None, PAST_LEN, kvw), lambda b, i: (b, l, 0, 0)),
            pl.BlockSpec((None, None, PAST_LEN, kvw), lambda b, i: (b, l, 0, 0)),
            pl.BlockSpec((ATT_TQ, 512), lambda b, i: (i, 0)),
            pl.BlockSpec((ATT_TQ, 512), lambda b, i: (i, 0)),
            pl.BlockSpec((DEC_SEQ, 256), lambda b, i: (0, 0)),
            pl.BlockSpec((DEC_SEQ, 256), lambda b, i: (0, 0)),
        ],
        out_specs=pl.BlockSpec((ATT_TQ, GROUP_W), lambda b, i: (b * LAT_QB + i, 0)),
        out_shape=jax.ShapeDtypeStruct((N_LAT, GROUP_W), F32),
        scratch_shapes=[pltpu.VMEM((DEC_SEQ, kvw), BF16)],
        compiler_params=_cparams("parallel", "arbitrary"),
        name="swa_latent",
    )(sink, proj, proj, proj, cache_k, cache_v,
      jnp.tile(cos128, (1, SWA_HEADS)), jnp.tile(sin128, (1, SWA_HEADS)),
      jnp.tile(cos128, (1, SWA_KV_HEADS)), jnp.tile(sin128, (1, SWA_KV_HEADS)))


DIFF_SCALE = DIFF_DH ** -0.5


def _diff_lambda(lq1_ref, lk1_ref, lq2_ref, lk2_ref):
    a = jnp.sum(lq1_ref[...] * lk1_ref[...], axis=1, keepdims=True)
    b = jnp.sum(lq2_ref[...] * lk2_ref[...], axis=1, keepdims=True)
    return jnp.exp(a) - jnp.exp(b)


def _diff_heads(q, k_parts, v_parts, lam, lam_init, ng_ref, out_ref, heads_per_group):
    for h0 in range(0, DIFF_HEADS, heads_per_group):
        hs = range(h0, h0 + heads_per_group)
        sls = [slice(h * 2 * DIFF_DH + c * DIFF_DH, h * 2 * DIFF_DH + (c + 1) * DIFF_DH)
               for h in hs for c in range(2)]
        qc = [q[:, sl].astype(BF16) for sl in sls]
        scores = [[_dot_nt(qc[j], kp[:, sls[j]]) * DIFF_SCALE for kp in k_parts] for j in range(len(sls))]
        es, dens = _softmax_parts_all(scores)
        ps = [[e / dens[j] for e in es[j]] for j in range(len(sls))]
        outs = []
        for i, h in enumerate(hs):
            vl = slice(h * 2 * DIFF_DH, (h + 1) * 2 * DIFF_DH)
            o = None
            for p1, p2, vp in zip(ps[2 * i], ps[2 * i + 1], v_parts):
                t = _dot((p1 - lam * p2).astype(BF16), vp[:, vl])
                o = t if o is None else o + t
            outs.append((vl, o))
        for vl, o in outs:
            out_ref[:, vl] = _rms(o, ng_ref[...]) * (1.0 - lam_init)


def _diff_ctx_kernel(q_ref, k_ref, v_ref, lq1_ref, lk1_ref, lq2_ref, lk2_ref, ng_ref, out_ref, *, lam_init):
    lam = _diff_lambda(lq1_ref, lk1_ref, lq2_ref, lk2_ref) + lam_init
    _diff_heads(q_ref[...], [k_ref[...].astype(BF16)], [v_ref[...].astype(BF16)], lam, lam_init, ng_ref, out_ref,
                heads_per_group=DIFF_HEADS)


def _diff_lat_kernel(q_ref, k_ref, v_ref, kc_ref, vc_ref, lq1_ref, lk1_ref, lq2_ref, lk2_ref, ng_ref,
                     cosq_ref, sinq_ref, cosk_ref, sink_ref, out_ref, kr_ref, *, lam_init):
    @pl.when(pl.program_id(1) == 0)
    def _():
        kr_ref[...] = _rope(k_ref[...], cosk_ref[...], sink_ref[...], DIFF_DH // 4).astype(BF16)

    lam = _diff_lambda(lq1_ref, lk1_ref, lq2_ref, lk2_ref) + lam_init
    q = _rope(q_ref[...], cosq_ref[...], sinq_ref[...], DIFF_DH // 4)
    _diff_heads(q, [kr_ref[...], kc_ref[...].astype(BF16)], [v_ref[...].astype(BF16), vc_ref[...].astype(BF16)],
                lam, lam_init, ng_ref, out_ref, heads_per_group=2)


def _vec_specs(n, width, l, nargs):
    return [pl.BlockSpec((None, 1, width), lambda *_: (l, 0, 0)) for _ in range(n)]


def _diff_ctx(proj, lq1, lk1, lq2, lk2, ng, l):
    lam_init = 0.8 - 0.6 * math.exp(-0.3 * l)
    return pl.pallas_call(
        functools.partial(_diff_ctx_kernel, lam_init=lam_init),
        grid=(BATCH,),
        in_specs=[
            pl.BlockSpec((SEQ, 512), lambda b: (b, C_DQ // 512)),
            pl.BlockSpec((SEQ, 512), lambda b: (b, C_DK // 512)),
            pl.BlockSpec((SEQ, 512), lambda b: (b, C_DV // 512)),
        ] + _vec_specs(4, DIFF_DH, l, 1) + _vec_specs(1, 2 * DIFF_DH, l, 1),
        out_specs=pl.BlockSpec((SEQ, GROUP_W), lambda b: (b, 0)),
        out_shape=jax.ShapeDtypeStruct((N_CTX, GROUP_W), F32),
        compiler_params=_cparams("parallel"),
        name="diff_context",
    )(proj, proj, proj, lq1, lk1, lq2, lk2, ng)


def _diff_lat(proj, lq1, lk1, lq2, lk2, ng, cache_k, cache_v, cos64, sin64, l):
    lam_init = 0.8 - 0.6 * math.exp(-0.3 * l)
    qb0 = N_CTX // ATT_TQ
    bb0 = N_CTX // DEC_SEQ
    cos_t = jnp.tile(cos64, (1, 2 * DIFF_HEADS))
    sin_t = jnp.tile(sin64, (1, 2 * DIFF_HEADS))
    return pl.pallas_call(
        functools.partial(_diff_lat_kernel, lam_init=lam_init),
        grid=(DEC_BATCH, LAT_QB),
        in_specs=[
            pl.BlockSpec((ATT_TQ, 512), lambda b, i: (qb0 + b * LAT_QB + i, C_DQ // 512)),
            pl.BlockSpec((DEC_SEQ, 512), lambda b, i: (bb0 + b, C_DK // 512)),
            pl.BlockSpec((DEC_SEQ, 512), lambda b, i: (bb0 + b, C_DV // 512)),
            pl.BlockSpec((None, None, PAST_LEN, GROUP_W), lambda b, i: (b, l, 0, 0)),
            pl.BlockSpec((None, None, PAST_LEN, GROUP_W), lambda b, i: (b, l, 0, 0)),
        ] + _vec_specs(4, DIFF_DH, l, 2) + _vec_specs(1, 2 * DIFF_DH, l, 2) + [
            pl.BlockSpec((ATT_TQ, 512), lambda b, i: (i, 0)),
            pl.BlockSpec((ATT_TQ, 512), lambda b, i: (i, 0)),
            pl.BlockSpec((DEC_SEQ, 512), lambda b, i: (0, 0)),
            pl.BlockSpec((DEC_SEQ, 512), lambda b, i: (0, 0)),
        ],
        out_specs=pl.BlockSpec((ATT_TQ, GROUP_W), lambda b, i: (b * LAT_QB + i, 0)),
        out_shape=jax.ShapeDtypeStruct((N_LAT, GROUP_W), F32),
        scratch_shapes=[pltpu.VMEM((DEC_SEQ, GROUP_W), BF16)],
        compiler_params=_cparams("parallel", "arbitrary"),
        name="diff_latent",
    )(proj, proj, proj, cache_k, cache_v, lq1, lk1, lq2, lk2, ng, cos_t, sin_t, cos_t, sin_t)


OUT_TM = 512


OUT_CTX_BLOCKS = N_CTX // OUT_TM


def _out_kernel(*refs):
    ctx_refs, lat_refs = refs[0:4], refs[4:8]
    x_ref, mod_ref, w_ref, x1_ref = refs[8:]
    is_ctx = pl.program_id(0) < OUT_CTX_BLOCKS
    acc = None
    for g in range(4):
        m = jnp.where(is_ctx, ctx_refs[g][...], lat_refs[g][...])
        t = _dot(m.astype(BF16), w_ref[g * GROUP_W:(g + 1) * GROUP_W, :])
        acc = t if acc is None else acc + t
    x1_ref[...] = x_ref[...] + mod_ref[2:3, :] * acc


def _out_proj(mixed_ctx, mixed_lat, x, mod4, w_out, l):
    row = lambda i: (i, 0)
    ctx_row = lambda i: (jnp.minimum(i, OUT_CTX_BLOCKS - 1), 0)
    lat_row = lambda i: (jnp.maximum(i - OUT_CTX_BLOCKS, 0), 0)
    return pl.pallas_call(
        _out_kernel,
        grid=(NT // OUT_TM,),
        in_specs=[pl.BlockSpec((OUT_TM, GROUP_W), ctx_row) for _ in range(4)]
        + [pl.BlockSpec((OUT_TM, GROUP_W), lat_row) for _ in range(4)] + [
            pl.BlockSpec((OUT_TM, D_MODEL), row),
            pl.BlockSpec((None, None, 6, D_MODEL), lambda i: (l, _mod_set(i * OUT_TM), 0, 0)),
            pl.BlockSpec((None, D_MODEL, D_MODEL), lambda i: (l, 0, 0)),
        ],
        out_specs=pl.BlockSpec((OUT_TM, D_MODEL), row),
        out_shape=jax.ShapeDtypeStruct((NT, D_MODEL), F32),
        compiler_params=_cparams("parallel"),
        name="out_proj_residual",
    )(*mixed_ctx, *mixed_lat, x, mod4, w_out)


def _peerq_kernel(x_ref, mod_ref, g_ref, w_ref, ht_ref, q_ref):
    h = _rms(x_ref[...], g_ref[...]) * (1.0 + mod_ref[4:5, :]) + mod_ref[3:4, :]
    ht_ref[...] = h.T.astype(BF16)
    q_ref[...] = _dot(h.astype(BF16), w_ref[...])


def _peer_query(x1, mod4, norm_g, w_q, l):
    row = lambda i: (i, 0)
    return pl.pallas_call(
        _peerq_kernel,
        grid=(NT // OUT_TM,),
        in_specs=[
            pl.BlockSpec((OUT_TM, D_MODEL), row),
            pl.BlockSpec((None, None, 6, D_MODEL), lambda i: (l, _mod_set(i * OUT_TM), 0, 0)),
            pl.BlockSpec((None, 1, D_MODEL), lambda i: (l, 0, 0)),
            pl.BlockSpec((None, D_MODEL, PEER_HEADS * PEER_QDIM), lambda i: (l, 0, 0)),
        ],
        out_specs=(pl.BlockSpec((D_MODEL, OUT_TM), lambda i: (0, i)),
                   pl.BlockSpec((OUT_TM, PEER_HEADS * PEER_QDIM), row)),
        out_shape=(jax.ShapeDtypeStruct((D_MODEL, NT), BF16),
                   jax.ShapeDtypeStruct((NT, PEER_HEADS * PEER_QDIM), F32)),
        compiler_params=_cparams("parallel"),
        name="adaln_peer_query",
    )(x1, mod4, norm_g, w_q)


ROUTE_TL = 256
ROUTE_LANES = 128
NOT_SEL = float(PEER_TOPK)


def _top16(s, index_ties):
    idx = lax.broadcasted_iota(jnp.int32, s.shape, 0).astype(F32)
    slot = lax.broadcasted_iota(jnp.int32, (PEER_TOPK, s.shape[1]), 0)
    rank = jnp.full(s.shape, NOT_SEL, F32)
    vals = jnp.zeros((PEER_TOPK, s.shape[1]), F32)
    for k in range(PEER_TOPK):
        m = jnp.max(s, axis=0, keepdims=True)
        sel = s == m
        if index_ties:
            sel = idx == jnp.min(jnp.where(sel, idx, float(PEER_NKEYS)), axis=0, keepdims=True)
        rank = jnp.where(sel, float(k), rank)
        s = jnp.where(sel, NEG, s)
        vals = jnp.where(slot == k, m, vals)
    return rank, vals


CAND_HALF = PEER_TOPK // 2
CAND_ROWS = PEER_TOPK + (CAND_HALF - 1) * CAND_HALF + CAND_HALF
FLAT_NONE = float(PEER_TOPK * PEER_TOPK)


def _cand_flat(tl):
    r = lax.broadcasted_iota(jnp.int32, (CAND_ROWS, tl), 0)
    mid = r - PEER_TOPK
    mid_flat = (1 + mid // CAND_HALF) * PEER_TOPK + mid % CAND_HALF
    last_flat = (CAND_HALF + r - (CAND_ROWS - CAND_HALF)) * PEER_TOPK
    flat = jnp.where(r < PEER_TOPK, r, jnp.where(r < CAND_ROWS - CAND_HALF, mid_flat, last_flat))
    return flat.astype(F32)


def _count(mask):
    return jnp.sum(jnp.where(mask, 1.0, 0.0), axis=0, keepdims=True)


def _route_head(s1, s2, flat, index_ties):
    tl = s1.shape[1]
    rank1, v1 = _top16(s1, index_ties)
    rank2, v2 = _top16(s2, index_ties)
    slabs = [v1[0:1, :] + v2]
    slabs += [v1[a:a + 1, :] + v2[0:CAND_HALF, :] for a in range(1, CAND_HALF)]
    slabs.append(v1[CAND_HALF:, :] + v2[0:1, :])
    cand = jnp.concatenate(slabs, axis=0)
    a_row = jnp.floor(flat * (1.0 / PEER_TOPK))
    top = v1[0:1, :] + v2[0:1, :]
    cnt1 = jnp.zeros(s1.shape, F32)
    z = jnp.zeros((1, tl), F32)
    for k in range(PEER_TOPK):
        m = jnp.max(cand, axis=0, keepdims=True)
        sel = cand == m
        if index_ties:
            first = jnp.min(jnp.where(sel, flat, FLAT_NONE), axis=0, keepdims=True)
            sel = flat == first
            a_sel = jnp.floor(first * (1.0 / PEER_TOPK))
        else:
            a_sel = jnp.max(jnp.where(sel, a_row, -1.0), axis=0, keepdims=True)
        cand = jnp.where(sel, NEG, cand)
        cnt1 = cnt1 + jnp.where(rank1 == a_sel, 1.0, 0.0)
        z = z + jnp.exp(m - top)
    e1 = jnp.exp(s1 - v1[0:1, :]) / z
    e2 = jnp.exp(s2 - v2[0:1, :])
    if index_ties:
        return rank2, cnt1, e1, e2, None
    full = float(PEER_TOPK)
    clean = ((_count(rank1 < NOT_SEL) == full) & (_count(rank2 < NOT_SEL) == full)
             & (_count(cand == NEG) == full))
    return rank2, cnt1, e1, e2, jnp.max(jnp.where(clean, 0.0, 1.0))


def _route_kernel(q_ref, k1_ref, k2_ref, rank2_ref, e2_ref, cnt1_ref, e1_ref):
    tl = q_ref.shape[0]
    half = PEER_QDIM // 2

    n_chunks = tl // ROUTE_LANES

    def scores(h):
        q1 = q_ref[:, h * PEER_QDIM:h * PEER_QDIM + half].astype(BF16)
        q2 = q_ref[:, h * PEER_QDIM + half:(h + 1) * PEER_QDIM].astype(BF16)
        return _dot_nt(k1_ref[...], q1), _dot_nt(k2_ref[...], q2)

    def route(h, c, s1, s2, index_ties):
        lanes = slice(c * ROUTE_LANES, (c + 1) * ROUTE_LANES)
        rank2, cnt1, e1, e2, tied = _route_head(s1[:, lanes], s2[:, lanes], _cand_flat(ROUTE_LANES), index_ties)
        rank2_ref[h, :, lanes] = rank2.astype(BF16)
        cnt1_ref[h, :, lanes] = cnt1
        e1_ref[h, :, lanes] = e1
        e2_ref[h, :, lanes] = e2.astype(BF16)
        return tied

    tied = {}
    for h in range(PEER_HEADS):
        s1, s2 = scores(h)
        for c in range(n_chunks):
            tied[h, c] = route(h, c, s1, s2, index_ties=False)

    for h in range(PEER_HEADS):
        for c in range(n_chunks):
            @pl.when(tied[h, c] > 0.0)
            def _():
                s1, s2 = scores(h)
                route(h, c, s1, s2, index_ties=True)


def _peer_route(q, keys, l):
    shp = jax.ShapeDtypeStruct((PEER_HEADS, PEER_NKEYS, NT), F32)
    shp_b = jax.ShapeDtypeStruct((PEER_HEADS, PEER_NKEYS, NT), BF16)
    spec = pl.BlockSpec((PEER_HEADS, PEER_NKEYS, ROUTE_TL), lambda i: (0, 0, i))
    half = PEER_QDIM // 2
    return pl.pallas_call(
        _route_kernel,
        grid=(NT // ROUTE_TL,),
        in_specs=[
            pl.BlockSpec((ROUTE_TL, PEER_HEADS * PEER_QDIM), lambda i: (i, 0)),
            pl.BlockSpec((None, None, PEER_NKEYS, half), lambda i: (l, 0, 0, 0)),
            pl.BlockSpec((None, None, PEER_NKEYS, half), lambda i: (l, 1, 0, 0)),
        ],
        out_specs=(spec, spec, spec, spec),
        out_shape=(shp_b, shp_b, shp, shp),
        compiler_params=_cparams("parallel"),
        name="peer_route",
    )(q, keys, keys)


PEER_TT = 512
PEER_EB = 1024
GELU_C = math.sqrt(2.0 / math.pi)


def _gelu_tanh(x):
    return 0.5 * x * (1.0 + jnp.tanh(GELU_C * (x + 0.044715 * (x * x * x))))


PEER_NE = PEER_N // PEER_EB
PEER_MT = 256


def _peer_kernel(ht_ref, u_ref, vt_ref, rank2_ref, e2_ref, cnt1_ref, e1_ref, x_ref, mod_ref, o_ref,
                 ga_ref, gb_ref, acc_ref):
    e = pl.program_id(1)

    def step(g_prev_ref, g_next_ref, activate=True, apply=True):
        blk = jnp.maximum(e - 1, 0)
        n_i = PEER_EB // PEER_NKEYS
        per_tile = PEER_MT // PEER_NKEYS

        def weighted(tok, kc):
            tiles = []
            for ii in range(kc * per_tile, (kc + 1) * per_tile):
                key1 = blk * n_i + ii
                w = jnp.zeros((PEER_NKEYS, PEER_MT), BF16)
                for h in range(PEER_HEADS):
                    cnt = cnt1_ref[h, pl.ds(key1, 1), tok].astype(BF16)
                    g1 = e1_ref[h, pl.ds(key1, 1), tok].astype(BF16)
                    w = w + jnp.where(rank2_ref[h, :, tok] < cnt, e2_ref[h, :, tok] * g1, 0.0)
                tiles.append(w * g_prev_ref[ii * PEER_NKEYS:(ii + 1) * PEER_NKEYS, tok])
            return jnp.concatenate(tiles, axis=0)

        n_kc = PEER_EB // PEER_MT
        for tc in range(PEER_TT // PEER_MT):
            tok = slice(tc * PEER_MT, (tc + 1) * PEER_MT)
            if activate:
                act = _dot(u_ref[...], ht_ref[:, tok])
            if not apply:
                g_next_ref[:, tok] = _gelu_tanh(act).astype(BF16)
                continue
            p = [weighted(tok, 0), weighted(tok, 1)]
            acc = None
            for kc in range(n_kc):
                if kc + 2 < n_kc:
                    p.append(weighted(tok, kc + 2))
                t = _dot(vt_ref[:, kc * PEER_MT:(kc + 1) * PEER_MT], p[kc])
                acc = t if acc is None else acc + t
                if activate and kc == n_kc - 2:
                    g_next_ref[:, tok] = _gelu_tanh(act).astype(BF16)
            acc_ref[:, tok] += acc

    assert PEER_NE % 2 == 0
    inner = jnp.logical_and(e > 0, e < PEER_NE)

    @pl.when(e == 0)
    def _():
        acc_ref[...] = jnp.zeros_like(acc_ref)
        step(None, ga_ref, apply=False)

    @pl.when(jnp.logical_and(inner, e % 2 == 0))
    def _():
        step(gb_ref, ga_ref)

    @pl.when(e % 2 == 1)
    def _():
        step(ga_ref, gb_ref)

    @pl.when(e == PEER_NE)
    def _():
        step(gb_ref, None, activate=False)
        o_ref[...] = x_ref[...] + mod_ref[5:6, :] * acc_ref[...].T


def _peer(ht, u_tab, vt_tab, routing, x1, mod4, l):
    rank2, e2, cnt1, e1 = routing
    once = pl.Buffered(1)
    rspec = pl.BlockSpec((PEER_HEADS, PEER_NKEYS, PEER_TT), lambda i, e: (0, 0, i), pipeline_mode=once)
    return pl.pallas_call(
        _peer_kernel,
        grid=(NT // PEER_TT, PEER_NE + 1),
        in_specs=[
            pl.BlockSpec((D_MODEL, PEER_TT), lambda i, e: (0, i), pipeline_mode=once),
            pl.BlockSpec((None, PEER_EB, D_MODEL), lambda i, e: (l, jnp.minimum(e, PEER_NE - 1), 0)),
            pl.BlockSpec((None, D_MODEL, PEER_EB), lambda i, e: (l, 0, jnp.maximum(e - 1, 0))),
            rspec, rspec, rspec, rspec,
            pl.BlockSpec((PEER_TT, D_MODEL), lambda i, e: (i, 0), pipeline_mode=once),
            pl.BlockSpec((None, None, 6, D_MODEL), lambda i, e: (l, _mod_set(i * PEER_TT), 0, 0)),
        ],
        out_specs=pl.BlockSpec((PEER_TT, D_MODEL), lambda i, e: (i, 0)),
        out_shape=jax.ShapeDtypeStruct((NT, D_MODEL), F32),
        scratch_shapes=[pltpu.VMEM((PEER_EB, PEER_TT), BF16), pltpu.VMEM((PEER_EB, PEER_TT), BF16),
                        pltpu.VMEM((D_MODEL, PEER_TT), F32)],
        compiler_params=_cparams("parallel", "arbitrary"),
        name="peer_experts",
    )(ht, u_tab, vt_tab, rank2, e2, cnt1, e1, x1, mod4)


FIN_TM = 512


def _final_kernel(x_ref, g_ref, o_ref):
    o_ref[...] = _rms(x_ref[...], g_ref[...])


def _final_norm(x, g):
    return pl.pallas_call(
        _final_kernel,
        grid=(NT // FIN_TM,),
        in_specs=[pl.BlockSpec((FIN_TM, D_MODEL), lambda i: (i, 0)),
                  pl.BlockSpec((1, D_MODEL), lambda i: (0, 0))],
        out_specs=pl.BlockSpec((FIN_TM, D_MODEL), lambda i: (i, 0)),
        out_shape=jax.ShapeDtypeStruct((NT, D_MODEL), F32),
        compiler_params=_cparams("parallel"),
        name="final_norm",
    )(x, g)


def _permute_w_in(w_in):
    sizes = (GROUP_W, GROUP_W, GROUP_W, GROUP_W, 4 * ML_HEADS, MLA_Q_RANK, MLA_KV_RANK, MLA_D_ROPE,
             SWA_HEADS * SWA_DH, SWA_KV_HEADS * SWA_DH, SWA_KV_HEADS * SWA_DH, GROUP_W, GROUP_W, GROUP_W)
    offs = [0]
    for s in sizes:
        offs.append(offs[-1] + s)
    part = lambda i: w_in[:, :, offs[i]:offs[i + 1]]
    order = [0, 1, 2, 3, 5, 6, 8, 9, 10, 11, 12, 13, 7, 4]
    cols = [part(i) for i in order]
    used = sum(sizes)
    cols.append(jnp.zeros(w_in.shape[:2] + (PROJ_W - used,), w_in.dtype))
    return jnp.concatenate(cols, axis=-1).astype(BF16)


def _permute_w_uq(w_uq):
    w = w_uq.reshape(DEPTH, MLA_Q_RANK, MLA_HEADS, MLA_D_NOPE + MLA_D_ROPE)
    nope = w[..., :MLA_D_NOPE].reshape(DEPTH, MLA_Q_RANK, MLA_HEADS * MLA_D_NOPE)
    rope = w[..., MLA_D_NOPE:].reshape(DEPTH, MLA_Q_RANK, MLA_HEADS * MLA_D_ROPE)
    return jnp.concatenate([nope, rope], axis=-1).astype(BF16)


def kernel(x_prompt, x_sample, c, cache_mla_ckv, cache_mla_krope, cache_swa_k, cache_swa_v, cache_diff_k,
           cache_diff_v, state_mlstm_C, state_mlstm_n, state_mlstm_m, c_ctx, w_mod, b_mod, norm1_g, w_in,
           mlstm_i_bias, mlstm_f_bias, mlstm_norm_g, mla_qnorm_g, mla_w_uq, mla_kvnorm_g, mla_w_ukv, swa_sink,
           diff_lq1, diff_lk1, diff_lq2, diff_lk2, diff_norm_g, w_out, norm2_g, peer_w_q, peer_sub_keys,
           peer_u, peer_v, final_norm_g):
    x = jnp.concatenate([x_prompt.reshape(N_CTX, D_MODEL), x_sample.reshape(N_LAT, D_MODEL)], axis=0)

    cvec = jnp.concatenate([c_ctx[None, :], c], axis=0)
    cvec_t = jnp.pad(cvec.T, ((0, 0), (0, 8 - N_SETS)))
    mod4 = _modulation(cvec_t, w_mod, b_mod).reshape(DEPTH, 8, 6, D_MODEL)

    w_in_p = _permute_w_in(w_in)
    w_uq_p = _permute_w_uq(mla_w_uq)
    w_ukv_b = mla_w_ukv.astype(BF16)
    w_out_b = w_out.astype(BF16)
    w_q_b = peer_w_q.astype(BF16)
    keys_b = peer_sub_keys.astype(BF16)
    u_b = peer_u.astype(BF16)
    vt_b = jnp.swapaxes(peer_v, 1, 2).astype(BF16)

    vec3 = lambda a: a.reshape(DEPTH, 1, a.shape[-1])
    norm1_3, norm2_3 = vec3(norm1_g), vec3(norm2_g)
    mlng_3, qg_3, kg_3, dng_3 = vec3(mlstm_norm_g), vec3(mla_qnorm_g), vec3(mla_kvnorm_g), vec3(diff_norm_g)
    lq1_3, lk1_3, lq2_3, lk2_3 = vec3(diff_lq1), vec3(diff_lk1), vec3(diff_lq2), vec3(diff_lk2)

    n_chain = 2 * ML_HEADS
    st_c = state_mlstm_C.reshape(DEC_BATCH, DEPTH, n_chain, ML_DH, ML_DH)
    st_n = state_mlstm_n.reshape(DEC_BATCH, DEPTH, n_chain, ML_DH)
    st_m = jnp.broadcast_to(state_mlstm_m.reshape(DEC_BATCH, DEPTH, n_chain, 1), (DEC_BATCH, DEPTH, n_chain, ML_DH))
    swk_c = cache_swa_k.reshape(DEC_BATCH, DEPTH, PAST_LEN, SWA_KV_HEADS * SWA_DH)
    swv_c = cache_swa_v.reshape(DEC_BATCH, DEPTH, PAST_LEN, SWA_KV_HEADS * SWA_DH)
    dfk_c = cache_diff_k.reshape(DEC_BATCH, DEPTH, PAST_LEN, GROUP_W)
    dfv_c = cache_diff_v.reshape(DEC_BATCH, DEPTH, PAST_LEN, GROUP_W)

    cos64, sin64 = _rope_tables(DEC_SEQ, 64)
    cos128, sin128 = _rope_tables(DEC_SEQ, 128)

    outs = [[] for _ in range(9)]
    for l in range(DEPTH):
        proj = _project(x, mod4, norm1_3, w_in_p, l)

        gates = proj[:, C_TAIL + TAIL_G:C_TAIL + TAIL_G + 16]
        gt = gates.reshape(NT // ML_CHUNK, ML_CHUNK, 16).transpose(0, 2, 1)
        bias = jnp.concatenate([mlstm_i_bias[l].reshape(-1), mlstm_f_bias[l].reshape(-1)])
        bcol = bias.reshape(16, 1)
        brow = jnp.zeros((1, 128), F32).at[0, TAIL_G:TAIL_G + 16].set(bias)

        ml_ctx, c_st, n_st, m_st = _mlstm(proj, gt, bcol, brow, mlng_3, l, latent=False)
        ml_lat = _mlstm(proj, gt, bcol, brow, mlng_3, l, latent=True, states=(st_c, st_n, st_m))
        mla_ctx, ckv_n = _mla_ctx(proj, qg_3, kg_3, w_uq_p, w_ukv_b, l)
        mla_lat = _mla_lat(proj, qg_3, kg_3, w_uq_p, w_ukv_b, cache_mla_ckv, cache_mla_krope, cos64, sin64, l)
        sink = swa_sink[l].reshape(1, SWA_HEADS)
        swa_ctx = _swa_ctx(proj, sink, l)
        swa_lat = _swa_lat(proj, sink, swk_c, swv_c, cos128, sin128, l)
        df_ctx = _diff_ctx(proj, lq1_3, lk1_3, lq2_3, lk2_3, dng_3, l)
        df_lat = _diff_lat(proj, lq1_3, lk1_3, lq2_3, lk2_3, dng_3, dfk_c, dfv_c, cos64, sin64, l)

        x1 = _out_proj((ml_ctx, mla_ctx, swa_ctx, df_ctx), (ml_lat, mla_lat, swa_lat, df_lat), x, mod4, w_out_b, l)
        h2t, q = _peer_query(x1, mod4, norm2_3, w_q_b, l)
        routing = _peer_route(q, keys_b, l)
        x = _peer(h2t, u_b, vt_b, routing, x1, mod4, l)

        pc = proj[:N_CTX]
        outs[0].append(ckv_n.reshape(BATCH, SEQ, MLA_KV_RANK))
        outs[1].append(pc[:, C_TAIL:C_TAIL + MLA_D_ROPE].reshape(BATCH, SEQ, MLA_D_ROPE))
        outs[2].append(pc[:, C_SK:C_SK + 256].reshape(BATCH, SEQ, SWA_KV_HEADS, SWA_DH))
        outs[3].append(pc[:, C_SV:C_SV + 256].reshape(BATCH, SEQ, SWA_KV_HEADS, SWA_DH))
        outs[4].append(pc[:, C_DK:C_DK + 512].reshape(BATCH, SEQ, DIFF_HEADS, 2 * DIFF_DH))
        outs[5].append(pc[:, C_DV:C_DV + 512].reshape(BATCH, SEQ, DIFF_HEADS, 2 * DIFF_DH))
        outs[6].append(c_st.reshape(BATCH, 2, ML_HEADS, ML_DH, ML_DH))
        outs[7].append(n_st.reshape(BATCH, 2, ML_HEADS, ML_DH))
        outs[8].append(m_st[:, :, 0].reshape(BATCH, 2, ML_HEADS))

    y = _final_norm(x, final_norm_g.reshape(1, D_MODEL))
    y_prompt = y[:N_CTX].reshape(BATCH, SEQ, D_MODEL)
    y_sample = y[N_CTX:].reshape(DEC_BATCH, DEC_SEQ, D_MODEL)
    return (y_prompt, y_sample) + tuple(jnp.stack(o, axis=1) for o in outs)
```

```python
import functools
import math

import jax
import jax.numpy as jnp
from jax import lax
from jax.experimental import pallas as pl
from jax.experimental.pallas import tpu as pltpu

F32 = jnp.float32
BF16 = jnp.bfloat16

D_MODEL = 2048
BATCH = 32
SEQ = 256
DEPTH = 4
DEC_BATCH = 2
DEC_SEQ = 1024
PAST_LEN = 256
GRID_W = 64
GROUP_W = D_MODEL // 4
ML_HEADS = 4
ML_DH = GROUP_W // ML_HEADS
ML_CHUNK = 64
MLA_HEADS = 4
MLA_D_NOPE = GROUP_W // MLA_HEADS
MLA_D_ROPE = 64
MLA_Q_RANK = D_MODEL // 8
MLA_KV_RANK = D_MODEL // 8
MLA_SCALE = (MLA_D_NOPE + MLA_D_ROPE) ** -0.5
SWA_HEADS = 4
SWA_KV_HEADS = 2
SWA_DH = GROUP_W // SWA_HEADS
WINDOW = 128
DIFF_HEADS = 4
DIFF_DH = GROUP_W // (2 * DIFF_HEADS)
PEER_HEADS = 8
PEER_QDIM = 256
PEER_NKEYS = 128
PEER_N = PEER_NKEYS * PEER_NKEYS
PEER_TOPK = 16
ROPE_BASE = 10000.0
EPS = 1e-6
NEG = -1e30

N_CTX = BATCH * SEQ
N_LAT = DEC_BATCH * DEC_SEQ
NT = N_CTX + N_LAT
N_SETS = 1 + DEC_BATCH

C_MLQ, C_MLK, C_MLV, C_MLO = 0, 512, 1024, 1536
C_CQ, C_CKV = 2048, 2304
C_SQ, C_SK, C_SV = 2560, 3072, 3328
C_DQ, C_DK, C_DV = 3584, 4096, 4608
C_TAIL = 5120
TAIL_G = 64
PROJ_W = 5376
PROJ_TN = 1792

VMEM_LIMIT = 56 * 1024 * 1024


def _cparams(*sem):
    return pltpu.CompilerParams(dimension_semantics=sem, vmem_limit_bytes=VMEM_LIMIT)


def _mod_set(row_start):
    return jnp.where(row_start >= N_CTX, (row_start - N_CTX) // DEC_SEQ + 1, 0)


def _rms(x, g):
    return x * lax.rsqrt(jnp.mean(x * x, axis=-1, keepdims=True) + EPS) * g


def _dot(a, b):
    return jnp.dot(a, b, preferred_element_type=F32)


def _dot_nt(a, b):
    return lax.dot_general(a, b, (((1,), (1,)), ((), ())), preferred_element_type=F32)


def _dot_tn(a, b):
    return lax.dot_general(a, b, (((0,), (0,)), ((), ())), preferred_element_type=F32)


def _dot_hi(a, b):
    return jnp.dot(a, b, preferred_element_type=F32, precision=lax.Precision.HIGHEST)


MOD_TK = 512
MOD_TN = 2048


def _mod_kernel(c_ref, w_ref, b_ref, o_ref, acc_ref):
    k = pl.program_id(2)

    @pl.when(k == 0)
    def _():
        acc_ref[...] = jnp.zeros_like(acc_ref)

    w = w_ref[...]
    for r in range(N_SETS):
        cv = c_ref[:, r:r + 1]
        sv = cv * jax.nn.sigmoid(cv)
        acc_ref[r] += (sv * w).reshape(MOD_TK // 8, 8, MOD_TN).sum(axis=0)

    @pl.when(k == pl.num_programs(2) - 1)
    def _():
        o_ref[...] = jnp.zeros_like(o_ref)
        for r in range(N_SETS):
            o_ref[r:r + 1, :] = acc_ref[r].sum(axis=0, keepdims=True) + b_ref[...]


def _modulation(cvec_t, w_mod, b_mod):
    return pl.pallas_call(
        _mod_kernel,
        grid=(DEPTH, 6 * D_MODEL // MOD_TN, D_MODEL // MOD_TK),
        in_specs=[
            pl.BlockSpec((MOD_TK, 8), lambda l, n, k: (k, 0)),
            pl.BlockSpec((None, MOD_TK, MOD_TN), lambda l, n, k: (l, k, n)),
            pl.BlockSpec((None, 1, MOD_TN), lambda l, n, k: (l, 0, n)),
        ],
        out_specs=pl.BlockSpec((None, 8, MOD_TN), lambda l, n, k: (l, 0, n)),
        out_shape=jax.ShapeDtypeStruct((DEPTH, 8, 6 * D_MODEL), F32),
        scratch_shapes=[pltpu.VMEM((N_SETS, 8, MOD_TN), F32)],
        compiler_params=_cparams("parallel", "parallel", "arbitrary"),
        name="modulation",
    )(cvec_t, w_mod, b_mod.reshape(DEPTH, 1, 6 * D_MODEL))


PROJ_TM = 1024


def _proj_kernel(x_ref, m_ref, g_ref, w_ref, o_ref, h_ref):
    @pl.when(pl.program_id(1) == 0)
    def _():
        h = _rms(x_ref[...], g_ref[...]) * (1.0 + m_ref[1:2, :]) + m_ref[0:1, :]
        h_ref[...] = h.astype(BF16)

    o_ref[...] = _dot(h_ref[...], w_ref[...])


def _project(x, mod4, norm_g, w_in_p, l):
    return pl.pallas_call(
        _proj_kernel,
        grid=(NT // PROJ_TM, PROJ_W // PROJ_TN),
        in_specs=[
            pl.BlockSpec((PROJ_TM, D_MODEL), lambda i, j: (i, 0), pipeline_mode=pl.Buffered(1)),
            pl.BlockSpec((None, None, 6, D_MODEL), lambda i, j: (l, _mod_set(i * PROJ_TM), 0, 0)),
            pl.BlockSpec((None, 1, D_MODEL), lambda i, j: (l, 0, 0)),
            pl.BlockSpec((None, D_MODEL, PROJ_TN), lambda i, j: (l, 0, j)),
        ],
        out_specs=pl.BlockSpec((PROJ_TM, PROJ_TN), lambda i, j: (i, j)),
        out_shape=jax.ShapeDtypeStruct((NT, PROJ_W), F32),
        scratch_shapes=[pltpu.VMEM((PROJ_TM, D_MODEL), BF16)],
        compiler_params=_cparams("parallel", "arbitrary"),
        name="adaln_in_proj",
    )(x, mod4, norm_g, w_in_p)


def _rope(x, cos, sin, quarter):
    width = x.shape[-1]
    lane = lax.broadcasted_iota(jnp.int32, x.shape, 1)
    first = (lane % (2 * quarter)) < quarter
    partner = jnp.where(first, pltpu.roll(x, width - quarter, 1), pltpu.roll(x, quarter, 1))
    return x * cos + partner * sin


def _rope_tables(n_tok, rot_dim):
    half = rot_dim // 2
    pos = jnp.arange(n_tok)
    row = (pos // GRID_W).astype(F32)
    col = (pos % GRID_W).astype(F32)
    inv = ROPE_BASE ** (-jnp.arange(0, half, 2, dtype=F32) / half)
    a_row = row[:, None] * inv[None, :]
    a_col = col[:, None] * inv[None, :]
    ang = jnp.concatenate([a_row, a_row, a_col, a_col], axis=-1)
    sign = jnp.tile(jnp.concatenate([-jnp.ones(half // 2, F32), jnp.ones(half // 2, F32)]), 2)
    return jnp.cos(ang), jnp.sin(ang) * sign[None, :]


def _log_sigmoid(x):
    return jnp.minimum(x, 0.0) - jnp.log(1.0 + jnp.exp(-jnp.abs(x)))


def _mlstm_kernel(*refs, n_tok, n_seq, has_state):
    if has_state:
        (q_ref, k_ref, v_ref, o_ref, tail_ref, gt_ref, bcol_ref, brow_ref, ng_ref, c0_ref, n0_ref, m0_ref,
         out_ref, hf_ref, hb_ref, cst_ref, nst_ref, mst_ref) = refs
    else:
        (q_ref, k_ref, v_ref, o_ref, tail_ref, gt_ref, bcol_ref, brow_ref, ng_ref,
         out_ref, cs_ref, ns_ref, ms_ref, hf_ref, hb_ref, cst_ref, nst_ref, mst_ref) = refs
    n_chunks = n_tok // ML_CHUNK
    scale = ML_DH ** -0.5

    if has_state:
        cst_ref[...] = c0_ref[...]
        nst_ref[...] = n0_ref[...]
        mst_ref[...] = m0_ref[...]
    else:
        cst_ref[...] = jnp.zeros_like(cst_ref)
        nst_ref[...] = jnp.zeros_like(nst_ref)
        mst_ref[...] = jnp.zeros_like(mst_ref)

    row = lax.broadcasted_iota(jnp.int32, (ML_CHUNK, ML_CHUNK), 0)
    col = lax.broadcasted_iota(jnp.int32, (ML_CHUNK, ML_CHUNK), 1)
    lower = (col <= row)
    upper = (col >= row)
    lower_f = lower.astype(F32)
    upper_f = upper.astype(F32)

    def scan_group(c, group):
        ch = []
        for b, d in group:
            cc = c if d == 0 else n_chunks - 1 - c
            t0 = pl.multiple_of(b * n_tok + cc * ML_CHUNK, ML_CHUNK)
            rows = pl.ds(t0, ML_CHUNK)
            g_col = tail_ref[rows, :] + brow_ref[...]
            g_row = gt_ref[b * n_chunks + cc] + bcol_ref[...]
            lf_col = _log_sigmoid(g_col)
            lf_row = _log_sigmoid(g_row)
            if d == 0:
                bcum_col = _dot_hi(lower_f, lf_col)
                bcum_row = _dot_hi(lf_row, upper_f)
                mask = lower
            else:
                bcum_col = _dot_hi(upper_f, lf_col)
                bcum_row = _dot_hi(lf_row, lower_f)
                mask = upper
            for h in range(ML_HEADS):
                r = d * ML_HEADS + h
                fr = 2 * ML_HEADS + r
                ch.append(dict(
                    b=b, d=d, r=r, rows=rows, lanes=slice(h * ML_DH, (h + 1) * ML_DH), mask=mask,
                    ig_c=g_col[:, TAIL_G + r:TAIL_G + r + 1],
                    b_c=bcum_col[:, TAIL_G + fr:TAIL_G + fr + 1],
                    ig_r=g_row[r:r + 1, :],
                    b_r=bcum_row[fr:fr + 1, :],
                    tot=jnp.sum(lf_row[fr:fr + 1, :], axis=1, keepdims=True)))
        n = range(len(ch))
        m_prev = [mst_ref[x["b"], x["r"]:x["r"] + 1, 0:1] for x in ch]
        n_prev = [nst_ref[x["b"], x["r"]:x["r"] + 1, :] for x in ch]
        c_prev = [cst_ref[x["b"], x["r"]] for x in ch]
        q = [q_ref[x["rows"], x["lanes"]] for x in ch]
        k = [k_ref[x["rows"], x["lanes"]] for x in ch]
        v = [v_ref[x["rows"], x["lanes"]].astype(BF16) for x in ch]
        qb = [x.astype(BF16) for x in q]

        dmat = [jnp.where(ch[i]["mask"], ch[i]["b_c"] - ch[i]["b_r"] + ch[i]["ig_r"], NEG) for i in n]
        inter = [ch[i]["b_c"] + m_prev[i] for i in n]
        mt = [jnp.maximum(inter[i], jnp.max(dmat[i], axis=1, keepdims=True)) for i in n]
        w = [jnp.exp(dmat[i] - mt[i]) for i in n]
        qk = [_dot_nt(qb[i], k[i].astype(BF16)) for i in n]
        qc = [_dot(qb[i], c_prev[i].astype(BF16)) for i in n]
        s = [qk[i] * scale * w[i] for i in n]
        a = [jnp.exp(inter[i] - mt[i]) for i in n]
        sv = [_dot(s[i].astype(BF16), v[i]) for i in n]
        den = [jnp.sum(s[i], axis=1, keepdims=True) + a[i] * jnp.sum(q[i] * n_prev[i], axis=1, keepdims=True)
               for i in n]
        hc = [(sv[i] + a[i] * qc[i]) / jnp.maximum(jnp.abs(den[i]), jnp.exp(-mt[i])) for i in n]
        for i in n:
            (hf_ref if ch[i]["d"] == 0 else hb_ref)[ch[i]["rows"], ch[i]["lanes"]] = hc[i]

        wlog_c = [ch[i]["tot"] - ch[i]["b_c"] + ch[i]["ig_c"] for i in n]
        wlog_r = [ch[i]["tot"] - ch[i]["b_r"] + ch[i]["ig_r"] for i in n]
        m_new = [jnp.maximum(ch[i]["tot"] + m_prev[i], jnp.max(wlog_r[i], axis=1, keepdims=True)) for i in n]
        decay = [jnp.exp(ch[i]["tot"] + m_prev[i] - m_new[i]) for i in n]
        kw = [k[i] * (scale * jnp.exp(wlog_c[i] - m_new[i])) for i in n]
        kv = [_dot_tn(kw[i].astype(BF16), v[i]) for i in n]
        for i in n:
            b, r = ch[i]["b"], ch[i]["r"]
            cst_ref[b, r] = decay[i] * c_prev[i] + kv[i]
            nst_ref[b, r:r + 1, :] = decay[i] * n_prev[i] + jnp.sum(kw[i], axis=0, keepdims=True)
            mst_ref[b, r:r + 1, :] = jnp.broadcast_to(m_new[i], (1, ML_DH))

    def chunk_step(c, carry):
        for b in range(n_seq):
            scan_group(c, [(b, 0)])
            scan_group(c, [(b, 1)])
        return carry

    lax.fori_loop(0, n_chunks, chunk_step, 0)

    for h in range(ML_HEADS):
        lanes = slice(h * ML_DH, (h + 1) * ML_DH)
        hs = hf_ref[:, lanes] + hb_ref[:, lanes]
        out_ref[:, lanes] = _rms(hs, ng_ref[:, lanes]) * jax.nn.sigmoid(o_ref[:, lanes])

    if not has_state:
        cs_ref[...] = cst_ref[...]
        ns_ref[...] = nst_ref[...]
        ms_ref[...] = mst_ref[...]


ML_SEQ_PER_STEP = 4


def _mlstm(proj, gt, bcol, brow, norm_g, l, *, latent, states=None):
    n_tok = DEC_SEQ if latent else SEQ
    n_seq = 1 if latent else ML_SEQ_PER_STEP
    n_b = DEC_BATCH if latent else BATCH
    rows = n_seq * n_tok
    blk0 = N_CTX // rows if latent else 0
    n_chunks = n_tok // ML_CHUNK

    def col_spec(c0):
        return pl.BlockSpec((rows, 512), lambda b: (blk0 + b, c0 // 512))

    in_specs = [
        col_spec(C_MLQ), col_spec(C_MLK), col_spec(C_MLV), col_spec(C_MLO),
        pl.BlockSpec((rows, 128), lambda b: (blk0 + b, C_TAIL // 128)),
        pl.BlockSpec((n_seq * n_chunks, 16, ML_CHUNK), lambda b: (blk0 + b, 0, 0)),
        pl.BlockSpec((16, 1), lambda b: (0, 0)),
        pl.BlockSpec((1, 128), lambda b: (0, 0)),
        pl.BlockSpec((None, 1, GROUP_W), lambda b: (l, 0, 0)),
    ]
    args = [proj, proj, proj, proj, proj, gt, bcol, brow, norm_g]
    n_chain = 2 * ML_HEADS
    scratch = [pltpu.VMEM((rows, GROUP_W), F32), pltpu.VMEM((rows, GROUP_W), F32),
               pltpu.VMEM((n_seq, n_chain, ML_DH, ML_DH), F32), pltpu.VMEM((n_seq, n_chain, ML_DH), F32),
               pltpu.VMEM((n_seq, n_chain, ML_DH), F32)]
    if latent:
        c0, n0, m0 = states
        in_specs += [
            pl.BlockSpec((n_seq, None, n_chain, ML_DH, ML_DH), lambda b: (b, l, 0, 0, 0)),
            pl.BlockSpec((n_seq, None, n_chain, ML_DH), lambda b: (b, l, 0, 0)),
            pl.BlockSpec((n_seq, None, n_chain, ML_DH), lambda b: (b, l, 0, 0)),
        ]
        args += [c0, n0, m0]
        out_shape = jax.ShapeDtypeStruct((N_LAT, GROUP_W), F32)
        out_specs = pl.BlockSpec((rows, GROUP_W), lambda b: (b, 0))
    else:
        out_shape = (jax.ShapeDtypeStruct((N_CTX, GROUP_W), F32),
                     jax.ShapeDtypeStruct((BATCH, n_chain, ML_DH, ML_DH), F32),
                     jax.ShapeDtypeStruct((BATCH, n_chain, ML_DH), F32),
                     jax.ShapeDtypeStruct((BATCH, n_chain, ML_DH), F32))
        out_specs = (pl.BlockSpec((rows, GROUP_W), lambda b: (b, 0)),
                     pl.BlockSpec((n_seq, n_chain, ML_DH, ML_DH), lambda b: (b, 0, 0, 0)),
                     pl.BlockSpec((n_seq, n_chain, ML_DH), lambda b: (b, 0, 0)),
                     pl.BlockSpec((n_seq, n_chain, ML_DH), lambda b: (b, 0, 0)))
    return pl.pallas_call(
        functools.partial(_mlstm_kernel, n_tok=n_tok, n_seq=n_seq, has_state=latent),
        grid=(n_b // n_seq,),
        in_specs=in_specs,
        out_specs=out_specs,
        out_shape=out_shape,
        scratch_shapes=scratch,
        compiler_params=_cparams("parallel"),
        name="mlstm_latent" if latent else "mlstm_context",
    )(*args)


def _softmax_parts(scores, sink=None):
    m = jnp.max(scores[0], axis=1, keepdims=True)
    for s in scores[1:]:
        m = jnp.maximum(m, jnp.max(s, axis=1, keepdims=True))
    if sink is not None:
        m = jnp.maximum(m, sink)
    es = [jnp.exp(s - m) for s in scores]
    den = jnp.sum(es[0], axis=1, keepdims=True)
    for e in es[1:]:
        den = den + jnp.sum(e, axis=1, keepdims=True)
    if sink is not None:
        den = den + jnp.exp(sink - m)
    return es, den


def _softmax_parts_all(score_lists, sinks=None):
    n = range(len(score_lists))
    sinks = [None] * len(score_lists) if sinks is None else sinks
    ms = []
    for i in n:
        m = jnp.max(score_lists[i][0], axis=1, keepdims=True)
        for s in score_lists[i][1:]:
            m = jnp.maximum(m, jnp.max(s, axis=1, keepdims=True))
        ms.append(m if sinks[i] is None else jnp.maximum(m, sinks[i]))
    es = [[jnp.exp(s - ms[i]) for s in score_lists[i]] for i in n]
    dens = []
    for i in n:
        den = jnp.sum(es[i][0], axis=1, keepdims=True)
        for e in es[i][1:]:
            den = den + jnp.sum(e, axis=1, keepdims=True)
        dens.append(den if sinks[i] is None else den + jnp.exp(sinks[i] - ms[i]))
    return es, dens


ATT_TQ = 256
LAT_QB = DEC_SEQ // ATT_TQ
N_KEYS_LAT = DEC_SEQ + PAST_LEN


def _mla_q(cq_ref, qg_ref, wuq_ref):
    return _dot(_rms(cq_ref[...], qg_ref[...]).astype(BF16), wuq_ref[...])


def _mla_heads(q, kv, kr, n_keys):
    hs = range(MLA_HEADS)
    r0 = MLA_HEADS * MLA_D_NOPE
    qn = [q[:, h * MLA_D_NOPE:(h + 1) * MLA_D_NOPE].astype(BF16) for h in hs]
    qr = [q[:, r0 + h * MLA_D_ROPE:r0 + (h + 1) * MLA_D_ROPE].astype(BF16) for h in hs]
    kn = [kv[:, h * 256:h * 256 + MLA_D_NOPE] for h in hs]
    v = [kv[:, h * 256 + MLA_D_NOPE:(h + 1) * 256] for h in hs]
    s = [(_dot_nt(qn[h], kn[h]) + _dot_nt(qr[h], kr)) * MLA_SCALE for h in hs]
    es, dens = _softmax_parts_all([[s[h]] for h in hs])
    pv = [_dot(es[h][0].astype(BF16), v[h]) for h in hs]
    return [pv[h] / dens[h] for h in hs]


def _mla_ctx_kernel(cq_ref, ckv_ref, tail_ref, qg_ref, kg_ref, wuq_ref, wukv_ref, out_ref, ckvn_ref):
    q = _mla_q(cq_ref, qg_ref, wuq_ref)
    ckvn = _rms(ckv_ref[...], kg_ref[...])
    ckvn_ref[...] = ckvn
    kv = _dot(ckvn.astype(BF16), wukv_ref[...]).astype(BF16)
    kr = tail_ref[:, 0:MLA_D_ROPE].astype(BF16)
    outs = _mla_heads(q, kv, kr, SEQ)
    for h in range(MLA_HEADS):
        out_ref[:, h * MLA_D_V:(h + 1) * MLA_D_V] = outs[h]


MLA_D_V = GROUP_W // MLA_HEADS


def _mla_lat_kernel(cq_ref, ckv_ref, tail_ref, qg_ref, kg_ref, wuq_ref, wukv_ref, ckvc_ref, krc_ref,
                    cosq_ref, sinq_ref, cosk_ref, sink_ref, out_ref, kv_ref, kr_ref):
    @pl.when(pl.program_id(1) == 0)
    def _():
        ckvn = _rms(ckv_ref[...], kg_ref[...])
        kv_ref[0:DEC_SEQ, :] = _dot(ckvn.astype(BF16), wukv_ref[...]).astype(BF16)
        kv_ref[DEC_SEQ:N_KEYS_LAT, :] = _dot(ckvc_ref[...].astype(BF16), wukv_ref[...]).astype(BF16)
        kr = _rope(tail_ref[...], cosk_ref[...], sink_ref[...], MLA_D_ROPE // 4)
        kr_ref[0:DEC_SEQ, :] = kr[:, 0:MLA_D_ROPE].astype(BF16)
        kr_ref[DEC_SEQ:N_KEYS_LAT, :] = krc_ref[...].astype(BF16)

    q = _mla_q(cq_ref, qg_ref, wuq_ref)
    n0 = MLA_HEADS * MLA_D_NOPE
    q_rope = _rope(q[:, n0:], cosq_ref[...], sinq_ref[...], MLA_D_ROPE // 4)
    q = jnp.concatenate([q[:, :n0], q_rope], axis=1)
    outs = _mla_heads(q, kv_ref[...], kr_ref[...], N_KEYS_LAT)
    for h in range(MLA_HEADS):
        out_ref[:, h * MLA_D_V:(h + 1) * MLA_D_V] = outs[h]


def _w_specs2(shape_a, shape_b, l):
    return [pl.BlockSpec((None,) + shape_a, lambda *_: (l,) + (0,) * len(shape_a)),
            pl.BlockSpec((None,) + shape_b, lambda *_: (l,) + (0,) * len(shape_b))]


def _mla_ctx(proj, qg, kg, wuq, wukv, l):
    return pl.pallas_call(
        _mla_ctx_kernel,
        grid=(BATCH,),
        in_specs=[
            pl.BlockSpec((SEQ, 256), lambda b: (b, C_CQ // 256)),
            pl.BlockSpec((SEQ, 256), lambda b: (b, C_CKV // 256)),
            pl.BlockSpec((SEQ, 128), lambda b: (b, C_TAIL // 128)),
            pl.BlockSpec((None, 1, MLA_Q_RANK), lambda b: (l, 0, 0)),
            pl.BlockSpec((None, 1, MLA_KV_RANK), lambda b: (l, 0, 0)),
            pl.BlockSpec((None, MLA_Q_RANK, 768), lambda b: (l, 0, 0)),
            pl.BlockSpec((None, MLA_KV_RANK, 1024), lambda b: (l, 0, 0)),
        ],
        out_specs=(pl.BlockSpec((SEQ, GROUP_W), lambda b: (b, 0)),
                   pl.BlockSpec((SEQ, MLA_KV_RANK), lambda b: (b, 0))),
        out_shape=(jax.ShapeDtypeStruct((N_CTX, GROUP_W), F32),
                   jax.ShapeDtypeStruct((N_CTX, MLA_KV_RANK), F32)),
        compiler_params=_cparams("parallel"),
        name="mla_context",
    )(proj, proj, proj, qg, kg, wuq, wukv)


def _mla_lat(proj, qg, kg, wuq, wukv, cache_ckv, cache_kr, cos64, sin64, l):
    qb0 = N_CTX // ATT_TQ
    bb0 = N_CTX // DEC_SEQ
    cosq = jnp.tile(cos64, (1, MLA_HEADS))
    sinq = jnp.tile(sin64, (1, MLA_HEADS))
    cosk = jnp.tile(cos64, (1, 2))
    sink = jnp.tile(sin64, (1, 2))
    return pl.pallas_call(
        _mla_lat_kernel,
        grid=(DEC_BATCH, LAT_QB),
        in_specs=[
            pl.BlockSpec((ATT_TQ, 256), lambda b, i: (qb0 + b * LAT_QB + i, C_CQ // 256)),
            pl.BlockSpec((DEC_SEQ, 256), lambda b, i: (bb0 + b, C_CKV // 256)),
            pl.BlockSpec((DEC_SEQ, 128), lambda b, i: (bb0 + b, C_TAIL // 128)),
            pl.BlockSpec((None, 1, MLA_Q_RANK), lambda b, i: (l, 0, 0)),
            pl.BlockSpec((None, 1, MLA_KV_RANK), lambda b, i: (l, 0, 0)),
            pl.BlockSpec((None, MLA_Q_RANK, 768), lambda b, i: (l, 0, 0)),
            pl.BlockSpec((None, MLA_KV_RANK, 1024), lambda b, i: (l, 0, 0)),
            pl.BlockSpec((None, None, PAST_LEN, MLA_KV_RANK), lambda b, i: (b, l, 0, 0)),
            pl.BlockSpec((None, None, PAST_LEN, MLA_D_ROPE), lambda b, i: (b, l, 0, 0)),
            pl.BlockSpec((ATT_TQ, 256), lambda b, i: (i, 0)),
            pl.BlockSpec((ATT_TQ, 256), lambda b, i: (i, 0)),
            pl.BlockSpec((DEC_SEQ, 128), lambda b, i: (0, 0)),
            pl.BlockSpec((DEC_SEQ, 128), lambda b, i: (0, 0)),
        ],
        out_specs=pl.BlockSpec((ATT_TQ, GROUP_W), lambda b, i: (b * LAT_QB + i, 0)),
        out_shape=jax.ShapeDtypeStruct((N_LAT, GROUP_W), F32),
        scratch_shapes=[pltpu.VMEM((N_KEYS_LAT, 1024), BF16), pltpu.VMEM((N_KEYS_LAT, MLA_D_ROPE), BF16)],
        compiler_params=_cparams("parallel", "arbitrary"),
        name="mla_latent",
    )(proj, proj, proj, qg, kg, wuq, wukv, cache_ckv, cache_kr, cosq, sinq, cosk, sink)


SWA_SCALE = SWA_DH ** -0.5
SWA_REP = SWA_HEADS // SWA_KV_HEADS
SWA_KWIN = ATT_TQ + 2 * WINDOW


def _swa_ctx_kernel(sink_ref, q_ref, k_ref, v_ref, out_ref):
    kb = k_ref[...].astype(BF16)
    vb = v_ref[...].astype(BF16)
    hs = range(SWA_HEADS)
    gl = [slice((h // SWA_REP) * SWA_DH, (h // SWA_REP + 1) * SWA_DH) for h in hs]
    q = [q_ref[:, h * SWA_DH:(h + 1) * SWA_DH].astype(BF16) for h in hs]
    s = [_dot_nt(q[h], kb[:, gl[h]]) * SWA_SCALE for h in hs]
    es, dens = _softmax_parts_all([[s[h]] for h in hs], sinks=[sink_ref[0, h] for h in hs])
    pv = [_dot(es[h][0].astype(BF16), vb[:, gl[h]]) for h in hs]
    for h in hs:
        out_ref[:, h * SWA_DH:(h + 1) * SWA_DH] = pv[h] / dens[h]


def _swa_lat_kernel(sink_ref, q_ref, k_ref, v_ref, kc_ref, vc_ref, cosq_ref, sinq_ref, cosk_ref, sink_t_ref,
                    out_ref, kr_ref):
    i = pl.program_id(1)

    @pl.when(i == 0)
    def _():
        kr_ref[...] = _rope(k_ref[...], cosk_ref[...], sink_t_ref[...], SWA_DH // 4).astype(BF16)

    q_all = _rope(q_ref[...], cosq_ref[...], sinq_ref[...], SWA_DH // 4)
    k0 = pl.multiple_of(jnp.clip(i * ATT_TQ - WINDOW, 0, DEC_SEQ - SWA_KWIN), WINDOW)
    kwin = kr_ref[pl.ds(k0, SWA_KWIN), :]
    vwin = v_ref[pl.ds(k0, SWA_KWIN), :].astype(BF16)
    kc = kc_ref[...].astype(BF16)
    vc = vc_ref[...].astype(BF16)
    qpos = i * ATT_TQ + lax.broadcasted_iota(jnp.int32, (ATT_TQ, SWA_KWIN), 0)
    kpos = k0 + lax.broadcasted_iota(jnp.int32, (ATT_TQ, SWA_KWIN), 1)
    band = jnp.abs(qpos - kpos) <= WINDOW
    hs = range(SWA_HEADS)
    gl = [slice((h // SWA_REP) * SWA_DH, (h // SWA_REP + 1) * SWA_DH) for h in hs]
    q = [q_all[:, h * SWA_DH:(h + 1) * SWA_DH].astype(BF16) for h in hs]
    s_loc = [jnp.where(band, _dot_nt(q[h], kwin[:, gl[h]]) * SWA_SCALE, NEG) for h in hs]
    s_ctx = [_dot_nt(q[h], kc[:, gl[h]]) * SWA_SCALE for h in hs]
    es, dens = _softmax_parts_all([[s_loc[h], s_ctx[h]] for h in hs], sinks=[sink_ref[0, h] for h in hs])
    o = [_dot(es[h][0].astype(BF16), vwin[:, gl[h]]) + _dot(es[h][1].astype(BF16), vc[:, gl[h]]) for h in hs]
    for h in hs:
        out_ref[:, h * SWA_DH:(h + 1) * SWA_DH] = o[h] / dens[h]


def _smem_spec():
    return pl.BlockSpec(memory_space=pltpu.SMEM)


def _swa_ctx(proj, sink, l):
    return pl.pallas_call(
        _swa_ctx_kernel,
        grid=(BATCH,),
        in_specs=[
            _smem_spec(),
            pl.BlockSpec((SEQ, 512), lambda b: (b, C_SQ // 512)),
            pl.BlockSpec((SEQ, 256), lambda b: (b, C_SK // 256)),
            pl.BlockSpec((SEQ, 256), lambda b: (b, C_SV // 256)),
        ],
        out_specs=pl.BlockSpec((SEQ, GROUP_W), lambda b: (b, 0)),
        out_shape=jax.ShapeDtypeStruct((N_CTX, GROUP_W), F32),
        compiler_params=_cparams("parallel"),
        name="swa_context",
    )(sink, proj, proj, proj)


def _swa_lat(proj, sink, cache_k, cache_v, cos128, sin128, l):
    qb0 = N_CTX // ATT_TQ
    bb0 = N_CTX // DEC_SEQ
    kvw = SWA_KV_HEADS * SWA_DH
    return pl.pallas_call(
        _swa_lat_kernel,
        grid=(DEC_BATCH, LAT_QB),
        in_specs=[
            _smem_spec(),
            pl.BlockSpec((ATT_TQ, 512), lambda b, i: (qb0 + b * LAT_QB + i, C_SQ // 512)),
            pl.BlockSpec((DEC_SEQ, 256), lambda b, i: (bb0 + b, C_SK // 256)),
            pl.BlockSpec((DEC_SEQ, 256), lambda b, i: (bb0 + b, C_SV // 256)),
            pl.BlockSpec((None, None, PAST_LEN, kvw), lambda b, i: (b, l, 0, 0)),
            pl.BlockSpec((None, None, PAST_LEN, kvw), lambda b, i: (b, l, 0, 0)),
            pl.BlockSpec((ATT_TQ, 512), lambda b, i: (i, 0)),
            pl.BlockSpec((ATT_TQ, 512), lambda b, i: (i, 0)),
            pl.BlockSpec((DEC_SEQ, 256), lambda b, i: (0, 0)),
            pl.BlockSpec((DEC_SEQ, 256), lambda b, i: (0, 0)),
        ],
        out_specs=pl.BlockSpec((ATT_TQ, GROUP_W), lambda b, i: (b * LAT_QB + i, 0)),
        out_shape=jax.ShapeDtypeStruct((N_LAT, GROUP_W), F32),
        scratch_shapes=[pltpu.VMEM((DEC_SEQ, kvw), BF16)],
        compiler_params=_cparams("parallel", "arbitrary"),
        name="swa_latent",
    )(sink, proj, proj, proj, cache_k, cache_v,
      jnp.tile(cos128, (1, SWA_HEADS)), jnp.tile(sin128, (1, SWA_HEADS)),
      jnp.tile(cos128, (1, SWA_KV_HEADS)), jnp.tile(sin128, (1, SWA_KV_HEADS)))


DIFF_SCALE = DIFF_DH ** -0.5


def _diff_lambda(lq1_ref, lk1_ref, lq2_ref, lk2_ref):
    a = jnp.sum(lq1_ref[...] * lk1_ref[...], axis=1, keepdims=True)
    b = jnp.sum(lq2_ref[...] * lk2_ref[...], axis=1, keepdims=True)
    return jnp.exp(a) - jnp.exp(b)


def _diff_heads(q, k_parts, v_parts, lam, lam_init, ng_ref, out_ref, heads_per_group):
    for h0 in range(0, DIFF_HEADS, heads_per_group):
        hs = range(h0, h0 + heads_per_group)
        sls = [slice(h * 2 * DIFF_DH + c * DIFF_DH, h * 2 * DIFF_DH + (c + 1) * DIFF_DH)
               for h in hs for c in range(2)]
        qc = [q[:, sl].astype(BF16) for sl in sls]
        scores = [[_dot_nt(qc[j], kp[:, sls[j]]) * DIFF_SCALE for kp in k_parts] for j in range(len(sls))]
        es, dens = _softmax_parts_all(scores)
        ps = [[e / dens[j] for e in es[j]] for j in range(len(sls))]
        outs = []
        for i, h in enumerate(hs):
            vl = slice(h * 2 * DIFF_DH, (h + 1) * 2 * DIFF_DH)
            o = None
            for p1, p2, vp in zip(ps[2 * i], ps[2 * i + 1], v_parts):
                t = _dot((p1 - lam * p2).astype(BF16), vp[:, vl])
                o = t if o is None else o + t
            outs.append((vl, o))
        for vl, o in outs:
            out_ref[:, vl] = _rms(o, ng_ref[...]) * (1.0 - lam_init)


def _diff_ctx_kernel(q_ref, k_ref, v_ref, lq1_ref, lk1_ref, lq2_ref, lk2_ref, ng_ref, out_ref, *, lam_init):
    lam = _diff_lambda(lq1_ref, lk1_ref, lq2_ref, lk2_ref) + lam_init
    _diff_heads(q_ref[...], [k_ref[...].astype(BF16)], [v_ref[...].astype(BF16)], lam, lam_init, ng_ref, out_ref,
                heads_per_group=DIFF_HEADS)


def _diff_lat_kernel(q_ref, k_ref, v_ref, kc_ref, vc_ref, lq1_ref, lk1_ref, lq2_ref, lk2_ref, ng_ref,
                     cosq_ref, sinq_ref, cosk_ref, sink_ref, out_ref, kr_ref, *, lam_init):
    @pl.when(pl.program_id(1) == 0)
    def _():
        kr_ref[...] = _rope(k_ref[...], cosk_ref[...], sink_ref[...], DIFF_DH // 4).astype(BF16)

    lam = _diff_lambda(lq1_ref, lk1_ref, lq2_ref, lk2_ref) + lam_init
    q = _rope(q_ref[...], cosq_ref[...], sinq_ref[...], DIFF_DH // 4)
    _diff_heads(q, [kr_ref[...], kc_ref[...].astype(BF16)], [v_ref[...].astype(BF16), vc_ref[...].astype(BF16)],
                lam, lam_init, ng_ref, out_ref, heads_per_group=2)


def _vec_specs(n, width, l, nargs):
    return [pl.BlockSpec((None, 1, width), lambda *_: (l, 0, 0)) for _ in range(n)]


def _diff_ctx(proj, lq1, lk1, lq2, lk2, ng, l):
    lam_init = 0.8 - 0.6 * math.exp(-0.3 * l)
    return pl.pallas_call(
        functools.partial(_diff_ctx_kernel, lam_init=lam_init),
        grid=(BATCH,),
        in_specs=[
            pl.BlockSpec((SEQ, 512), lambda b: (b, C_DQ // 512)),
            pl.BlockSpec((SEQ, 512), lambda b: (b, C_DK // 512)),
            pl.BlockSpec((SEQ, 512), lambda b: (b, C_DV // 512)),
        ] + _vec_specs(4, DIFF_DH, l, 1) + _vec_specs(1, 2 * DIFF_DH, l, 1),
        out_specs=pl.BlockSpec((SEQ, GROUP_W), lambda b: (b, 0)),
        out_shape=jax.ShapeDtypeStruct((N_CTX, GROUP_W), F32),
        compiler_params=_cparams("parallel"),
        name="diff_context",
    )(proj, proj, proj, lq1, lk1, lq2, lk2, ng)


def _diff_lat(proj, lq1, lk1, lq2, lk2, ng, cache_k, cache_v, cos64, sin64, l):
    lam_init = 0.8 - 0.6 * math.exp(-0.3 * l)
    qb0 = N_CTX // ATT_TQ
    bb0 = N_CTX // DEC_SEQ
    cos_t = jnp.tile(cos64, (1, 2 * DIFF_HEADS))
    sin_t = jnp.tile(sin64, (1, 2 * DIFF_HEADS))
    return pl.pallas_call(
        functools.partial(_diff_lat_kernel, lam_init=lam_init),
        grid=(DEC_BATCH, LAT_QB),
        in_specs=[
            pl.BlockSpec((ATT_TQ, 512), lambda b, i: (qb0 + b * LAT_QB + i, C_DQ // 512)),
            pl.BlockSpec((DEC_SEQ, 512), lambda b, i: (bb0 + b, C_DK // 512)),
            pl.BlockSpec((DEC_SEQ, 512), lambda b, i: (bb0 + b, C_DV // 512)),
            pl.BlockSpec((None, None, PAST_LEN, GROUP_W), lambda b, i: (b, l, 0, 0)),
            pl.BlockSpec((None, None, PAST_LEN, GROUP_W), lambda b, i: (b, l, 0, 0)),
        ] + _vec_specs(4, DIFF_DH, l, 2) + _vec_specs(1, 2 * DIFF_DH, l, 2) + [
            pl.BlockSpec((ATT_TQ, 512), lambda b, i: (i, 0)),
            pl.BlockSpec((ATT_TQ, 512), lambda b, i: (i, 0)),
            pl.BlockSpec((DEC_SEQ, 512), lambda b, i: (0, 0)),
            pl.BlockSpec((DEC_SEQ, 512), lambda b, i: (0, 0)),
        ],
        out_specs=pl.BlockSpec((ATT_TQ, GROUP_W), lambda b, i: (b * LAT_QB + i, 0)),
        out_shape=jax.ShapeDtypeStruct((N_LAT, GROUP_W), F32),
        scratch_shapes=[pltpu.VMEM((DEC_SEQ, GROUP_W), BF16)],
        compiler_params=_cparams("parallel", "arbitrary"),
        name="diff_latent",
    )(proj, proj, proj, cache_k, cache_v, lq1, lk1, lq2, lk2, ng, cos_t, sin_t, cos_t, sin_t)


OUT_TM = 512


OUT_CTX_BLOCKS = N_CTX // OUT_TM


def _out_kernel(*refs):
    ctx_refs, lat_refs = refs[0:4], refs[4:8]
    x_ref, mod_ref, w_ref, x1_ref = refs[8:]
    is_ctx = pl.program_id(0) < OUT_CTX_BLOCKS
    acc = None
    for g in range(4):
        m = jnp.where(is_ctx, ctx_refs[g][...], lat_refs[g][...])
        t = _dot(m.astype(BF16), w_ref[g * GROUP_W:(g + 1) * GROUP_W, :])
        acc = t if acc is None else acc + t
    x1_ref[...] = x_ref[...] + mod_ref[2:3, :] * acc


def _out_proj(mixed_ctx, mixed_lat, x, mod4, w_out, l):
    row = lambda i: (i, 0)
    ctx_row = lambda i: (jnp.minimum(i, OUT_CTX_BLOCKS - 1), 0)
    lat_row = lambda i: (jnp.maximum(i - OUT_CTX_BLOCKS, 0), 0)
    return pl.pallas_call(
        _out_kernel,
        grid=(NT // OUT_TM,),
        in_specs=[pl.BlockSpec((OUT_TM, GROUP_W), ctx_row) for _ in range(4)]
        + [pl.BlockSpec((OUT_TM, GROUP_W), lat_row) for _ in range(4)] + [
            pl.BlockSpec((OUT_TM, D_MODEL), row),
            pl.BlockSpec((None, None, 6, D_MODEL), lambda i: (l, _mod_set(i * OUT_TM), 0, 0)),
            pl.BlockSpec((None, D_MODEL, D_MODEL), lambda i: (l, 0, 0)),
        ],
        out_specs=pl.BlockSpec((OUT_TM, D_MODEL), row),
        out_shape=jax.ShapeDtypeStruct((NT, D_MODEL), F32),
        compiler_params=_cparams("parallel"),
        name="out_proj_residual",
    )(*mixed_ctx, *mixed_lat, x, mod4, w_out)


def _peerq_kernel(x_ref, mod_ref, g_ref, w_ref, ht_ref, q_ref):
    h = _rms(x_ref[...], g_ref[...]) * (1.0 + mod_ref[4:5, :]) + mod_ref[3:4, :]
    ht_ref[...] = h.T.astype(BF16)
    q_ref[...] = _dot(h.astype(BF16), w_ref[...])


def _peer_query(x1, mod4, norm_g, w_q, l):
    row = lambda i: (i, 0)
    return pl.pallas_call(
        _peerq_kernel,
        grid=(NT // OUT_TM,),
        in_specs=[
            pl.BlockSpec((OUT_TM, D_MODEL), row),
            pl.BlockSpec((None, None, 6, D_MODEL), lambda i: (l, _mod_set(i * OUT_TM), 0, 0)),
            pl.BlockSpec((None, 1, D_MODEL), lambda i: (l, 0, 0)),
            pl.BlockSpec((None, D_MODEL, PEER_HEADS * PEER_QDIM), lambda i: (l, 0, 0)),
        ],
        out_specs=(pl.BlockSpec((D_MODEL, OUT_TM), lambda i: (0, i)),
                   pl.BlockSpec((OUT_TM, PEER_HEADS * PEER_QDIM), row)),
        out_shape=(jax.ShapeDtypeStruct((D_MODEL, NT), BF16),
                   jax.ShapeDtypeStruct((NT, PEER_HEADS * PEER_QDIM), F32)),
        compiler_params=_cparams("parallel"),
        name="adaln_peer_query",
    )(x1, mod4, norm_g, w_q)


ROUTE_TL = 256
ROUTE_LANES = 128
NOT_SEL = float(PEER_TOPK)


def _top16(s, index_ties):
    idx = lax.broadcasted_iota(jnp.int32, s.shape, 0).astype(F32)
    slot = lax.broadcasted_iota(jnp.int32, (PEER_TOPK, s.shape[1]), 0)
    rank = jnp.full(s.shape, NOT_SEL, F32)
    vals = jnp.zeros((PEER_TOPK, s.shape[1]), F32)
    for k in range(PEER_TOPK):
        m = jnp.max(s, axis=0, keepdims=True)
        sel = s == m
        if index_ties:
            sel = idx == jnp.min(jnp.where(sel, idx, float(PEER_NKEYS)), axis=0, keepdims=True)
        rank = jnp.where(sel, float(k), rank)
        s = jnp.where(sel, NEG, s)
        vals = jnp.where(slot == k, m, vals)
    return rank, vals


CAND_HALF = PEER_TOPK // 2
CAND_ROWS = PEER_TOPK + (CAND_HALF - 1) * CAND_HALF + CAND_HALF
FLAT_NONE = float(PEER_TOPK * PEER_TOPK)


def _cand_flat(tl):
    r = lax.broadcasted_iota(jnp.int32, (CAND_ROWS, tl), 0)
    mid = r - PEER_TOPK
    mid_flat = (1 + mid // CAND_HALF) * PEER_TOPK + mid % CAND_HALF
    last_flat = (CAND_HALF + r - (CAND_ROWS - CAND_HALF)) * PEER_TOPK
    flat = jnp.where(r < PEER_TOPK, r, jnp.where(r < CAND_ROWS - CAND_HALF, mid_flat, last_flat))
    return flat.astype(F32)


def _count(mask):
    return jnp.sum(jnp.where(mask, 1.0, 0.0), axis=0, keepdims=True)


def _route_head(s1, s2, flat, index_ties):
    tl = s1.shape[1]
    rank1, v1 = _top16(s1, index_ties)
    rank2, v2 = _top16(s2, index_ties)
    slabs = [v1[0:1, :] + v2]
    slabs += [v1[a:a + 1, :] + v2[0:CAND_HALF, :] for a in range(1, CAND_HALF)]
    slabs.append(v1[CAND_HALF:, :] + v2[0:1, :])
    cand = jnp.concatenate(slabs, axis=0)
    a_row = jnp.floor(flat * (1.0 / PEER_TOPK))
    top = v1[0:1, :] + v2[0:1, :]
    cnt1 = jnp.zeros(s1.shape, F32)
    z = jnp.zeros((1, tl), F32)
    for k in range(PEER_TOPK):
        m = jnp.max(cand, axis=0, keepdims=True)
        sel = cand == m
        if index_ties:
            first = jnp.min(jnp.where(sel, flat, FLAT_NONE), axis=0, keepdims=True)
            sel = flat == first
            a_sel = jnp.floor(first * (1.0 / PEER_TOPK))
        else:
            a_sel = jnp.max(jnp.where(sel, a_row, -1.0), axis=0, keepdims=True)
        cand = jnp.where(sel, NEG, cand)
        cnt1 = cnt1 + jnp.where(rank1 == a_sel, 1.0, 0.0)
        z = z + jnp.exp(m - top)
    e1 = jnp.exp(s1 - v1[0:1, :]) / z
    e2 = jnp.exp(s2 - v2[0:1, :])
    if index_ties:
        return rank2, cnt1, e1, e2, None
    full = float(PEER_TOPK)
    clean = ((_count(rank1 < NOT_SEL) == full) & (_count(rank2 < NOT_SEL) == full)
             & (_count(cand == NEG) == full))
    return rank2, cnt1, e1, e2, jnp.max(jnp.where(clean, 0.0, 1.0))


def _route_kernel(q_ref, k1_ref, k2_ref, rank2_ref, e2_ref, cnt1_ref, e1_ref):
    tl = q_ref.shape[0]
    half = PEER_QDIM // 2

    n_chunks = tl // ROUTE_LANES

    def scores(h):
        q1 = q_ref[:, h * PEER_QDIM:h * PEER_QDIM + half].astype(BF16)
        q2 = q_ref[:, h * PEER_QDIM + half:(h + 1) * PEER_QDIM].astype(BF16)
        return _dot_nt(k1_ref[...], q1), _dot_nt(k2_ref[...], q2)

    def route(h, c, s1, s2, index_ties):
        lanes = slice(c * ROUTE_LANES, (c + 1) * ROUTE_LANES)
        rank2, cnt1, e1, e2, tied = _route_head(s1[:, lanes], s2[:, lanes], _cand_flat(ROUTE_LANES), index_ties)
        rank2_ref[h, :, lanes] = rank2.astype(BF16)
        cnt1_ref[h, :, lanes] = cnt1
        e1_ref[h, :, lanes] = e1
        e2_ref[h, :, lanes] = e2.astype(BF16)
        return tied

    tied = {}
    for h in range(PEER_HEADS):
        s1, s2 = scores(h)
        for c in range(n_chunks):
            tied[h, c] = route(h, c, s1, s2, index_ties=False)

    for h in range(PEER_HEADS):
        for c in range(n_chunks):
            @pl.when(tied[h, c] > 0.0)
            def _():
                s1, s2 = scores(h)
                route(h, c, s1, s2, index_ties=True)


def _peer_route(q, keys, l):
    shp = jax.ShapeDtypeStruct((PEER_HEADS, PEER_NKEYS, NT), F32)
    shp_b = jax.ShapeDtypeStruct((PEER_HEADS, PEER_NKEYS, NT), BF16)
    spec = pl.BlockSpec((PEER_HEADS, PEER_NKEYS, ROUTE_TL), lambda i: (0, 0, i))
    half = PEER_QDIM // 2
    return pl.pallas_call(
        _route_kernel,
        grid=(NT // ROUTE_TL,),
        in_specs=[
            pl.BlockSpec((ROUTE_TL, PEER_HEADS * PEER_QDIM), lambda i: (i, 0)),
            pl.BlockSpec((None, None, PEER_NKEYS, half), lambda i: (l, 0, 0, 0)),
            pl.BlockSpec((None, None, PEER_NKEYS, half), lambda i: (l, 1, 0, 0)),
        ],
        out_specs=(spec, spec, spec, spec),
        out_shape=(shp_b, shp_b, shp, shp),
        compiler_params=_cparams("parallel"),
        name="peer_route",
    )(q, keys, keys)


PEER_TT = 1024
PEER_EB = 512
GELU_C = math.sqrt(2.0 / math.pi)


def _gelu_tanh(x):
    return 0.5 * x * (1.0 + jnp.tanh(GELU_C * (x + 0.044715 * (x * x * x))))


PEER_NE = PEER_N // PEER_EB
PEER_MT = 256


def _peer_kernel(ht_ref, u_ref, vt_ref, rank2_ref, e2_ref, cnt1_ref, e1_ref, x_ref, mod_ref, o_ref,
                 ga_ref, gb_ref, acc_ref):
    e = pl.program_id(1)

    def step(g_prev_ref, g_next_ref, activate=True, apply=True):
        blk = jnp.maximum(e - 1, 0)
        n_i = PEER_EB // PEER_NKEYS
        per_tile = PEER_MT // PEER_NKEYS

        def weighted(tok, kc):
            tiles = []
            for ii in range(kc * per_tile, (kc + 1) * per_tile):
                key1 = blk * n_i + ii
                w = jnp.zeros((PEER_NKEYS, PEER_MT), BF16)
                for h in range(PEER_HEADS):
                    cnt = cnt1_ref[h, pl.ds(key1, 1), tok].astype(BF16)
                    g1 = e1_ref[h, pl.ds(key1, 1), tok].astype(BF16)
                    w = w + jnp.where(rank2_ref[h, :, tok] < cnt, e2_ref[h, :, tok] * g1, 0.0)
                tiles.append(w * g_prev_ref[ii * PEER_NKEYS:(ii + 1) * PEER_NKEYS, tok])
            return jnp.concatenate(tiles, axis=0)

        n_kc = PEER_EB // PEER_MT
        for tc in range(PEER_TT // PEER_MT):
            tok = slice(tc * PEER_MT, (tc + 1) * PEER_MT)
            if activate:
                act = _dot(u_ref[...], ht_ref[:, tok])
            if not apply:
                g_next_ref[:, tok] = _gelu_tanh(act).astype(BF16)
                continue
            p = [weighted(tok, 0), weighted(tok, 1)]
            acc = None
            for kc in range(n_kc):
                if kc + 2 < n_kc:
                    p.append(weighted(tok, kc + 2))
                t = _dot(vt_ref[:, kc * PEER_MT:(kc + 1) * PEER_MT], p[kc])
                acc = t if acc is None else acc + t
                if activate and kc == n_kc - 2:
                    g_next_ref[:, tok] = _gelu_tanh(act).astype(BF16)
            acc_ref[:, tok] += acc

    assert PEER_NE % 2 == 0
    inner = jnp.logical_and(e > 0, e < PEER_NE)

    @pl.when(e == 0)
    def _():
        acc_ref[...] = jnp.zeros_like(acc_ref)
        step(None, ga_ref, apply=False)

    @pl.when(jnp.logical_and(inner, e % 2 == 0))
    def _():
        step(gb_ref, ga_ref)

    @pl.when(e % 2 == 1)
    def _():
        step(ga_ref, gb_ref)

    @pl.when(e == PEER_NE)
    def _():
        step(gb_ref, None, activate=False)
        o_ref[...] = x_ref[...] + mod_ref[5:6, :] * acc_ref[...].T


def _peer(ht, u_tab, vt_tab, routing, x1, mod4, l):
    rank2, e2, cnt1, e1 = routing
    once = pl.Buffered(1)
    rspec = pl.BlockSpec((PEER_HEADS, PEER_NKEYS, PEER_TT), lambda i, e: (0, 0, i), pipeline_mode=once)
    return pl.pallas_call(
        _peer_kernel,
        grid=(NT // PEER_TT, PEER_NE + 1),
        in_specs=[
            pl.BlockSpec((D_MODEL, PEER_TT), lambda i, e: (0, i), pipeline_mode=once),
            pl.BlockSpec((None, PEER_EB, D_MODEL), lambda i, e: (l, jnp.minimum(e, PEER_NE - 1), 0)),
            pl.BlockSpec((None, D_MODEL, PEER_EB), lambda i, e: (l, 0, jnp.maximum(e - 1, 0))),
            rspec, rspec, rspec, rspec,
            pl.BlockSpec((PEER_TT, D_MODEL), lambda i, e: (i, 0), pipeline_mode=once),
            pl.BlockSpec((None, None, 6, D_MODEL), lambda i, e: (l, _mod_set(i * PEER_TT), 0, 0)),
        ],
        out_specs=pl.BlockSpec((PEER_TT, D_MODEL), lambda i, e: (i, 0), pipeline_mode=once),
        out_shape=jax.ShapeDtypeStruct((NT, D_MODEL), F32),
        scratch_shapes=[pltpu.VMEM((PEER_EB, PEER_TT), BF16), pltpu.VMEM((PEER_EB, PEER_TT), BF16),
                        pltpu.VMEM((D_MODEL, PEER_TT), F32)],
        compiler_params=_cparams("parallel", "arbitrary"),
        name="peer_experts",
    )(ht, u_tab, vt_tab, rank2, e2, cnt1, e1, x1, mod4)


FIN_TM = 512


def _final_kernel(x_ref, g_ref, o_ref):
    o_ref[...] = _rms(x_ref[...], g_ref[...])


def _final_norm(x, g):
    return pl.pallas_call(
        _final_kernel,
        grid=(NT // FIN_TM,),
        in_specs=[pl.BlockSpec((FIN_TM, D_MODEL), lambda i: (i, 0)),
                  pl.BlockSpec((1, D_MODEL), lambda i: (0, 0))],
        out_specs=pl.BlockSpec((FIN_TM, D_MODEL), lambda i: (i, 0)),
        out_shape=jax.ShapeDtypeStruct((NT, D_MODEL), F32),
        compiler_params=_cparams("parallel"),
        name="final_norm",
    )(x, g)


def _permute_w_in(w_in):
    sizes = (GROUP_W, GROUP_W, GROUP_W, GROUP_W, 4 * ML_HEADS, MLA_Q_RANK, MLA_KV_RANK, MLA_D_ROPE,
             SWA_HEADS * SWA_DH, SWA_KV_HEADS * SWA_DH, SWA_KV_HEADS * SWA_DH, GROUP_W, GROUP_W, GROUP_W)
    offs = [0]
    for s in sizes:
        offs.append(offs[-1] + s)
    part = lambda i: w_in[:, :, offs[i]:offs[i + 1]]
    order = [0, 1, 2, 3, 5, 6, 8, 9, 10, 11, 12, 13, 7, 4]
    cols = [part(i) for i in order]
    used = sum(sizes)
    cols.append(jnp.zeros(w_in.shape[:2] + (PROJ_W - used,), w_in.dtype))
    return jnp.concatenate(cols, axis=-1).astype(BF16)


def _permute_w_uq(w_uq):
    w = w_uq.reshape(DEPTH, MLA_Q_RANK, MLA_HEADS, MLA_D_NOPE + MLA_D_ROPE)
    nope = w[..., :MLA_D_NOPE].reshape(DEPTH, MLA_Q_RANK, MLA_HEADS * MLA_D_NOPE)
    rope = w[..., MLA_D_NOPE:].reshape(DEPTH, MLA_Q_RANK, MLA_HEADS * MLA_D_ROPE)
    return jnp.concatenate([nope, rope], axis=-1).astype(BF16)


def kernel(x_prompt, x_sample, c, cache_mla_ckv, cache_mla_krope, cache_swa_k, cache_swa_v, cache_diff_k,
           cache_diff_v, state_mlstm_C, state_mlstm_n, state_mlstm_m, c_ctx, w_mod, b_mod, norm1_g, w_in,
           mlstm_i_bias, mlstm_f_bias, mlstm_norm_g, mla_qnorm_g, mla_w_uq, mla_kvnorm_g, mla_w_ukv, swa_sink,
           diff_lq1, diff_lk1, diff_lq2, diff_lk2, diff_norm_g, w_out, norm2_g, peer_w_q, peer_sub_keys,
           peer_u, peer_v, final_norm_g):
    x = jnp.concatenate([x_prompt.reshape(N_CTX, D_MODEL), x_sample.reshape(N_LAT, D_MODEL)], axis=0)

    cvec = jnp.concatenate([c_ctx[None, :], c], axis=0)
    cvec_t = jnp.pad(cvec.T, ((0, 0), (0, 8 - N_SETS)))
    mod4 = _modulation(cvec_t, w_mod, b_mod).reshape(DEPTH, 8, 6, D_MODEL)

    w_in_p = _permute_w_in(w_in)
    w_uq_p = _permute_w_uq(mla_w_uq)
    w_ukv_b = mla_w_ukv.astype(BF16)
    w_out_b = w_out.astype(BF16)
    w_q_b = peer_w_q.astype(BF16)
    keys_b = peer_sub_keys.astype(BF16)
    u_b = peer_u.astype(BF16)
    vt_b = jnp.swapaxes(peer_v, 1, 2).astype(BF16)

    vec3 = lambda a: a.reshape(DEPTH, 1, a.shape[-1])
    norm1_3, norm2_3 = vec3(norm1_g), vec3(norm2_g)
    mlng_3, qg_3, kg_3, dng_3 = vec3(mlstm_norm_g), vec3(mla_qnorm_g), vec3(mla_kvnorm_g), vec3(diff_norm_g)
    lq1_3, lk1_3, lq2_3, lk2_3 = vec3(diff_lq1), vec3(diff_lk1), vec3(diff_lq2), vec3(diff_lk2)

    n_chain = 2 * ML_HEADS
    st_c = state_mlstm_C.reshape(DEC_BATCH, DEPTH, n_chain, ML_DH, ML_DH)
    st_n = state_mlstm_n.reshape(DEC_BATCH, DEPTH, n_chain, ML_DH)
    st_m = jnp.broadcast_to(state_mlstm_m.reshape(DEC_BATCH, DEPTH, n_chain, 1), (DEC_BATCH, DEPTH, n_chain, ML_DH))
    swk_c = cache_swa_k.reshape(DEC_BATCH, DEPTH, PAST_LEN, SWA_KV_HEADS * SWA_DH)
    swv_c = cache_swa_v.reshape(DEC_BATCH, DEPTH, PAST_LEN, SWA_KV_HEADS * SWA_DH)
    dfk_c = cache_diff_k.reshape(DEC_BATCH, DEPTH, PAST_LEN, GROUP_W)
    dfv_c = cache_diff_v.reshape(DEC_BATCH, DEPTH, PAST_LEN, GROUP_W)

    cos64, sin64 = _rope_tables(DEC_SEQ, 64)
    cos128, sin128 = _rope_tables(DEC_SEQ, 128)

    outs = [[] for _ in range(9)]
    for l in range(DEPTH):
        proj = _project(x, mod4, norm1_3, w_in_p, l)

        gates = proj[:, C_TAIL + TAIL_G:C_TAIL + TAIL_G + 16]
        gt = gates.reshape(NT // ML_CHUNK, ML_CHUNK, 16).transpose(0, 2, 1)
        bias = jnp.concatenate([mlstm_i_bias[l].reshape(-1), mlstm_f_bias[l].reshape(-1)])
        bcol = bias.reshape(16, 1)
        brow = jnp.zeros((1, 128), F32).at[0, TAIL_G:TAIL_G + 16].set(bias)

        ml_ctx, c_st, n_st, m_st = _mlstm(proj, gt, bcol, brow, mlng_3, l, latent=False)
        ml_lat = _mlstm(proj, gt, bcol, brow, mlng_3, l, latent=True, states=(st_c, st_n, st_m))
        mla_ctx, ckv_n = _mla_ctx(proj, qg_3, kg_3, w_uq_p, w_ukv_b, l)
        mla_lat = _mla_lat(proj, qg_3, kg_3, w_uq_p, w_ukv_b, cache_mla_ckv, cache_mla_krope, cos64, sin64, l)
        sink = swa_sink[l].reshape(1, SWA_HEADS)
        swa_ctx = _swa_ctx(proj, sink, l)
        swa_lat = _swa_lat(proj, sink, swk_c, swv_c, cos128, sin128, l)
        df_ctx = _diff_ctx(proj, lq1_3, lk1_3, lq2_3, lk2_3, dng_3, l)
        df_lat = _diff_lat(proj, lq1_3, lk1_3, lq2_3, lk2_3, dng_3, dfk_c, dfv_c, cos64, sin64, l)

        x1 = _out_proj((ml_ctx, mla_ctx, swa_ctx, df_ctx), (ml_lat, mla_lat, swa_lat, df_lat), x, mod4, w_out_b, l)
        h2t, q = _peer_query(x1, mod4, norm2_3, w_q_b, l)
        routing = _peer_route(q, keys_b, l)
        x = _peer(h2t, u_b, vt_b, routing, x1, mod4, l)

        pc = proj[:N_CTX]
        outs[0].append(ckv_n.reshape(BATCH, SEQ, MLA_KV_RANK))
        outs[1].append(pc[:, C_TAIL:C_TAIL + MLA_D_ROPE].reshape(BATCH, SEQ, MLA_D_ROPE))
        outs[2].append(pc[:, C_SK:C_SK + 256].reshape(BATCH, SEQ, SWA_KV_HEADS, SWA_DH))
        outs[3].append(pc[:, C_SV:C_SV + 256].reshape(BATCH, SEQ, SWA_KV_HEADS, SWA_DH))
        outs[4].append(pc[:, C_DK:C_DK + 512].reshape(BATCH, SEQ, DIFF_HEADS, 2 * DIFF_DH))
        outs[5].append(pc[:, C_DV:C_DV + 512].reshape(BATCH, SEQ, DIFF_HEADS, 2 * DIFF_DH))
        outs[6].append(c_st.reshape(BATCH, 2, ML_HEADS, ML_DH, ML_DH))
        outs[7].append(n_st.reshape(BATCH, 2, ML_HEADS, ML_DH))
        outs[8].append(m_st[:, :, 0].reshape(BATCH, 2, ML_HEADS))

    y = _final_norm(x, final_norm_g.reshape(1, D_MODEL))
    y_prompt = y[:N_CTX].reshape(BATCH, SEQ, D_MODEL)
    y_sample = y[N_CTX:].reshape(DEC_BATCH, DEC_SEQ, D_MODEL)
    return (y_prompt, y_sample) + tuple(jnp.stack(o, axis=1) for o in outs)
```

```python
import functools
import math

import jax
import jax.numpy as jnp
from jax import lax
from jax.experimental import pallas as pl
from jax.experimental.pallas import tpu as pltpu

F32 = jnp.float32
BF16 = jnp.bfloat16

D_MODEL = 2048
BATCH = 32
SEQ = 256
DEPTH = 4
DEC_BATCH = 2
DEC_SEQ = 1024
PAST_LEN = 256
GRID_W = 64
GROUP_W = D_MODEL // 4
ML_HEADS = 4
ML_DH = GROUP_W // ML_HEADS
ML_CHUNK = 64
MLA_HEADS = 4
MLA_D_NOPE = GROUP_W // MLA_HEADS
MLA_D_ROPE = 64
MLA_Q_RANK = D_MODEL // 8
MLA_KV_RANK = D_MODEL // 8
MLA_SCALE = (MLA_D_NOPE + MLA_D_ROPE) ** -0.5
SWA_HEADS = 4
SWA_KV_HEADS = 2
SWA_DH = GROUP_W // SWA_HEADS
WINDOW = 128
DIFF_HEADS = 4
DIFF_DH = GROUP_W // (2 * DIFF_HEADS)
PEER_HEADS = 8
PEER_QDIM = 256
PEER_NKEYS = 128
PEER_N = PEER_NKEYS * PEER_NKEYS
PEER_TOPK = 16
ROPE_BASE = 10000.0
EPS = 1e-6
NEG = -1e30

N_CTX = BATCH * SEQ
N_LAT = DEC_BATCH * DEC_SEQ
NT = N_CTX + N_LAT
N_SETS = 1 + DEC_BATCH

C_MLQ, C_MLK, C_MLV, C_MLO = 0, 512, 1024, 1536
C_CQ, C_CKV = 2048, 2304
C_SQ, C_SK, C_SV = 2560, 3072, 3328
C_DQ, C_DK, C_DV = 3584, 4096, 4608
C_TAIL = 5120
TAIL_G = 64
PROJ_W = 5376
PROJ_TN = 1792

VMEM_LIMIT = 56 * 1024 * 1024


def _cparams(*sem):
    return pltpu.CompilerParams(dimension_semantics=sem, vmem_limit_bytes=VMEM_LIMIT)


def _mod_set(row_start):
    return jnp.where(row_start >= N_CTX, (row_start - N_CTX) // DEC_SEQ + 1, 0)


def _rms(x, g):
    return x * lax.rsqrt(jnp.mean(x * x, axis=-1, keepdims=True) + EPS) * g


def _dot(a, b):
    return jnp.dot(a, b, preferred_element_type=F32)


def _dot_nt(a, b):
    return lax.dot_general(a, b, (((1,), (1,)), ((), ())), preferred_element_type=F32)


def _dot_tn(a, b):
    return lax.dot_general(a, b, (((0,), (0,)), ((), ())), preferred_element_type=F32)


def _dot_hi(a, b):
    return jnp.dot(a, b, preferred_element_type=F32, precision=lax.Precision.HIGHEST)


MOD_TK = 512
MOD_TN = 2048


def _mod_kernel(c_ref, w_ref, b_ref, o_ref, acc_ref):
    k = pl.program_id(2)

    @pl.when(k == 0)
    def _():
        acc_ref[...] = jnp.zeros_like(acc_ref)

    w = w_ref[...]
    for r in range(N_SETS):
        cv = c_ref[:, r:r + 1]
        sv = cv * jax.nn.sigmoid(cv)
        acc_ref[r] += (sv * w).reshape(MOD_TK // 8, 8, MOD_TN).sum(axis=0)

    @pl.when(k == pl.num_programs(2) - 1)
    def _():
        o_ref[...] = jnp.zeros_like(o_ref)
        for r in range(N_SETS):
            o_ref[r:r + 1, :] = acc_ref[r].sum(axis=0, keepdims=True) + b_ref[...]


def _modulation(cvec_t, w_mod, b_mod):
    return pl.pallas_call(
        _mod_kernel,
        grid=(DEPTH, 6 * D_MODEL // MOD_TN, D_MODEL // MOD_TK),
        in_specs=[
            pl.BlockSpec((MOD_TK, 8), lambda l, n, k: (k, 0)),
            pl.BlockSpec((None, MOD_TK, MOD_TN), lambda l, n, k: (l, k, n)),
            pl.BlockSpec((None, 1, MOD_TN), lambda l, n, k: (l, 0, n)),
        ],
        out_specs=pl.BlockSpec((None, 8, MOD_TN), lambda l, n, k: (l, 0, n)),
        out_shape=jax.ShapeDtypeStruct((DEPTH, 8, 6 * D_MODEL), F32),
        scratch_shapes=[pltpu.VMEM((N_SETS, 8, MOD_TN), F32)],
        compiler_params=_cparams("parallel", "parallel", "arbitrary"),
        name="modulation",
    )(cvec_t, w_mod, b_mod.reshape(DEPTH, 1, 6 * D_MODEL))


PROJ_TM = 1024


def _proj_kernel(x_ref, m_ref, g_ref, w_ref, o_ref, h_ref):
    @pl.when(pl.program_id(1) == 0)
    def _():
        h = _rms(x_ref[...], g_ref[...]) * (1.0 + m_ref[1:2, :]) + m_ref[0:1, :]
        h_ref[...] = h.astype(BF16)

    o_ref[...] = _dot(h_ref[...], w_ref[...])


def _project(x, mod4, norm_g, w_in_p, l):
    return pl.pallas_call(
        _proj_kernel,
        grid=(NT // PROJ_TM, PROJ_W // PROJ_TN),
        in_specs=[
            pl.BlockSpec((PROJ_TM, D_MODEL), lambda i, j: (i, 0), pipeline_mode=pl.Buffered(1)),
            pl.BlockSpec((None, None, 6, D_MODEL), lambda i, j: (l, _mod_set(i * PROJ_TM), 0, 0)),
            pl.BlockSpec((None, 1, D_MODEL), lambda i, j: (l, 0, 0)),
            pl.BlockSpec((None, D_MODEL, PROJ_TN), lambda i, j: (l, 0, j)),
        ],
        out_specs=pl.BlockSpec((PROJ_TM, PROJ_TN), lambda i, j: (i, j)),
        out_shape=jax.ShapeDtypeStruct((NT, PROJ_W), F32),
        scratch_shapes=[pltpu.VMEM((PROJ_TM, D_MODEL), BF16)],
        compiler_params=_cparams("parallel", "arbitrary"),
        name="adaln_in_proj",
    )(x, mod4, norm_g, w_in_p)


def _rope(x, cos, sin, quarter):
    width = x.shape[-1]
    lane = lax.broadcasted_iota(jnp.int32, x.shape, 1)
    first = (lane % (2 * quarter)) < quarter
    partner = jnp.where(first, pltpu.roll(x, width - quarter, 1), pltpu.roll(x, quarter, 1))
    return x * cos + partner * sin


def _rope_tables(n_tok, rot_dim):
    half = rot_dim // 2
    pos = jnp.arange(n_tok)
    row = (pos // GRID_W).astype(F32)
    col = (pos % GRID_W).astype(F32)
    inv = ROPE_BASE ** (-jnp.arange(0, half, 2, dtype=F32) / half)
    a_row = row[:, None] * inv[None, :]
    a_col = col[:, None] * inv[None, :]
    ang = jnp.concatenate([a_row, a_row, a_col, a_col], axis=-1)
    sign = jnp.tile(jnp.concatenate([-jnp.ones(half // 2, F32), jnp.ones(half // 2, F32)]), 2)
    return jnp.cos(ang), jnp.sin(ang) * sign[None, :]


def _log_sigmoid(x):
    return jnp.minimum(x, 0.0) - jnp.log(1.0 + jnp.exp(-jnp.abs(x)))


def _mlstm_kernel(*refs, n_tok, n_seq, has_state):
    if has_state:
        (q_ref, k_ref, v_ref, o_ref, tail_ref, gt_ref, bcol_ref, brow_ref, ng_ref, c0_ref, n0_ref, m0_ref,
         out_ref, hf_ref, hb_ref, cst_ref, nst_ref, mst_ref) = refs
    else:
        (q_ref, k_ref, v_ref, o_ref, tail_ref, gt_ref, bcol_ref, brow_ref, ng_ref,
         out_ref, cs_ref, ns_ref, ms_ref, hf_ref, hb_ref, cst_ref, nst_ref, mst_ref) = refs
    n_chunks = n_tok // ML_CHUNK
    scale = ML_DH ** -0.5

    if has_state:
        cst_ref[...] = c0_ref[...]
        nst_ref[...] = n0_ref[...]
        mst_ref[...] = m0_ref[...]
    else:
        cst_ref[...] = jnp.zeros_like(cst_ref)
        nst_ref[...] = jnp.zeros_like(nst_ref)
        mst_ref[...] = jnp.zeros_like(mst_ref)

    row = lax.broadcasted_iota(jnp.int32, (ML_CHUNK, ML_CHUNK), 0)
    col = lax.broadcasted_iota(jnp.int32, (ML_CHUNK, ML_CHUNK), 1)
    lower = (col <= row)
    upper = (col >= row)
    lower_f = lower.astype(F32)
    upper_f = upper.astype(F32)

    def scan_group(c, group):
        ch = []
        for b, d in group:
            cc = c if d == 0 else n_chunks - 1 - c
            t0 = pl.multiple_of(b * n_tok + cc * ML_CHUNK, ML_CHUNK)
            rows = pl.ds(t0, ML_CHUNK)
            g_col = tail_ref[rows, :] + brow_ref[...]
            g_row = gt_ref[b * n_chunks + cc] + bcol_ref[...]
            lf_col = _log_sigmoid(g_col)
            lf_row = _log_sigmoid(g_row)
            if d == 0:
                bcum_col = _dot_hi(lower_f, lf_col)
                bcum_row = _dot_hi(lf_row, upper_f)
                mask = lower
            else:
                bcum_col = _dot_hi(upper_f, lf_col)
                bcum_row = _dot_hi(lf_row, lower_f)
                mask = upper
            for h in range(ML_HEADS):
                r = d * ML_HEADS + h
                fr = 2 * ML_HEADS + r
                ch.append(dict(
                    b=b, d=d, r=r, rows=rows, lanes=slice(h * ML_DH, (h + 1) * ML_DH), mask=mask,
                    ig_c=g_col[:, TAIL_G + r:TAIL_G + r + 1],
                    b_c=bcum_col[:, TAIL_G + fr:TAIL_G + fr + 1],
                    ig_r=g_row[r:r + 1, :],
                    b_r=bcum_row[fr:fr + 1, :],
                    tot=jnp.sum(lf_row[fr:fr + 1, :], axis=1, keepdims=True)))
        n = range(len(ch))
        m_prev = [mst_ref[x["b"], x["r"]:x["r"] + 1, 0:1] for x in ch]
        n_prev = [nst_ref[x["b"], x["r"]:x["r"] + 1, :] for x in ch]
        c_prev = [cst_ref[x["b"], x["r"]] for x in ch]
        q = [q_ref[x["rows"], x["lanes"]] for x in ch]
        k = [k_ref[x["rows"], x["lanes"]] for x in ch]
        v = [v_ref[x["rows"], x["lanes"]].astype(BF16) for x in ch]
        qb = [x.astype(BF16) for x in q]

        dmat = [jnp.where(ch[i]["mask"], ch[i]["b_c"] - ch[i]["b_r"] + ch[i]["ig_r"], NEG) for i in n]
        inter = [ch[i]["b_c"] + m_prev[i] for i in n]
        mt = [jnp.maximum(inter[i], jnp.max(dmat[i], axis=1, keepdims=True)) for i in n]
        w = [jnp.exp(dmat[i] - mt[i]) for i in n]
        qk = [_dot_nt(qb[i], k[i].astype(BF16)) for i in n]
        qc = [_dot(qb[i], c_prev[i].astype(BF16)) for i in n]
        s = [qk[i] * scale * w[i] for i in n]
        a = [jnp.exp(inter[i] - mt[i]) for i in n]
        sv = [_dot(s[i].astype(BF16), v[i]) for i in n]
        den = [jnp.sum(s[i], axis=1, keepdims=True) + a[i] * jnp.sum(q[i] * n_prev[i], axis=1, keepdims=True)
               for i in n]
        hc = [(sv[i] + a[i] * qc[i]) / jnp.maximum(jnp.abs(den[i]), jnp.exp(-mt[i])) for i in n]
        for i in n:
            (hf_ref if ch[i]["d"] == 0 else hb_ref)[ch[i]["rows"], ch[i]["lanes"]] = hc[i]

        wlog_c = [ch[i]["tot"] - ch[i]["b_c"] + ch[i]["ig_c"] for i in n]
        wlog_r = [ch[i]["tot"] - ch[i]["b_r"] + ch[i]["ig_r"] for i in n]
        m_new = [jnp.maximum(ch[i]["tot"] + m_prev[i], jnp.max(wlog_r[i], axis=1, keepdims=True)) for i in n]
        decay = [jnp.exp(ch[i]["tot"] + m_prev[i] - m_new[i]) for i in n]
        kw = [k[i] * (scale * jnp.exp(wlog_c[i] - m_new[i])) for i in n]
        kv = [_dot_tn(kw[i].astype(BF16), v[i]) for i in n]
        for i in n:
            b, r = ch[i]["b"], ch[i]["r"]
            cst_ref[b, r] = decay[i] * c_prev[i] + kv[i]
            nst_ref[b, r:r + 1, :] = decay[i] * n_prev[i] + jnp.sum(kw[i], axis=0, keepdims=True)
            mst_ref[b, r:r + 1, :] = jnp.broadcast_to(m_new[i], (1, ML_DH))

    def chunk_step(c, carry):
        for b in range(n_seq):
            scan_group(c, [(b, 0)])
            scan_group(c, [(b, 1)])
        return carry

    lax.fori_loop(0, n_chunks, chunk_step, 0)

    for h in range(ML_HEADS):
        lanes = slice(h * ML_DH, (h + 1) * ML_DH)
        hs = hf_ref[:, lanes] + hb_ref[:, lanes]
        out_ref[:, lanes] = _rms(hs, ng_ref[:, lanes]) * jax.nn.sigmoid(o_ref[:, lanes])

    if not has_state:
        cs_ref[...] = cst_ref[...]
        ns_ref[...] = nst_ref[...]
        ms_ref[...] = mst_ref[...]


ML_SEQ_PER_STEP = 4


def _mlstm(proj, gt, bcol, brow, norm_g, l, *, latent, states=None):
    n_tok = DEC_SEQ if latent else SEQ
    n_seq = 1 if latent else ML_SEQ_PER_STEP
    n_b = DEC_BATCH if latent else BATCH
    rows = n_seq * n_tok
    blk0 = N_CTX // rows if latent else 0
    n_chunks = n_tok // ML_CHUNK

    def col_spec(c0):
        return pl.BlockSpec((rows, 512), lambda b: (blk0 + b, c0 // 512))

    in_specs = [
        col_spec(C_MLQ), col_spec(C_MLK), col_spec(C_MLV), col_spec(C_MLO),
        pl.BlockSpec((rows, 128), lambda b: (blk0 + b, C_TAIL // 128)),
        pl.BlockSpec((n_seq * n_chunks, 16, ML_CHUNK), lambda b: (blk0 + b, 0, 0)),
        pl.BlockSpec((16, 1), lambda b: (0, 0)),
        pl.BlockSpec((1, 128), lambda b: (0, 0)),
        pl.BlockSpec((None, 1, GROUP_W), lambda b: (l, 0, 0)),
    ]
    args = [proj, proj, proj, proj, proj, gt, bcol, brow, norm_g]
    n_chain = 2 * ML_HEADS
    scratch = [pltpu.VMEM((rows, GROUP_W), F32), pltpu.VMEM((rows, GROUP_W), F32),
               pltpu.VMEM((n_seq, n_chain, ML_DH, ML_DH), F32), pltpu.VMEM((n_seq, n_chain, ML_DH), F32),
               pltpu.VMEM((n_seq, n_chain, ML_DH), F32)]
    if latent:
        c0, n0, m0 = states
        in_specs += [
            pl.BlockSpec((n_seq, None, n_chain, ML_DH, ML_DH), lambda b: (b, l, 0, 0, 0)),
            pl.BlockSpec((n_seq, None, n_chain, ML_DH), lambda b: (b, l, 0, 0)),
            pl.BlockSpec((n_seq, None, n_chain, ML_DH), lambda b: (b, l, 0, 0)),
        ]
        args += [c0, n0, m0]
        out_shape = jax.ShapeDtypeStruct((N_LAT, GROUP_W), F32)
        out_specs = pl.BlockSpec((rows, GROUP_W), lambda b: (b, 0))
    else:
        out_shape = (jax.ShapeDtypeStruct((N_CTX, GROUP_W), F32),
                     jax.ShapeDtypeStruct((BATCH, n_chain, ML_DH, ML_DH), F32),
                     jax.ShapeDtypeStruct((BATCH, n_chain, ML_DH), F32),
                     jax.ShapeDtypeStruct((BATCH, n_chain, ML_DH), F32))
        out_specs = (pl.BlockSpec((rows, GROUP_W), lambda b: (b, 0)),
                     pl.BlockSpec((n_seq, n_chain, ML_DH, ML_DH), lambda b: (b, 0, 0, 0)),
                     pl.BlockSpec((n_seq, n_chain, ML_DH), lambda b: (b, 0, 0)),
                     pl.BlockSpec((n_seq, n_chain, ML_DH), lambda b: (b, 0, 0)))
    return pl.pallas_call(
        functools.partial(_mlstm_kernel, n_tok=n_tok, n_seq=n_seq, has_state=latent),
        grid=(n_b // n_seq,),
        in_specs=in_specs,
        out_specs=out_specs,
        out_shape=out_shape,
        scratch_shapes=scratch,
        compiler_params=_cparams("parallel"),
        name="mlstm_latent" if latent else "mlstm_context",
    )(*args)


def _softmax_parts(scores, sink=None):
    m = jnp.max(scores[0], axis=1, keepdims=True)
    for s in scores[1:]:
        m = jnp.maximum(m, jnp.max(s, axis=1, keepdims=True))
    if sink is not None:
        m = jnp.maximum(m, sink)
    es = [jnp.exp(s - m) for s in scores]
    den = jnp.sum(es[0], axis=1, keepdims=True)
    for e in es[1:]:
        den = den + jnp.sum(e, axis=1, keepdims=True)
    if sink is not None:
        den = den + jnp.exp(sink - m)
    return es, den


def _softmax_parts_all(score_lists, sinks=None):
    n = range(len(score_lists))
    sinks = [None] * len(score_lists) if sinks is None else sinks
    ms = []
    for i in n:
        m = jnp.max(score_lists[i][0], axis=1, keepdims=True)
        for s in score_lists[i][1:]:
            m = jnp.maximum(m, jnp.max(s, axis=1, keepdims=True))
        ms.append(m if sinks[i] is None else jnp.maximum(m, sinks[i]))
    es = [[jnp.exp(s - ms[i]) for s in score_lists[i]] for i in n]
    dens = []
    for i in n:
        den = jnp.sum(es[i][0], axis=1, keepdims=True)
        for e in es[i][1:]:
            den = den + jnp.sum(e, axis=1, keepdims=True)
        dens.append(den if sinks[i] is None else den + jnp.exp(sinks[i] - ms[i]))
    return es, dens


ATT_TQ = 256
LAT_QB = DEC_SEQ // ATT_TQ
N_KEYS_LAT = DEC_SEQ + PAST_LEN


def _mla_q(cq_ref, qg_ref, wuq_ref):
    return _dot(_rms(cq_ref[...], qg_ref[...]).astype(BF16), wuq_ref[...])


def _mla_heads(q, kv, kr, n_keys):
    hs = range(MLA_HEADS)
    r0 = MLA_HEADS * MLA_D_NOPE
    qn = [q[:, h * MLA_D_NOPE:(h + 1) * MLA_D_NOPE].astype(BF16) for h in hs]
    qr = [q[:, r0 + h * MLA_D_ROPE:r0 + (h + 1) * MLA_D_ROPE].astype(BF16) for h in hs]
    kn = [kv[:, h * 256:h * 256 + MLA_D_NOPE] for h in hs]
    v = [kv[:, h * 256 + MLA_D_NOPE:(h + 1) * 256] for h in hs]
    s = [(_dot_nt(qn[h], kn[h]) + _dot_nt(qr[h], kr)) * MLA_SCALE for h in hs]
    es, dens = _softmax_parts_all([[s[h]] for h in hs])
    pv = [_dot(es[h][0].astype(BF16), v[h]) for h in hs]
    return [pv[h] / dens[h] for h in hs]


def _mla_ctx_kernel(cq_ref, ckv_ref, tail_ref, qg_ref, kg_ref, wuq_ref, wukv_ref, out_ref, ckvn_ref):
    q = _mla_q(cq_ref, qg_ref, wuq_ref)
    ckvn = _rms(ckv_ref[...], kg_ref[...])
    ckvn_ref[...] = ckvn
    kv = _dot(ckvn.astype(BF16), wukv_ref[...]).astype(BF16)
    kr = tail_ref[:, 0:MLA_D_ROPE].astype(BF16)
    outs = _mla_heads(q, kv, kr, SEQ)
    for h in range(MLA_HEADS):
        out_ref[:, h * MLA_D_V:(h + 1) * MLA_D_V] = outs[h]


MLA_D_V = GROUP_W // MLA_HEADS


def _mla_lat_kernel(cq_ref, ckv_ref, tail_ref, qg_ref, kg_ref, wuq_ref, wukv_ref, ckvc_ref, krc_ref,
                    cosq_ref, sinq_ref, cosk_ref, sink_ref, out_ref, kv_ref, kr_ref):
    @pl.when(pl.program_id(1) == 0)
    def _():
        ckvn = _rms(ckv_ref[...], kg_ref[...])
        kv_ref[0:DEC_SEQ, :] = _dot(ckvn.astype(BF16), wukv_ref[...]).astype(BF16)
        kv_ref[DEC_SEQ:N_KEYS_LAT, :] = _dot(ckvc_ref[...].astype(BF16), wukv_ref[...]).astype(BF16)
        kr = _rope(tail_ref[...], cosk_ref[...], sink_ref[...], MLA_D_ROPE // 4)
        kr_ref[0:DEC_SEQ, :] = kr[:, 0:MLA_D_ROPE].astype(BF16)
        kr_ref[DEC_SEQ:N_KEYS_LAT, :] = krc_ref[...].astype(BF16)

    q = _mla_q(cq_ref, qg_ref, wuq_ref)
    n0 = MLA_HEADS * MLA_D_NOPE
    q_rope = _rope(q[:, n0:], cosq_ref[...], sinq_ref[...], MLA_D_ROPE // 4)
    q = jnp.concatenate([q[:, :n0], q_rope], axis=1)
    outs = _mla_heads(q, kv_ref[...], kr_ref[...], N_KEYS_LAT)
    for h in range(MLA_HEADS):
        out_ref[:, h * MLA_D_V:(h + 1) * MLA_D_V] = outs[h]


def _w_specs2(shape_a, shape_b, l):
    return [pl.BlockSpec((None,) + shape_a, lambda *_: (l,) + (0,) * len(shape_a)),
            pl.BlockSpec((None,) + shape_b, lambda *_: (l,) + (0,) * len(shape_b))]


def _mla_ctx(proj, qg, kg, wuq, wukv, l):
    return pl.pallas_call(
        _mla_ctx_kernel,
        grid=(BATCH,),
        in_specs=[
            pl.BlockSpec((SEQ, 256), lambda b: (b, C_CQ // 256)),
            pl.BlockSpec((SEQ, 256), lambda b: (b, C_CKV // 256)),
            pl.BlockSpec((SEQ, 128), lambda b: (b, C_TAIL // 128)),
            pl.BlockSpec((None, 1, MLA_Q_RANK), lambda b: (l, 0, 0)),
            pl.BlockSpec((None, 1, MLA_KV_RANK), lambda b: (l, 0, 0)),
            pl.BlockSpec((None, MLA_Q_RANK, 768), lambda b: (l, 0, 0)),
            pl.BlockSpec((None, MLA_KV_RANK, 1024), lambda b: (l, 0, 0)),
        ],
        out_specs=(pl.BlockSpec((SEQ, GROUP_W), lambda b: (b, 0)),
                   pl.BlockSpec((SEQ, MLA_KV_RANK), lambda b: (b, 0))),
        out_shape=(jax.ShapeDtypeStruct((N_CTX, GROUP_W), F32),
                   jax.ShapeDtypeStruct((N_CTX, MLA_KV_RANK), F32)),
        compiler_params=_cparams("parallel"),
        name="mla_context",
    )(proj, proj, proj, qg, kg, wuq, wukv)


def _mla_lat(proj, qg, kg, wuq, wukv, cache_ckv, cache_kr, cos64, sin64, l):
    qb0 = N_CTX // ATT_TQ
    bb0 = N_CTX // DEC_SEQ
    cosq = jnp.tile(cos64, (1, MLA_HEADS))
    sinq = jnp.tile(sin64, (1, MLA_HEADS))
    cosk = jnp.tile(cos64, (1, 2))
    sink = jnp.tile(sin64, (1, 2))
    return pl.pallas_call(
        _mla_lat_kernel,
        grid=(DEC_BATCH, LAT_QB),
        in_specs=[
            pl.BlockSpec((ATT_TQ, 256), lambda b, i: (qb0 + b * LAT_QB + i, C_CQ // 256)),
            pl.BlockSpec((DEC_SEQ, 256), lambda b, i: (bb0 + b, C_CKV // 256)),
            pl.BlockSpec((DEC_SEQ, 128), lambda b, i: (bb0 + b, C_TAIL // 128)),
            pl.BlockSpec((None, 1, MLA_Q_RANK), lambda b, i: (l, 0, 0)),
            pl.BlockSpec((None, 1, MLA_KV_RANK), lambda b, i: (l, 0, 0)),
            pl.BlockSpec((None, MLA_Q_RANK, 768), lambda b, i: (l, 0, 0)),
            pl.BlockSpec((None, MLA_KV_RANK, 1024), lambda b, i: (l, 0, 0)),
            pl.BlockSpec((None, None, PAST_LEN, MLA_KV_RANK), lambda b, i: (b, l, 0, 0)),
            pl.BlockSpec((None, None, PAST_LEN, MLA_D_ROPE), lambda b, i: (b, l, 0, 0)),
            pl.BlockSpec((ATT_TQ, 256), lambda b, i: (i, 0)),
            pl.BlockSpec((ATT_TQ, 256), lambda b, i: (i, 0)),
            pl.BlockSpec((DEC_SEQ, 128), lambda b, i: (0, 0)),
            pl.BlockSpec((DEC_SEQ, 128), lambda b, i: (0, 0)),
        ],
        out_specs=pl.BlockSpec((ATT_TQ, GROUP_W), lambda b, i: (b * LAT_QB + i, 0)),
        out_shape=jax.ShapeDtypeStruct((N_LAT, GROUP_W), F32),
        scratch_shapes=[pltpu.VMEM((N_KEYS_LAT, 1024), BF16), pltpu.VMEM((N_KEYS_LAT, MLA_D_ROPE), BF16)],
        compiler_params=_cparams("parallel", "arbitrary"),
        name="mla_latent",
    )(proj, proj, proj, qg, kg, wuq, wukv, cache_ckv, cache_kr, cosq, sinq, cosk, sink)


SWA_SCALE = SWA_DH ** -0.5
SWA_REP = SWA_HEADS // SWA_KV_HEADS
SWA_KWIN = ATT_TQ + 2 * WINDOW


def _swa_ctx_kernel(sink_ref, q_ref, k_ref, v_ref, out_ref):
    kb = k_ref[...].astype(BF16)
    vb = v_ref[...].astype(BF16)
    hs = range(SWA_HEADS)
    gl = [slice((h // SWA_REP) * SWA_DH, (h // SWA_REP + 1) * SWA_DH) for h in hs]
    q = [q_ref[:, h * SWA_DH:(h + 1) * SWA_DH].astype(BF16) for h in hs]
    s = [_dot_nt(q[h], kb[:, gl[h]]) * SWA_SCALE for h in hs]
    es, dens = _softmax_parts_all([[s[h]] for h in hs], sinks=[sink_ref[0, h] for h in hs])
    pv = [_dot(es[h][0].astype(BF16), vb[:, gl[h]]) for h in hs]
    for h in hs:
        out_ref[:, h * SWA_DH:(h + 1) * SWA_DH] = pv[h] / dens[h]


def _swa_lat_kernel(sink_ref, q_ref, k_ref, v_ref, kc_ref, vc_ref, cosq_ref, sinq_ref, cosk_ref, sink_t_ref,
                    out_ref, kr_ref):
    i = pl.program_id(1)

    @pl.when(i == 0)
    def _():
        kr_ref[...] = _rope(k_ref[...], cosk_ref[...], sink_t_ref[...], SWA_DH // 4).astype(BF16)

    q_all = _rope(q_ref[...], cosq_ref[...], sinq_ref[...], SWA_DH // 4)
    k0 = pl.multiple_of(jnp.clip(i * ATT_TQ - WINDOW, 0, DEC_SEQ - SWA_KWIN), WINDOW)
    kwin = kr_ref[pl.ds(k0, SWA_KWIN), :]
    vwin = v_ref[pl.ds(k0, SWA_KWIN), :].astype(BF16)
    kc = kc_ref[...].astype(BF16)
    vc = vc_ref[...].astype(BF16)
    qpos = i * ATT_TQ + lax.broadcasted_iota(jnp.int32, (ATT_TQ, SWA_KWIN), 0)
    kpos = k0 + lax.broadcasted_iota(jnp.int32, (ATT_TQ, SWA_KWIN), 1)
    band = jnp.abs(qpos - kpos) <= WINDOW
    hs = range(SWA_HEADS)
    gl = [slice((h // SWA_REP) * SWA_DH, (h // SWA_REP + 1) * SWA_DH) for h in hs]
    q = [q_all[:, h * SWA_DH:(h + 1) * SWA_DH].astype(BF16) for h in hs]
    s_loc = [jnp.where(band, _dot_nt(q[h], kwin[:, gl[h]]) * SWA_SCALE, NEG) for h in hs]
    s_ctx = [_dot_nt(q[h], kc[:, gl[h]]) * SWA_SCALE for h in hs]
    es, dens = _softmax_parts_all([[s_loc[h], s_ctx[h]] for h in hs], sinks=[sink_ref[0, h] for h in hs])
    o = [_dot(es[h][0].astype(BF16), vwin[:, gl[h]]) + _dot(es[h][1].astype(BF16), vc[:, gl[h]]) for h in hs]
    for h in hs:
        out_ref[:, h * SWA_DH:(h + 1) * SWA_DH] = o[h] / dens[h]


def _smem_spec():
    return pl.BlockSpec(memory_space=pltpu.SMEM)


def _swa_ctx(proj, sink, l):
    return pl.pallas_call(
        _swa_ctx_kernel,
        grid=(BATCH,),
        in_specs=[
            _smem_spec(),
            pl.BlockSpec((SEQ, 512), lambda b: (b, C_SQ // 512)),
            pl.BlockSpec((SEQ, 256), lambda b: (b, C_SK // 256)),
            pl.BlockSpec((SEQ, 256), lambda b: (b, C_SV // 256)),
        ],
        out_specs=pl.BlockSpec((SEQ, GROUP_W), lambda b: (b, 0)),
        out_shape=jax.ShapeDtypeStruct((N_CTX, GROUP_W), F32),
        compiler_params=_cparams("parallel"),
        name="swa_context",
    )(sink, proj, proj, proj)


def _swa_lat(proj, sink, cache_k, cache_v, cos128, sin128, l):
    qb0 = N_CTX // ATT_TQ
    bb0 = N_CTX // DEC_SEQ
    kvw = SWA_KV_HEADS * SWA_DH
    return pl.pallas_call(
        _swa_lat_kernel,
        grid=(DEC_BATCH, LAT_QB),
        in_specs=[
            _smem_spec(),
            pl.BlockSpec((ATT_TQ, 512), lambda b, i: (qb0 + b * LAT_QB + i, C_SQ // 512)),
            pl.BlockSpec((DEC_SEQ, 256), lambda b, i: (bb0 + b, C_SK // 256)),
            pl.BlockSpec((DEC_SEQ, 256), lambda b, i: (bb0 + b, C_SV // 256)),
            pl.BlockSpec((None, None, PAST_LEN, kvw), lambda b, i: (b, l, 0, 0)),
            pl.BlockSpec((None, None, PAST_LEN, kvw), lambda b, i: (b, l, 0, 0)),
            pl.BlockSpec((ATT_TQ, 512), lambda b, i: (i, 0)),
            pl.BlockSpec((ATT_TQ, 512), lambda b, i: (i, 0)),
            pl.BlockSpec((DEC_SEQ, 256), lambda b, i: (0, 0)),
            pl.BlockSpec((DEC_SEQ, 256), lambda b, i: (0, 0)),
        ],
        out_specs=pl.BlockSpec((ATT_TQ, GROUP_W), lambda b, i: (b * LAT_QB + i, 0)),
        out_shape=jax.ShapeDtypeStruct((N_LAT, GROUP_W), F32),
        scratch_shapes=[pltpu.VMEM((DEC_SEQ, kvw), BF16)],
        compiler_params=_cparams("parallel", "arbitrary"),
        name="swa_latent",
    )(sink, proj, proj, proj, cache_k, cache_v,
      jnp.tile(cos128, (1, SWA_HEADS)), jnp.tile(sin128, (1, SWA_HEADS)),
      jnp.tile(cos128, (1, SWA_KV_HEADS)), jnp.tile(sin128, (1, SWA_KV_HEADS)))


DIFF_SCALE = DIFF_DH ** -0.5


def _diff_lambda(lq1_ref, lk1_ref, lq2_ref, lk2_ref):
    a = jnp.sum(lq1_ref[...] * lk1_ref[...], axis=1, keepdims=True)
    b = jnp.sum(lq2_ref[...] * lk2_ref[...], axis=1, keepdims=True)
    return jnp.exp(a) - jnp.exp(b)


def _diff_heads(q, k_parts, v_parts, lam, lam_init, ng_ref, out_ref, heads_per_group):
    for h0 in range(0, DIFF_HEADS, heads_per_group):
        hs = range(h0, h0 + heads_per_group)
        sls = [slice(h * 2 * DIFF_DH + c * DIFF_DH, h * 2 * DIFF_DH + (c + 1) * DIFF_DH)
               for h in hs for c in range(2)]
        qc = [q[:, sl].astype(BF16) for sl in sls]
        scores = [[_dot_nt(qc[j], kp[:, sls[j]]) * DIFF_SCALE for kp in k_parts] for j in range(len(sls))]
        es, dens = _softmax_parts_all(scores)
        ps = [[e / dens[j] for e in es[j]] for j in range(len(sls))]
        outs = []
        for i, h in enumerate(hs):
            vl = slice(h * 2 * DIFF_DH, (h + 1) * 2 * DIFF_DH)
            o = None
            for p1, p2, vp in zip(ps[2 * i], ps[2 * i + 1], v_parts):
                t = _dot((p1 - lam * p2).astype(BF16), vp[:, vl])
                o = t if o is None else o + t
            outs.append((vl, o))
        for vl, o in outs:
            out_ref[:, vl] = _rms(o, ng_ref[...]) * (1.0 - lam_init)


def _diff_ctx_kernel(q_ref, k_ref, v_ref, lq1_ref, lk1_ref, lq2_ref, lk2_ref, ng_ref, out_ref, *, lam_init):
    lam = _diff_lambda(lq1_ref, lk1_ref, lq2_ref, lk2_ref) + lam_init
    _diff_heads(q_ref[...], [k_ref[...].astype(BF16)], [v_ref[...].astype(BF16)], lam, lam_init, ng_ref, out_ref,
                heads_per_group=DIFF_HEADS)


def _diff_lat_kernel(q_ref, k_ref, v_ref, kc_ref, vc_ref, lq1_ref, lk1_ref, lq2_ref, lk2_ref, ng_ref,
                     cosq_ref, sinq_ref, cosk_ref, sink_ref, out_ref, kr_ref, *, lam_init):
    @pl.when(pl.program_id(1) == 0)
    def _():
        kr_ref[...] = _rope(k_ref[...], cosk_ref[...], sink_ref[...], DIFF_DH // 4).astype(BF16)

    lam = _diff_lambda(lq1_ref, lk1_ref, lq2_ref, lk2_ref) + lam_init
    q = _rope(q_ref[...], cosq_ref[...], sinq_ref[...], DIFF_DH // 4)
    _diff_heads(q, [kr_ref[...], kc_ref[...].astype(BF16)], [v_ref[...].astype(BF16), vc_ref[...].astype(BF16)],
                lam, lam_init, ng_ref, out_ref, heads_per_group=2)


def _vec_specs(n, width, l, nargs):
    return [pl.BlockSpec((None, 1, width), lambda *_: (l, 0, 0)) for _ in range(n)]


def _diff_ctx(proj, lq1, lk1, lq2, lk2, ng, l):
    lam_init = 0.8 - 0.6 * math.exp(-0.3 * l)
    return pl.pallas_call(
        functools.partial(_diff_ctx_kernel, lam_init=lam_init),
        grid=(BATCH,),
        in_specs=[
            pl.BlockSpec((SEQ, 512), lambda b: (b, C_DQ // 512)),
            pl.BlockSpec((SEQ, 512), lambda b: (b, C_DK // 512)),
            pl.BlockSpec((SEQ, 512), lambda b: (b, C_DV // 512)),
        ] + _vec_specs(4, DIFF_DH, l, 1) + _vec_specs(1, 2 * DIFF_DH, l, 1),
        out_specs=pl.BlockSpec((SEQ, GROUP_W), lambda b: (b, 0)),
        out_shape=jax.ShapeDtypeStruct((N_CTX, GROUP_W), F32),
        compiler_params=_cparams("parallel"),
        name="diff_context",
    )(proj, proj, proj, lq1, lk1, lq2, lk2, ng)


def _diff_lat(proj, lq1, lk1, lq2, lk2, ng, cache_k, cache_v, cos64, sin64, l):
    lam_init = 0.8 - 0.6 * math.exp(-0.3 * l)
    qb0 = N_CTX // ATT_TQ
    bb0 = N_CTX // DEC_SEQ
    cos_t = jnp.tile(cos64, (1, 2 * DIFF_HEADS))
    sin_t = jnp.tile(sin64, (1, 2 * DIFF_HEADS))
    return pl.pallas_call(
        functools.partial(_diff_lat_kernel, lam_init=lam_init),
        grid=(DEC_BATCH, LAT_QB),
        in_specs=[
            pl.BlockSpec((ATT_TQ, 512), lambda b, i: (qb0 + b * LAT_QB + i, C_DQ // 512)),
            pl.BlockSpec((DEC_SEQ, 512), lambda b, i: (bb0 + b, C_DK // 512)),
            pl.BlockSpec((DEC_SEQ, 512), lambda b, i: (bb0 + b, C_DV // 512)),
            pl.BlockSpec((None, None, PAST_LEN, GROUP_W), lambda b, i: (b, l, 0, 0)),
            pl.BlockSpec((None, None, PAST_LEN, GROUP_W), lambda b, i: (b, l, 0, 0)),
        ] + _vec_specs(4, DIFF_DH, l, 2) + _vec_specs(1, 2 * DIFF_DH, l, 2) + [
            pl.BlockSpec((ATT_TQ, 512), lambda b, i: (i, 0)),
            pl.BlockSpec((ATT_TQ, 512), lambda b, i: (i, 0)),
            pl.BlockSpec((DEC_SEQ, 512), lambda b, i: (0, 0)),
            pl.BlockSpec((DEC_SEQ, 512), lambda b, i: (0, 0)),
        ],
        out_specs=pl.BlockSpec((ATT_TQ, GROUP_W), lambda b, i: (b * LAT_QB + i, 0)),
        out_shape=jax.ShapeDtypeStruct((N_LAT, GROUP_W), F32),
        scratch_shapes=[pltpu.VMEM((DEC_SEQ, GROUP_W), BF16)],
        compiler_params=_cparams("parallel", "arbitrary"),
        name="diff_latent",
    )(proj, proj, proj, cache_k, cache_v, lq1, lk1, lq2, lk2, ng, cos_t, sin_t, cos_t, sin_t)


OUT_TM = 512


OUT_CTX_BLOCKS = N_CTX // OUT_TM


def _out_kernel(*refs):
    ctx_refs, lat_refs = refs[0:4], refs[4:8]
    x_ref, mod_ref, w_ref, x1_ref = refs[8:]
    is_ctx = pl.program_id(0) < OUT_CTX_BLOCKS
    acc = None
    for g in range(4):
        m = jnp.where(is_ctx, ctx_refs[g][...], lat_refs[g][...])
        t = _dot(m.astype(BF16), w_ref[g * GROUP_W:(g + 1) * GROUP_W, :])
        acc = t if acc is None else acc + t
    x1_ref[...] = x_ref[...] + mod_ref[2:3, :] * acc


def _out_proj(mixed_ctx, mixed_lat, x, mod4, w_out, l):
    row = lambda i: (i, 0)
    ctx_row = lambda i: (jnp.minimum(i, OUT_CTX_BLOCKS - 1), 0)
    lat_row = lambda i: (jnp.maximum(i - OUT_CTX_BLOCKS, 0), 0)
    return pl.pallas_call(
        _out_kernel,
        grid=(NT // OUT_TM,),
        in_specs=[pl.BlockSpec((OUT_TM, GROUP_W), ctx_row) for _ in range(4)]
        + [pl.BlockSpec((OUT_TM, GROUP_W), lat_row) for _ in range(4)] + [
            pl.BlockSpec((OUT_TM, D_MODEL), row),
            pl.BlockSpec((None, None, 6, D_MODEL), lambda i: (l, _mod_set(i * OUT_TM), 0, 0)),
            pl.BlockSpec((None, D_MODEL, D_MODEL), lambda i: (l, 0, 0)),
        ],
        out_specs=pl.BlockSpec((OUT_TM, D_MODEL), row),
        out_shape=jax.ShapeDtypeStruct((NT, D_MODEL), F32),
        compiler_params=_cparams("parallel"),
        name="out_proj_residual",
    )(*mixed_ctx, *mixed_lat, x, mod4, w_out)


def _peerq_kernel(x_ref, mod_ref, g_ref, w_ref, ht_ref, q_ref):
    h = _rms(x_ref[...], g_ref[...]) * (1.0 + mod_ref[4:5, :]) + mod_ref[3:4, :]
    ht_ref[...] = h.T.astype(BF16)
    q_ref[...] = _dot(h.astype(BF16), w_ref[...])


def _peer_query(x1, mod4, norm_g, w_q, l):
    row = lambda i: (i, 0)
    return pl.pallas_call(
        _peerq_kernel,
        grid=(NT // OUT_TM,),
        in_specs=[
            pl.BlockSpec((OUT_TM, D_MODEL), row),
            pl.BlockSpec((None, None, 6, D_MODEL), lambda i: (l, _mod_set(i * OUT_TM), 0, 0)),
            pl.BlockSpec((None, 1, D_MODEL), lambda i: (l, 0, 0)),
            pl.BlockSpec((None, D_MODEL, PEER_HEADS * PEER_QDIM), lambda i: (l, 0, 0)),
        ],
        out_specs=(pl.BlockSpec((D_MODEL, OUT_TM), lambda i: (0, i)),
                   pl.BlockSpec((OUT_TM, PEER_HEADS * PEER_QDIM), row)),
        out_shape=(jax.ShapeDtypeStruct((D_MODEL, NT), BF16),
                   jax.ShapeDtypeStruct((NT, PEER_HEADS * PEER_QDIM), F32)),
        compiler_params=_cparams("parallel"),
        name="adaln_peer_query",
    )(x1, mod4, norm_g, w_q)


ROUTE_TL = 256
ROUTE_LANES = 128
NOT_SEL = float(PEER_TOPK)


def _top16(s, index_ties, want_rank=True):
    idx = lax.broadcasted_iota(jnp.int32, s.shape, 0).astype(F32)
    slot = lax.broadcasted_iota(jnp.int32, (PEER_TOPK, s.shape[1]), 0)
    rank = jnp.full(s.shape, NOT_SEL, F32)
    vals = jnp.zeros((PEER_TOPK, s.shape[1]), F32)
    for k in range(PEER_TOPK):
        m = jnp.max(s, axis=0, keepdims=True)
        sel = s == m
        if index_ties:
            sel = idx == jnp.min(jnp.where(sel, idx, float(PEER_NKEYS)), axis=0, keepdims=True)
        if want_rank:
            rank = jnp.where(sel, float(k), rank)
        s = jnp.where(sel, NEG, s)
        vals = jnp.where(slot == k, m, vals)
    return (rank if want_rank else None), vals


CAND_HALF = PEER_TOPK // 2
CAND_ROWS = PEER_TOPK + (CAND_HALF - 1) * CAND_HALF + CAND_HALF
FLAT_NONE = float(PEER_TOPK * PEER_TOPK)


def _cand_flat(tl):
    r = lax.broadcasted_iota(jnp.int32, (CAND_ROWS, tl), 0)
    mid = r - PEER_TOPK
    mid_flat = (1 + mid // CAND_HALF) * PEER_TOPK + mid % CAND_HALF
    last_flat = (CAND_HALF + r - (CAND_ROWS - CAND_HALF)) * PEER_TOPK
    flat = jnp.where(r < PEER_TOPK, r, jnp.where(r < CAND_ROWS - CAND_HALF, mid_flat, last_flat))
    return flat.astype(F32)


def _count(mask):
    return jnp.sum(jnp.where(mask, 1.0, 0.0), axis=0, keepdims=True)


def _route_head(s1, s2, flat, index_ties):
    tl = s1.shape[1]
    rank1, v1 = _top16(s1, index_ties, want_rank=index_ties)
    rank2, v2 = _top16(s2, index_ties)
    slabs = [v1[0:1, :] + v2]
    slabs += [v1[a:a + 1, :] + v2[0:CAND_HALF, :] for a in range(1, CAND_HALF)]
    slabs.append(v1[CAND_HALF:, :] + v2[0:1, :])
    cand = jnp.concatenate(slabs, axis=0)
    if not index_ties:
        v1_row = jnp.concatenate(
            [jnp.broadcast_to(v1[0:1, :], (PEER_TOPK, tl))]
            + [jnp.broadcast_to(v1[a:a + 1, :], (CAND_HALF, tl)) for a in range(1, CAND_HALF)]
            + [v1[CAND_HALF:, :]], axis=0)
    top = v1[0:1, :] + v2[0:1, :]
    cnt1 = jnp.zeros(s1.shape, F32)
    z = jnp.zeros((1, tl), F32)
    for k in range(PEER_TOPK):
        m = jnp.max(cand, axis=0, keepdims=True)
        sel = cand == m
        if index_ties:
            first = jnp.min(jnp.where(sel, flat, FLAT_NONE), axis=0, keepdims=True)
            sel = flat == first
            hit = rank1 == jnp.floor(first * (1.0 / PEER_TOPK))
        else:
            hit = s1 == jnp.max(jnp.where(sel, v1_row, NEG), axis=0, keepdims=True)
        cand = jnp.where(sel, NEG, cand)
        cnt1 = cnt1 + jnp.where(hit, 1.0, 0.0)
        z = z + jnp.exp(m - top)
    e1 = jnp.exp(s1 - v1[0:1, :]) / z
    e2 = jnp.exp(s2 - v2[0:1, :])
    if index_ties:
        return rank2, cnt1, e1, e2, None
    full = float(PEER_TOPK)
    clean = ((_count(s1 >= v1[PEER_TOPK - 1:PEER_TOPK, :]) == full) & (_count(rank2 < NOT_SEL) == full)
             & (_count(cand == NEG) == full))
    return rank2, cnt1, e1, e2, jnp.max(jnp.where(clean, 0.0, 1.0))


def _route_kernel(q_ref, k1_ref, k2_ref, rank2_ref, e2_ref, cnt1_ref, e1_ref):
    tl = q_ref.shape[0]
    half = PEER_QDIM // 2

    n_chunks = tl // ROUTE_LANES

    def scores(h):
        q1 = q_ref[:, h * PEER_QDIM:h * PEER_QDIM + half].astype(BF16)
        q2 = q_ref[:, h * PEER_QDIM + half:(h + 1) * PEER_QDIM].astype(BF16)
        return _dot_nt(k1_ref[...], q1), _dot_nt(k2_ref[...], q2)

    def route(h, c, s1, s2, index_ties):
        lanes = slice(c * ROUTE_LANES, (c + 1) * ROUTE_LANES)
        rank2, cnt1, e1, e2, tied = _route_head(s1[:, lanes], s2[:, lanes], _cand_flat(ROUTE_LANES), index_ties)
        rank2_ref[h, :, lanes] = rank2.astype(BF16)
        cnt1_ref[h, :, lanes] = cnt1
        e1_ref[h, :, lanes] = e1
        e2_ref[h, :, lanes] = e2.astype(BF16)
        return tied

    tied = {}
    for h in range(PEER_HEADS):
        s1, s2 = scores(h)
        for c in range(n_chunks):
            tied[h, c] = route(h, c, s1, s2, index_ties=False)

    for h in range(PEER_HEADS):
        for c in range(n_chunks):
            @pl.when(tied[h, c] > 0.0)
            def _():
                s1, s2 = scores(h)
                route(h, c, s1, s2, index_ties=True)


def _peer_route(q, keys, l):
    shp = jax.ShapeDtypeStruct((PEER_HEADS, PEER_NKEYS, NT), F32)
    shp_b = jax.ShapeDtypeStruct((PEER_HEADS, PEER_NKEYS, NT), BF16)
    spec = pl.BlockSpec((PEER_HEADS, PEER_NKEYS, ROUTE_TL), lambda i: (0, 0, i))
    half = PEER_QDIM // 2
    return pl.pallas_call(
        _route_kernel,
        grid=(NT // ROUTE_TL,),
        in_specs=[
            pl.BlockSpec((ROUTE_TL, PEER_HEADS * PEER_QDIM), lambda i: (i, 0)),
            pl.BlockSpec((None, None, PEER_NKEYS, half), lambda i: (l, 0, 0, 0)),
            pl.BlockSpec((None, None, PEER_NKEYS, half), lambda i: (l, 1, 0, 0)),
        ],
        out_specs=(spec, spec, spec, spec),
        out_shape=(shp_b, shp_b, shp, shp),
        compiler_params=_cparams("parallel"),
        name="peer_route",
    )(q, keys, keys)


PEER_TT = 512
PEER_EB = 1024
GELU_C = math.sqrt(2.0 / math.pi)


def _gelu_tanh(x):
    return 0.5 * x * (1.0 + jnp.tanh(GELU_C * (x + 0.044715 * (x * x * x))))


PEER_NE = PEER_N // PEER_EB
PEER_MT = 256


def _peer_kernel(ht_ref, u_ref, vt_ref, rank2_ref, e2_ref, cnt1_ref, e1_ref, x_ref, mod_ref, o_ref,
                 ga_ref, gb_ref, acc_ref):
    e = pl.program_id(1)

    def step(g_prev_ref, g_next_ref, activate=True, apply=True):
        blk = jnp.maximum(e - 1, 0)
        n_i = PEER_EB // PEER_NKEYS
        per_tile = PEER_MT // PEER_NKEYS

        def weighted(tok, kc):
            tiles = []
            for ii in range(kc * per_tile, (kc + 1) * per_tile):
                key1 = blk * n_i + ii
                w = jnp.zeros((PEER_NKEYS, PEER_MT), BF16)
                for h in range(PEER_HEADS):
                    cnt = cnt1_ref[h, pl.ds(key1, 1), tok].astype(BF16)
                    g1 = e1_ref[h, pl.ds(key1, 1), tok].astype(BF16)
                    w = w + jnp.where(rank2_ref[h, :, tok] < cnt, e2_ref[h, :, tok] * g1, 0.0)
                tiles.append(w * g_prev_ref[ii * PEER_NKEYS:(ii + 1) * PEER_NKEYS, tok])
            return jnp.concatenate(tiles, axis=0)

        n_kc = PEER_EB // PEER_MT
        for tc in range(PEER_TT // PEER_MT):
            tok = slice(tc * PEER_MT, (tc + 1) * PEER_MT)
            if activate:
                act = _dot(u_ref[...], ht_ref[:, tok])
            if not apply:
                g_next_ref[:, tok] = _gelu_tanh(act).astype(BF16)
                continue
            p = [weighted(tok, 0), weighted(tok, 1)]
            acc = None
            for kc in range(n_kc):
                if kc + 2 < n_kc:
                    p.append(weighted(tok, kc + 2))
                t = _dot(vt_ref[:, kc * PEER_MT:(kc + 1) * PEER_MT], p[kc])
                acc = t if acc is None else acc + t
                if activate and kc == n_kc - 2:
                    g_next_ref[:, tok] = _gelu_tanh(act).astype(BF16)
            acc_ref[:, tok] += acc

    assert PEER_NE % 2 == 0
    inner = jnp.logical_and(e > 0, e < PEER_NE)

    @pl.when(e == 0)
    def _():
        acc_ref[...] = jnp.zeros_like(acc_ref)
        step(None, ga_ref, apply=False)

    @pl.when(jnp.logical_and(inner, e % 2 == 0))
    def _():
        step(gb_ref, ga_ref)

    @pl.when(e % 2 == 1)
    def _():
        step(ga_ref, gb_ref)

    @pl.when(e == PEER_NE)
    def _():
        step(gb_ref, None, activate=False)
        o_ref[...] = x_ref[...] + mod_ref[5:6, :] * acc_ref[...].T


def _peer(ht, u_tab, vt_tab, routing, x1, mod4, l):
    rank2, e2, cnt1, e1 = routing
    once = pl.Buffered(1)
    rspec = pl.BlockSpec((PEER_HEADS, PEER_NKEYS, PEER_TT), lambda i, e: (0, 0, i), pipeline_mode=once)
    return pl.pallas_call(
        _peer_kernel,
        grid=(NT // PEER_TT, PEER_NE + 1),
        in_specs=[
            pl.BlockSpec((D_MODEL, PEER_TT), lambda i, e: (0, i), pipeline_mode=once),
            pl.BlockSpec((None, PEER_EB, D_MODEL), lambda i, e: (l, jnp.minimum(e, PEER_NE - 1), 0)),
            pl.BlockSpec((None, D_MODEL, PEER_EB), lambda i, e: (l, 0, jnp.maximum(e - 1, 0))),
            rspec, rspec, rspec, rspec,
            pl.BlockSpec((PEER_TT, D_MODEL), lambda i, e: (i, 0), pipeline_mode=once),
            pl.BlockSpec((None, None, 6, D_MODEL), lambda i, e: (l, _mod_set(i * PEER_TT), 0, 0)),
        ],
        out_specs=pl.BlockSpec((PEER_TT, D_MODEL), lambda i, e: (i, 0)),
        out_shape=jax.ShapeDtypeStruct((NT, D_MODEL), F32),
        scratch_shapes=[pltpu.VMEM((PEER_EB, PEER_TT), BF16), pltpu.VMEM((PEER_EB, PEER_TT), BF16),
                        pltpu.VMEM((D_MODEL, PEER_TT), F32)],
        compiler_params=_cparams("parallel", "arbitrary"),
        name="peer_experts",
    )(ht, u_tab, vt_tab, rank2, e2, cnt1, e1, x1, mod4)


FIN_TM = 512


def _final_kernel(x_ref, g_ref, o_ref):
    o_ref[...] = _rms(x_ref[...], g_ref[...])


def _final_norm(x, g):
    return pl.pallas_call(
        _final_kernel,
        grid=(NT // FIN_TM,),
        in_specs=[pl.BlockSpec((FIN_TM, D_MODEL), lambda i: (i, 0)),
                  pl.BlockSpec((1, D_MODEL), lambda i: (0, 0))],
        out_specs=pl.BlockSpec((FIN_TM, D_MODEL), lambda i: (i, 0)),
        out_shape=jax.ShapeDtypeStruct((NT, D_MODEL), F32),
        compiler_params=_cparams("parallel"),
        name="final_norm",
    )(x, g)


def _permute_w_in(w_in):
    sizes = (GROUP_W, GROUP_W, GROUP_W, GROUP_W, 4 * ML_HEADS, MLA_Q_RANK, MLA_KV_RANK, MLA_D_ROPE,
             SWA_HEADS * SWA_DH, SWA_KV_HEADS * SWA_DH, SWA_KV_HEADS * SWA_DH, GROUP_W, GROUP_W, GROUP_W)
    offs = [0]
    for s in sizes:
        offs.append(offs[-1] + s)
    part = lambda i: w_in[:, :, offs[i]:offs[i + 1]]
    order = [0, 1, 2, 3, 5, 6, 8, 9, 10, 11, 12, 13, 7, 4]
    cols = [part(i) for i in order]
    used = sum(sizes)
    cols.append(jnp.zeros(w_in.shape[:2] + (PROJ_W - used,), w_in.dtype))
    return jnp.concatenate(cols, axis=-1).astype(BF16)


def _permute_w_uq(w_uq):
    w = w_uq.reshape(DEPTH, MLA_Q_RANK, MLA_HEADS, MLA_D_NOPE + MLA_D_ROPE)
    nope = w[..., :MLA_D_NOPE].reshape(DEPTH, MLA_Q_RANK, MLA_HEADS * MLA_D_NOPE)
    rope = w[..., MLA_D_NOPE:].reshape(DEPTH, MLA_Q_RANK, MLA_HEADS * MLA_D_ROPE)
    return jnp.concatenate([nope, rope], axis=-1).astype(BF16)


def kernel(x_prompt, x_sample, c, cache_mla_ckv, cache_mla_krope, cache_swa_k, cache_swa_v, cache_diff_k,
           cache_diff_v, state_mlstm_C, state_mlstm_n, state_mlstm_m, c_ctx, w_mod, b_mod, norm1_g, w_in,
           mlstm_i_bias, mlstm_f_bias, mlstm_norm_g, mla_qnorm_g, mla_w_uq, mla_kvnorm_g, mla_w_ukv, swa_sink,
           diff_lq1, diff_lk1, diff_lq2, diff_lk2, diff_norm_g, w_out, norm2_g, peer_w_q, peer_sub_keys,
           peer_u, peer_v, final_norm_g):
    x = jnp.concatenate([x_prompt.reshape(N_CTX, D_MODEL), x_sample.reshape(N_LAT, D_MODEL)], axis=0)

    cvec = jnp.concatenate([c_ctx[None, :], c], axis=0)
    cvec_t = jnp.pad(cvec.T, ((0, 0), (0, 8 - N_SETS)))
    mod4 = _modulation(cvec_t, w_mod, b_mod).reshape(DEPTH, 8, 6, D_MODEL)

    w_in_p = _permute_w_in(w_in)
    w_uq_p = _permute_w_uq(mla_w_uq)
    w_ukv_b = mla_w_ukv.astype(BF16)
    w_out_b = w_out.astype(BF16)
    w_q_b = peer_w_q.astype(BF16)
    keys_b = peer_sub_keys.astype(BF16)
    u_b = peer_u.astype(BF16)
    vt_b = jnp.swapaxes(peer_v, 1, 2).astype(BF16)

    vec3 = lambda a: a.reshape(DEPTH, 1, a.shape[-1])
    norm1_3, norm2_3 = vec3(norm1_g), vec3(norm2_g)
    mlng_3, qg_3, kg_3, dng_3 = vec3(mlstm_norm_g), vec3(mla_qnorm_g), vec3(mla_kvnorm_g), vec3(diff_norm_g)
    lq1_3, lk1_3, lq2_3, lk2_3 = vec3(diff_lq1), vec3(diff_lk1), vec3(diff_lq2), vec3(diff_lk2)

    n_chain = 2 * ML_HEADS
    st_c = state_mlstm_C.reshape(DEC_BATCH, DEPTH, n_chain, ML_DH, ML_DH)
    st_n = state_mlstm_n.reshape(DEC_BATCH, DEPTH, n_chain, ML_DH)
    st_m = jnp.broadcast_to(state_mlstm_m.reshape(DEC_BATCH, DEPTH, n_chain, 1), (DEC_BATCH, DEPTH, n_chain, ML_DH))
    swk_c = cache_swa_k.reshape(DEC_BATCH, DEPTH, PAST_LEN, SWA_KV_HEADS * SWA_DH)
    swv_c = cache_swa_v.reshape(DEC_BATCH, DEPTH, PAST_LEN, SWA_KV_HEADS * SWA_DH)
    dfk_c = cache_diff_k.reshape(DEC_BATCH, DEPTH, PAST_LEN, GROUP_W)
    dfv_c = cache_diff_v.reshape(DEC_BATCH, DEPTH, PAST_LEN, GROUP_W)

    cos64, sin64 = _rope_tables(DEC_SEQ, 64)
    cos128, sin128 = _rope_tables(DEC_SEQ, 128)

    outs = [[] for _ in range(9)]
    for l in range(DEPTH):
        proj = _project(x, mod4, norm1_3, w_in_p, l)

        gates = proj[:, C_TAIL + TAIL_G:C_TAIL + TAIL_G + 16]
        gt = gates.reshape(NT // ML_CHUNK, ML_CHUNK, 16).transpose(0, 2, 1)
        bias = jnp.concatenate([mlstm_i_bias[l].reshape(-1), mlstm_f_bias[l].reshape(-1)])
        bcol = bias.reshape(16, 1)
        brow = jnp.zeros((1, 128), F32).at[0, TAIL_G:TAIL_G + 16].set(bias)

        ml_ctx, c_st, n_st, m_st = _mlstm(proj, gt, bcol, brow, mlng_3, l, latent=False)
        ml_lat = _mlstm(proj, gt, bcol, brow, mlng_3, l, latent=True, states=(st_c, st_n, st_m))
        mla_ctx, ckv_n = _mla_ctx(proj, qg_3, kg_3, w_uq_p, w_ukv_b, l)
        mla_lat = _mla_lat(proj, qg_3, kg_3, w_uq_p, w_ukv_b, cache_mla_ckv, cache_mla_krope, cos64, sin64, l)
        sink = swa_sink[l].reshape(1, SWA_HEADS)
        swa_ctx = _swa_ctx(proj, sink, l)
        swa_lat = _swa_lat(proj, sink, swk_c, swv_c, cos128, sin128, l)
        df_ctx = _diff_ctx(proj, lq1_3, lk1_3, lq2_3, lk2_3, dng_3, l)
        df_lat = _diff_lat(proj, lq1_3, lk1_3, lq2_3, lk2_3, dng_3, dfk_c, dfv_c, cos64, sin64, l)

        x1 = _out_proj((ml_ctx, mla_ctx, swa_ctx, df_ctx), (ml_lat, mla_lat, swa_lat, df_lat), x, mod4, w_out_b, l)
        h2t, q = _peer_query(x1, mod4, norm2_3, w_q_b, l)
        routing = _peer_route(q, keys_b, l)
        x = _peer(h2t, u_b, vt_b, routing, x1, mod4, l)

        pc = proj[:N_CTX]
        outs[0].append(ckv_n.reshape(BATCH, SEQ, MLA_KV_RANK))
        outs[1].append(pc[:, C_TAIL:C_TAIL + MLA_D_ROPE].reshape(BATCH, SEQ, MLA_D_ROPE))
        outs[2].append(pc[:, C_SK:C_SK + 256].reshape(BATCH, SEQ, SWA_KV_HEADS, SWA_DH))
        outs[3].append(pc[:, C_SV:C_SV + 256].reshape(BATCH, SEQ, SWA_KV_HEADS, SWA_DH))
        outs[4].append(pc[:, C_DK:C_DK + 512].reshape(BATCH, SEQ, DIFF_HEADS, 2 * DIFF_DH))
        outs[5].append(pc[:, C_DV:C_DV + 512].reshape(BATCH, SEQ, DIFF_HEADS, 2 * DIFF_DH))
        outs[6].append(c_st.reshape(BATCH, 2, ML_HEADS, ML_DH, ML_DH))
        outs[7].append(n_st.reshape(BATCH, 2, ML_HEADS, ML_DH))
        outs[8].append(m_st[:, :, 0].reshape(BATCH, 2, ML_HEADS))

    y = _final_norm(x, final_norm_g.reshape(1, D_MODEL))
    y_prompt = y[:N_CTX].reshape(BATCH, SEQ, D_MODEL)
    y_sample = y[N_CTX:].reshape(DEC_BATCH, DEC_SEQ, D_MODEL)
    return (y_prompt, y_sample) + tuple(jnp.stack(o, axis=1) for o in outs)
```
